```python
import jax, jax.numpy as jnp
from jax import lax
import numpy as np

D_MODEL = 1024
BATCH = 4
SEQ = 8192
DEPTH = 1
DEC_BATCH = 32
DEC_SEQ = 64
PAST_LEN = 1024

CHUNK = 64
N_META = 16
N_HEADS = 4
HEAD_K = 64
HEAD_V = 128
RET_K = N_HEADS * HEAD_K
RET_V = N_HEADS * HEAD_V
GLA_K = N_HEADS * HEAD_K
GLA_V = N_HEADS * HEAD_V
GLA_RANK = 16
GATE_NORM = 16.0
ROPE_BASE = 10000.0
N_GROUPS = 4
EXPERTS_PER_GROUP = 8
N_EXPERTS = N_GROUPS * EXPERTS_PER_GROUP
TOP_K_INNER = 2
D_EXPERT = 512
MOE_BLOCK = 128
NORM_EPS = 1e-6
PROJ_SIZES = (RET_K, RET_K, RET_V, RET_V, GLA_K, GLA_K, GLA_V, GLA_V, GLA_RANK, D_MODEL, D_MODEL)
D_PROJ = 2 * (2 * RET_K + 2 * RET_V) + GLA_RANK + 2 * D_MODEL

kernel_name = "hybrid_retention_gla_hiermoe_stream_step"


def _rmsnorm(x, g):
    xf = x.astype(jnp.float32)
    y = xf * lax.rsqrt(jnp.mean(xf * xf, axis=-1, keepdims=True) + NORM_EPS)
    return (y * g.astype(jnp.float32)).astype(x.dtype)


def _rotary(x, pos):
    half = HEAD_K // 2
    inv = ROPE_BASE ** (-2.0 * jnp.arange(half, dtype=jnp.float32) / HEAD_K)
    ang = pos[..., None] * inv
    cos = jnp.cos(ang)[None, :, :, None, :]
    sin = jnp.sin(ang)[None, :, :, None, :]
    x1, x2 = x[..., :half], x[..., half:]
    return jnp.concatenate([x1 * cos - x2 * sin, x1 * sin + x2 * cos], axis=-1)


def _retention(q, k, v, g, pos, s0):
    B, NC, L = q.shape[:3]
    q = _rotary(q, pos)
    k = _rotary(k, pos) * (HEAD_K ** -0.5)
    log_g = jnp.log1p(-jnp.exp2(-5.0 - jnp.arange(N_HEADS, dtype=jnp.float32)))
    idx = jnp.arange(L, dtype=jnp.float32)
    dmat = jnp.exp(jnp.abs(idx[:, None] - idx[None, :])[None] * log_g[:, None, None])
    scores = jnp.einsum("bclhk,bcmhk->bchlm", q, k) * dmat
    intra = jnp.einsum("bchlm,bcmhv->bclhv", scores, v)
    k_dec = jnp.exp((L - 1.0 - idx)[:, None] * log_g)[:, :, None]
    u = jnp.einsum("bclhk,bclhv->bchkv", k * k_dec, v)
    c_dec = jnp.exp(L * log_g)[:, None, None]

    def step(s, u_c):
        return s * c_dec + u_c, s

    s_last, s_prev = lax.scan(step, s0, jnp.moveaxis(u, 1, 0))
    q_dec = jnp.exp((idx + 1.0)[:, None] * log_g)[:, :, None]
    inter = jnp.einsum("bclhk,cbhkv->bclhv", q * q_dec, s_prev)
    o = intra + inter
    mu = jnp.mean(o, axis=-1, keepdims=True)
    var = jnp.mean(jnp.square(o - mu), axis=-1, keepdims=True)
    o = (o - mu) * lax.rsqrt(var + NORM_EPS)
    return o.reshape(B, NC, L, RET_V) * jax.nn.silu(g), s_last


def _gla(q, k, v, g, gate_logits, s0, g_norm):
    B, NC, L = q.shape[:3]
    q = q * (HEAD_K ** -0.5)
    log_a = jax.nn.log_sigmoid(gate_logits) / GATE_NORM
    b = jnp.cumsum(log_a, axis=2)
    b_last = b[:, :, -1]
    u = jnp.einsum("bclhk,bclhv->bchkv", k * jnp.exp(b_last[:, :, None] - b), v)
    a = jnp.exp(b_last)[..., None]

    def step(s, inp):
        a_c, u_c = inp
        s = a_c * s + u_c
        return s, s

    s_last, s_all = lax.scan(step, s0, (jnp.moveaxis(a, 1, 0), jnp.moveaxis(u, 1, 0)))
    o = jnp.einsum("bclhk,cbhkv->bclhv", q, s_all)
    o = o * lax.rsqrt(jnp.mean(o * o, axis=-1, keepdims=True) + NORM_EPS) * g_norm.astype(jnp.float32)
    return o.reshape(B, NC, L, GLA_V) * jax.nn.silu(g), s_last


def _token_mixer(h, pos, s_ret, s_gla, w_in, w_gla_gk2, b_gla_gk, g_gla_norm, w_br_ret, w_br_gla, w_out):
    B, NC, L, _ = h.shape
    f32 = jnp.float32
    proj = (h @ w_in).astype(f32)
    rq, rk, rv, rg, gq, gk, gv, gg, glr, z_ret, z_gla = jnp.split(
        proj, [int(c) for c in np.cumsum(PROJ_SIZES)[:-1]], axis=-1)
    heads = lambda t, d: t.reshape(B, NC, L, N_HEADS, d)
    o_ret, s_ret_new = _retention(heads(rq, HEAD_K), heads(rk, HEAD_K), heads(rv, HEAD_V), rg,
                                  pos, s_ret.astype(f32))
    gate_logits = glr @ w_gla_gk2.astype(f32) + b_gla_gk.astype(f32)
    o_gla, s_gla_new = _gla(heads(gq, HEAD_K), heads(gk, HEAD_K), heads(gv, HEAD_V), gg,
                            heads(gate_logits, HEAD_K), s_gla.astype(f32), g_gla_norm)
    merged = jax.nn.sigmoid(z_ret) * (o_ret @ w_br_ret) + jax.nn.sigmoid(z_gla) * (o_gla @ w_br_gla)
    y = merged.astype(h.dtype) @ w_out
    return y.astype(h.dtype), s_ret_new, s_gla_new


def _hier_moe(x, w_rg, b_rg, w_re, b_re, w_gate, w_up, w_down):
    T, D = x.shape
    f32 = jnp.float32
    lg = (x @ w_rg + b_rg).astype(f32)
    grp = jnp.argmax(lg, axis=-1)
    p_grp = jnp.take_along_axis(jax.nn.softmax(lg, axis=-1), grp[:, None], axis=1)[:, 0]
    le = (x @ w_re + b_re).astype(f32).reshape(T, N_GROUPS, EXPERTS_PER_GROUP)
    le_sel = le[jnp.arange(T), grp]
    top_v, top_i = lax.top_k(le_sel, TOP_K_INNER)
    gate = jax.nn.softmax(top_v, axis=-1) * p_grp[:, None]
    eid = (grp[:, None] * EXPERTS_PER_GROUP + top_i).reshape(-1).astype(jnp.int32)
    tok = jnp.repeat(jnp.arange(T, dtype=jnp.int32), TOP_K_INNER)
    wgt = gate.reshape(-1)
    A = T * TOP_K_INNER
    order = jnp.argsort(eid)
    e_s, tok_s, w_s = eid[order], tok[order], wgt[order]
    counts = jnp.bincount(eid, length=N_EXPERTS)
    padded = (counts + MOE_BLOCK - 1) // MOE_BLOCK * MOE_BLOCK
    pad_end = jnp.cumsum(padded)
    pad_start = pad_end - padded
    start = jnp.cumsum(counts) - counts
    dest = pad_start[e_s] + jnp.arange(A, dtype=jnp.int32) - start[e_s]
    n_blocks = -(-A // MOE_BLOCK) + N_EXPERTS
    P = n_blocks * MOE_BLOCK
    buf_tok = jnp.full((P,), T, jnp.int32).at[dest].set(tok_s)
    buf_w = jnp.zeros((P,), f32).at[dest].set(w_s)
    blk_e = jnp.minimum(jnp.searchsorted(pad_end, jnp.arange(n_blocks) * MOE_BLOCK, side="right"),
                        N_EXPERTS - 1)
    x_pad = jnp.concatenate([x, jnp.zeros((1, D), x.dtype)], axis=0)
    xb = x_pad[buf_tok].reshape(n_blocks, MOE_BLOCK, D)

    def run_block(args):
        xb_i, e = args
        hid = jax.nn.silu(xb_i @ w_gate[e]) * (xb_i @ w_up[e])
        return hid @ w_down[e]

    yb = lax.map(run_block, (xb, blk_e))
    y = jnp.zeros((T + 1, D), f32).at[buf_tok].add(yb.reshape(P, D).astype(f32) * buf_w[:, None])
    return y[:T].astype(x.dtype)


def _layer(x, pad, pos, s_ret, s_gla, g_mix, w_in, w_gla_gk2, b_gla_gk, g_gla_norm, w_br_ret,
           w_br_gla, w_out, g_ffn, w_rg, b_rg, w_re, b_re, w_gate, w_up, w_down):
    B, N, D = x.shape
    NC, L = pos.shape
    h = jnp.pad(_rmsnorm(x, g_mix), ((0, 0), (pad, 0), (0, 0))).reshape(B, NC, L, D)
    y, s_ret_new, s_gla_new = _token_mixer(h, pos, s_ret, s_gla, w_in, w_gla_gk2, b_gla_gk,
                                           g_gla_norm, w_br_ret, w_br_gla, w_out)
    x = x + y.reshape(B, NC * L, D)[:, pad:]
    h2 = _rmsnorm(x, g_ffn).reshape(B * N, D)
    x = x + _hier_moe(h2, w_rg, b_rg, w_re, b_re, w_gate, w_up, w_down).reshape(B, N, D)
    return x, s_ret_new, s_gla_new


def setup_inputs(seed: int = 0) -> dict:
    key = jax.random.key(seed)
    ks = jax.random.split(key, 22)
    f32 = jnp.float32
    nrm = lambda k, shape, scale: jax.random.normal(k, shape, f32) * scale
    st = (DEPTH, DEC_BATCH, N_HEADS, HEAD_K, HEAD_V)
    return {
        "x_prompt": nrm(ks[0], (BATCH, SEQ, D_MODEL), 1.0),
        "x_sample": nrm(ks[1], (DEC_BATCH, DEC_SEQ, D_MODEL), 1.0),
        "state_ret": nrm(ks[2], st, 0.5),
        "state_gla": nrm(ks[3], st, 0.5),
        "meta_tokens": nrm(ks[4], (N_META, D_MODEL), 1.0),
        "g_mix": 1.0 + nrm(ks[5], (DEPTH, D_MODEL), 0.01),
        "w_in": nrm(ks[6], (DEPTH, D_MODEL, D_PROJ), D_MODEL ** -0.5),
        "w_gla_gk2": nrm(ks[7], (DEPTH, GLA_RANK, GLA_K), GLA_RANK ** -0.5),
        "b_gla_gk": nrm(ks[8], (DEPTH, GLA_K), 0.1),
        "g_gla_norm": 1.0 + nrm(ks[9], (DEPTH, HEAD_V), 0.01),
        "w_br_ret": nrm(ks[10], (DEPTH, RET_V, D_MODEL), RET_V ** -0.5),
        "w_br_gla": nrm(ks[11], (DEPTH, GLA_V, D_MODEL), GLA_V ** -0.5),
        "w_out": nrm(ks[12], (DEPTH, D_MODEL, D_MODEL), D_MODEL ** -0.5),
        "g_ffn": 1.0 + nrm(ks[13], (DEPTH, D_MODEL), 0.01),
        "w_router_group": nrm(ks[14], (DEPTH, D_MODEL, N_GROUPS), D_MODEL ** -0.5),
        "b_router_group": nrm(ks[15], (DEPTH, N_GROUPS), 0.01),
        "w_router_expert": nrm(ks[16], (DEPTH, D_MODEL, N_EXPERTS), D_MODEL ** -0.5),
        "b_router_expert": nrm(ks[17], (DEPTH, N_EXPERTS), 0.01),
        "w_exp_gate": nrm(ks[18], (DEPTH, N_EXPERTS, D_MODEL, D_EXPERT), D_MODEL ** -0.5),
        "w_exp_up": nrm(ks[19], (DEPTH, N_EXPERTS, D_MODEL, D_EXPERT), D_MODEL ** -0.5),
        "w_exp_down": nrm(ks[20], (DEPTH, N_EXPERTS, D_EXPERT, D_MODEL), D_EXPERT ** -0.5),
        "g_final": 1.0 + nrm(ks[21], (D_MODEL,), 0.01),
    }


def reference(x_prompt, x_sample, state_ret, state_gla, meta_tokens, g_mix, w_in, w_gla_gk2, b_gla_gk,
              g_gla_norm, w_br_ret, w_br_gla, w_out, g_ffn, w_router_group, b_router_group,
              w_router_expert, b_router_expert, w_exp_gate, w_exp_up, w_exp_down, g_final):
    f32 = jnp.float32
    B, S, D = x_prompt.shape
    Ts = x_sample.shape[1]
    xp = jnp.concatenate([jnp.broadcast_to(meta_tokens.astype(x_prompt.dtype)[None], (B, N_META, D)),
                          x_prompt], axis=1)
    pad_p = CHUNK - N_META
    pos_p = (jnp.arange(CHUNK + S, dtype=f32) - CHUNK).reshape(-1, CHUNK)
    pos_s = (PAST_LEN + jnp.arange(Ts, dtype=f32)).reshape(1, Ts)
    zero_state = jnp.zeros((B, N_HEADS, HEAD_K, HEAD_V), f32)
    xs = x_sample
    ret_p, gla_p, ret_s, gla_s = [], [], [], []
    for l in range(DEPTH):
        lw = (g_mix[l], w_in[l], w_gla_gk2[l], b_gla_gk[l], g_gla_norm[l], w_br_ret[l], w_br_gla[l],
              w_out[l], g_ffn[l], w_router_group[l], b_router_group[l], w_router_expert[l],
              b_router_expert[l], w_exp_gate[l], w_exp_up[l], w_exp_down[l])
        xp, sr, sg = _layer(xp, pad_p, pos_p, zero_state, zero_state, *lw)
        ret_p.append(sr.astype(x_prompt.dtype))
        gla_p.append(sg.astype(x_prompt.dtype))
        xs, sr, sg = _layer(xs, 0, pos_s, state_ret[l], state_gla[l], *lw)
        ret_s.append(sr.astype(state_ret.dtype))
        gla_s.append(sg.astype(state_gla.dtype))
    y_prompt = _rmsnorm(xp, g_final)[:, N_META:]
    y_sample = _rmsnorm(xs, g_final)
    return (y_prompt, y_sample, jnp.stack(ret_p), jnp.stack(gla_p), jnp.stack(ret_s), jnp.stack(gla_s))
```

```python
import functools
import math

import jax
import jax.numpy as jnp
import numpy as np
from jax import lax
from jax.experimental import pallas as pl
from jax.experimental.pallas import tpu as pltpu

D_MODEL = 1024
CHUNK = 64
PAST_LEN = 1024
N_META = 16
N_HEADS = 4
HEAD_K = 64
HEAD_V = 128
HK = N_HEADS * HEAD_K
HV = N_HEADS * HEAD_V
GLA_RANK = 16
GATE_NORM = 16.0
ROPE_BASE = 10000.0
N_GROUPS = 4
EXPERTS_PER_GROUP = 8
N_EXPERTS = N_GROUPS * EXPERTS_PER_GROUP
D_EXPERT = 512
NORM_EPS = 1e-6

LANES = 128
SUBLANES = 8
TILE_CHUNKS = 8
TILE_ROWS = TILE_CHUNKS * CHUNK
ROUTE_ROWS = 8
ROUTER_LANES = 128
MOE_BLOCK = 256
VMEM_LIMIT = 56 * 1024 * 1024

C_RQ, C_RK, C_RV, C_RG = 0, 256, 512, 1024
C_GQ, C_GK, C_GV, C_GG = 1536, 1792, 2048, 2560
C_ZR, C_ZG, C_LR = 3072, 4096, 5120
W_CAT = 5248

F32 = jnp.float32
BF16 = jnp.bfloat16
LOG_G = tuple(math.log1p(-(2.0 ** (-5.0 - h))) for h in range(N_HEADS))


def _dot(a, b):
    return jnp.dot(a, b, preferred_element_type=F32)


def _dot_nt(a, b):
    return lax.dot_general(a, b, (((1,), (1,)), ((), ())), preferred_element_type=F32)


def _dot_tn(a, b):
    return lax.dot_general(a, b, (((0,), (0,)), ((), ())), preferred_element_type=F32)


def _sigmoid(x):
    return 1.0 / (1.0 + jnp.exp(-x))


def _head_of_lane(shape, width):
    return lax.broadcasted_iota(jnp.int32, shape, len(shape) - 1) >> int(math.log2(width))


def _per_head_lane_const(vals, shape, width):
    hd = _head_of_lane(shape, width)
    out = jnp.full(shape, vals[N_HEADS - 1], F32)
    for h in range(N_HEADS - 2, -1, -1):
        out = jnp.where(hd == h, vals[h], out)
    return out


def _tables_kernel(inv_ref, cp_ref, sp_ref, cm_ref, sm_ref, cs_ref, ss_ref, *, past_len):
    inv = inv_ref[...]

    def fill(c_ref, s_ref, pos0):
        rows = c_ref.shape[0]
        pos = (lax.broadcasted_iota(jnp.int32, (rows, LANES), 0) + pos0).astype(F32)
        ang = pos * inv
        lane = lax.broadcasted_iota(jnp.int32, (rows, LANES), 1)
        first_half = (lane & (HEAD_K - 1)) < (HEAD_K // 2)
        c_ref[...] = jnp.cos(ang)
        s = jnp.sin(ang)
        s_ref[...] = jnp.where(first_half, -s, s)

    fill(cp_ref, sp_ref, 0)
    fill(cm_ref, sm_ref, -CHUNK)
    fill(cs_ref, ss_ref, past_len)


def _rotary_tables(seq, past_len):
    half = HEAD_K // 2
    inv = ROPE_BASE ** (-2.0 * jnp.arange(half, dtype=F32) / HEAD_K)
    inv = jnp.tile(inv, LANES // half).reshape(1, LANES)
    shp = lambda r: jax.ShapeDtypeStruct((r, LANES), F32)
    return pl.pallas_call(
        functools.partial(_tables_kernel, past_len=past_len),
        out_shape=(shp(seq), shp(seq), shp(CHUNK), shp(CHUNK), shp(CHUNK), shp(CHUNK)),
        compiler_params=pltpu.CompilerParams(vmem_limit_bytes=VMEM_LIMIT),
        name="rotary_tables",
    )(inv)


def _mixer_kernel(x_ref, cos_ref, sin_ref, sri_ref, sgi_ref, base_in_ref,
                  gmix_ref, wcat_ref, wgk_ref, bgk_ref, gnorm_ref, wbr_ref, wbg_ref, wout_ref,
                  gffn_ref, wrt_ref, brt_ref, bdtri_ref, ltri_ref,
                  xnew_ref, h2_ref, route_ref, base_out_ref, sro_ref, sgo_ref,
                  qb, qdb, kb, kkb, vb, gqb, gkkb, gvb, ga, o_ret, o_gla, st_ret, st_gla, base_sc,
                  *, n_chunks, carry, route):
    tm = n_chunks * CHUNK
    t_idx = pl.program_id(1)
    n_t = pl.num_programs(1)

    x = x_ref[...].reshape(tm, D_MODEL)
    h = x * lax.rsqrt(jnp.mean(x * x, axis=-1, keepdims=True) + NORM_EPS) * gmix_ref[...]
    hb = h.astype(BF16)

    def proj(c0, width):
        return _dot(hb, wcat_ref[:, c0:c0 + width])

    cos = cos_ref[...]
    sin = sin_ref[...]
    tc = cos.shape[0] // CHUNK
    cos3 = jnp.concatenate([cos, cos], axis=1).reshape(tc, CHUNK, HK)
    sin3 = jnp.concatenate([sin, sin], axis=1).reshape(tc, CHUNK, HK)
    lane_hk = lax.broadcasted_iota(jnp.int32, (tm, HK), 1)
    first_half = (lane_hk & (HEAD_K - 1)) < (HEAD_K // 2)

    def rotary(t):
        swapped = jnp.where(first_half, pltpu.roll(t, HK - HEAD_K // 2, 1), pltpu.roll(t, HEAD_K // 2, 1))
        t3 = t.reshape(n_chunks, CHUNK, HK)
        return (t3 * cos3 + swapped.reshape(n_chunks, CHUNK, HK) * sin3).reshape(tm, HK)

    logg_hk = _per_head_lane_const(LOG_G, (CHUNK, HK), HEAD_K)
    l_idx = lax.broadcasted_iota(jnp.int32, (CHUNK, HK), 0).astype(F32)
    qdec = jnp.exp((l_idx + 1.0) * logg_hk)
    kdec = jnp.exp((CHUNK - 1.0 - l_idx) * logg_hk)
    cdec = jnp.exp(float(CHUNK) * _per_head_lane_const(LOG_G, (1, HK), HEAD_K))
    r_idx = lax.broadcasted_iota(jnp.int32, (N_HEADS * CHUNK, CHUNK), 0)
    m_idx = lax.broadcasted_iota(jnp.int32, (N_HEADS * CHUNK, CHUNK), 1)
    logg_rows = jnp.full((N_HEADS * CHUNK, CHUNK), LOG_G[N_HEADS - 1], F32)
    for hh in range(N_HEADS - 2, -1, -1):
        logg_rows = jnp.where((r_idx >> 6) == hh, LOG_G[hh], logg_rows)
    dmat = jnp.exp(jnp.abs((r_idx & (CHUNK - 1)) - m_idx).astype(F32) * logg_rows)

    rq = rotary(proj(C_RQ, HK))
    rk = rotary(proj(C_RK, HK)) * (HEAD_K ** -0.5)
    qb[...] = rq.astype(BF16)
    qdb[...] = (rq.reshape(n_chunks, CHUNK, HK) * qdec).reshape(tm, HK).astype(BF16)
    kb[...] = rk.astype(BF16)
    kkb[...] = (rk.reshape(n_chunks, CHUNK, HK) * kdec).reshape(tm, HK).astype(BF16)
    vb[...] = proj(C_RV, HV).astype(BF16)

    gqb[...] = (proj(C_GQ, HK) * (HEAD_K ** -0.5)).astype(BF16)
    gvb[...] = proj(C_GV, HV).astype(BF16)
    gk = proj(C_GK, HK)
    glr = proj(C_LR, LANES)
    gl = _dot(glr.astype(BF16), wgk_ref[...]) + bgk_ref[...]
    log_a = (jnp.minimum(gl, 0.0) - jnp.log1p(jnp.exp(-jnp.abs(gl)))) / GATE_NORM
    la_hi = log_a.astype(BF16)
    la_lo = (log_a - la_hi.astype(F32)).astype(BF16)
    bdtri = bdtri_ref[...]
    bcum = _dot(bdtri, la_hi) + _dot(bdtri, la_lo)
    b3 = bcum.reshape(n_chunks, CHUNK, HK)
    bl3 = b3[:, CHUNK - 1:CHUNK, :]
    gkkb[...] = (gk.reshape(n_chunks, CHUNK, HK) * jnp.exp(bl3 - b3)).reshape(tm, HK).astype(BF16)
    ga[...] = jnp.broadcast_to(jnp.exp(bl3), (n_chunks, SUBLANES, HK))

    head_hk = _head_of_lane((CHUNK, HK), HEAD_K)
    head_st = _head_of_lane((HEAD_V, HK), HEAD_K)

    def stack_heads(a):
        zero = jnp.zeros_like(a)
        return jnp.concatenate([jnp.where(head_hk == hh, a, zero) for hh in range(N_HEADS)], axis=0)

    def head_rows_to_lanes(a):
        return jnp.concatenate([a[hh * CHUNK:(hh + 1) * CHUNK, :] for hh in range(N_HEADS)], axis=1)

    def diag_blocks(a):
        out = jnp.where(head_st == 0, a[0:HEAD_V, :], 0.0)
        for hh in range(1, N_HEADS):
            out = out + jnp.where(head_st == hh, a[hh * HEAD_V:(hh + 1) * HEAD_V, :], 0.0)
        return out

    if carry:
        @pl.when(t_idx == 0)
        def _():
            st_ret[...] = sri_ref[0].T
            st_gla[...] = sgi_ref[0].T

    def chunk_body(c, carry_val):
        rows = pl.ds(pl.multiple_of(c * CHUNK, CHUNK), CHUNK)
        if not carry:
            st_ret[...] = sri_ref[c].T
            st_gla[...] = sgi_ref[c].T
        q = qb[rows, :]
        v = vb[rows, :]
        sc = _dot_nt(stack_heads(q), kb[rows, :]) * dmat
        p = sc.astype(BF16)
        intra = jnp.concatenate(
            [_dot(p[hh * CHUNK:(hh + 1) * CHUNK, :], v[:, hh * HEAD_V:(hh + 1) * HEAD_V]) for hh in range(N_HEADS)],
            axis=1)
        s_prev = st_ret[...]
        inter = head_rows_to_lanes(_dot_nt(stack_heads(qdb[rows, :]), s_prev.astype(BF16)))
        o_ret[rows, :] = intra + inter
        st_ret[...] = s_prev * cdec + diag_blocks(_dot_tn(v, kkb[rows, :]))
        gv = gvb[rows, :]
        s_new = st_gla[...] * ga[c][0:1, :] + diag_blocks(_dot_tn(gv, gkkb[rows, :]))
        st_gla[...] = s_new
        o_gla[rows, :] = head_rows_to_lanes(_dot_nt(stack_heads(gqb[rows, :]), s_new.astype(BF16)))
        if not carry:
            sro_ref[c] = st_ret[...].T
            sgo_ref[c] = s_new.T
        return carry_val

    lax.fori_loop(0, n_chunks, chunk_body, 0)

    if carry:
        @pl.when(t_idx == n_t - 1)
        def _():
            sro_ref[0] = st_ret[...].T
            sgo_ref[0] = st_gla[...].T

    rg = proj(C_RG, HV)
    gg = proj(C_GG, HV)
    gnorm = gnorm_ref[...]
    orr = o_ret[...]
    ogg = o_gla[...]
    ret_parts, gla_parts = [], []
    for hh in range(N_HEADS):
        sl = slice(hh * HEAD_V, (hh + 1) * HEAD_V)
        oh = orr[:, sl]
        mu = jnp.mean(oh, axis=-1, keepdims=True)
        dev = oh - mu
        var = jnp.mean(dev * dev, axis=-1, keepdims=True)
        ret_parts.append(dev * lax.rsqrt(var + NORM_EPS))
        og = ogg[:, sl]
        gla_parts.append(og * lax.rsqrt(jnp.mean(og * og, axis=-1, keepdims=True) + NORM_EPS) * gnorm)
    o_r = jnp.concatenate(ret_parts, axis=1) * (rg * _sigmoid(rg))
    o_g = jnp.concatenate(gla_parts, axis=1) * (gg * _sigmoid(gg))
    merged = (_sigmoid(proj(C_ZR, D_MODEL)) * _dot(o_r.astype(BF16), wbr_ref[...])
              + _sigmoid(proj(C_ZG, D_MODEL)) * _dot(o_g.astype(BF16), wbg_ref[...]))
    xn = x + _dot(merged.astype(BF16), wout_ref[...])
    xnew_ref[...] = xn.reshape(xnew_ref.shape)

    if not route:
        h2_ref[...] = jnp.zeros(h2_ref.shape, F32)
        route_ref[...] = jnp.zeros(route_ref.shape, jnp.int32)
        base_out_ref[...] = base_in_ref[...]
        return

    h2 = xn * lax.rsqrt(jnp.mean(xn * xn, axis=-1, keepdims=True) + NORM_EPS) * gffn_ref[...]
    h2_ref[...] = h2.reshape(h2_ref.shape)
    logits = jnp.dot(h2, wrt_ref[...], preferred_element_type=F32, precision=lax.Precision.HIGHEST) + brt_ref[...]
    lt = logits.T
    row8 = lax.broadcasted_iota(jnp.int32, (SUBLANES, tm), 0)
    neg_inf = jnp.float32(-jnp.inf)
    glog = jnp.where(row8 < N_GROUPS, lt[0:SUBLANES, :], neg_inf)
    gmax = jnp.max(glog, axis=0, keepdims=True)
    grp = jnp.min(jnp.where(glog == gmax, row8, SUBLANES), axis=0, keepdims=True)
    p_grp = 1.0 / jnp.sum(jnp.exp(glog - gmax), axis=0, keepdims=True)
    le = jnp.zeros((SUBLANES, tm), F32)
    for g in range(N_GROUPS):
        le = jnp.where(grp == g, lt[SUBLANES * (g + 1):SUBLANES * (g + 2), :], le)
    m1 = jnp.max(le, axis=0, keepdims=True)
    i1 = jnp.min(jnp.where(le == m1, row8, SUBLANES), axis=0, keepdims=True)
    le2 = jnp.where(row8 == i1, neg_inf, le)
    m2 = jnp.max(le2, axis=0, keepdims=True)
    i2 = jnp.min(jnp.where(le2 == m2, row8, SUBLANES), axis=0, keepdims=True)
    e0 = grp * EXPERTS_PER_GROUP + i1
    e1 = grp * EXPERTS_PER_GROUP + i2
    t21 = jnp.exp(m2 - m1)
    w0 = p_grp / (1.0 + t21)
    w1 = p_grp * t21 / (1.0 + t21)

    @pl.when(jnp.logical_and(pl.program_id(0) == 0, t_idx == 0))
    def _():
        base_sc[...] = base_in_ref[...]

    erow = lax.broadcasted_iota(jnp.int32, (N_EXPERTS, tm), 0)
    hit0 = erow == e0
    hit1 = erow == e1
    onehot = jnp.where(jnp.logical_or(hit0, hit1), 1.0, 0.0).astype(BF16)
    cum = _dot_nt(onehot, ltri_ref[...])
    base = base_sc[...][:, 0:1]
    pos = base + cum - 1.0
    rank0 = jnp.sum(jnp.where(hit0, pos, 0.0), axis=0, keepdims=True)
    rank1 = jnp.sum(jnp.where(hit1, pos, 0.0), axis=0, keepdims=True)
    new_base = base + cum[:, tm - 1:tm]
    base_sc[...] = jnp.broadcast_to(new_base, (N_EXPERTS, LANES))
    base_out_ref[...] = jnp.broadcast_to(new_base, (N_EXPERTS, LANES))
    zero_row = jnp.zeros((1, tm), jnp.int32)
    rec = jnp.concatenate([e0, e1, rank0.astype(jnp.int32), rank1.astype(jnp.int32),
                           pltpu.bitcast(w0, jnp.int32), pltpu.bitcast(w1, jnp.int32), zero_row, zero_row], axis=0)
    route_ref[...] = rec.reshape(route_ref.shape)


def _const_spec(shape):
    nd = len(shape)
    return pl.BlockSpec(shape, lambda *_: (0,) * nd, pipeline_mode=pl.Buffered(1))


def _mixer_call(x, cos, sin, s_ret_in, s_gla_in, base_in, weights, consts, *, n_chunks, carry, route):
    tm = n_chunks * CHUNK
    gmix, wcat, wgk, bgk, gnorm, wbr, wbg, wout, gffn, wrt, brt = weights
    bdtri, ltri = consts
    if carry:
        n_rows, n_seq, _ = x.shape
        n_t = n_seq // tm
        grid = (n_rows, n_t)
        x_spec = pl.BlockSpec((1, tm, D_MODEL), lambda r, t: (r, t, 0))
        cos_spec = pl.BlockSpec((tm, LANES), lambda r, t: (t, 0))
        st_in_spec = pl.BlockSpec((1, HK, HEAD_V), lambda r, t: (0, 0, 0))
        st_out_spec = pl.BlockSpec((1, HK, HEAD_V), lambda r, t: (r, 0, 0))
        n_streams = n_rows
        tile_of = lambda r, t: r * n_t + t
    else:
        n_streams = x.shape[0]
        n_t = n_streams // n_chunks
        grid = (1, n_t)
        x_spec = pl.BlockSpec((n_chunks, CHUNK, D_MODEL), lambda r, t: (t, 0, 0))
        cos_spec = pl.BlockSpec((CHUNK, LANES), lambda r, t: (0, 0))
        st_in_spec = pl.BlockSpec((n_chunks, HK, HEAD_V), lambda r, t: (t, 0, 0))
        st_out_spec = st_in_spec
        tile_of = lambda r, t: t

    n_tiles = grid[0] * grid[1]
    pool_x = jax.ShapeDtypeStruct((n_tiles * tm, D_MODEL), F32)
    pool_r = jax.ShapeDtypeStruct((n_tiles, ROUTE_ROWS, tm), jnp.int32)
    out_shape = (pool_x, pool_x, pool_r,
                 jax.ShapeDtypeStruct((N_EXPERTS, LANES), F32),
                 jax.ShapeDtypeStruct((n_streams, HK, HEAD_V), F32),
                 jax.ShapeDtypeStruct((n_streams, HK, HEAD_V), F32))
    out_specs = (pl.BlockSpec((tm, D_MODEL), lambda r, t: (tile_of(r, t), 0)),
                 pl.BlockSpec((tm, D_MODEL), lambda r, t: (tile_of(r, t), 0)),
                 pl.BlockSpec((1, ROUTE_ROWS, tm), lambda r, t: (tile_of(r, t), 0, 0)),
                 pl.BlockSpec((N_EXPERTS, LANES), lambda r, t: (0, 0)),
                 st_out_spec, st_out_spec)
    in_specs = [x_spec, cos_spec, cos_spec, st_in_spec, st_in_spec,
                pl.BlockSpec((N_EXPERTS, LANES), lambda r, t: (0, 0)),
                _const_spec(gmix.shape), _const_spec(wcat.shape), _const_spec(wgk.shape), _const_spec(bgk.shape),
                _const_spec(gnorm.shape), _const_spec(wbr.shape), _const_spec(wbg.shape), _const_spec(wout.shape),
                _const_spec(gffn.shape), _const_spec(wrt.shape), _const_spec(brt.shape),
                _const_spec(bdtri.shape), _const_spec(ltri.shape)]
    args = [x, cos, sin, s_ret_in, s_gla_in, base_in, gmix, wcat, wgk, bgk, gnorm, wbr, wbg, wout, gffn, wrt, brt,
            bdtri, ltri]
    scratch = [pltpu.VMEM((tm, HK), BF16), pltpu.VMEM((tm, HK), BF16), pltpu.VMEM((tm, HK), BF16),
               pltpu.VMEM((tm, HK), BF16), pltpu.VMEM((tm, HV), BF16),
               pltpu.VMEM((tm, HK), BF16), pltpu.VMEM((tm, HK), BF16), pltpu.VMEM((tm, HV), BF16),
               pltpu.VMEM((n_chunks, SUBLANES, HK), F32),
               pltpu.VMEM((tm, HV), F32), pltpu.VMEM((tm, HV), F32),
               pltpu.VMEM((HEAD_V, HK), F32), pltpu.VMEM((HEAD_V, HK), F32),
               pltpu.VMEM((N_EXPERTS, LANES), F32)]

    return pl.pallas_call(
        functools.partial(_mixer_kernel, n_chunks=n_chunks, carry=carry, route=route),
        grid=grid, in_specs=in_specs, out_specs=out_specs, out_shape=out_shape, scratch_shapes=scratch,
        compiler_params=pltpu.CompilerParams(dimension_semantics=("arbitrary", "arbitrary"),
                                             vmem_limit_bytes=VMEM_LIMIT),
        name="mixer_carry" if carry else ("mixer_streams" if route else "mixer_meta"),
    )(*args)


def _plan_kernel(route_ref, counts_ref, dest_ref, meta_ref, *, n_blocks):
    nb_lanes = meta_ref.shape[1]
    counts = counts_ref[...]
    padded = jnp.ceil(counts / MOE_BLOCK) * MOE_BLOCK
    padded_l = jnp.concatenate([padded] * (nb_lanes // LANES), axis=1)
    er = lax.broadcasted_iota(jnp.int32, (N_EXPERTS, nb_lanes), 0)
    ln = lax.broadcasted_iota(jnp.int32, (N_EXPERTS, nb_lanes), 1)
    pad_end_l = jnp.sum(jnp.where(er <= ln, padded_l, 0.0), axis=0, keepdims=True)
    pad_start_l = jnp.sum(jnp.where(er < ln, padded_l, 0.0), axis=0, keepdims=True)
    pad_end_col = jnp.zeros((N_EXPERTS, nb_lanes), F32)
    for ex in range(N_EXPERTS):
        pad_end_col = pad_end_col + jnp.where(er >= ex, padded_l[ex:ex + 1, :], 0.0)

    rec = route_ref[...]

    def dest_of(e, rank):
        d = jnp.zeros(e.shape, F32)
        for ex in range(N_EXPERTS):
            d = jnp.where(e == ex, pad_start_l[0:1, ex:ex + 1], d)
        return (d + rank.astype(F32)).astype(jnp.int32)

    d0 = dest_of(rec[:, 0:1, :], rec[:, 2:3, :])
    d1 = dest_of(rec[:, 1:2, :], rec[:, 3:4, :])
    dest_ref[...] = jnp.concatenate([d0, d1] + [jnp.zeros_like(d0)] * (ROUTE_ROWS - 2), axis=1)

    blk_start = (ln * MOE_BLOCK).astype(F32)
    blk_e = jnp.sum(jnp.where(pad_end_col <= blk_start, 1.0, 0.0), axis=0, keepdims=True)
    blk_e = jnp.minimum(blk_e, N_EXPERTS - 1.0)
    last_blk = jnp.maximum(pad_end_l - MOE_BLOCK, 0.0)
    used = jnp.sum(padded_l, axis=0, keepdims=True) / MOE_BLOCK
    meta = jnp.concatenate([blk_e, last_blk, used] + [jnp.zeros((1, nb_lanes), F32)] * (SUBLANES - 3), axis=0)
    meta_ref[...] = meta.astype(jnp.int32)


def _plan_call(route, counts, n_blocks):
    nb_lanes = LANES * ((n_blocks + LANES - 1) // LANES)
    return pl.pallas_call(
        functools.partial(_plan_kernel, n_blocks=n_blocks),
        out_shape=(jax.ShapeDtypeStruct(route.shape, jnp.int32),
                   jax.ShapeDtypeStruct((SUBLANES, nb_lanes), jnp.int32)),
        compiler_params=pltpu.CompilerParams(vmem_limit_bytes=VMEM_LIMIT),
        name="moe_plan",
    )(route, counts)


def _dispatch_kernel(dest_ref, meta_ref, h2a_ref, h2b_ref, xs_ref, zero_buf, sem_z, sem, *, tm, tiles_a, n_blocks):
    t = pl.program_id(0)

    @pl.when(t == 0)
    def _():
        zero_buf[...] = jnp.zeros(zero_buf.shape, F32)

        def zero_copy(first_row):
            first = pl.multiple_of(first_row, MOE_BLOCK)
            return pltpu.make_async_copy(zero_buf, xs_ref.at[pl.ds(first, MOE_BLOCK), :], sem_z)

        used = meta_ref[2, 0]
        for e in range(N_EXPERTS):
            zero_copy(meta_ref[1, e]).start()
        lax.fori_loop(used, n_blocks, lambda b, c: (zero_copy(b * MOE_BLOCK).start(), c)[1], 0)
        for e in range(N_EXPERTS):
            zero_copy(meta_ref[1, e]).wait()
        lax.fori_loop(used, n_blocks, lambda b, c: (zero_copy(b * MOE_BLOCK).wait(), c)[1], 0)

    def scatter_rows(src_ref):
        def row_copy(slot, i):
            return pltpu.make_async_copy(src_ref.at[pl.ds(i, 1), :], xs_ref.at[pl.ds(dest_ref[0, slot, i], 1), :], sem)

        def start(i, c):
            row_copy(0, i).start()
            row_copy(1, i).start()
            return c

        def wait(i, c):
            row_copy(0, i).wait()
            row_copy(1, i).wait()
            return c

        lax.fori_loop(0, tm, start, 0, unroll=8)
        lax.fori_loop(0, tm, wait, 0, unroll=8)

    @pl.when(t < tiles_a)
    def _():
        scatter_rows(h2a_ref)

    @pl.when(t >= tiles_a)
    def _():
        scatter_rows(h2b_ref)


def _dispatch_call(dest, meta, h2a, h2b, n_blocks, tm):
    tiles_a, tiles_b = h2a.shape[0] // tm, h2b.shape[0] // tm
    return pl.pallas_call(
        functools.partial(_dispatch_kernel, tm=tm, tiles_a=tiles_a, n_blocks=n_blocks),
        grid=(tiles_a + tiles_b,),
        in_specs=[pl.BlockSpec((1, ROUTE_ROWS, tm), lambda t: (t, 0, 0), memory_space=pltpu.SMEM),
                  pl.BlockSpec(meta.shape, lambda t: (0, 0), memory_space=pltpu.SMEM),
                  pl.BlockSpec((tm, D_MODEL), lambda t: (jnp.minimum(t, tiles_a - 1), 0)),
                  pl.BlockSpec((tm, D_MODEL), lambda t: (jnp.maximum(t - tiles_a, 0), 0))],
        out_specs=pl.BlockSpec(memory_space=pl.ANY),
        out_shape=jax.ShapeDtypeStruct((n_blocks * MOE_BLOCK, D_MODEL), F32),
        scratch_shapes=[pltpu.VMEM((MOE_BLOCK, D_MODEL), F32), pltpu.SemaphoreType.DMA, pltpu.SemaphoreType.DMA],
        compiler_params=pltpu.CompilerParams(dimension_semantics=("arbitrary",), vmem_limit_bytes=VMEM_LIMIT),
        name="moe_dispatch",
    )(dest, meta, h2a, h2b)


def _experts_kernel(meta_ref, xs_ref, wg_ref, wu_ref, wd_ref, yb_ref):
    b = pl.program_id(0)
    used = meta_ref[2, 0]

    @pl.when(b < used)
    def _():
        xb = xs_ref[...].astype(BF16)
        gate = _dot(xb, wg_ref[0])
        up = _dot(xb, wu_ref[0])
        hid = (gate * _sigmoid(gate)) * up
        yb_ref[...] = _dot(hid.astype(BF16), wd_ref[0])

    @pl.when(b >= used)
    def _():
        yb_ref[...] = jnp.zeros(yb_ref.shape, F32)


def _experts_call(meta, xs, wg, wu, wd, n_blocks):
    grid_spec = pltpu.PrefetchScalarGridSpec(
        num_scalar_prefetch=1,
        grid=(n_blocks,),
        in_specs=[pl.BlockSpec((MOE_BLOCK, D_MODEL), lambda b, m: (b, 0)),
                  pl.BlockSpec((1, D_MODEL, D_EXPERT), lambda b, m: (m[0, b], 0, 0)),
                  pl.BlockSpec((1, D_MODEL, D_EXPERT), lambda b, m: (m[0, b], 0, 0)),
                  pl.BlockSpec((1, D_EXPERT, D_MODEL), lambda b, m: (m[0, b], 0, 0))],
        out_specs=pl.BlockSpec((MOE_BLOCK, D_MODEL), lambda b, m: (b, 0)),
    )
    return pl.pallas_call(
        _experts_kernel, grid_spec=grid_spec,
        out_shape=jax.ShapeDtypeStruct(xs.shape, F32),
        compiler_params=pltpu.CompilerParams(dimension_semantics=("arbitrary",), vmem_limit_bytes=VMEM_LIMIT),
        name="moe_experts",
    )(meta, xs, wg, wu, wd)


def _combine_kernel(dest_ref, route_ref, xa_ref, xb_ref, yb_ref, gfin_ref, out_a_ref, out_b_ref, y0, y1, sem,
                    *, tm, tiles_a):
    t = pl.program_id(0)

    def row_copy(slot, buf, i):
        return pltpu.make_async_copy(yb_ref.at[pl.ds(dest_ref[0, slot, i], 1), :], buf.at[pl.ds(i, 1), :], sem)

    def start(i, c):
        row_copy(0, y0, i).start()
        row_copy(1, y1, i).start()
        return c

    def wait(i, c):
        row_copy(0, y0, i).wait()
        row_copy(1, y1, i).wait()
        return c

    lax.fori_loop(0, tm, start, 0, unroll=8)
    lax.fori_loop(0, tm, wait, 0, unroll=8)

    rec = route_ref[0]
    w_rows = pltpu.bitcast(rec, F32)[4:6, :]
    w_cols = jnp.concatenate([w_rows, jnp.zeros((LANES - 2, tm), F32)], axis=0).T
    moe = y0[...] * w_cols[:, 0:1] + y1[...] * w_cols[:, 1:2]

    def finish(x_ref, out_ref):
        xf = x_ref[...] + moe
        out_ref[...] = xf * lax.rsqrt(jnp.mean(xf * xf, axis=-1, keepdims=True) + NORM_EPS) * gfin_ref[...]

    @pl.when(t < tiles_a)
    def _():
        finish(xa_ref, out_a_ref)

    @pl.when(t >= tiles_a)
    def _():
        finish(xb_ref, out_b_ref)


def _combine_call(dest, route, xa, xb, yb, gfin, tm):
    tiles_a, tiles_b = xa.shape[0] // tm, xb.shape[0] // tm
    spec_a = pl.BlockSpec((tm, D_MODEL), lambda t: (jnp.minimum(t, tiles_a - 1), 0))
    spec_b = pl.BlockSpec((tm, D_MODEL), lambda t: (jnp.maximum(t - tiles_a, 0), 0))
    return pl.pallas_call(
        functools.partial(_combine_kernel, tm=tm, tiles_a=tiles_a),
        grid=(tiles_a + tiles_b,),
        in_specs=[pl.BlockSpec((1, ROUTE_ROWS, tm), lambda t: (t, 0, 0), memory_space=pltpu.SMEM),
                  pl.BlockSpec((1, ROUTE_ROWS, tm), lambda t: (t, 0, 0)),
                  spec_a, spec_b,
                  pl.BlockSpec(memory_space=pl.ANY),
                  pl.BlockSpec((1, D_MODEL), lambda t: (0, 0))],
        out_specs=(spec_a, spec_b),
        out_shape=(jax.ShapeDtypeStruct(xa.shape, F32), jax.ShapeDtypeStruct(xb.shape, F32)),
        scratch_shapes=[pltpu.VMEM((tm, D_MODEL), F32), pltpu.VMEM((tm, D_MODEL), F32), pltpu.SemaphoreType.DMA],
        compiler_params=pltpu.CompilerParams(dimension_semantics=("arbitrary",), vmem_limit_bytes=VMEM_LIMIT),
        name="moe_combine",
    )(dest, route, xa, xb, yb, gfin)


def _prep_weights(g_mix, w_in, w_gla_gk2, b_gla_gk, g_gla_norm, w_br_ret, w_br_gla, w_out, g_ffn,
                  w_rg, b_rg, w_re, b_re):
    offs = np.cumsum((0, HK, HK, HV, HV, HK, HK, HV, HV, GLA_RANK, D_MODEL, D_MODEL))
    seg = lambda i: w_in[:, offs[i]:offs[i + 1]]
    lr = jnp.pad(seg(8), ((0, 0), (0, LANES - GLA_RANK)))
    wcat = jnp.concatenate([seg(0), seg(1), seg(2), seg(3), seg(4), seg(5), seg(6), seg(7), seg(9), seg(10), lr],
                           axis=1).astype(BF16)
    wgk = jnp.pad(w_gla_gk2, ((0, LANES - GLA_RANK), (0, 0))).astype(BF16)
    wrt = jnp.zeros((D_MODEL, ROUTER_LANES), F32)
    wrt = wrt.at[:, 0:N_GROUPS].set(w_rg).at[:, SUBLANES:SUBLANES + N_EXPERTS].set(w_re)
    brt = jnp.zeros((1, ROUTER_LANES), F32)
    brt = brt.at[0, 0:N_GROUPS].set(b_rg).at[0, SUBLANES:SUBLANES + N_EXPERTS].set(b_re)
    return (g_mix.reshape(1, D_MODEL), wcat, wgk, b_gla_gk.reshape(1, HK), g_gla_norm.reshape(1, HEAD_V),
            w_br_ret.astype(BF16), w_br_gla.astype(BF16), w_out.astype(BF16), g_ffn.reshape(1, D_MODEL), wrt, brt)


def _tri_consts(tm):
    r = np.arange(tm)
    low = (r[None, :] <= r[:, None])
    bd = low & ((r[None, :] // CHUNK) == (r[:, None] // CHUNK))
    return jnp.asarray(bd, BF16), jnp.asarray(low, BF16)


def kernel(x_prompt, x_sample, state_ret, state_gla, meta_tokens, g_mix, w_in, w_gla_gk2, b_gla_gk, g_gla_norm, w_br_ret, w_br_gla, w_out, g_ffn, w_router_group, b_router_group, w_router_expert, b_router_expert, w_exp_gate, w_exp_up, w_exp_down, g_final):
    n_b, seq, _ = x_prompt.shape
    n_s, dec_seq, _ = x_sample.shape
    depth = state_ret.shape[0]
    assert depth == 1 and dec_seq == CHUNK and seq % TILE_ROWS == 0 and n_s % TILE_CHUNKS == 0
    cos_p, sin_p, cos_m, sin_m, cos_s, sin_s = _rotary_tables(seq, PAST_LEN)
    weights = _prep_weights(g_mix[0], w_in[0], w_gla_gk2[0], b_gla_gk[0], g_gla_norm[0], w_br_ret[0], w_br_gla[0],
                            w_out[0], g_ffn[0], w_router_group[0], b_router_group[0], w_router_expert[0],
                            b_router_expert[0])
    consts = _tri_consts(TILE_ROWS)
    consts_meta = _tri_consts(CHUNK)
    zero_counts = jnp.zeros((N_EXPERTS, LANES), F32)
    zero_state = jnp.zeros((1, HK, HEAD_V), F32)

    x_meta = jnp.concatenate([jnp.zeros((CHUNK - N_META, D_MODEL), F32), meta_tokens.astype(F32)], axis=0)
    _, _, _, _, meta_ret, meta_gla = _mixer_call(
        x_meta.reshape(1, CHUNK, D_MODEL), cos_m, sin_m, zero_state, zero_state, zero_counts, weights, consts_meta,
        n_chunks=1, carry=False, route=False)

    xnew_p, h2_p, route_p, counts_p, ret_p, gla_p = _mixer_call(
        x_prompt, cos_p, sin_p, meta_ret, meta_gla, zero_counts, weights, consts,
        n_chunks=TILE_CHUNKS, carry=True, route=True)
    xnew_s, h2_s, route_s, counts, ret_s, gla_s = _mixer_call(
        x_sample, cos_s, sin_s, state_ret[0].reshape(n_s, HK, HEAD_V), state_gla[0].reshape(n_s, HK, HEAD_V),
        counts_p, weights, consts, n_chunks=TILE_CHUNKS, carry=False, route=True)

    route = jnp.concatenate([route_p, route_s], axis=0)
    n_pairs = 2 * route.shape[0] * TILE_ROWS
    n_blocks = -(-n_pairs // MOE_BLOCK) + N_EXPERTS
    dest, meta = _plan_call(route, counts, n_blocks)
    xs = _dispatch_call(dest, meta, h2_p, h2_s, n_blocks, TILE_ROWS)
    yb = _experts_call(meta, xs, w_exp_gate[0].astype(BF16), w_exp_up[0].astype(BF16), w_exp_down[0].astype(BF16),
                       n_blocks)
    y_p, y_s = _combine_call(dest, route, xnew_p, xnew_s, yb, g_final.reshape(1, D_MODEL), TILE_ROWS)

    st = lambda a, n: a.reshape(1, n, N_HEADS, HEAD_K, HEAD_V)
    return (y_p.reshape(n_b, seq, D_MODEL), y_s.reshape(n_s, dec_seq, D_MODEL),
            st(ret_p, n_b), st(gla_p, n_b), st(ret_s, n_s), st(gla_s, n_s))
```

```python
import functools
import math

import jax
import jax.numpy as jnp
import numpy as np
from jax import lax
from jax.experimental import pallas as pl
from jax.experimental.pallas import tpu as pltpu

D_MODEL = 1024
CHUNK = 64
PAST_LEN = 1024
N_META = 16
N_HEADS = 4
HEAD_K = 64
HEAD_V = 128
HK = N_HEADS * HEAD_K
HV = N_HEADS * HEAD_V
GLA_RANK = 16
GATE_NORM = 16.0
ROPE_BASE = 10000.0
N_GROUPS = 4
EXPERTS_PER_GROUP = 8
N_EXPERTS = N_GROUPS * EXPERTS_PER_GROUP
D_EXPERT = 512
NORM_EPS = 1e-6

LANES = 128
SUBLANES = 8
TILE_CHUNKS = 8
TILE_ROWS = TILE_CHUNKS * CHUNK
ROUTE_ROWS = 8
ROUTER_LANES = 128
UNIT = SUBLANES
MOE_BLOCK = 256
BLOCK_UNITS = MOE_BLOCK // UNIT
LOCAL_ROWS = 2 * TILE_ROWS + N_EXPERTS * UNIT
LOCAL_UNITS = LOCAL_ROWS // UNIT
VMEM_LIMIT = 56 * 1024 * 1024

C_RQ, C_RK, C_RV, C_RG = 0, 256, 512, 1024
C_GQ, C_GK, C_GV, C_GG = 1536, 1792, 2048, 2560
C_ZR, C_ZG, C_LR = 3072, 4096, 5120
W_CAT = 5248

F32 = jnp.float32
BF16 = jnp.bfloat16
LOG_G = tuple(math.log1p(-(2.0 ** (-5.0 - h))) for h in range(N_HEADS))


def _dot(a, b):
    return jnp.dot(a, b, preferred_element_type=F32)


def _dot_nt(a, b):
    return lax.dot_general(a, b, (((1,), (1,)), ((), ())), preferred_element_type=F32)


def _dot_tn(a, b):
    return lax.dot_general(a, b, (((0,), (0,)), ((), ())), preferred_element_type=F32)


def _sigmoid(x):
    return 1.0 / (1.0 + jnp.exp(-x))


def _head_of_lane(shape, width):
    return lax.broadcasted_iota(jnp.int32, shape, len(shape) - 1) >> int(math.log2(width))


def _per_head_lane_const(vals, shape, width):
    hd = _head_of_lane(shape, width)
    out = jnp.full(shape, vals[N_HEADS - 1], F32)
    for h in range(N_HEADS - 2, -1, -1):
        out = jnp.where(hd == h, vals[h], out)
    return out


def _tables_kernel(inv_ref, cp_ref, sp_ref, cm_ref, sm_ref, *, seq, past_len):
    inv = inv_ref[...]

    def fill(c_ref, s_ref, rows, row0, pos_of_row):
        r = lax.broadcasted_iota(jnp.int32, (rows, LANES), 0)
        ang = pos_of_row(r).astype(F32) * inv
        lane = lax.broadcasted_iota(jnp.int32, (rows, LANES), 1)
        first_half = (lane & (HEAD_K - 1)) < (HEAD_K // 2)
        c_ref[pl.ds(row0, rows), :] = jnp.cos(ang)
        s = jnp.sin(ang)
        s_ref[pl.ds(row0, rows), :] = jnp.where(first_half, -s, s)

    fill(cp_ref, sp_ref, seq, 0, lambda r: r)
    fill(cp_ref, sp_ref, TILE_ROWS, seq, lambda r: past_len + (r & (CHUNK - 1)))
    fill(cm_ref, sm_ref, CHUNK, 0, lambda r: r - CHUNK)


def _rotary_tables(seq, past_len):
    half = HEAD_K // 2
    inv = ROPE_BASE ** (-2.0 * jnp.arange(half, dtype=F32) / HEAD_K)
    inv = jnp.tile(inv, LANES // half).reshape(1, LANES)
    shp = lambda r: jax.ShapeDtypeStruct((r, LANES), F32)
    return pl.pallas_call(
        functools.partial(_tables_kernel, seq=seq, past_len=past_len),
        out_shape=(shp(seq + TILE_ROWS), shp(seq + TILE_ROWS), shp(CHUNK), shp(CHUNK)),
        compiler_params=pltpu.CompilerParams(vmem_limit_bytes=VMEM_LIMIT),
        name="rotary_tables",
    )(inv)


def _mixer_kernel(*refs, n_chunks, tiles_p, tiles_per_row, route):
    tm = n_chunks * CHUNK
    if tiles_p:
        xp_ref, refs = refs[0], refs[1:]
    (xs_ref, cos_ref, sin_ref, spi_r_ref, spi_g_ref, ssi_r_ref, ssi_g_ref,
     gmix_ref, wcat_ref, wgk_ref, bgk_ref, gnorm_ref, wbr_ref, wbg_ref, wout_ref,
     gffn_ref, wrt_ref, brt_ref, bdtri_ref, ltri_ref,
     xnew_ref, xloc_ref, route_ref, ku_ref, spo_r_ref, spo_g_ref, sso_r_ref, sso_g_ref,
     qb, qdb, kb, kkb, vb, gqb, gkkb, gvb, ga, o_ret, o_gla, st_ret, st_gla) = refs

    i = pl.program_id(0)
    if tiles_p:
        is_s = i >= tiles_p
        t_idx = jnp.minimum(i, tiles_p - 1) % tiles_per_row
        x = jnp.where(is_s, xs_ref[...].reshape(tm, D_MODEL), xp_ref[...].reshape(tm, D_MODEL))

        @pl.when(jnp.logical_and(jnp.logical_not(is_s), t_idx == 0))
        def _():
            st_ret[...] = spi_r_ref[0]
            st_gla[...] = spi_g_ref[0]
    else:
        is_s = None
        x = xs_ref[...].reshape(tm, D_MODEL)

    h = x * lax.rsqrt(jnp.mean(x * x, axis=-1, keepdims=True) + NORM_EPS) * gmix_ref[...]
    hb = h.astype(BF16)

    def proj(c0, width):
        return _dot(hb, wcat_ref[:, c0:c0 + width])

    cos3 = jnp.concatenate([cos_ref[...]] * 2, axis=1).reshape(n_chunks, CHUNK, HK)
    sin3 = jnp.concatenate([sin_ref[...]] * 2, axis=1).reshape(n_chunks, CHUNK, HK)
    lane_hk = lax.broadcasted_iota(jnp.int32, (tm, HK), 1)
    first_half = (lane_hk & (HEAD_K - 1)) < (HEAD_K // 2)

    def rotary(t):
        swapped = jnp.where(first_half, pltpu.roll(t, HK - HEAD_K // 2, 1), pltpu.roll(t, HEAD_K // 2, 1))
        t3 = t.reshape(n_chunks, CHUNK, HK)
        return (t3 * cos3 + swapped.reshape(n_chunks, CHUNK, HK) * sin3).reshape(tm, HK)

    logg_hk = _per_head_lane_const(LOG_G, (CHUNK, HK), HEAD_K)
    l_idx = lax.broadcasted_iota(jnp.int32, (CHUNK, HK), 0).astype(F32)
    qdec = jnp.exp((l_idx + 1.0) * logg_hk)
    kdec = jnp.exp((CHUNK - 1.0 - l_idx) * logg_hk)
    cdec = jnp.exp(float(CHUNK) * _per_head_lane_const(LOG_G, (1, HK), HEAD_K))
    r_idx = lax.broadcasted_iota(jnp.int32, (N_HEADS * CHUNK, CHUNK), 0)
    m_idx = lax.broadcasted_iota(jnp.int32, (N_HEADS * CHUNK, CHUNK), 1)
    logg_rows = jnp.full((N_HEADS * CHUNK, CHUNK), LOG_G[N_HEADS - 1], F32)
    for hh in range(N_HEADS - 2, -1, -1):
        logg_rows = jnp.where((r_idx >> 6) == hh, LOG_G[hh], logg_rows)
    dmat = jnp.exp(jnp.abs((r_idx & (CHUNK - 1)) - m_idx).astype(F32) * logg_rows)

    rq = rotary(proj(C_RQ, HK))
    rk = rotary(proj(C_RK, HK)) * (HEAD_K ** -0.5)
    qb[...] = rq.astype(BF16)
    qdb[...] = (rq.reshape(n_chunks, CHUNK, HK) * qdec).reshape(tm, HK).astype(BF16)
    kb[...] = rk.astype(BF16)
    kkb[...] = (rk.reshape(n_chunks, CHUNK, HK) * kdec).reshape(tm, HK).astype(BF16)
    vb[...] = proj(C_RV, HV).astype(BF16)

    gqb[...] = (proj(C_GQ, HK) * (HEAD_K ** -0.5)).astype(BF16)
    gvb[...] = proj(C_GV, HV).astype(BF16)
    gk = proj(C_GK, HK)
    glr = proj(C_LR, LANES)
    gl = _dot(glr.astype(BF16), wgk_ref[...]) + bgk_ref[...]
    log_a = (jnp.minimum(gl, 0.0) - jnp.log1p(jnp.exp(-jnp.abs(gl)))) / GATE_NORM
    la_hi = log_a.astype(BF16)
    la_lo = (log_a - la_hi.astype(F32)).astype(BF16)
    bdtri = bdtri_ref[...]
    bcum = _dot(bdtri, la_hi) + _dot(bdtri, la_lo)
    b3 = bcum.reshape(n_chunks, CHUNK, HK)
    bl3 = b3[:, CHUNK - 1:CHUNK, :]
    gkkb[...] = (gk.reshape(n_chunks, CHUNK, HK) * jnp.exp(bl3 - b3)).reshape(tm, HK).astype(BF16)
    ga[...] = jnp.broadcast_to(jnp.exp(bl3), (n_chunks, SUBLANES, HK))

    head_hk = _head_of_lane((CHUNK, HK), HEAD_K)
    head_st = _head_of_lane((HEAD_V, HK), HEAD_K)

    def stack_heads(a):
        zero = jnp.zeros_like(a)
        return jnp.concatenate([jnp.where(head_hk == hh, a, zero) for hh in range(N_HEADS)], axis=0)

    def head_rows_to_lanes(a):
        return jnp.concatenate([a[hh * CHUNK:(hh + 1) * CHUNK, :] for hh in range(N_HEADS)], axis=1)

    def diag_blocks(a):
        out = jnp.where(head_st == 0, a[0:HEAD_V, :], 0.0)
        for hh in range(1, N_HEADS):
            out = out + jnp.where(head_st == hh, a[hh * HEAD_V:(hh + 1) * HEAD_V, :], 0.0)
        return out

    for c in range(n_chunks):
        rows = slice(c * CHUNK, (c + 1) * CHUNK)
        if is_s is None:
            s_prev = ssi_r_ref[c]
            g_prev = ssi_g_ref[c]
        else:
            s_prev = jnp.where(is_s, ssi_r_ref[c], st_ret[...])
            g_prev = jnp.where(is_s, ssi_g_ref[c], st_gla[...])
        q = qb[rows, :]
        v = vb[rows, :]
        sc = _dot_nt(stack_heads(q), kb[rows, :]) * dmat
        p = sc.astype(BF16)
        intra = jnp.concatenate(
            [_dot(p[hh * CHUNK:(hh + 1) * CHUNK, :], v[:, hh * HEAD_V:(hh + 1) * HEAD_V]) for hh in range(N_HEADS)],
            axis=1)
        inter = head_rows_to_lanes(_dot_nt(stack_heads(qdb[rows, :]), s_prev.astype(BF16)))
        o_ret[rows, :] = intra + inter
        s_new = s_prev * cdec + diag_blocks(_dot_tn(v, kkb[rows, :]))
        st_ret[...] = s_new
        sso_r_ref[c] = s_new
        g_new = g_prev * ga[c][0:1, :] + diag_blocks(_dot_tn(gvb[rows, :], gkkb[rows, :]))
        st_gla[...] = g_new
        sso_g_ref[c] = g_new
        o_gla[rows, :] = head_rows_to_lanes(_dot_nt(stack_heads(gqb[rows, :]), g_new.astype(BF16)))

    rg = proj(C_RG, HV)
    gg = proj(C_GG, HV)
    gnorm = gnorm_ref[...]
    orr = o_ret[...]
    ogg = o_gla[...]
    ret_parts, gla_parts = [], []
    for hh in range(N_HEADS):
        sl = slice(hh * HEAD_V, (hh + 1) * HEAD_V)
        oh = orr[:, sl]
        mu = jnp.mean(oh, axis=-1, keepdims=True)
        dev = oh - mu
        var = jnp.mean(dev * dev, axis=-1, keepdims=True)
        ret_parts.append(dev * lax.rsqrt(var + NORM_EPS))
        og = ogg[:, sl]
        gla_parts.append(og * lax.rsqrt(jnp.mean(og * og, axis=-1, keepdims=True) + NORM_EPS) * gnorm)
    o_r = jnp.concatenate(ret_parts, axis=1) * (rg * _sigmoid(rg))
    o_g = jnp.concatenate(gla_parts, axis=1) * (gg * _sigmoid(gg))
    merged = (_sigmoid(proj(C_ZR, D_MODEL)) * _dot(o_r.astype(BF16), wbr_ref[...])
              + _sigmoid(proj(C_ZG, D_MODEL)) * _dot(o_g.astype(BF16), wbg_ref[...]))
    xn = x + _dot(merged.astype(BF16), wout_ref[...])
    xnew_ref[...] = xn

    if route:
        h2 = xn * lax.rsqrt(jnp.mean(xn * xn, axis=-1, keepdims=True) + NORM_EPS) * gffn_ref[...]
        h2_hi = h2.astype(BF16)
        h2_lo = (h2 - h2_hi.astype(F32)).astype(BF16)
        logits = (_dot(h2_hi, wrt_ref[0]) + _dot(h2_lo, wrt_ref[0]) + _dot(h2_hi, wrt_ref[1])) + brt_ref[...]
        lt = logits.T
        row8 = lax.broadcasted_iota(jnp.int32, (SUBLANES, tm), 0)
        neg_inf = jnp.float32(-jnp.inf)
        glog = jnp.where(row8 < N_GROUPS, lt[0:SUBLANES, :], neg_inf)
        gmax = jnp.max(glog, axis=0, keepdims=True)
        grp = jnp.min(jnp.where(glog == gmax, row8, SUBLANES), axis=0, keepdims=True)
        p_grp = 1.0 / jnp.sum(jnp.exp(glog - gmax), axis=0, keepdims=True)
        le = jnp.zeros((SUBLANES, tm), F32)
        for g in range(N_GROUPS):
            le = jnp.where(grp == g, lt[SUBLANES * (g + 1):SUBLANES * (g + 2), :], le)
        m1 = jnp.max(le, axis=0, keepdims=True)
        i1 = jnp.min(jnp.where(le == m1, row8, SUBLANES), axis=0, keepdims=True)
        le2 = jnp.where(row8 == i1, neg_inf, le)
        m2 = jnp.max(le2, axis=0, keepdims=True)
        i2 = jnp.min(jnp.where(le2 == m2, row8, SUBLANES), axis=0, keepdims=True)
        e0 = grp * EXPERTS_PER_GROUP + i1
        e1 = grp * EXPERTS_PER_GROUP + i2
        t21 = jnp.exp(m2 - m1)
        w0 = p_grp / (1.0 + t21)
        w1 = p_grp * t21 / (1.0 + t21)

        erow = lax.broadcasted_iota(jnp.int32, (N_EXPERTS, tm), 0)
        hit0 = erow == e0
        hit1 = erow == e1
        onehot = jnp.where(jnp.logical_or(hit0, hit1), 1.0, 0.0).astype(BF16)
        cum = _dot_nt(onehot, ltri_ref[...])
        n_run = cum[:, tm - 1:tm]
        n_pad = jnp.ceil(n_run / UNIT) * UNIT
        rank0 = jnp.sum(jnp.where(hit0, cum - 1.0, 0.0), axis=0, keepdims=True)
        rank1 = jnp.sum(jnp.where(hit1, cum - 1.0, 0.0), axis=0, keepdims=True)
        start0 = jnp.sum(jnp.where(erow < e0, n_pad, 0.0), axis=0, keepdims=True)
        start1 = jnp.sum(jnp.where(erow < e1, n_pad, 0.0), axis=0, keepdims=True)
        ld0 = (start0 + rank0).astype(jnp.int32)
        ld1 = (start1 + rank1).astype(jnp.int32)
        lrow = lax.broadcasted_iota(jnp.int32, (LOCAL_ROWS, tm), 0)
        perm = jnp.where(jnp.logical_or(lrow == ld0, lrow == ld1), 1.0, 0.0).astype(BF16)
        xloc_ref[...] = _dot(perm, h2_hi)
        lane_e = lax.broadcasted_iota(jnp.int32, (N_EXPERTS, LANES), 1)
        erow_l = lax.broadcasted_iota(jnp.int32, (N_EXPERTS, LANES), 0)
        units_row = jnp.sum(jnp.where(erow_l == lane_e, n_pad / UNIT, 0.0), axis=0, keepdims=True)
        ku_ref[...] = jnp.broadcast_to(units_row, (SUBLANES, LANES)).astype(jnp.int32).reshape(ku_ref.shape)
        zero_row = jnp.zeros((1, tm), jnp.int32)
        rec = jnp.concatenate([e0, e1, ld0, ld1, pltpu.bitcast(w0, jnp.int32), pltpu.bitcast(w1, jnp.int32),
                               zero_row, zero_row], axis=0)
        route_ref[...] = rec.reshape(route_ref.shape)
    else:
        xloc_ref[...] = jnp.zeros(xloc_ref.shape, F32)
        route_ref[...] = jnp.zeros(route_ref.shape, jnp.int32)
        ku_ref[...] = jnp.zeros(ku_ref.shape, jnp.int32)

    if tiles_p:
        @pl.when(jnp.logical_and(jnp.logical_not(is_s), t_idx == tiles_per_row - 1))
        def _():
            spo_r_ref[0] = st_ret[...]
            spo_g_ref[0] = st_gla[...]
    else:
        spo_r_ref[0] = st_ret[...]
        spo_g_ref[0] = st_gla[...]


def _const_spec(shape):
    nd = len(shape)
    return pl.BlockSpec(shape, lambda *_: (0,) * nd, pipeline_mode=pl.Buffered(1))


def _mixer_call(x_prompt, x_streams, cos, sin, st_prompt, st_streams, weights, consts, *, n_chunks, route):
    tm = n_chunks * CHUNK
    n_streams = x_streams.shape[0]
    tiles_s = n_streams // n_chunks
    if x_prompt is not None:
        n_rows, n_seq, _ = x_prompt.shape
        tpr = n_seq // tm
        tiles_p = n_rows * tpr
    else:
        n_rows, tpr, tiles_p = 1, 1, 0
    n_tiles = tiles_p + tiles_s
    p_idx = lambda i: jnp.minimum(i, tiles_p - 1)
    s_idx = lambda i: jnp.maximum(i - tiles_p, 0)

    st_blk = (1, HEAD_V, HK)
    in_specs, args = [], []
    if tiles_p:
        in_specs.append(pl.BlockSpec((1, tm, D_MODEL), lambda i: (p_idx(i) // tpr, p_idx(i) % tpr, 0)))
        args.append(x_prompt)
        cos_spec = pl.BlockSpec((tm, LANES), lambda i: (jnp.where(i < tiles_p, i % tpr, tpr), 0))
    else:
        cos_spec = pl.BlockSpec((tm, LANES), lambda i: (0, 0))
    once = pl.Buffered(1)
    stream_in = pl.BlockSpec((n_chunks, HEAD_V, HK), lambda i: (s_idx(i), 0, 0), pipeline_mode=once)
    in_specs += [pl.BlockSpec((n_chunks, CHUNK, D_MODEL), lambda i: (s_idx(i), 0, 0), pipeline_mode=once),
                 cos_spec, cos_spec, _const_spec(st_blk), _const_spec(st_blk), stream_in, stream_in]
    args += [x_streams, cos, sin, st_prompt[0], st_prompt[1], st_streams[0], st_streams[1]]
    in_specs += [_const_spec(w.shape) for w in weights] + [_const_spec(c.shape) for c in consts]
    args += list(weights) + list(consts)

    out_shape = (jax.ShapeDtypeStruct((n_tiles * tm, D_MODEL), F32),
                 jax.ShapeDtypeStruct((n_tiles * LOCAL_ROWS, D_MODEL), F32),
                 jax.ShapeDtypeStruct((n_tiles, ROUTE_ROWS, tm), jnp.int32),
                 jax.ShapeDtypeStruct((n_tiles, SUBLANES, LANES), jnp.int32),
                 jax.ShapeDtypeStruct((n_rows, HEAD_V, HK), F32), jax.ShapeDtypeStruct((n_rows, HEAD_V, HK), F32),
                 jax.ShapeDtypeStruct((n_streams, HEAD_V, HK), F32),
                 jax.ShapeDtypeStruct((n_streams, HEAD_V, HK), F32))
    row_spec = pl.BlockSpec(st_blk, lambda i: (p_idx(i) // tpr if tiles_p else 0, 0, 0))
    stream_spec = pl.BlockSpec((n_chunks, HEAD_V, HK), lambda i: (s_idx(i), 0, 0))
    out_specs = (pl.BlockSpec((tm, D_MODEL), lambda i: (i, 0)),
                 pl.BlockSpec((LOCAL_ROWS, D_MODEL), lambda i: (i, 0)),
                 pl.BlockSpec((1, ROUTE_ROWS, tm), lambda i: (i, 0, 0)),
                 pl.BlockSpec((1, SUBLANES, LANES), lambda i: (i, 0, 0)),
                 row_spec, row_spec, stream_spec, stream_spec)

    scratch = [pltpu.VMEM((tm, HK), BF16), pltpu.VMEM((tm, HK), BF16), pltpu.VMEM((tm, HK), BF16),
               pltpu.VMEM((tm, HK), BF16), pltpu.VMEM((tm, HV), BF16),
               pltpu.VMEM((tm, HK), BF16), pltpu.VMEM((tm, HK), BF16), pltpu.VMEM((tm, HV), BF16),
               pltpu.VMEM((n_chunks, SUBLANES, HK), F32),
               pltpu.VMEM((tm, HV), F32), pltpu.VMEM((tm, HV), F32),
               pltpu.VMEM((HEAD_V, HK), F32), pltpu.VMEM((HEAD_V, HK), F32)]

    return pl.pallas_call(
        functools.partial(_mixer_kernel, n_chunks=n_chunks, tiles_p=tiles_p, tiles_per_row=tpr, route=route),
        grid=(n_tiles,), in_specs=in_specs, out_specs=out_specs, out_shape=out_shape, scratch_shapes=scratch,
        compiler_params=pltpu.CompilerParams(dimension_semantics=("arbitrary",), vmem_limit_bytes=VMEM_LIMIT),
        name="mixer" if route else "mixer_meta",
    )(*args)


def _plan_kernel(ku_ref, src_ref, blk_ref, next_unit, *, n_tiles, n_blocks):
    shift = int(math.log2(BLOCK_UNITS))

    def init_tile(t, c):
        next_unit[t] = t * LOCAL_UNITS
        return c

    lax.fori_loop(0, n_tiles, init_tile, 0)

    def init_block(b, c):
        blk_ref[0, b] = N_EXPERTS - 1
        blk_ref[1, b] = 0
        blk_ref[2, b] = 0
        return c

    lax.fori_loop(0, n_blocks, init_block, 0)

    def per_expert(e, g0):
        def per_tile(t, g):
            k = ku_ref[t, e]
            base = next_unit[t]

            def per_unit(j, c):
                src_ref[g + j] = base + j
                return c

            lax.fori_loop(0, k, per_unit, 0)
            next_unit[t] = base + k
            return g + k

        g_end = lax.fori_loop(0, n_tiles, per_tile, g0)
        g_pad = ((g_end + (BLOCK_UNITS - 1)) >> shift) << shift

        def pad_unit(j, c):
            src_ref[j] = src_ref[g_pad - BLOCK_UNITS]
            return c

        lax.fori_loop(g_end, g_pad, pad_unit, 0)

        def set_block(b, c):
            blk_ref[0, b] = e
            blk_ref[1, b] = jnp.minimum(g_end - (b << shift), BLOCK_UNITS)
            return c

        lax.fori_loop(g0 >> shift, g_pad >> shift, set_block, 0)
        return g_pad

    g_total = lax.fori_loop(0, N_EXPERTS, per_expert, 0)

    def tail_unit(j, c):
        src_ref[j] = 0
        return c

    lax.fori_loop(g_total, n_blocks * BLOCK_UNITS, tail_unit, 0)
    blk_ref[2, 0] = g_total >> shift


def _plan_call(ku, n_blocks):
    n_tiles = ku.shape[0]
    smem = pl.BlockSpec(memory_space=pltpu.SMEM)
    return pl.pallas_call(
        functools.partial(_plan_kernel, n_tiles=n_tiles, n_blocks=n_blocks),
        in_specs=[smem], out_specs=(smem, smem),
        out_shape=(jax.ShapeDtypeStruct((n_blocks * BLOCK_UNITS,), jnp.int32),
                   jax.ShapeDtypeStruct((3, n_blocks), jnp.int32)),
        scratch_shapes=[pltpu.SMEM((n_tiles,), jnp.int32)],
        name="moe_plan",
    )(ku)


def _experts_kernel(src_ref, blk_ref, x_hbm, wg_ref, wu_ref, wd_ref, y_hbm,
                    xin, yout, wgb, wub, wdb, sem_in, sem_out):
    b = pl.program_id(0)
    used = blk_ref[2, 0]
    slot = b & 1

    def unit_rows(blk, u):
        return pl.ds(pl.multiple_of(src_ref[blk * BLOCK_UNITS + u] * UNIT, UNIT), UNIT)

    def in_copy(blk, u, s):
        return pltpu.make_async_copy(x_hbm.at[unit_rows(blk, u), :], xin.at[s, pl.ds(u * UNIT, UNIT), :],
                                     sem_in.at[s])

    def out_copy(blk, u, s):
        return pltpu.make_async_copy(yout.at[s, pl.ds(u * UNIT, UNIT), :], y_hbm.at[unit_rows(blk, u), :],
                                     sem_out.at[s])

    @pl.when(jnp.logical_and(b == 0, used > 0))
    def _():
        for u in range(BLOCK_UNITS):
            in_copy(0, u, 0).start()

    @pl.when(b < used)
    def _():
        expert = blk_ref[0, b]
        changed = jnp.logical_or(b == 0, expert != blk_ref[0, jnp.maximum(b - 1, 0)])

        @pl.when(changed)
        def _():
            wgb[...] = wg_ref[0].astype(BF16)
            wub[...] = wu_ref[0].astype(BF16)
            wdb[...] = wd_ref[0].astype(BF16)

        @pl.when(b + 1 < used)
        def _():
            for u in range(BLOCK_UNITS):
                in_copy(b + 1, u, 1 - slot).start()

        for u in range(BLOCK_UNITS):
            in_copy(b, u, slot).wait()

        xb = xin[slot].astype(BF16)
        gate = _dot(xb, wgb[...])
        up = _dot(xb, wub[...])
        hid = (gate * _sigmoid(gate)) * up
        yout[slot] = _dot(hid.astype(BF16), wdb[...])

        n_real = blk_ref[1, b]

        @pl.when(n_real == BLOCK_UNITS)
        def _():
            for u in range(BLOCK_UNITS):
                out_copy(b, u, slot).start()

        @pl.when(n_real < BLOCK_UNITS)
        def _():
            lax.fori_loop(0, n_real, lambda u, c: (out_copy(b, u, slot).start(), c)[1], 0)

        @pl.when(b > 0)
        def _():
            lax.fori_loop(0, blk_ref[1, b - 1], lambda u, c: (out_copy(b - 1, u, 1 - slot).wait(), c)[1], 0)

        @pl.when(b == used - 1)
        def _():
            lax.fori_loop(0, n_real, lambda u, c: (out_copy(b, u, slot).wait(), c)[1], 0)


def _experts_call(src, blk, xloc, wg, wu, wd, n_blocks):
    w_in_spec = pl.BlockSpec((1, D_MODEL, D_EXPERT), lambda b, s, m: (m[0, b], 0, 0))
    grid_spec = pltpu.PrefetchScalarGridSpec(
        num_scalar_prefetch=2,
        grid=(n_blocks,),
        in_specs=[pl.BlockSpec(memory_space=pl.ANY), w_in_spec, w_in_spec,
                  pl.BlockSpec((1, D_EXPERT, D_MODEL), lambda b, s, m: (m[0, b], 0, 0))],
        out_specs=pl.BlockSpec(memory_space=pl.ANY),
        scratch_shapes=[pltpu.VMEM((2, MOE_BLOCK, D_MODEL), F32), pltpu.VMEM((2, MOE_BLOCK, D_MODEL), F32),
                        pltpu.VMEM((D_MODEL, D_EXPERT), BF16), pltpu.VMEM((D_MODEL, D_EXPERT), BF16),
                        pltpu.VMEM((D_EXPERT, D_MODEL), BF16),
                        pltpu.SemaphoreType.DMA((2,)), pltpu.SemaphoreType.DMA((2,))],
    )
    return pl.pallas_call(
        _experts_kernel, grid_spec=grid_spec,
        out_shape=jax.ShapeDtypeStruct(xloc.shape, F32),
        input_output_aliases={2: 0},
        compiler_params=pltpu.CompilerParams(dimension_semantics=("arbitrary",), vmem_limit_bytes=VMEM_LIMIT),
        name="moe_experts",
    )(src, blk, xloc, wg, wu, wd)


def _combine_kernel(route_ref, x_ref, y_ref, gfin_ref, out_a_ref, out_b_ref, *, tm, tiles_a):
    t = pl.program_id(0)
    rec = route_ref[0]
    recf = jnp.concatenate([rec[2:4, :].astype(F32), pltpu.bitcast(rec, F32)[4:6, :],
                            jnp.zeros((LANES - 4, tm), F32)], axis=0)
    cols = recf.T
    lrow = lax.broadcasted_iota(jnp.int32, (tm, LOCAL_ROWS), 1).astype(F32)
    yb = y_ref[...].astype(BF16)
    y0 = _dot(jnp.where(lrow == cols[:, 0:1], 1.0, 0.0).astype(BF16), yb)
    y1 = _dot(jnp.where(lrow == cols[:, 1:2], 1.0, 0.0).astype(BF16), yb)
    xf = x_ref[...] + (y0 * cols[:, 2:3] + y1 * cols[:, 3:4])
    out = xf * lax.rsqrt(jnp.mean(xf * xf, axis=-1, keepdims=True) + NORM_EPS) * gfin_ref[...]

    @pl.when(t < tiles_a)
    def _():
        out_a_ref[...] = out

    @pl.when(t >= tiles_a)
    def _():
        out_b_ref[...] = out


def _combine_call(route, xnew, yloc, gfin, tm, tiles_a):
    n_tiles = route.shape[0]
    tiles_b = n_tiles - tiles_a
    return pl.pallas_call(
        functools.partial(_combine_kernel, tm=tm, tiles_a=tiles_a),
        grid=(n_tiles,),
        in_specs=[pl.BlockSpec((1, ROUTE_ROWS, tm), lambda t: (t, 0, 0)),
                  pl.BlockSpec((tm, D_MODEL), lambda t: (t, 0)),
                  pl.BlockSpec((LOCAL_ROWS, D_MODEL), lambda t: (t, 0)),
                  pl.BlockSpec((1, D_MODEL), lambda t: (0, 0))],
        out_specs=(pl.BlockSpec((tm, D_MODEL), lambda t: (jnp.minimum(t, tiles_a - 1), 0)),
                   pl.BlockSpec((tm, D_MODEL), lambda t: (jnp.maximum(t - tiles_a, 0), 0))),
        out_shape=(jax.ShapeDtypeStruct((tiles_a * tm, D_MODEL), F32),
                   jax.ShapeDtypeStruct((tiles_b * tm, D_MODEL), F32)),
        compiler_params=pltpu.CompilerParams(dimension_semantics=("arbitrary",), vmem_limit_bytes=VMEM_LIMIT),
        name="moe_combine",
    )(route, xnew, yloc, gfin)


def _prep_weights(g_mix, w_in, w_gla_gk2, b_gla_gk, g_gla_norm, w_br_ret, w_br_gla, w_out, g_ffn,
                  w_rg, b_rg, w_re, b_re):
    offs = np.cumsum((0, HK, HK, HV, HV, HK, HK, HV, HV, GLA_RANK, D_MODEL, D_MODEL))
    seg = lambda i: w_in[:, offs[i]:offs[i + 1]]
    lr = jnp.pad(seg(8), ((0, 0), (0, LANES - GLA_RANK)))
    wcat = jnp.concatenate([seg(0), seg(1), seg(2), seg(3), seg(4), seg(5), seg(6), seg(7), seg(9), seg(10), lr],
                           axis=1).astype(BF16)
    wgk = jnp.pad(w_gla_gk2, ((0, LANES - GLA_RANK), (0, 0))).astype(BF16)
    wrt = jnp.zeros((D_MODEL, ROUTER_LANES), F32)
    wrt = wrt.at[:, 0:N_GROUPS].set(w_rg).at[:, SUBLANES:SUBLANES + N_EXPERTS].set(w_re)
    brt = jnp.zeros((1, ROUTER_LANES), F32)
    brt = brt.at[0, 0:N_GROUPS].set(b_rg).at[0, SUBLANES:SUBLANES + N_EXPERTS].set(b_re)
    wrt_hi = wrt.astype(BF16)
    wrt = jnp.stack([wrt_hi, (wrt - wrt_hi.astype(F32)).astype(BF16)])
    return (g_mix.reshape(1, D_MODEL), wcat, wgk, b_gla_gk.reshape(1, HK), g_gla_norm.reshape(1, HEAD_V),
            w_br_ret.astype(BF16), w_br_gla.astype(BF16), w_out.astype(BF16), g_ffn.reshape(1, D_MODEL), wrt, brt)


def _tri_consts(tm):
    r = np.arange(tm)
    low = (r[None, :] <= r[:, None])
    bd = low & ((r[None, :] // CHUNK) == (r[:, None] // CHUNK))
    return jnp.asarray(bd, BF16), jnp.asarray(low, BF16)


def _state_to_kernel(s):
    return jnp.swapaxes(s.reshape(s.shape[0], HK, HEAD_V), 1, 2)


def _state_from_kernel(s):
    return jnp.swapaxes(s, 1, 2).reshape(1, s.shape[0], N_HEADS, HEAD_K, HEAD_V)


def kernel(x_prompt, x_sample, state_ret, state_gla, meta_tokens, g_mix, w_in, w_gla_gk2, b_gla_gk, g_gla_norm, w_br_ret, w_br_gla, w_out, g_ffn, w_router_group, b_router_group, w_router_expert, b_router_expert, w_exp_gate, w_exp_up, w_exp_down, g_final):
    n_b, seq, _ = x_prompt.shape
    n_s, dec_seq, _ = x_sample.shape
    depth = state_ret.shape[0]
    assert depth == 1 and dec_seq == CHUNK and seq % TILE_ROWS == 0 and n_s % TILE_CHUNKS == 0
    cos_t, sin_t, cos_m, sin_m = _rotary_tables(seq, PAST_LEN)
    weights = _prep_weights(g_mix[0], w_in[0], w_gla_gk2[0], b_gla_gk[0], g_gla_norm[0], w_br_ret[0], w_br_gla[0],
                            w_out[0], g_ffn[0], w_router_group[0], b_router_group[0], w_router_expert[0],
                            b_router_expert[0])
    zero_state = jnp.zeros((1, HEAD_V, HK), F32)

    x_meta = jnp.concatenate([jnp.zeros((CHUNK - N_META, D_MODEL), F32), meta_tokens.astype(F32)], axis=0)
    meta_out = _mixer_call(None, x_meta.reshape(1, CHUNK, D_MODEL), cos_m, sin_m, (zero_state, zero_state),
                           (zero_state, zero_state), weights, _tri_consts(CHUNK), n_chunks=1, route=False)
    meta_ret, meta_gla = meta_out[6], meta_out[7]

    xnew, xloc, route, ku, ret_p, gla_p, ret_s, gla_s = _mixer_call(
        x_prompt, x_sample, cos_t, sin_t, (meta_ret, meta_gla),
        (_state_to_kernel(state_ret[0]), _state_to_kernel(state_gla[0])), weights, _tri_consts(TILE_ROWS),
        n_chunks=TILE_CHUNKS, route=True)

    n_tiles = route.shape[0]
    n_blocks = n_tiles * LOCAL_UNITS // BLOCK_UNITS + N_EXPERTS
    src, blk = _plan_call(ku[:, 0, :N_EXPERTS], n_blocks)
    yloc = _experts_call(src, blk, xloc, w_exp_gate[0], w_exp_up[0], w_exp_down[0], n_blocks)
    y_p, y_s = _combine_call(route, xnew, yloc, g_final.reshape(1, D_MODEL), TILE_ROWS, n_b * seq // TILE_ROWS)

    return (y_p.reshape(n_b, seq, D_MODEL), y_s.reshape(n_s, dec_seq, D_MODEL),
            _state_from_kernel(ret_p), _state_from_kernel(gla_p), _state_from_kernel(ret_s), _state_from_kernel(gla_s))
```

```python
import functools
import math

import jax
import jax.numpy as jnp
import numpy as np
from jax import lax
from jax.experimental import pallas as pl
from jax.experimental.pallas import tpu as pltpu

D_MODEL = 1024
CHUNK = 64
PAST_LEN = 1024
N_META = 16
N_HEADS = 4
HEAD_K = 64
HEAD_V = 128
HK = N_HEADS * HEAD_K
HV = N_HEADS * HEAD_V
GLA_RANK = 16
GATE_NORM = 16.0
ROPE_BASE = 10000.0
N_GROUPS = 4
EXPERTS_PER_GROUP = 8
N_EXPERTS = N_GROUPS * EXPERTS_PER_GROUP
D_EXPERT = 512
NORM_EPS = 1e-6

LANES = 128
SUBLANES = 8
TILE_CHUNKS = 8
TILE_ROWS = TILE_CHUNKS * CHUNK
ROUTE_ROWS = 8
ROUTER_LANES = 128
UNIT = SUBLANES
MOE_BLOCK = 512
BLOCK_UNITS = MOE_BLOCK // UNIT
LOCAL_ROWS = 2 * TILE_ROWS + N_EXPERTS * UNIT
LOCAL_UNITS = LOCAL_ROWS // UNIT
PLAN_FAST = 8
VMEM_LIMIT = 56 * 1024 * 1024

C_RQ, C_RK, C_RV, C_RG = 0, 256, 512, 1024
C_GQ, C_GK, C_GV, C_GG = 1536, 1792, 2048, 2560
C_ZR, C_ZG, C_LR = 3072, 4096, 5120
W_CAT = 5248

F32 = jnp.float32
BF16 = jnp.bfloat16
LOG_G = tuple(math.log1p(-(2.0 ** (-5.0 - h))) for h in range(N_HEADS))


def _dot(a, b):
    return jnp.dot(a, b, preferred_element_type=F32)


def _dot_nt(a, b):
    return lax.dot_general(a, b, (((1,), (1,)), ((), ())), preferred_element_type=F32)


def _dot_tn(a, b):
    return lax.dot_general(a, b, (((0,), (0,)), ((), ())), preferred_element_type=F32)


def _sigmoid(x):
    return 1.0 / (1.0 + jnp.exp(-x))


def _head_of_lane(shape, width):
    return lax.broadcasted_iota(jnp.int32, shape, len(shape) - 1) >> int(math.log2(width))


def _per_head_lane_const(vals, shape, width):
    hd = _head_of_lane(shape, width)
    out = jnp.full(shape, vals[N_HEADS - 1], F32)
    for h in range(N_HEADS - 2, -1, -1):
        out = jnp.where(hd == h, vals[h], out)
    return out


def _tables_kernel(inv_ref, cp_ref, sp_ref, cm_ref, sm_ref, *, seq, past_len):
    inv = inv_ref[...]

    def fill(c_ref, s_ref, rows, row0, pos_of_row):
        r = lax.broadcasted_iota(jnp.int32, (rows, LANES), 0)
        ang = pos_of_row(r).astype(F32) * inv
        lane = lax.broadcasted_iota(jnp.int32, (rows, LANES), 1)
        first_half = (lane & (HEAD_K - 1)) < (HEAD_K // 2)
        c_ref[pl.ds(row0, rows), :] = jnp.cos(ang)
        s = jnp.sin(ang)
        s_ref[pl.ds(row0, rows), :] = jnp.where(first_half, -s, s)

    fill(cp_ref, sp_ref, seq, 0, lambda r: r)
    fill(cp_ref, sp_ref, TILE_ROWS, seq, lambda r: past_len + (r & (CHUNK - 1)))
    fill(cm_ref, sm_ref, CHUNK, 0, lambda r: r - CHUNK)


def _rotary_tables(seq, past_len):
    half = HEAD_K // 2
    inv = ROPE_BASE ** (-2.0 * jnp.arange(half, dtype=F32) / HEAD_K)
    inv = jnp.tile(inv, LANES // half).reshape(1, LANES)
    shp = lambda r: jax.ShapeDtypeStruct((r, LANES), F32)
    return pl.pallas_call(
        functools.partial(_tables_kernel, seq=seq, past_len=past_len),
        out_shape=(shp(seq + TILE_ROWS), shp(seq + TILE_ROWS), shp(CHUNK), shp(CHUNK)),
        compiler_params=pltpu.CompilerParams(vmem_limit_bytes=VMEM_LIMIT),
        name="rotary_tables",
    )(inv)


def _mixer_kernel(*refs, n_chunks, tiles_p, tiles_per_row, route):
    tm = n_chunks * CHUNK
    if tiles_p:
        xp_ref, refs = refs[0], refs[1:]
    (xs_ref, cos_ref, sin_ref, spi_r_ref, spi_g_ref, ssi_r_ref, ssi_g_ref,
     gmix_ref, wcat_ref, wgk_ref, bgk_ref, gnorm_ref, wbr_ref, wbg_ref, wout_ref,
     gffn_ref, wrt_ref, brt_ref, bdtri_ref, ltri_ref,
     xnew_ref, xloc_ref, route_ref, ku_ref, spo_r_ref, spo_g_ref, sso_r_ref, sso_g_ref,
     qb, qdb, kb, kkb, vb, gqb, gkkb, gvb, ga, o_ret, o_gla, st_ret, st_gla) = refs

    i = pl.program_id(0)
    if tiles_p:
        is_s = i >= tiles_p
        t_idx = jnp.minimum(i, tiles_p - 1) % tiles_per_row
        x = jnp.where(is_s, xs_ref[...].reshape(tm, D_MODEL), xp_ref[...].reshape(tm, D_MODEL))

        @pl.when(jnp.logical_and(jnp.logical_not(is_s), t_idx == 0))
        def _():
            st_ret[...] = spi_r_ref[0]
            st_gla[...] = spi_g_ref[0]
    else:
        is_s = None
        x = xs_ref[...].reshape(tm, D_MODEL)

    h = x * lax.rsqrt(jnp.mean(x * x, axis=-1, keepdims=True) + NORM_EPS) * gmix_ref[...]
    hb = h.astype(BF16)

    def proj(c0, width):
        return _dot(hb, wcat_ref[:, c0:c0 + width])

    cos3 = jnp.concatenate([cos_ref[...]] * 2, axis=1).reshape(n_chunks, CHUNK, HK)
    sin3 = jnp.concatenate([sin_ref[...]] * 2, axis=1).reshape(n_chunks, CHUNK, HK)
    lane_hk = lax.broadcasted_iota(jnp.int32, (tm, HK), 1)
    first_half = (lane_hk & (HEAD_K - 1)) < (HEAD_K // 2)

    def rotary(t):
        swapped = jnp.where(first_half, pltpu.roll(t, HK - HEAD_K // 2, 1), pltpu.roll(t, HEAD_K // 2, 1))
        t3 = t.reshape(n_chunks, CHUNK, HK)
        return (t3 * cos3 + swapped.reshape(n_chunks, CHUNK, HK) * sin3).reshape(tm, HK)

    logg_hk = _per_head_lane_const(LOG_G, (CHUNK, HK), HEAD_K)
    l_idx = lax.broadcasted_iota(jnp.int32, (CHUNK, HK), 0).astype(F32)
    qdec = jnp.exp((l_idx + 1.0) * logg_hk)
    kdec = jnp.exp((CHUNK - 1.0 - l_idx) * logg_hk)
    cdec = jnp.exp(float(CHUNK) * _per_head_lane_const(LOG_G, (1, HK), HEAD_K))
    r_idx = lax.broadcasted_iota(jnp.int32, (N_HEADS * CHUNK, CHUNK), 0)
    m_idx = lax.broadcasted_iota(jnp.int32, (N_HEADS * CHUNK, CHUNK), 1)
    logg_rows = jnp.full((N_HEADS * CHUNK, CHUNK), LOG_G[N_HEADS - 1], F32)
    for hh in range(N_HEADS - 2, -1, -1):
        logg_rows = jnp.where((r_idx >> 6) == hh, LOG_G[hh], logg_rows)
    dmat = jnp.exp(jnp.abs((r_idx & (CHUNK - 1)) - m_idx).astype(F32) * logg_rows)

    glr = proj(C_LR, LANES)
    rq = rotary(proj(C_RQ, HK))
    qb[...] = rq.astype(BF16)
    qdb[...] = (rq.reshape(n_chunks, CHUNK, HK) * qdec).reshape(tm, HK).astype(BF16)
    gl = _dot(glr.astype(BF16), wgk_ref[...]) + bgk_ref[...]
    rk = rotary(proj(C_RK, HK)) * (HEAD_K ** -0.5)
    kb[...] = rk.astype(BF16)
    kkb[...] = (rk.reshape(n_chunks, CHUNK, HK) * kdec).reshape(tm, HK).astype(BF16)
    log_a = (jnp.minimum(gl, 0.0) - jnp.log1p(jnp.exp(-jnp.abs(gl)))) / GATE_NORM
    la_hi = log_a.astype(BF16)
    la_lo = (log_a - la_hi.astype(F32)).astype(BF16)
    vb[...] = proj(C_RV, HV).astype(BF16)
    bdtri = bdtri_ref[...]
    bcum = _dot(bdtri, la_hi) + _dot(bdtri, la_lo)
    gqb[...] = (proj(C_GQ, HK) * (HEAD_K ** -0.5)).astype(BF16)
    gvb[...] = proj(C_GV, HV).astype(BF16)
    gk = proj(C_GK, HK)
    b3 = bcum.reshape(n_chunks, CHUNK, HK)
    bl3 = b3[:, CHUNK - 1:CHUNK, :]
    gkkb[...] = (gk.reshape(n_chunks, CHUNK, HK) * jnp.exp(bl3 - b3)).reshape(tm, HK).astype(BF16)
    ga[...] = jnp.broadcast_to(jnp.exp(bl3), (n_chunks, SUBLANES, HK))

    def stack_masked(a, width):
        head = _head_of_lane(a.shape, width)
        zero = jnp.zeros_like(a)
        return jnp.concatenate([jnp.where(head == hh, a, zero) for hh in range(N_HEADS)], axis=0)

    def heads_to_rows(a):
        return jnp.concatenate([a[:, hh * HEAD_V:(hh + 1) * HEAD_V] for hh in range(N_HEADS)], axis=0)

    def rows_to_heads(a):
        return jnp.concatenate([a[hh * CHUNK:(hh + 1) * CHUNK, :] for hh in range(N_HEADS)], axis=1)

    chunk_rows = [slice(c * CHUNK, (c + 1) * CHUNK) for c in range(n_chunks)]
    probs = [(_dot_nt(stack_masked(qb[r, :], HEAD_K), kb[r, :]) * dmat).astype(BF16) for r in chunk_rows]
    inc_ret = [_dot_tn(heads_to_rows(vb[r, :]), stack_masked(kkb[r, :], HEAD_K)) for r in chunk_rows]
    inc_gla = [_dot_tn(heads_to_rows(gvb[r, :]), stack_masked(gkkb[r, :], HEAD_K)) for r in chunk_rows]
    rg = proj(C_RG, HV)
    gg = proj(C_GG, HV)
    s_in, g_out = [], []
    s_cur, g_cur = st_ret[...], st_gla[...]
    for c in range(n_chunks):
        if is_s is None:
            s_cur, g_cur = ssi_r_ref[c], ssi_g_ref[c]
        else:
            s_cur = jnp.where(is_s, ssi_r_ref[c], s_cur)
            g_cur = jnp.where(is_s, ssi_g_ref[c], g_cur)
        s_in.append(s_cur.astype(BF16))
        s_cur = s_cur * cdec + inc_ret[c]
        g_cur = g_cur * ga[c][0:1, :] + inc_gla[c]
        g_out.append(g_cur.astype(BF16))
        sso_r_ref[c] = s_cur
        sso_g_ref[c] = g_cur
    st_ret[...] = s_cur
    st_gla[...] = g_cur
    for c, r in enumerate(chunk_rows):
        v = vb[r, :]
        intra = jnp.concatenate(
            [_dot(probs[c][hh * CHUNK:(hh + 1) * CHUNK, :], v[:, hh * HEAD_V:(hh + 1) * HEAD_V])
             for hh in range(N_HEADS)], axis=1)
        inter = rows_to_heads(_dot_nt(stack_masked(qdb[r, :], HEAD_K), s_in[c]))
        o_ret[r, :] = intra + inter
        o_gla[r, :] = rows_to_heads(_dot_nt(stack_masked(gqb[r, :], HEAD_K), g_out[c]))

    gnorm = gnorm_ref[...]
    orr = o_ret[...]
    ogg = o_gla[...]
    ret_parts, gla_parts = [], []
    for hh in range(N_HEADS):
        sl = slice(hh * HEAD_V, (hh + 1) * HEAD_V)
        oh = orr[:, sl]
        mu = jnp.mean(oh, axis=-1, keepdims=True)
        dev = oh - mu
        var = jnp.mean(dev * dev, axis=-1, keepdims=True)
        ret_parts.append(dev * lax.rsqrt(var + NORM_EPS))
        og = ogg[:, sl]
        gla_parts.append(og * lax.rsqrt(jnp.mean(og * og, axis=-1, keepdims=True) + NORM_EPS) * gnorm)
    o_r = jnp.concatenate(ret_parts, axis=1) * (rg * _sigmoid(rg))
    o_g = jnp.concatenate(gla_parts, axis=1) * (gg * _sigmoid(gg))
    merged = (_sigmoid(proj(C_ZR, D_MODEL)) * _dot(o_r.astype(BF16), wbr_ref[...])
              + _sigmoid(proj(C_ZG, D_MODEL)) * _dot(o_g.astype(BF16), wbg_ref[...]))
    xn = x + _dot(merged.astype(BF16), wout_ref[...])
    xnew_ref[...] = xn

    if route:
        h2 = xn * lax.rsqrt(jnp.mean(xn * xn, axis=-1, keepdims=True) + NORM_EPS) * gffn_ref[...]
        h2_hi = h2.astype(BF16)
        h2_lo = (h2 - h2_hi.astype(F32)).astype(BF16)
        hi_both = _dot(h2_hi, wrt_ref[...])
        logits = (hi_both[:, :ROUTER_LANES] + _dot(h2_lo, wrt_ref[:, :ROUTER_LANES])
                  + hi_both[:, ROUTER_LANES:]) + brt_ref[...]
        lt = logits.T
        row8 = lax.broadcasted_iota(jnp.int32, (SUBLANES, tm), 0)
        neg_inf = jnp.float32(-jnp.inf)
        glog = jnp.where(row8 < N_GROUPS, lt[0:SUBLANES, :], neg_inf)
        gmax = jnp.max(glog, axis=0, keepdims=True)
        grp = jnp.min(jnp.where(glog == gmax, row8, SUBLANES), axis=0, keepdims=True)
        p_grp = 1.0 / jnp.sum(jnp.exp(glog - gmax), axis=0, keepdims=True)
        le = jnp.zeros((SUBLANES, tm), F32)
        for g in range(N_GROUPS):
            le = jnp.where(grp == g, lt[SUBLANES * (g + 1):SUBLANES * (g + 2), :], le)
        m1 = jnp.max(le, axis=0, keepdims=True)
        i1 = jnp.min(jnp.where(le == m1, row8, SUBLANES), axis=0, keepdims=True)
        le2 = jnp.where(row8 == i1, neg_inf, le)
        m2 = jnp.max(le2, axis=0, keepdims=True)
        i2 = jnp.min(jnp.where(le2 == m2, row8, SUBLANES), axis=0, keepdims=True)
        e0 = grp * EXPERTS_PER_GROUP + i1
        e1 = grp * EXPERTS_PER_GROUP + i2
        t21 = jnp.exp(m2 - m1)
        w0 = p_grp / (1.0 + t21)
        w1 = p_grp * t21 / (1.0 + t21)

        erow = lax.broadcasted_iota(jnp.int32, (N_EXPERTS, tm), 0)
        hit0 = erow == e0
        hit1 = erow == e1
        onehot = jnp.where(jnp.logical_or(hit0, hit1), 1.0, 0.0).astype(BF16)
        cum = _dot_nt(onehot, ltri_ref[...])
        n_run = cum[:, tm - 1:tm]
        n_pad = jnp.ceil(n_run / UNIT) * UNIT
        rank0 = jnp.sum(jnp.where(hit0, cum - 1.0, 0.0), axis=0, keepdims=True)
        rank1 = jnp.sum(jnp.where(hit1, cum - 1.0, 0.0), axis=0, keepdims=True)
        start0 = jnp.sum(jnp.where(erow < e0, n_pad, 0.0), axis=0, keepdims=True)
        start1 = jnp.sum(jnp.where(erow < e1, n_pad, 0.0), axis=0, keepdims=True)
        ld0 = (start0 + rank0).astype(jnp.int32)
        ld1 = (start1 + rank1).astype(jnp.int32)
        lrow = lax.broadcasted_iota(jnp.int32, (LOCAL_ROWS, tm), 0)
        perm = jnp.where(jnp.logical_or(lrow == ld0, lrow == ld1), 1.0, 0.0).astype(BF16)
        xloc_ref[...] = _dot(perm, h2_hi)
        lane_e = lax.broadcasted_iota(jnp.int32, (N_EXPERTS, LANES), 1)
        erow_l = lax.broadcasted_iota(jnp.int32, (N_EXPERTS, LANES), 0)
        units_row = jnp.sum(jnp.where(erow_l == lane_e, n_pad / UNIT, 0.0), axis=0, keepdims=True)
        ku_ref[...] = jnp.broadcast_to(units_row, (SUBLANES, LANES)).astype(jnp.int32).reshape(ku_ref.shape)
        zero_row = jnp.zeros((1, tm), jnp.int32)
        rec = jnp.concatenate([e0, e1, ld0, ld1, pltpu.bitcast(w0, jnp.int32), pltpu.bitcast(w1, jnp.int32),
                               zero_row, zero_row], axis=0)
        route_ref[...] = rec.reshape(route_ref.shape)
    else:
        xloc_ref[...] = jnp.zeros(xloc_ref.shape, F32)
        route_ref[...] = jnp.zeros(route_ref.shape, jnp.int32)
        ku_ref[...] = jnp.zeros(ku_ref.shape, jnp.int32)

    if tiles_p:
        @pl.when(jnp.logical_and(jnp.logical_not(is_s), t_idx == tiles_per_row - 1))
        def _():
            spo_r_ref[0] = st_ret[...]
            spo_g_ref[0] = st_gla[...]
    else:
        spo_r_ref[0] = st_ret[...]
        spo_g_ref[0] = st_gla[...]


def _const_spec(shape):
    nd = len(shape)
    return pl.BlockSpec(shape, lambda *_: (0,) * nd, pipeline_mode=pl.Buffered(1))


def _mixer_call(x_prompt, x_streams, cos, sin, st_prompt, st_streams, weights, consts, *, n_chunks, route):
    tm = n_chunks * CHUNK
    n_streams = x_streams.shape[0]
    tiles_s = n_streams // n_chunks
    if x_prompt is not None:
        n_rows, n_seq, _ = x_prompt.shape
        tpr = n_seq // tm
        tiles_p = n_rows * tpr
    else:
        n_rows, tpr, tiles_p = 1, 1, 0
    n_tiles = tiles_p + tiles_s
    p_idx = lambda i: jnp.minimum(i, tiles_p - 1)
    s_idx = lambda i: jnp.maximum(i - tiles_p, 0)

    st_blk = (1, HEAD_V, HK)
    in_specs, args = [], []
    if tiles_p:
        in_specs.append(pl.BlockSpec((1, tm, D_MODEL), lambda i: (p_idx(i) // tpr, p_idx(i) % tpr, 0)))
        args.append(x_prompt)
        cos_spec = pl.BlockSpec((tm, LANES), lambda i: (jnp.where(i < tiles_p, i % tpr, tpr), 0))
    else:
        cos_spec = pl.BlockSpec((tm, LANES), lambda i: (0, 0))
    once = pl.Buffered(1)
    stream_in = pl.BlockSpec((n_chunks, HEAD_V, HK), lambda i: (s_idx(i), 0, 0), pipeline_mode=once)
    in_specs += [pl.BlockSpec((n_chunks, CHUNK, D_MODEL), lambda i: (s_idx(i), 0, 0), pipeline_mode=once),
                 cos_spec, cos_spec, _const_spec(st_blk), _const_spec(st_blk), stream_in, stream_in]
    args += [x_streams, cos, sin, st_prompt[0], st_prompt[1], st_streams[0], st_streams[1]]
    in_specs += [_const_spec(w.shape) for w in weights] + [_const_spec(c.shape) for c in consts]
    args += list(weights) + list(consts)

    out_shape = (jax.ShapeDtypeStruct((n_tiles * tm, D_MODEL), F32),
                 jax.ShapeDtypeStruct((n_tiles * LOCAL_ROWS, D_MODEL), F32),
                 jax.ShapeDtypeStruct((n_tiles, ROUTE_ROWS, tm), jnp.int32),
                 jax.ShapeDtypeStruct((n_tiles, SUBLANES, LANES), jnp.int32),
                 jax.ShapeDtypeStruct((n_rows, HEAD_V, HK), F32), jax.ShapeDtypeStruct((n_rows, HEAD_V, HK), F32),
                 jax.ShapeDtypeStruct((n_streams, HEAD_V, HK), F32),
                 jax.ShapeDtypeStruct((n_streams, HEAD_V, HK), F32))
    row_spec = pl.BlockSpec(st_blk, lambda i: (p_idx(i) // tpr if tiles_p else 0, 0, 0))
    stream_spec = pl.BlockSpec((n_chunks, HEAD_V, HK), lambda i: (s_idx(i), 0, 0))
    out_specs = (pl.BlockSpec((tm, D_MODEL), lambda i: (i, 0)),
                 pl.BlockSpec((LOCAL_ROWS, D_MODEL), lambda i: (i, 0)),
                 pl.BlockSpec((1, ROUTE_ROWS, tm), lambda i: (i, 0, 0)),
                 pl.BlockSpec((1, SUBLANES, LANES), lambda i: (i, 0, 0)),
                 row_spec, row_spec, stream_spec, stream_spec)

    scratch = [pltpu.VMEM((tm, HK), BF16), pltpu.VMEM((tm, HK), BF16), pltpu.VMEM((tm, HK), BF16),
               pltpu.VMEM((tm, HK), BF16), pltpu.VMEM((tm, HV), BF16),
               pltpu.VMEM((tm, HK), BF16), pltpu.VMEM((tm, HK), BF16), pltpu.VMEM((tm, HV), BF16),
               pltpu.VMEM((n_chunks, SUBLANES, HK), F32),
               pltpu.VMEM((tm, HV), F32), pltpu.VMEM((tm, HV), F32),
               pltpu.VMEM((HEAD_V, HK), F32), pltpu.VMEM((HEAD_V, HK), F32)]

    return pl.pallas_call(
        functools.partial(_mixer_kernel, n_chunks=n_chunks, tiles_p=tiles_p, tiles_per_row=tpr, route=route),
        grid=(n_tiles,), in_specs=in_specs, out_specs=out_specs, out_shape=out_shape, scratch_shapes=scratch,
        compiler_params=pltpu.CompilerParams(dimension_semantics=("arbitrary",), vmem_limit_bytes=VMEM_LIMIT),
        name="mixer" if route else "mixer_meta",
    )(*args)


def _plan_kernel(ku_ref, src_ref, blk_ref, next_unit, *, n_tiles, n_blocks):
    shift = int(math.log2(BLOCK_UNITS))

    def init_tile(t, c):
        next_unit[t] = t * LOCAL_UNITS
        return c

    lax.fori_loop(0, n_tiles, init_tile, 0)

    def init_block(b, c):
        blk_ref[0, b] = N_EXPERTS - 1
        blk_ref[1, b] = 0
        blk_ref[2, b] = 0
        return c

    lax.fori_loop(0, n_blocks, init_block, 0)

    def per_expert(e, g0):
        def per_tile(t, g):
            k = ku_ref[t, e]
            base = next_unit[t]

            def per_unit(j, c):
                src_ref[g + j] = base + j
                return c

            for j in range(PLAN_FAST):
                src_ref[g + j] = base + j

            @pl.when(k > PLAN_FAST)
            def _():
                lax.fori_loop(PLAN_FAST, k, per_unit, 0)

            next_unit[t] = base + k
            return g + k

        g_end = lax.fori_loop(0, n_tiles, per_tile, g0)
        g_pad = ((g_end + (BLOCK_UNITS - 1)) >> shift) << shift

        def pad_unit(j, c):
            src_ref[j] = src_ref[g_pad - BLOCK_UNITS]
            return c

        lax.fori_loop(g_end, g_pad, pad_unit, 0)

        def set_block(b, c):
            blk_ref[0, b] = e
            blk_ref[1, b] = jnp.minimum(g_end - (b << shift), BLOCK_UNITS)
            return c

        lax.fori_loop(g0 >> shift, g_pad >> shift, set_block, 0)
        return g_pad

    g_total = lax.fori_loop(0, N_EXPERTS, per_expert, 0)

    def tail_unit(j, c):
        src_ref[j] = 0
        return c

    lax.fori_loop(g_total, n_blocks * BLOCK_UNITS + PLAN_FAST, tail_unit, 0)
    blk_ref[2, 0] = g_total >> shift


def _plan_call(ku, n_blocks):
    n_tiles = ku.shape[0]
    smem = pl.BlockSpec(memory_space=pltpu.SMEM)
    return pl.pallas_call(
        functools.partial(_plan_kernel, n_tiles=n_tiles, n_blocks=n_blocks),
        in_specs=[smem], out_specs=(smem, smem),
        out_shape=(jax.ShapeDtypeStruct((n_blocks * BLOCK_UNITS + PLAN_FAST,), jnp.int32),
                   jax.ShapeDtypeStruct((3, n_blocks), jnp.int32)),
        scratch_shapes=[pltpu.SMEM((n_tiles,), jnp.int32)],
        name="moe_plan",
    )(ku)


def _experts_kernel(src_ref, blk_ref, x_hbm, wg_ref, wu_ref, wd_ref, y_hbm,
                    xin, yout, wgub, wdb, sem_in, sem_out):
    b = pl.program_id(0)
    used = blk_ref[2, 0]
    slot = b & 1

    def unit_rows(blk, u):
        return pl.ds(pl.multiple_of(src_ref[blk * BLOCK_UNITS + u] * UNIT, UNIT), UNIT)

    def in_copy(blk, u, s):
        return pltpu.make_async_copy(x_hbm.at[unit_rows(blk, u), :], xin.at[s, pl.ds(u * UNIT, UNIT), :],
                                     sem_in.at[s])

    def out_copy(blk, u, s):
        return pltpu.make_async_copy(yout.at[s, pl.ds(u * UNIT, UNIT), :], y_hbm.at[unit_rows(blk, u), :],
                                     sem_out.at[s])

    @pl.when(jnp.logical_and(b == 0, used > 0))
    def _():
        for u in range(BLOCK_UNITS):
            in_copy(0, u, 0).start()

    @pl.when(b < used)
    def _():
        expert = blk_ref[0, b]
        changed = jnp.logical_or(b == 0, expert != blk_ref[0, jnp.maximum(b - 1, 0)])

        @pl.when(changed)
        def _():
            wgub[:, :D_EXPERT] = wg_ref[0].astype(BF16)
            wgub[:, D_EXPERT:] = wu_ref[0].astype(BF16)
            wdb[...] = wd_ref[0].astype(BF16)

        @pl.when(b + 1 < used)
        def _():
            for u in range(BLOCK_UNITS):
                in_copy(b + 1, u, 1 - slot).start()

        for u in range(BLOCK_UNITS):
            in_copy(b, u, slot).wait()

        xb = xin[slot].astype(BF16)
        gate_up = _dot(xb, wgub[...])
        gate = gate_up[:, :D_EXPERT]
        hid = (gate * _sigmoid(gate)) * gate_up[:, D_EXPERT:]
        yout[slot] = _dot(hid.astype(BF16), wdb[...])

        n_real = blk_ref[1, b]

        @pl.when(n_real == BLOCK_UNITS)
        def _():
            for u in range(BLOCK_UNITS):
                out_copy(b, u, slot).start()

        @pl.when(n_real < BLOCK_UNITS)
        def _():
            lax.fori_loop(0, n_real, lambda u, c: (out_copy(b, u, slot).start(), c)[1], 0)

        @pl.when(b > 0)
        def _():
            lax.fori_loop(0, blk_ref[1, b - 1], lambda u, c: (out_copy(b - 1, u, 1 - slot).wait(), c)[1], 0)

        @pl.when(b == used - 1)
        def _():
            lax.fori_loop(0, n_real, lambda u, c: (out_copy(b, u, slot).wait(), c)[1], 0)


def _experts_call(src, blk, xloc, wg, wu, wd, n_blocks):
    w_in_spec = pl.BlockSpec((1, D_MODEL, D_EXPERT), lambda b, s, m: (m[0, b], 0, 0))
    grid_spec = pltpu.PrefetchScalarGridSpec(
        num_scalar_prefetch=2,
        grid=(n_blocks,),
        in_specs=[pl.BlockSpec(memory_space=pl.ANY), w_in_spec, w_in_spec,
                  pl.BlockSpec((1, D_EXPERT, D_MODEL), lambda b, s, m: (m[0, b], 0, 0))],
        out_specs=pl.BlockSpec(memory_space=pl.ANY),
        scratch_shapes=[pltpu.VMEM((2, MOE_BLOCK, D_MODEL), F32), pltpu.VMEM((2, MOE_BLOCK, D_MODEL), F32),
                        pltpu.VMEM((D_MODEL, 2 * D_EXPERT), BF16), pltpu.VMEM((D_EXPERT, D_MODEL), BF16),
                        pltpu.SemaphoreType.DMA((2,)), pltpu.SemaphoreType.DMA((2,))],
    )
    return pl.pallas_call(
        _experts_kernel, grid_spec=grid_spec,
        out_shape=jax.ShapeDtypeStruct(xloc.shape, F32),
        input_output_aliases={2: 0},
        compiler_params=pltpu.CompilerParams(dimension_semantics=("arbitrary",), vmem_limit_bytes=VMEM_LIMIT),
        name="moe_experts",
    )(src, blk, xloc, wg, wu, wd)


def _combine_kernel(route_ref, x_ref, y_ref, gfin_ref, out_a_ref, out_b_ref, *, tm, tiles_a):
    t = pl.program_id(0)
    rec = route_ref[0]
    recf = jnp.concatenate([rec[2:4, :].astype(F32), pltpu.bitcast(rec, F32)[4:6, :],
                            jnp.zeros((LANES - 4, tm), F32)], axis=0)
    cols = recf.T
    lrow = lax.broadcasted_iota(jnp.int32, (tm, LOCAL_ROWS), 1).astype(F32)
    select = jnp.where(lrow == cols[:, 0:1], cols[:, 2:3], 0.0) + jnp.where(lrow == cols[:, 1:2], cols[:, 3:4], 0.0)
    xf = x_ref[...] + _dot(select.astype(BF16), y_ref[...].astype(BF16))
    out = xf * lax.rsqrt(jnp.mean(xf * xf, axis=-1, keepdims=True) + NORM_EPS) * gfin_ref[...]

    @pl.when(t < tiles_a)
    def _():
        out_a_ref[...] = out

    @pl.when(t >= tiles_a)
    def _():
        out_b_ref[...] = out


def _combine_call(route, xnew, yloc, gfin, tm, tiles_a):
    n_tiles = route.shape[0]
    tiles_b = n_tiles - tiles_a
    return pl.pallas_call(
        functools.partial(_combine_kernel, tm=tm, tiles_a=tiles_a),
        grid=(n_tiles,),
        in_specs=[pl.BlockSpec((1, ROUTE_ROWS, tm), lambda t: (t, 0, 0)),
                  pl.BlockSpec((tm, D_MODEL), lambda t: (t, 0)),
                  pl.BlockSpec((LOCAL_ROWS, D_MODEL), lambda t: (t, 0)),
                  pl.BlockSpec((1, D_MODEL), lambda t: (0, 0))],
        out_specs=(pl.BlockSpec((tm, D_MODEL), lambda t: (jnp.minimum(t, tiles_a - 1), 0)),
                   pl.BlockSpec((tm, D_MODEL), lambda t: (jnp.maximum(t - tiles_a, 0), 0))),
        out_shape=(jax.ShapeDtypeStruct((tiles_a * tm, D_MODEL), F32),
                   jax.ShapeDtypeStruct((tiles_b * tm, D_MODEL), F32)),
        compiler_params=pltpu.CompilerParams(dimension_semantics=("arbitrary",), vmem_limit_bytes=VMEM_LIMIT),
        name="moe_combine",
    )(route, xnew, yloc, gfin)


def _prep_weights(g_mix, w_in, w_gla_gk2, b_gla_gk, g_gla_norm, w_br_ret, w_br_gla, w_out, g_ffn,
                  w_rg, b_rg, w_re, b_re):
    offs = np.cumsum((0, HK, HK, HV, HV, HK, HK, HV, HV, GLA_RANK, D_MODEL, D_MODEL))
    seg = lambda i: w_in[:, offs[i]:offs[i + 1]]
    lr = jnp.pad(seg(8), ((0, 0), (0, LANES - GLA_RANK)))
    wcat = jnp.concatenate([seg(0), seg(1), seg(2), seg(3), seg(4), seg(5), seg(6), seg(7), seg(9), seg(10), lr],
                           axis=1).astype(BF16)
    wgk = jnp.pad(w_gla_gk2, ((0, LANES - GLA_RANK), (0, 0))).astype(BF16)
    wrt = jnp.zeros((D_MODEL, ROUTER_LANES), F32)
    wrt = wrt.at[:, 0:N_GROUPS].set(w_rg).at[:, SUBLANES:SUBLANES + N_EXPERTS].set(w_re)
    brt = jnp.zeros((1, ROUTER_LANES), F32)
    brt = brt.at[0, 0:N_GROUPS].set(b_rg).at[0, SUBLANES:SUBLANES + N_EXPERTS].set(b_re)
    wrt_hi = wrt.astype(BF16)
    wrt = jnp.concatenate([wrt_hi, (wrt - wrt_hi.astype(F32)).astype(BF16)], axis=1)
    return (g_mix.reshape(1, D_MODEL), wcat, wgk, b_gla_gk.reshape(1, HK), g_gla_norm.reshape(1, HEAD_V),
            w_br_ret.astype(BF16), w_br_gla.astype(BF16), w_out.astype(BF16), g_ffn.reshape(1, D_MODEL), wrt, brt)


def _tri_consts(tm):
    r = np.arange(tm)
    low = (r[None, :] <= r[:, None])
    bd = low & ((r[None, :] // CHUNK) == (r[:, None] // CHUNK))
    return jnp.asarray(bd, BF16), jnp.asarray(low, BF16)


def _state_to_kernel(s):
    return jnp.swapaxes(s.reshape(s.shape[0], HK, HEAD_V), 1, 2)


def _state_from_kernel(s):
    return jnp.swapaxes(s, 1, 2).reshape(1, s.shape[0], N_HEADS, HEAD_K, HEAD_V)


def kernel(x_prompt, x_sample, state_ret, state_gla, meta_tokens, g_mix, w_in, w_gla_gk2, b_gla_gk, g_gla_norm, w_br_ret, w_br_gla, w_out, g_ffn, w_router_group, b_router_group, w_router_expert, b_router_expert, w_exp_gate, w_exp_up, w_exp_down, g_final):
    n_b, seq, _ = x_prompt.shape
    n_s, dec_seq, _ = x_sample.shape
    depth = state_ret.shape[0]
    assert depth == 1 and dec_seq == CHUNK and seq % TILE_ROWS == 0 and n_s % TILE_CHUNKS == 0
    cos_t, sin_t, cos_m, sin_m = _rotary_tables(seq, PAST_LEN)
    weights = _prep_weights(g_mix[0], w_in[0], w_gla_gk2[0], b_gla_gk[0], g_gla_norm[0], w_br_ret[0], w_br_gla[0],
                            w_out[0], g_ffn[0], w_router_group[0], b_router_group[0], w_router_expert[0],
                            b_router_expert[0])
    zero_state = jnp.zeros((1, HEAD_V, HK), F32)

    x_meta = jnp.concatenate([jnp.zeros((CHUNK - N_META, D_MODEL), F32), meta_tokens.astype(F32)], axis=0)
    meta_out = _mixer_call(None, x_meta.reshape(1, CHUNK, D_MODEL), cos_m, sin_m, (zero_state, zero_state),
                           (zero_state, zero_state), weights, _tri_consts(CHUNK), n_chunks=1, route=False)
    meta_ret, meta_gla = meta_out[6], meta_out[7]

    xnew, xloc, route, ku, ret_p, gla_p, ret_s, gla_s = _mixer_call(
        x_prompt, x_sample, cos_t, sin_t, (meta_ret, meta_gla),
        (_state_to_kernel(state_ret[0]), _state_to_kernel(state_gla[0])), weights, _tri_consts(TILE_ROWS),
        n_chunks=TILE_CHUNKS, route=True)

    n_tiles = route.shape[0]
    n_blocks = n_tiles * LOCAL_UNITS // BLOCK_UNITS + N_EXPERTS
    src, blk = _plan_call(ku[:, 0, :N_EXPERTS], n_blocks)
    yloc = _experts_call(src, blk, xloc, w_exp_gate[0], w_exp_up[0], w_exp_down[0], n_blocks)
    y_p, y_s = _combine_call(route, xnew, yloc, g_final.reshape(1, D_MODEL), TILE_ROWS, n_b * seq // TILE_ROWS)

    return (y_p.reshape(n_b, seq, D_MODEL), y_s.reshape(n_s, dec_seq, D_MODEL),
            _state_from_kernel(ret_p), _state_from_kernel(gla_p), _state_from_kernel(ret_s), _state_from_kernel(gla_s))
```

```python
import functools
import math

import jax
import jax.numpy as jnp
import numpy as np
from jax import lax
from jax.experimental import pallas as pl
from jax.experimental.pallas import tpu as pltpu

D_MODEL = 1024
CHUNK = 64
PAST_LEN = 1024
N_META = 16
N_HEADS = 4
HEAD_K = 64
HEAD_V = 128
HK = N_HEADS * HEAD_K
HV = N_HEADS * HEAD_V
GLA_RANK = 16
GATE_NORM = 16.0
ROPE_BASE = 10000.0
N_GROUPS = 4
EXPERTS_PER_GROUP = 8
N_EXPERTS = N_GROUPS * EXPERTS_PER_GROUP
D_EXPERT = 512
NORM_EPS = 1e-6

LANES = 128
SUBLANES = 8
TILE_CHUNKS = 8
TILE_ROWS = TILE_CHUNKS * CHUNK
ROUTE_ROWS = 8
ROUTER_LANES = 128
UNIT = SUBLANES
MOE_BLOCK = 512
BLOCK_UNITS = MOE_BLOCK // UNIT
LOCAL_ROWS = 2 * TILE_ROWS + N_EXPERTS * UNIT
LOCAL_UNITS = LOCAL_ROWS // UNIT
PLAN_FAST = 8
VMEM_LIMIT = 56 * 1024 * 1024

C_RQ, C_RK, C_RV, C_RG = 0, 256, 512, 1024
C_GQ, C_GK, C_GV, C_GG = 1536, 1792, 2048, 2560
C_ZR, C_ZG, C_LR = 3072, 4096, 5120
W_CAT = 5248

F32 = jnp.float32
BF16 = jnp.bfloat16
LOG_G = tuple(math.log1p(-(2.0 ** (-5.0 - h))) for h in range(N_HEADS))


def _dot(a, b):
    return jnp.dot(a, b, preferred_element_type=F32)


def _dot_nt(a, b):
    return lax.dot_general(a, b, (((1,), (1,)), ((), ())), preferred_element_type=F32)


def _dot_tn(a, b):
    return lax.dot_general(a, b, (((0,), (0,)), ((), ())), preferred_element_type=F32)


def _sigmoid(x):
    return 1.0 / (1.0 + jnp.exp(-x))


def _pack_halves(x):
    half = x.shape[1] // 2
    bits = pltpu.bitcast(x, jnp.int32)
    return lax.shift_right_logical(bits[:, :half], 16) | (bits[:, half:] & jnp.int32(-65536))


def _unpack_halves(w):
    lo = pltpu.bitcast(lax.shift_left(w, 16), F32)
    hi = pltpu.bitcast(w & jnp.int32(-65536), F32)
    return jnp.concatenate([lo, hi], axis=1).astype(BF16)


def _head_of_lane(shape, width):
    return lax.broadcasted_iota(jnp.int32, shape, len(shape) - 1) >> int(math.log2(width))


def _per_head_lane_const(vals, shape, width):
    hd = _head_of_lane(shape, width)
    out = jnp.full(shape, vals[N_HEADS - 1], F32)
    for h in range(N_HEADS - 2, -1, -1):
        out = jnp.where(hd == h, vals[h], out)
    return out


def _tables_kernel(inv_ref, cp_ref, sp_ref, cm_ref, sm_ref, *, seq, past_len):
    inv = inv_ref[...]

    def fill(c_ref, s_ref, rows, row0, pos_of_row):
        r = lax.broadcasted_iota(jnp.int32, (rows, LANES), 0)
        ang = pos_of_row(r).astype(F32) * inv
        lane = lax.broadcasted_iota(jnp.int32, (rows, LANES), 1)
        first_half = (lane & (HEAD_K - 1)) < (HEAD_K // 2)
        c_ref[pl.ds(row0, rows), :] = jnp.cos(ang)
        s = jnp.sin(ang)
        s_ref[pl.ds(row0, rows), :] = jnp.where(first_half, -s, s)

    fill(cp_ref, sp_ref, seq, 0, lambda r: r)
    fill(cp_ref, sp_ref, TILE_ROWS, seq, lambda r: past_len + (r & (CHUNK - 1)))
    fill(cm_ref, sm_ref, CHUNK, 0, lambda r: r - CHUNK)


def _rotary_tables(seq, past_len):
    half = HEAD_K // 2
    inv = ROPE_BASE ** (-2.0 * jnp.arange(half, dtype=F32) / HEAD_K)
    inv = jnp.tile(inv, LANES // half).reshape(1, LANES)
    shp = lambda r: jax.ShapeDtypeStruct((r, LANES), F32)
    return pl.pallas_call(
        functools.partial(_tables_kernel, seq=seq, past_len=past_len),
        out_shape=(shp(seq + TILE_ROWS), shp(seq + TILE_ROWS), shp(CHUNK), shp(CHUNK)),
        compiler_params=pltpu.CompilerParams(vmem_limit_bytes=VMEM_LIMIT),
        name="rotary_tables",
    )(inv)


def _mixer_kernel(*refs, n_chunks, tiles_p, tiles_per_row, route):
    tm = n_chunks * CHUNK
    if tiles_p:
        xp_ref, refs = refs[0], refs[1:]
    (xs_ref, cos_ref, sin_ref, spi_r_ref, spi_g_ref, ssi_r_ref, ssi_g_ref,
     gmix_ref, wcat_ref, wgk_ref, bgk_ref, gnorm_ref, wbr_ref, wbg_ref, wout_ref,
     gffn_ref, wrt_ref, brt_ref, bdtri_ref, ltri_ref,
     xnew_ref, xloc_ref, route_ref, ku_ref, spo_r_ref, spo_g_ref, sso_r_ref, sso_g_ref,
     qb, qdb, kb, kkb, vb, gqb, gkkb, gvb, ga, o_ret, o_gla, st_ret, st_gla) = refs

    i = pl.program_id(0)
    if tiles_p:
        is_s = i >= tiles_p
        t_idx = jnp.minimum(i, tiles_p - 1) % tiles_per_row
        x = jnp.where(is_s, xs_ref[...].reshape(tm, D_MODEL), xp_ref[...].reshape(tm, D_MODEL))

        @pl.when(jnp.logical_and(jnp.logical_not(is_s), t_idx == 0))
        def _():
            st_ret[...] = spi_r_ref[0]
            st_gla[...] = spi_g_ref[0]
    else:
        is_s = None
        x = xs_ref[...].reshape(tm, D_MODEL)

    h = x * lax.rsqrt(jnp.mean(x * x, axis=-1, keepdims=True) + NORM_EPS) * gmix_ref[...]
    hb = h.astype(BF16)

    def proj(c0, width):
        return _dot(hb, wcat_ref[:, c0:c0 + width])

    cos3 = jnp.concatenate([cos_ref[...]] * 2, axis=1).reshape(n_chunks, CHUNK, HK)
    sin3 = jnp.concatenate([sin_ref[...]] * 2, axis=1).reshape(n_chunks, CHUNK, HK)
    lane_hk = lax.broadcasted_iota(jnp.int32, (tm, HK), 1)
    first_half = (lane_hk & (HEAD_K - 1)) < (HEAD_K // 2)

    def rotary(t):
        swapped = jnp.where(first_half, pltpu.roll(t, HK - HEAD_K // 2, 1), pltpu.roll(t, HEAD_K // 2, 1))
        t3 = t.reshape(n_chunks, CHUNK, HK)
        return (t3 * cos3 + swapped.reshape(n_chunks, CHUNK, HK) * sin3).reshape(tm, HK)

    logg_hk = _per_head_lane_const(LOG_G, (CHUNK, HK), HEAD_K)
    l_idx = lax.broadcasted_iota(jnp.int32, (CHUNK, HK), 0).astype(F32)
    qdec = jnp.exp((l_idx + 1.0) * logg_hk)
    kdec = jnp.exp((CHUNK - 1.0 - l_idx) * logg_hk)
    cdec = jnp.exp(float(CHUNK) * _per_head_lane_const(LOG_G, (1, HK), HEAD_K))
    r_idx = lax.broadcasted_iota(jnp.int32, (N_HEADS * CHUNK, CHUNK), 0)
    m_idx = lax.broadcasted_iota(jnp.int32, (N_HEADS * CHUNK, CHUNK), 1)
    logg_rows = jnp.full((N_HEADS * CHUNK, CHUNK), LOG_G[N_HEADS - 1], F32)
    for hh in range(N_HEADS - 2, -1, -1):
        logg_rows = jnp.where((r_idx >> 6) == hh, LOG_G[hh], logg_rows)
    dmat = jnp.exp(jnp.abs((r_idx & (CHUNK - 1)) - m_idx).astype(F32) * logg_rows)

    glr = proj(C_LR, LANES)
    rq = rotary(proj(C_RQ, HK))
    qb[...] = rq.astype(BF16)
    qdb[...] = (rq.reshape(n_chunks, CHUNK, HK) * qdec).reshape(tm, HK).astype(BF16)
    gl = _dot(glr.astype(BF16), wgk_ref[...]) + bgk_ref[...]
    rk = rotary(proj(C_RK, HK)) * (HEAD_K ** -0.5)
    kb[...] = rk.astype(BF16)
    kkb[...] = (rk.reshape(n_chunks, CHUNK, HK) * kdec).reshape(tm, HK).astype(BF16)
    log_a = (jnp.minimum(gl, 0.0) - jnp.log1p(jnp.exp(-jnp.abs(gl)))) / GATE_NORM
    la_hi = log_a.astype(BF16)
    la_lo = (log_a - la_hi.astype(F32)).astype(BF16)
    vb[...] = proj(C_RV, HV).astype(BF16)
    bdtri = bdtri_ref[...]
    bcum = _dot(bdtri, la_hi) + _dot(bdtri, la_lo)
    gqb[...] = (proj(C_GQ, HK) * (HEAD_K ** -0.5)).astype(BF16)
    gvb[...] = proj(C_GV, HV).astype(BF16)
    gk = proj(C_GK, HK)
    b3 = bcum.reshape(n_chunks, CHUNK, HK)
    bl3 = b3[:, CHUNK - 1:CHUNK, :]
    gkkb[...] = (gk.reshape(n_chunks, CHUNK, HK) * jnp.exp(bl3 - b3)).reshape(tm, HK).astype(BF16)
    ga[...] = jnp.broadcast_to(jnp.exp(bl3), (n_chunks, SUBLANES, HK))

    def stack_masked(a, width):
        head = _head_of_lane(a.shape, width)
        zero = jnp.zeros_like(a)
        return jnp.concatenate([jnp.where(head == hh, a, zero) for hh in range(N_HEADS)], axis=0)

    def heads_to_rows(a):
        return jnp.concatenate([a[:, hh * HEAD_V:(hh + 1) * HEAD_V] for hh in range(N_HEADS)], axis=0)

    def rows_to_heads(a):
        return jnp.concatenate([a[hh * CHUNK:(hh + 1) * CHUNK, :] for hh in range(N_HEADS)], axis=1)

    chunk_rows = [slice(c * CHUNK, (c + 1) * CHUNK) for c in range(n_chunks)]
    probs = [(_dot_nt(stack_masked(qb[r, :], HEAD_K), kb[r, :]) * dmat).astype(BF16) for r in chunk_rows]
    inc_ret = [_dot_tn(heads_to_rows(vb[r, :]), stack_masked(kkb[r, :], HEAD_K)) for r in chunk_rows]
    inc_gla = [_dot_tn(heads_to_rows(gvb[r, :]), stack_masked(gkkb[r, :], HEAD_K)) for r in chunk_rows]
    rg = proj(C_RG, HV)
    gg = proj(C_GG, HV)
    s_in, g_out = [], []
    s_cur, g_cur = st_ret[...], st_gla[...]
    for c in range(n_chunks):
        if is_s is None:
            s_cur, g_cur = ssi_r_ref[c], ssi_g_ref[c]
        else:
            s_cur = jnp.where(is_s, ssi_r_ref[c], s_cur)
            g_cur = jnp.where(is_s, ssi_g_ref[c], g_cur)
        s_in.append(s_cur.astype(BF16))
        s_cur = s_cur * cdec + inc_ret[c]
        g_cur = g_cur * ga[c][0:1, :] + inc_gla[c]
        g_out.append(g_cur.astype(BF16))
        sso_r_ref[c] = s_cur
        sso_g_ref[c] = g_cur
    st_ret[...] = s_cur
    st_gla[...] = g_cur
    for c, r in enumerate(chunk_rows):
        v = vb[r, :]
        intra = jnp.concatenate(
            [_dot(probs[c][hh * CHUNK:(hh + 1) * CHUNK, :], v[:, hh * HEAD_V:(hh + 1) * HEAD_V])
             for hh in range(N_HEADS)], axis=1)
        inter = rows_to_heads(_dot_nt(stack_masked(qdb[r, :], HEAD_K), s_in[c]))
        o_ret[r, :] = intra + inter
        o_gla[r, :] = rows_to_heads(_dot_nt(stack_masked(gqb[r, :], HEAD_K), g_out[c]))

    gnorm = gnorm_ref[...]
    orr = o_ret[...]
    ogg = o_gla[...]
    ret_parts, gla_parts = [], []
    for hh in range(N_HEADS):
        sl = slice(hh * HEAD_V, (hh + 1) * HEAD_V)
        oh = orr[:, sl]
        mu = jnp.mean(oh, axis=-1, keepdims=True)
        dev = oh - mu
        var = jnp.mean(dev * dev, axis=-1, keepdims=True)
        ret_parts.append(dev * lax.rsqrt(var + NORM_EPS))
        og = ogg[:, sl]
        gla_parts.append(og * lax.rsqrt(jnp.mean(og * og, axis=-1, keepdims=True) + NORM_EPS) * gnorm)
    o_r = jnp.concatenate(ret_parts, axis=1) * (rg * _sigmoid(rg))
    o_g = jnp.concatenate(gla_parts, axis=1) * (gg * _sigmoid(gg))
    merged = (_sigmoid(proj(C_ZR, D_MODEL)) * _dot(o_r.astype(BF16), wbr_ref[...])
              + _sigmoid(proj(C_ZG, D_MODEL)) * _dot(o_g.astype(BF16), wbg_ref[...]))
    xn = x + _dot(merged.astype(BF16), wout_ref[...])
    xnew_ref[...] = xn

    if route:
        h2 = xn * lax.rsqrt(jnp.mean(xn * xn, axis=-1, keepdims=True) + NORM_EPS) * gffn_ref[...]
        h2_hi = h2.astype(BF16)
        h2_lo = (h2 - h2_hi.astype(F32)).astype(BF16)
        hi_both = _dot(h2_hi, wrt_ref[...])
        logits = (hi_both[:, :ROUTER_LANES] + _dot(h2_lo, wrt_ref[:, :ROUTER_LANES])
                  + hi_both[:, ROUTER_LANES:]) + brt_ref[...]
        lt = logits.T
        row8 = lax.broadcasted_iota(jnp.int32, (SUBLANES, tm), 0)
        neg_inf = jnp.float32(-jnp.inf)
        glog = jnp.where(row8 < N_GROUPS, lt[0:SUBLANES, :], neg_inf)
        gmax = jnp.max(glog, axis=0, keepdims=True)
        grp = jnp.min(jnp.where(glog == gmax, row8, SUBLANES), axis=0, keepdims=True)
        p_grp = 1.0 / jnp.sum(jnp.exp(glog - gmax), axis=0, keepdims=True)
        le = jnp.zeros((SUBLANES, tm), F32)
        for g in range(N_GROUPS):
            le = jnp.where(grp == g, lt[SUBLANES * (g + 1):SUBLANES * (g + 2), :], le)
        m1 = jnp.max(le, axis=0, keepdims=True)
        i1 = jnp.min(jnp.where(le == m1, row8, SUBLANES), axis=0, keepdims=True)
        le2 = jnp.where(row8 == i1, neg_inf, le)
        m2 = jnp.max(le2, axis=0, keepdims=True)
        i2 = jnp.min(jnp.where(le2 == m2, row8, SUBLANES), axis=0, keepdims=True)
        e0 = grp * EXPERTS_PER_GROUP + i1
        e1 = grp * EXPERTS_PER_GROUP + i2
        t21 = jnp.exp(m2 - m1)
        w0 = p_grp / (1.0 + t21)
        w1 = p_grp * t21 / (1.0 + t21)

        erow = lax.broadcasted_iota(jnp.int32, (N_EXPERTS, tm), 0)
        hit0 = erow == e0
        hit1 = erow == e1
        onehot = jnp.where(jnp.logical_or(hit0, hit1), 1.0, 0.0).astype(BF16)
        cum = _dot_nt(onehot, ltri_ref[...])
        n_run = cum[:, tm - 1:tm]
        n_pad = jnp.ceil(n_run / UNIT) * UNIT
        rank0 = jnp.sum(jnp.where(hit0, cum - 1.0, 0.0), axis=0, keepdims=True)
        rank1 = jnp.sum(jnp.where(hit1, cum - 1.0, 0.0), axis=0, keepdims=True)
        start0 = jnp.sum(jnp.where(erow < e0, n_pad, 0.0), axis=0, keepdims=True)
        start1 = jnp.sum(jnp.where(erow < e1, n_pad, 0.0), axis=0, keepdims=True)
        ld0 = (start0 + rank0).astype(jnp.int32)
        ld1 = (start1 + rank1).astype(jnp.int32)
        lrow = lax.broadcasted_iota(jnp.int32, (LOCAL_ROWS, tm), 0)
        perm = jnp.where(jnp.logical_or(lrow == ld0, lrow == ld1), 1.0, 0.0).astype(BF16)
        xloc_ref[...] = _pack_halves(_dot(perm, h2_hi))
        lane_e = lax.broadcasted_iota(jnp.int32, (N_EXPERTS, LANES), 1)
        erow_l = lax.broadcasted_iota(jnp.int32, (N_EXPERTS, LANES), 0)
        units_row = jnp.sum(jnp.where(erow_l == lane_e, n_pad / UNIT, 0.0), axis=0, keepdims=True)
        ku_ref[...] = jnp.broadcast_to(units_row, (SUBLANES, LANES)).astype(jnp.int32).reshape(ku_ref.shape)
        zero_row = jnp.zeros((1, tm), jnp.int32)
        rec = jnp.concatenate([e0, e1, ld0, ld1, pltpu.bitcast(w0, jnp.int32), pltpu.bitcast(w1, jnp.int32),
                               zero_row, zero_row], axis=0)
        route_ref[...] = rec.reshape(route_ref.shape)
    else:
        xloc_ref[...] = jnp.zeros(xloc_ref.shape, jnp.int32)
        route_ref[...] = jnp.zeros(route_ref.shape, jnp.int32)
        ku_ref[...] = jnp.zeros(ku_ref.shape, jnp.int32)

    if tiles_p:
        @pl.when(jnp.logical_and(jnp.logical_not(is_s), t_idx == tiles_per_row - 1))
        def _():
            spo_r_ref[0] = st_ret[...]
            spo_g_ref[0] = st_gla[...]
    else:
        spo_r_ref[0] = st_ret[...]
        spo_g_ref[0] = st_gla[...]


def _const_spec(shape):
    nd = len(shape)
    return pl.BlockSpec(shape, lambda *_: (0,) * nd, pipeline_mode=pl.Buffered(1))


def _mixer_call(x_prompt, x_streams, cos, sin, st_prompt, st_streams, weights, consts, *, n_chunks, route):
    tm = n_chunks * CHUNK
    n_streams = x_streams.shape[0]
    tiles_s = n_streams // n_chunks
    if x_prompt is not None:
        n_rows, n_seq, _ = x_prompt.shape
        tpr = n_seq // tm
        tiles_p = n_rows * tpr
    else:
        n_rows, tpr, tiles_p = 1, 1, 0
    n_tiles = tiles_p + tiles_s
    p_idx = lambda i: jnp.minimum(i, tiles_p - 1)
    s_idx = lambda i: jnp.maximum(i - tiles_p, 0)

    st_blk = (1, HEAD_V, HK)
    in_specs, args = [], []
    if tiles_p:
        in_specs.append(pl.BlockSpec((1, tm, D_MODEL), lambda i: (p_idx(i) // tpr, p_idx(i) % tpr, 0)))
        args.append(x_prompt)
        cos_spec = pl.BlockSpec((tm, LANES), lambda i: (jnp.where(i < tiles_p, i % tpr, tpr), 0))
    else:
        cos_spec = pl.BlockSpec((tm, LANES), lambda i: (0, 0))
    once = pl.Buffered(1)
    stream_in = pl.BlockSpec((n_chunks, HEAD_V, HK), lambda i: (s_idx(i), 0, 0), pipeline_mode=once)
    in_specs += [pl.BlockSpec((n_chunks, CHUNK, D_MODEL), lambda i: (s_idx(i), 0, 0), pipeline_mode=once),
                 cos_spec, cos_spec, _const_spec(st_blk), _const_spec(st_blk), stream_in, stream_in]
    args += [x_streams, cos, sin, st_prompt[0], st_prompt[1], st_streams[0], st_streams[1]]
    in_specs += [_const_spec(w.shape) for w in weights] + [_const_spec(c.shape) for c in consts]
    args += list(weights) + list(consts)

    out_shape = (jax.ShapeDtypeStruct((n_tiles * tm, D_MODEL), F32),
                 jax.ShapeDtypeStruct((n_tiles * LOCAL_ROWS, D_MODEL // 2), jnp.int32),
                 jax.ShapeDtypeStruct((n_tiles, ROUTE_ROWS, tm), jnp.int32),
                 jax.ShapeDtypeStruct((n_tiles, SUBLANES, LANES), jnp.int32),
                 jax.ShapeDtypeStruct((n_rows, HEAD_V, HK), F32), jax.ShapeDtypeStruct((n_rows, HEAD_V, HK), F32),
                 jax.ShapeDtypeStruct((n_streams, HEAD_V, HK), F32),
                 jax.ShapeDtypeStruct((n_streams, HEAD_V, HK), F32))
    row_spec = pl.BlockSpec(st_blk, lambda i: (p_idx(i) // tpr if tiles_p else 0, 0, 0))
    stream_spec = pl.BlockSpec((n_chunks, HEAD_V, HK), lambda i: (s_idx(i), 0, 0))
    out_specs = (pl.BlockSpec((tm, D_MODEL), lambda i: (i, 0)),
                 pl.BlockSpec((LOCAL_ROWS, D_MODEL // 2), lambda i: (i, 0)),
                 pl.BlockSpec((1, ROUTE_ROWS, tm), lambda i: (i, 0, 0)),
                 pl.BlockSpec((1, SUBLANES, LANES), lambda i: (i, 0, 0)),
                 row_spec, row_spec, stream_spec, stream_spec)

    scratch = [pltpu.VMEM((tm, HK), BF16), pltpu.VMEM((tm, HK), BF16), pltpu.VMEM((tm, HK), BF16),
               pltpu.VMEM((tm, HK), BF16), pltpu.VMEM((tm, HV), BF16),
               pltpu.VMEM((tm, HK), BF16), pltpu.VMEM((tm, HK), BF16), pltpu.VMEM((tm, HV), BF16),
               pltpu.VMEM((n_chunks, SUBLANES, HK), F32),
               pltpu.VMEM((tm, HV), F32), pltpu.VMEM((tm, HV), F32),
               pltpu.VMEM((HEAD_V, HK), F32), pltpu.VMEM((HEAD_V, HK), F32)]

    return pl.pallas_call(
        functools.partial(_mixer_kernel, n_chunks=n_chunks, tiles_p=tiles_p, tiles_per_row=tpr, route=route),
        grid=(n_tiles,), in_specs=in_specs, out_specs=out_specs, out_shape=out_shape, scratch_shapes=scratch,
        compiler_params=pltpu.CompilerParams(dimension_semantics=("arbitrary",), vmem_limit_bytes=VMEM_LIMIT),
        name="mixer" if route else "mixer_meta",
    )(*args)


def _plan_kernel(ku_ref, src_ref, blk_ref, next_unit, *, n_tiles, n_blocks):
    shift = int(math.log2(BLOCK_UNITS))

    def init_tile(t, c):
        next_unit[t] = t * LOCAL_UNITS
        return c

    lax.fori_loop(0, n_tiles, init_tile, 0)

    def init_block(b, c):
        blk_ref[0, b] = N_EXPERTS - 1
        blk_ref[1, b] = 0
        blk_ref[2, b] = 0
        return c

    lax.fori_loop(0, n_blocks, init_block, 0)

    def per_expert(e, g0):
        def per_tile(t, g):
            k = ku_ref[t, e]
            base = next_unit[t]

            def per_unit(j, c):
                src_ref[g + j] = base + j
                return c

            for j in range(PLAN_FAST):
                src_ref[g + j] = base + j

            @pl.when(k > PLAN_FAST)
            def _():
                lax.fori_loop(PLAN_FAST, k, per_unit, 0)

            next_unit[t] = base + k
            return g + k

        g_end = lax.fori_loop(0, n_tiles, per_tile, g0)
        g_pad = ((g_end + (BLOCK_UNITS - 1)) >> shift) << shift

        def pad_unit(j, c):
            src_ref[j] = src_ref[g_pad - BLOCK_UNITS]
            return c

        lax.fori_loop(g_end, g_pad, pad_unit, 0)

        def set_block(b, c):
            blk_ref[0, b] = e
            blk_ref[1, b] = jnp.minimum(g_end - (b << shift), BLOCK_UNITS)
            return c

        lax.fori_loop(g0 >> shift, g_pad >> shift, set_block, 0)
        return g_pad

    g_total = lax.fori_loop(0, N_EXPERTS, per_expert, 0)

    def tail_unit(j, c):
        src_ref[j] = 0
        return c

    lax.fori_loop(g_total, n_blocks * BLOCK_UNITS + PLAN_FAST, tail_unit, 0)
    blk_ref[2, 0] = g_total >> shift


def _plan_call(ku, n_blocks):
    n_tiles = ku.shape[0]
    smem = pl.BlockSpec(memory_space=pltpu.SMEM)
    return pl.pallas_call(
        functools.partial(_plan_kernel, n_tiles=n_tiles, n_blocks=n_blocks),
        in_specs=[smem], out_specs=(smem, smem),
        out_shape=(jax.ShapeDtypeStruct((n_blocks * BLOCK_UNITS + PLAN_FAST,), jnp.int32),
                   jax.ShapeDtypeStruct((3, n_blocks), jnp.int32)),
        scratch_shapes=[pltpu.SMEM((n_tiles,), jnp.int32)],
        name="moe_plan",
    )(ku)


def _experts_kernel(src_ref, blk_ref, x_hbm, wg_ref, wu_ref, wd_ref, y_hbm,
                    xin, yout, wgub, wdb, sem_in, sem_out):
    b = pl.program_id(0)
    used = blk_ref[2, 0]
    slot = b & 1

    def unit_rows(blk, u):
        return pl.ds(pl.multiple_of(src_ref[blk * BLOCK_UNITS + u] * UNIT, UNIT), UNIT)

    def in_copy(blk, u, s):
        return pltpu.make_async_copy(x_hbm.at[unit_rows(blk, u), :], xin.at[s, pl.ds(u * UNIT, UNIT), :],
                                     sem_in.at[s])

    def out_copy(blk, u, s):
        return pltpu.make_async_copy(yout.at[s, pl.ds(u * UNIT, UNIT), :], y_hbm.at[unit_rows(blk, u), :],
                                     sem_out.at[s])

    @pl.when(jnp.logical_and(b == 0, used > 0))
    def _():
        for u in range(BLOCK_UNITS):
            in_copy(0, u, 0).start()

    @pl.when(b < used)
    def _():
        expert = blk_ref[0, b]
        changed = jnp.logical_or(b == 0, expert != blk_ref[0, jnp.maximum(b - 1, 0)])

        @pl.when(changed)
        def _():
            wgub[:, :D_EXPERT] = wg_ref[0].astype(BF16)
            wgub[:, D_EXPERT:] = wu_ref[0].astype(BF16)
            wdb[...] = wd_ref[0].astype(BF16)

        @pl.when(b + 1 < used)
        def _():
            for u in range(BLOCK_UNITS):
                in_copy(b + 1, u, 1 - slot).start()

        for u in range(BLOCK_UNITS):
            in_copy(b, u, slot).wait()

        xb = _unpack_halves(xin[slot])
        gate_up = _dot(xb, wgub[...])
        gate = gate_up[:, :D_EXPERT]
        hid = (gate * _sigmoid(gate)) * gate_up[:, D_EXPERT:]
        yout[slot] = _pack_halves(_dot(hid.astype(BF16), wdb[...]).astype(BF16).astype(F32))

        n_real = blk_ref[1, b]

        @pl.when(n_real == BLOCK_UNITS)
        def _():
            for u in range(BLOCK_UNITS):
                out_copy(b, u, slot).start()

        @pl.when(n_real < BLOCK_UNITS)
        def _():
            lax.fori_loop(0, n_real, lambda u, c: (out_copy(b, u, slot).start(), c)[1], 0)

        @pl.when(b > 0)
        def _():
            lax.fori_loop(0, blk_ref[1, b - 1], lambda u, c: (out_copy(b - 1, u, 1 - slot).wait(), c)[1], 0)

        @pl.when(b == used - 1)
        def _():
            lax.fori_loop(0, n_real, lambda u, c: (out_copy(b, u, slot).wait(), c)[1], 0)


def _experts_call(src, blk, xloc, wg, wu, wd, n_blocks):
    w_in_spec = pl.BlockSpec((1, D_MODEL, D_EXPERT), lambda b, s, m: (m[0, b], 0, 0))
    grid_spec = pltpu.PrefetchScalarGridSpec(
        num_scalar_prefetch=2,
        grid=(n_blocks,),
        in_specs=[pl.BlockSpec(memory_space=pl.ANY), w_in_spec, w_in_spec,
                  pl.BlockSpec((1, D_EXPERT, D_MODEL), lambda b, s, m: (m[0, b], 0, 0))],
        out_specs=pl.BlockSpec(memory_space=pl.ANY),
        scratch_shapes=[pltpu.VMEM((2, MOE_BLOCK, D_MODEL // 2), jnp.int32),
                        pltpu.VMEM((2, MOE_BLOCK, D_MODEL // 2), jnp.int32),
                        pltpu.VMEM((D_MODEL, 2 * D_EXPERT), BF16), pltpu.VMEM((D_EXPERT, D_MODEL), BF16),
                        pltpu.SemaphoreType.DMA((2,)), pltpu.SemaphoreType.DMA((2,))],
    )
    return pl.pallas_call(
        _experts_kernel, grid_spec=grid_spec,
        out_shape=jax.ShapeDtypeStruct(xloc.shape, jnp.int32),
        input_output_aliases={2: 0},
        compiler_params=pltpu.CompilerParams(dimension_semantics=("arbitrary",), vmem_limit_bytes=VMEM_LIMIT),
        name="moe_experts",
    )(src, blk, xloc, wg, wu, wd)


def _combine_kernel(route_ref, x_ref, y_ref, gfin_ref, out_a_ref, out_b_ref, *, tm, tiles_a):
    t = pl.program_id(0)
    rec = route_ref[0]
    recf = jnp.concatenate([rec[2:4, :].astype(F32), pltpu.bitcast(rec, F32)[4:6, :],
                            jnp.zeros((LANES - 4, tm), F32)], axis=0)
    cols = recf.T
    lrow = lax.broadcasted_iota(jnp.int32, (tm, LOCAL_ROWS), 1).astype(F32)
    select = jnp.where(lrow == cols[:, 0:1], cols[:, 2:3], 0.0) + jnp.where(lrow == cols[:, 1:2], cols[:, 3:4], 0.0)
    xf = x_ref[...] + _dot(select.astype(BF16), _unpack_halves(y_ref[...]))
    out = xf * lax.rsqrt(jnp.mean(xf * xf, axis=-1, keepdims=True) + NORM_EPS) * gfin_ref[...]

    @pl.when(t < tiles_a)
    def _():
        out_a_ref[...] = out

    @pl.when(t >= tiles_a)
    def _():
        out_b_ref[...] = out


def _combine_call(route, xnew, yloc, gfin, tm, tiles_a):
    n_tiles = route.shape[0]
    tiles_b = n_tiles - tiles_a
    return pl.pallas_call(
        functools.partial(_combine_kernel, tm=tm, tiles_a=tiles_a),
        grid=(n_tiles,),
        in_specs=[pl.BlockSpec((1, ROUTE_ROWS, tm), lambda t: (t, 0, 0)),
                  pl.BlockSpec((tm, D_MODEL), lambda t: (t, 0)),
                  pl.BlockSpec((LOCAL_ROWS, D_MODEL // 2), lambda t: (t, 0)),
                  pl.BlockSpec((1, D_MODEL), lambda t: (0, 0))],
        out_specs=(pl.BlockSpec((tm, D_MODEL), lambda t: (jnp.minimum(t, tiles_a - 1), 0)),
                   pl.BlockSpec((tm, D_MODEL), lambda t: (jnp.maximum(t - tiles_a, 0), 0))),
        out_shape=(jax.ShapeDtypeStruct((tiles_a * tm, D_MODEL), F32),
                   jax.ShapeDtypeStruct((tiles_b * tm, D_MODEL), F32)),
        compiler_params=pltpu.CompilerParams(dimension_semantics=("arbitrary",), vmem_limit_bytes=VMEM_LIMIT),
        name="moe_combine",
    )(route, xnew, yloc, gfin)


def _prep_weights(g_mix, w_in, w_gla_gk2, b_gla_gk, g_gla_norm, w_br_ret, w_br_gla, w_out, g_ffn,
                  w_rg, b_rg, w_re, b_re):
    offs = np.cumsum((0, HK, HK, HV, HV, HK, HK, HV, HV, GLA_RANK, D_MODEL, D_MODEL))
    seg = lambda i: w_in[:, offs[i]:offs[i + 1]]
    lr = jnp.pad(seg(8), ((0, 0), (0, LANES - GLA_RANK)))
    wcat = jnp.concatenate([seg(0), seg(1), seg(2), seg(3), seg(4), seg(5), seg(6), seg(7), seg(9), seg(10), lr],
                           axis=1).astype(BF16)
    wgk = jnp.pad(w_gla_gk2, ((0, LANES - GLA_RANK), (0, 0))).astype(BF16)
    wrt = jnp.zeros((D_MODEL, ROUTER_LANES), F32)
    wrt = wrt.at[:, 0:N_GROUPS].set(w_rg).at[:, SUBLANES:SUBLANES + N_EXPERTS].set(w_re)
    brt = jnp.zeros((1, ROUTER_LANES), F32)
    brt = brt.at[0, 0:N_GROUPS].set(b_rg).at[0, SUBLANES:SUBLANES + N_EXPERTS].set(b_re)
    wrt_hi = wrt.astype(BF16)
    wrt = jnp.concatenate([wrt_hi, (wrt - wrt_hi.astype(F32)).astype(BF16)], axis=1)
    return (g_mix.reshape(1, D_MODEL), wcat, wgk, b_gla_gk.reshape(1, HK), g_gla_norm.reshape(1, HEAD_V),
            w_br_ret.astype(BF16), w_br_gla.astype(BF16), w_out.astype(BF16), g_ffn.reshape(1, D_MODEL), wrt, brt)


def _tri_consts(tm):
    r = np.arange(tm)
    low = (r[None, :] <= r[:, None])
    bd = low & ((r[None, :] // CHUNK) == (r[:, None] // CHUNK))
    return jnp.asarray(bd, BF16), jnp.asarray(low, BF16)


def _state_to_kernel(s):
    return jnp.swapaxes(s.reshape(s.shape[0], HK, HEAD_V), 1, 2)


def _state_from_kernel(s):
    return jnp.swapaxes(s, 1, 2).reshape(1, s.shape[0], N_HEADS, HEAD_K, HEAD_V)


def kernel(x_prompt, x_sample, state_ret, state_gla, meta_tokens, g_mix, w_in, w_gla_gk2, b_gla_gk, g_gla_norm, w_br_ret, w_br_gla, w_out, g_ffn, w_router_group, b_router_group, w_router_expert, b_router_expert, w_exp_gate, w_exp_up, w_exp_down, g_final):
    n_b, seq, _ = x_prompt.shape
    n_s, dec_seq, _ = x_sample.shape
    depth = state_ret.shape[0]
    assert depth == 1 and dec_seq == CHUNK and seq % TILE_ROWS == 0 and n_s % TILE_CHUNKS == 0
    cos_t, sin_t, cos_m, sin_m = _rotary_tables(seq, PAST_LEN)
    weights = _prep_weights(g_mix[0], w_in[0], w_gla_gk2[0], b_gla_gk[0], g_gla_norm[0], w_br_ret[0], w_br_gla[0],
                            w_out[0], g_ffn[0], w_router_group[0], b_router_group[0], w_router_expert[0],
                            b_router_expert[0])
    zero_state = jnp.zeros((1, HEAD_V, HK), F32)

    x_meta = jnp.concatenate([jnp.zeros((CHUNK - N_META, D_MODEL), F32), meta_tokens.astype(F32)], axis=0)
    meta_out = _mixer_call(None, x_meta.reshape(1, CHUNK, D_MODEL), cos_m, sin_m, (zero_state, zero_state),
                           (zero_state, zero_state), weights, _tri_consts(CHUNK), n_chunks=1, route=False)
    meta_ret, meta_gla = meta_out[6], meta_out[7]

    xnew, xloc, route, ku, ret_p, gla_p, ret_s, gla_s = _mixer_call(
        x_prompt, x_sample, cos_t, sin_t, (meta_ret, meta_gla),
        (_state_to_kernel(state_ret[0]), _state_to_kernel(state_gla[0])), weights, _tri_consts(TILE_ROWS),
        n_chunks=TILE_CHUNKS, route=True)

    n_tiles = route.shape[0]
    n_blocks = n_tiles * LOCAL_UNITS // BLOCK_UNITS + N_EXPERTS
    src, blk = _plan_call(ku[:, 0, :N_EXPERTS], n_blocks)
    yloc = _experts_call(src, blk, xloc, w_exp_gate[0], w_exp_up[0], w_exp_down[0], n_blocks)
    y_p, y_s = _combine_call(route, xnew, yloc, g_final.reshape(1, D_MODEL), TILE_ROWS, n_b * seq // TILE_ROWS)

    return (y_p.reshape(n_b, seq, D_MODEL), y_s.reshape(n_s, dec_seq, D_MODEL),
            _state_from_kernel(ret_p), _state_from_kernel(gla_p), _state_from_kernel(ret_s), _state_from_kernel(gla_s))
```

```python
import functools
import math

import jax
import jax.numpy as jnp
import numpy as np
from jax import lax
from jax.experimental import pallas as pl
from jax.experimental.pallas import tpu as pltpu

D_MODEL = 1024
CHUNK = 64
PAST_LEN = 1024
N_META = 16
N_HEADS = 4
HEAD_K = 64
HEAD_V = 128
HK = N_HEADS * HEAD_K
HV = N_HEADS * HEAD_V
GLA_RANK = 16
GATE_NORM = 16.0
ROPE_BASE = 10000.0
N_GROUPS = 4
EXPERTS_PER_GROUP = 8
N_EXPERTS = N_GROUPS * EXPERTS_PER_GROUP
D_EXPERT = 512
NORM_EPS = 1e-6

LANES = 128
SUBLANES = 8
TILE_CHUNKS = 8
TILE_ROWS = TILE_CHUNKS * CHUNK
ROUTE_ROWS = 8
ROUTER_LANES = 128
UNIT = SUBLANES
MOE_BLOCK = 512
BLOCK_UNITS = MOE_BLOCK // UNIT
LOCAL_ROWS = 2 * TILE_ROWS + N_EXPERTS * UNIT
LOCAL_UNITS = LOCAL_ROWS // UNIT
PLAN_FAST = 8
EXPERT_PHASES = 4
VMEM_LIMIT = 56 * 1024 * 1024

C_RQ, C_RK, C_RV, C_RG = 0, 256, 512, 1024
C_GQ, C_GK, C_GV, C_GG = 1536, 1792, 2048, 2560
C_ZR, C_ZG, C_LR = 3072, 4096, 5120
W_CAT = 5248

F32 = jnp.float32
BF16 = jnp.bfloat16
LOG_G = tuple(math.log1p(-(2.0 ** (-5.0 - h))) for h in range(N_HEADS))


def _dot(a, b):
    return jnp.dot(a, b, preferred_element_type=F32)


def _dot_nt(a, b):
    return lax.dot_general(a, b, (((1,), (1,)), ((), ())), preferred_element_type=F32)


def _dot_tn(a, b):
    return lax.dot_general(a, b, (((0,), (0,)), ((), ())), preferred_element_type=F32)


def _sigmoid(x):
    return 1.0 / (1.0 + jnp.exp(-x))


def _pack_halves(x):
    half = x.shape[1] // 2
    bits = pltpu.bitcast(x, jnp.int32)
    return lax.shift_right_logical(bits[:, :half], 16) | (bits[:, half:] & jnp.int32(-65536))


def _unpack_halves(w):
    lo = pltpu.bitcast(lax.shift_left(w, 16), F32)
    hi = pltpu.bitcast(w & jnp.int32(-65536), F32)
    return jnp.concatenate([lo, hi], axis=1).astype(BF16)


def _head_of_lane(shape, width):
    return lax.broadcasted_iota(jnp.int32, shape, len(shape) - 1) >> int(math.log2(width))


def _per_head_lane_const(vals, shape, width):
    hd = _head_of_lane(shape, width)
    out = jnp.full(shape, vals[N_HEADS - 1], F32)
    for h in range(N_HEADS - 2, -1, -1):
        out = jnp.where(hd == h, vals[h], out)
    return out


def _tables_kernel(inv_ref, cp_ref, sp_ref, cm_ref, sm_ref, *, seq, past_len):
    inv = inv_ref[...]

    def fill(c_ref, s_ref, rows, row0, pos_of_row):
        r = lax.broadcasted_iota(jnp.int32, (rows, LANES), 0)
        ang = pos_of_row(r).astype(F32) * inv
        lane = lax.broadcasted_iota(jnp.int32, (rows, LANES), 1)
        first_half = (lane & (HEAD_K - 1)) < (HEAD_K // 2)
        c_ref[pl.ds(row0, rows), :] = jnp.cos(ang)
        s = jnp.sin(ang)
        s_ref[pl.ds(row0, rows), :] = jnp.where(first_half, -s, s)

    fill(cp_ref, sp_ref, seq, 0, lambda r: r)
    fill(cp_ref, sp_ref, TILE_ROWS, seq, lambda r: past_len + (r & (CHUNK - 1)))
    fill(cm_ref, sm_ref, CHUNK, 0, lambda r: r - CHUNK)


def _rotary_tables(seq, past_len):
    half = HEAD_K // 2
    inv = ROPE_BASE ** (-2.0 * jnp.arange(half, dtype=F32) / HEAD_K)
    inv = jnp.tile(inv, LANES // half).reshape(1, LANES)
    shp = lambda r: jax.ShapeDtypeStruct((r, LANES), F32)
    return pl.pallas_call(
        functools.partial(_tables_kernel, seq=seq, past_len=past_len),
        out_shape=(shp(seq + TILE_ROWS), shp(seq + TILE_ROWS), shp(CHUNK), shp(CHUNK)),
        compiler_params=pltpu.CompilerParams(vmem_limit_bytes=VMEM_LIMIT),
        name="rotary_tables",
    )(inv)


def _mixer_kernel(*refs, n_chunks, tiles_p, tiles_per_row, route):
    tm = n_chunks * CHUNK
    if tiles_p:
        xp_ref, refs = refs[0], refs[1:]
    (xs_ref, cos_ref, sin_ref, spi_r_ref, spi_g_ref, ssi_r_ref, ssi_g_ref,
     gmix_ref, wcat_ref, wgk_ref, bgk_ref, gnorm_ref, wbr_ref, wbg_ref, wout_ref,
     gffn_ref, wrt_ref, brt_ref, bdtri_ref, ltri_ref,
     xnew_ref, xloc_ref, route_ref, ku_ref, spo_r_ref, spo_g_ref, sso_r_ref, sso_g_ref,
     qb, qdb, kb, kkb, vb, gqb, gkkb, gvb, ga, o_ret, o_gla, st_ret, st_gla) = refs

    i = pl.program_id(0)
    if tiles_p:
        is_s = i >= tiles_p
        t_idx = jnp.minimum(i, tiles_p - 1) % tiles_per_row
        x = jnp.where(is_s, xs_ref[...].reshape(tm, D_MODEL), xp_ref[...].reshape(tm, D_MODEL))

        @pl.when(jnp.logical_and(jnp.logical_not(is_s), t_idx == 0))
        def _():
            st_ret[...] = spi_r_ref[0]
            st_gla[...] = spi_g_ref[0]
    else:
        is_s = None
        x = xs_ref[...].reshape(tm, D_MODEL)

    h = x * lax.rsqrt(jnp.mean(x * x, axis=-1, keepdims=True) + NORM_EPS) * gmix_ref[...]
    hb = h.astype(BF16)

    def proj(c0, width):
        return _dot(hb, wcat_ref[:, c0:c0 + width])

    cos3 = jnp.concatenate([cos_ref[...]] * 2, axis=1).reshape(n_chunks, CHUNK, HK)
    sin3 = jnp.concatenate([sin_ref[...]] * 2, axis=1).reshape(n_chunks, CHUNK, HK)
    lane_hk = lax.broadcasted_iota(jnp.int32, (tm, HK), 1)
    first_half = (lane_hk & (HEAD_K - 1)) < (HEAD_K // 2)

    def rotary(t):
        swapped = jnp.where(first_half, pltpu.roll(t, HK - HEAD_K // 2, 1), pltpu.roll(t, HEAD_K // 2, 1))
        t3 = t.reshape(n_chunks, CHUNK, HK)
        return (t3 * cos3 + swapped.reshape(n_chunks, CHUNK, HK) * sin3).reshape(tm, HK)

    logg_hk = _per_head_lane_const(LOG_G, (CHUNK, HK), HEAD_K)
    l_idx = lax.broadcasted_iota(jnp.int32, (CHUNK, HK), 0).astype(F32)
    qdec = jnp.exp((l_idx + 1.0) * logg_hk)
    kdec = jnp.exp((CHUNK - 1.0 - l_idx) * logg_hk)
    cdec = jnp.exp(float(CHUNK) * _per_head_lane_const(LOG_G, (1, HK), HEAD_K))
    r_idx = lax.broadcasted_iota(jnp.int32, (N_HEADS * CHUNK, CHUNK), 0)
    m_idx = lax.broadcasted_iota(jnp.int32, (N_HEADS * CHUNK, CHUNK), 1)
    logg_rows = jnp.full((N_HEADS * CHUNK, CHUNK), LOG_G[N_HEADS - 1], F32)
    for hh in range(N_HEADS - 2, -1, -1):
        logg_rows = jnp.where((r_idx >> 6) == hh, LOG_G[hh], logg_rows)
    dmat = jnp.exp(jnp.abs((r_idx & (CHUNK - 1)) - m_idx).astype(F32) * logg_rows)

    glr = proj(C_LR, LANES)
    rq = rotary(proj(C_RQ, HK))
    qb[...] = rq.astype(BF16)
    qdb[...] = (rq.reshape(n_chunks, CHUNK, HK) * qdec).reshape(tm, HK).astype(BF16)
    gl = _dot(glr.astype(BF16), wgk_ref[...]) + bgk_ref[...]
    rk = rotary(proj(C_RK, HK)) * (HEAD_K ** -0.5)
    kb[...] = rk.astype(BF16)
    kkb[...] = (rk.reshape(n_chunks, CHUNK, HK) * kdec).reshape(tm, HK).astype(BF16)
    log_a = (jnp.minimum(gl, 0.0) - jnp.log1p(jnp.exp(-jnp.abs(gl)))) / GATE_NORM
    la_hi = log_a.astype(BF16)
    la_lo = (log_a - la_hi.astype(F32)).astype(BF16)
    vb[...] = proj(C_RV, HV).astype(BF16)
    bdtri = bdtri_ref[...]
    bcum = _dot(bdtri, la_hi) + _dot(bdtri, la_lo)
    gqb[...] = (proj(C_GQ, HK) * (HEAD_K ** -0.5)).astype(BF16)
    gvb[...] = proj(C_GV, HV).astype(BF16)
    gk = proj(C_GK, HK)
    b3 = bcum.reshape(n_chunks, CHUNK, HK)
    bl3 = b3[:, CHUNK - 1:CHUNK, :]
    gkkb[...] = (gk.reshape(n_chunks, CHUNK, HK) * jnp.exp(bl3 - b3)).reshape(tm, HK).astype(BF16)
    ga[...] = jnp.broadcast_to(jnp.exp(bl3), (n_chunks, SUBLANES, HK))

    def stack_masked(a, width):
        head = _head_of_lane(a.shape, width)
        zero = jnp.zeros_like(a)
        return jnp.concatenate([jnp.where(head == hh, a, zero) for hh in range(N_HEADS)], axis=0)

    def heads_to_rows(a):
        return jnp.concatenate([a[:, hh * HEAD_V:(hh + 1) * HEAD_V] for hh in range(N_HEADS)], axis=0)

    def rows_to_heads(a):
        return jnp.concatenate([a[hh * CHUNK:(hh + 1) * CHUNK, :] for hh in range(N_HEADS)], axis=1)

    chunk_rows = [slice(c * CHUNK, (c + 1) * CHUNK) for c in range(n_chunks)]
    probs = [(_dot_nt(stack_masked(qb[r, :], HEAD_K), kb[r, :]) * dmat).astype(BF16) for r in chunk_rows]
    inc_ret = [_dot_tn(heads_to_rows(vb[r, :]), stack_masked(kkb[r, :], HEAD_K)) for r in chunk_rows]
    inc_gla = [_dot_tn(heads_to_rows(gvb[r, :]), stack_masked(gkkb[r, :], HEAD_K)) for r in chunk_rows]
    rg = proj(C_RG, HV)
    gg = proj(C_GG, HV)
    s_in, g_out = [], []
    s_cur, g_cur = st_ret[...], st_gla[...]
    for c in range(n_chunks):
        if is_s is None:
            s_cur, g_cur = ssi_r_ref[c], ssi_g_ref[c]
        else:
            s_cur = jnp.where(is_s, ssi_r_ref[c], s_cur)
            g_cur = jnp.where(is_s, ssi_g_ref[c], g_cur)
        s_in.append(s_cur.astype(BF16))
        s_cur = s_cur * cdec + inc_ret[c]
        g_cur = g_cur * ga[c][0:1, :] + inc_gla[c]
        g_out.append(g_cur.astype(BF16))
        sso_r_ref[c] = s_cur
        sso_g_ref[c] = g_cur
    st_ret[...] = s_cur
    st_gla[...] = g_cur
    for c, r in enumerate(chunk_rows):
        v = vb[r, :]
        intra = jnp.concatenate(
            [_dot(probs[c][hh * CHUNK:(hh + 1) * CHUNK, :], v[:, hh * HEAD_V:(hh + 1) * HEAD_V])
             for hh in range(N_HEADS)], axis=1)
        inter = rows_to_heads(_dot_nt(stack_masked(qdb[r, :], HEAD_K), s_in[c]))
        o_ret[r, :] = intra + inter
        o_gla[r, :] = rows_to_heads(_dot_nt(stack_masked(gqb[r, :], HEAD_K), g_out[c]))

    gnorm = gnorm_ref[...]
    orr = o_ret[...]
    ogg = o_gla[...]
    ret_parts, gla_parts = [], []
    for hh in range(N_HEADS):
        sl = slice(hh * HEAD_V, (hh + 1) * HEAD_V)
        oh = orr[:, sl]
        mu = jnp.mean(oh, axis=-1, keepdims=True)
        dev = oh - mu
        var = jnp.mean(dev * dev, axis=-1, keepdims=True)
        ret_parts.append(dev * lax.rsqrt(var + NORM_EPS))
        og = ogg[:, sl]
        gla_parts.append(og * lax.rsqrt(jnp.mean(og * og, axis=-1, keepdims=True) + NORM_EPS) * gnorm)
    o_r = jnp.concatenate(ret_parts, axis=1) * (rg * _sigmoid(rg))
    o_g = jnp.concatenate(gla_parts, axis=1) * (gg * _sigmoid(gg))
    merged = (_sigmoid(proj(C_ZR, D_MODEL)) * _dot(o_r.astype(BF16), wbr_ref[...])
              + _sigmoid(proj(C_ZG, D_MODEL)) * _dot(o_g.astype(BF16), wbg_ref[...]))
    xn = x + _dot(merged.astype(BF16), wout_ref[...])
    xnew_ref[...] = xn

    if route:
        h2 = xn * lax.rsqrt(jnp.mean(xn * xn, axis=-1, keepdims=True) + NORM_EPS) * gffn_ref[...]
        h2_hi = h2.astype(BF16)
        h2_lo = (h2 - h2_hi.astype(F32)).astype(BF16)
        hi_both = _dot(h2_hi, wrt_ref[...])
        logits = (hi_both[:, :ROUTER_LANES] + _dot(h2_lo, wrt_ref[:, :ROUTER_LANES])
                  + hi_both[:, ROUTER_LANES:]) + brt_ref[...]
        lt = logits.T
        row8 = lax.broadcasted_iota(jnp.int32, (SUBLANES, tm), 0)
        neg_inf = jnp.float32(-jnp.inf)
        glog = jnp.where(row8 < N_GROUPS, lt[0:SUBLANES, :], neg_inf)
        gmax = jnp.max(glog, axis=0, keepdims=True)
        grp = jnp.min(jnp.where(glog == gmax, row8, SUBLANES), axis=0, keepdims=True)
        p_grp = 1.0 / jnp.sum(jnp.exp(glog - gmax), axis=0, keepdims=True)
        le = jnp.zeros((SUBLANES, tm), F32)
        for g in range(N_GROUPS):
            le = jnp.where(grp == g, lt[SUBLANES * (g + 1):SUBLANES * (g + 2), :], le)
        m1 = jnp.max(le, axis=0, keepdims=True)
        i1 = jnp.min(jnp.where(le == m1, row8, SUBLANES), axis=0, keepdims=True)
        le2 = jnp.where(row8 == i1, neg_inf, le)
        m2 = jnp.max(le2, axis=0, keepdims=True)
        i2 = jnp.min(jnp.where(le2 == m2, row8, SUBLANES), axis=0, keepdims=True)
        e0 = grp * EXPERTS_PER_GROUP + i1
        e1 = grp * EXPERTS_PER_GROUP + i2
        t21 = jnp.exp(m2 - m1)
        w0 = p_grp / (1.0 + t21)
        w1 = p_grp * t21 / (1.0 + t21)

        erow = lax.broadcasted_iota(jnp.int32, (N_EXPERTS, tm), 0)
        hit0 = erow == e0
        hit1 = erow == e1
        onehot = jnp.where(jnp.logical_or(hit0, hit1), 1.0, 0.0).astype(BF16)
        cum = _dot_nt(onehot, ltri_ref[...])
        n_run = cum[:, tm - 1:tm]
        n_pad = jnp.ceil(n_run / UNIT) * UNIT
        rank0 = jnp.sum(jnp.where(hit0, cum - 1.0, 0.0), axis=0, keepdims=True)
        rank1 = jnp.sum(jnp.where(hit1, cum - 1.0, 0.0), axis=0, keepdims=True)
        start0 = jnp.sum(jnp.where(erow < e0, n_pad, 0.0), axis=0, keepdims=True)
        start1 = jnp.sum(jnp.where(erow < e1, n_pad, 0.0), axis=0, keepdims=True)
        ld0 = (start0 + rank0).astype(jnp.int32)
        ld1 = (start1 + rank1).astype(jnp.int32)
        lrow = lax.broadcasted_iota(jnp.int32, (LOCAL_ROWS, tm), 0)
        perm = jnp.where(jnp.logical_or(lrow == ld0, lrow == ld1), 1.0, 0.0).astype(BF16)
        xloc_ref[...] = _pack_halves(_dot(perm, h2_hi))
        lane_e = lax.broadcasted_iota(jnp.int32, (N_EXPERTS, LANES), 1)
        erow_l = lax.broadcasted_iota(jnp.int32, (N_EXPERTS, LANES), 0)
        units_row = jnp.sum(jnp.where(erow_l == lane_e, n_pad / UNIT, 0.0), axis=0, keepdims=True)
        ku_ref[...] = jnp.broadcast_to(units_row, (SUBLANES, LANES)).astype(jnp.int32).reshape(ku_ref.shape)
        zero_row = jnp.zeros((1, tm), jnp.int32)
        rec = jnp.concatenate([e0, e1, ld0, ld1, pltpu.bitcast(w0, jnp.int32), pltpu.bitcast(w1, jnp.int32),
                               zero_row, zero_row], axis=0)
        route_ref[...] = rec.reshape(route_ref.shape)
    else:
        xloc_ref[...] = jnp.zeros(xloc_ref.shape, jnp.int32)
        route_ref[...] = jnp.zeros(route_ref.shape, jnp.int32)
        ku_ref[...] = jnp.zeros(ku_ref.shape, jnp.int32)

    if tiles_p:
        @pl.when(jnp.logical_and(jnp.logical_not(is_s), t_idx == tiles_per_row - 1))
        def _():
            spo_r_ref[0] = st_ret[...]
            spo_g_ref[0] = st_gla[...]
    else:
        spo_r_ref[0] = st_ret[...]
        spo_g_ref[0] = st_gla[...]


def _const_spec(shape):
    nd = len(shape)
    return pl.BlockSpec(shape, lambda *_: (0,) * nd, pipeline_mode=pl.Buffered(1))


def _mixer_call(x_prompt, x_streams, cos, sin, st_prompt, st_streams, weights, consts, *, n_chunks, route):
    tm = n_chunks * CHUNK
    n_streams = x_streams.shape[0]
    tiles_s = n_streams // n_chunks
    if x_prompt is not None:
        n_rows, n_seq, _ = x_prompt.shape
        tpr = n_seq // tm
        tiles_p = n_rows * tpr
    else:
        n_rows, tpr, tiles_p = 1, 1, 0
    n_tiles = tiles_p + tiles_s
    p_idx = lambda i: jnp.minimum(i, tiles_p - 1)
    s_idx = lambda i: jnp.maximum(i - tiles_p, 0)

    st_blk = (1, HEAD_V, HK)
    in_specs, args = [], []
    if tiles_p:
        in_specs.append(pl.BlockSpec((1, tm, D_MODEL), lambda i: (p_idx(i) // tpr, p_idx(i) % tpr, 0)))
        args.append(x_prompt)
        cos_spec = pl.BlockSpec((tm, LANES), lambda i: (jnp.where(i < tiles_p, i % tpr, tpr), 0))
    else:
        cos_spec = pl.BlockSpec((tm, LANES), lambda i: (0, 0))
    once = pl.Buffered(1)
    stream_in = pl.BlockSpec((n_chunks, HEAD_V, HK), lambda i: (s_idx(i), 0, 0), pipeline_mode=once)
    in_specs += [pl.BlockSpec((n_chunks, CHUNK, D_MODEL), lambda i: (s_idx(i), 0, 0), pipeline_mode=once),
                 cos_spec, cos_spec, _const_spec(st_blk), _const_spec(st_blk), stream_in, stream_in]
    args += [x_streams, cos, sin, st_prompt[0], st_prompt[1], st_streams[0], st_streams[1]]
    in_specs += [_const_spec(w.shape) for w in weights] + [_const_spec(c.shape) for c in consts]
    args += list(weights) + list(consts)

    out_shape = (jax.ShapeDtypeStruct((n_tiles * tm, D_MODEL), F32),
                 jax.ShapeDtypeStruct((n_tiles * LOCAL_ROWS, D_MODEL // 2), jnp.int32),
                 jax.ShapeDtypeStruct((n_tiles, ROUTE_ROWS, tm), jnp.int32),
                 jax.ShapeDtypeStruct((n_tiles, SUBLANES, LANES), jnp.int32),
                 jax.ShapeDtypeStruct((n_rows, HEAD_V, HK), F32), jax.ShapeDtypeStruct((n_rows, HEAD_V, HK), F32),
                 jax.ShapeDtypeStruct((n_streams, HEAD_V, HK), F32),
                 jax.ShapeDtypeStruct((n_streams, HEAD_V, HK), F32))
    row_spec = pl.BlockSpec(st_blk, lambda i: (p_idx(i) // tpr if tiles_p else 0, 0, 0))
    stream_spec = pl.BlockSpec((n_chunks, HEAD_V, HK), lambda i: (s_idx(i), 0, 0))
    out_specs = (pl.BlockSpec((tm, D_MODEL), lambda i: (i, 0)),
                 pl.BlockSpec((LOCAL_ROWS, D_MODEL // 2), lambda i: (i, 0)),
                 pl.BlockSpec((1, ROUTE_ROWS, tm), lambda i: (i, 0, 0)),
                 pl.BlockSpec((1, SUBLANES, LANES), lambda i: (i, 0, 0)),
                 row_spec, row_spec, stream_spec, stream_spec)

    scratch = [pltpu.VMEM((tm, HK), BF16), pltpu.VMEM((tm, HK), BF16), pltpu.VMEM((tm, HK), BF16),
               pltpu.VMEM((tm, HK), BF16), pltpu.VMEM((tm, HV), BF16),
               pltpu.VMEM((tm, HK), BF16), pltpu.VMEM((tm, HK), BF16), pltpu.VMEM((tm, HV), BF16),
               pltpu.VMEM((n_chunks, SUBLANES, HK), F32),
               pltpu.VMEM((tm, HV), F32), pltpu.VMEM((tm, HV), F32),
               pltpu.VMEM((HEAD_V, HK), F32), pltpu.VMEM((HEAD_V, HK), F32)]

    return pl.pallas_call(
        functools.partial(_mixer_kernel, n_chunks=n_chunks, tiles_p=tiles_p, tiles_per_row=tpr, route=route),
        grid=(n_tiles,), in_specs=in_specs, out_specs=out_specs, out_shape=out_shape, scratch_shapes=scratch,
        compiler_params=pltpu.CompilerParams(dimension_semantics=("arbitrary",), vmem_limit_bytes=VMEM_LIMIT),
        name="mixer" if route else "mixer_meta",
    )(*args)


def _plan_kernel(ku_ref, src_ref, blk_ref, next_unit, *, n_tiles, n_blocks):
    shift = int(math.log2(BLOCK_UNITS))

    def init_tile(t, c):
        next_unit[t] = t * LOCAL_UNITS
        return c

    lax.fori_loop(0, n_tiles, init_tile, 0)

    def init_block(b, c):
        blk_ref[0, b] = N_EXPERTS - 1
        blk_ref[1, b] = 0
        blk_ref[2, b] = 0
        return c

    lax.fori_loop(0, n_blocks, init_block, 0)

    def per_expert(e, g0):
        def per_tile(t, g):
            k = ku_ref[t, e]
            base = next_unit[t]

            def per_unit(j, c):
                src_ref[g + j] = base + j
                return c

            for j in range(PLAN_FAST):
                src_ref[g + j] = base + j

            @pl.when(k > PLAN_FAST)
            def _():
                lax.fori_loop(PLAN_FAST, k, per_unit, 0)

            next_unit[t] = base + k
            return g + k

        g_end = lax.fori_loop(0, n_tiles, per_tile, g0)
        g_pad = ((g_end + (BLOCK_UNITS - 1)) >> shift) << shift

        def pad_unit(j, c):
            src_ref[j] = src_ref[g_pad - BLOCK_UNITS]
            return c

        lax.fori_loop(g_end, g_pad, pad_unit, 0)

        def set_block(b, c):
            blk_ref[0, b] = e
            blk_ref[1, b] = jnp.minimum(g_end - (b << shift), BLOCK_UNITS)
            return c

        lax.fori_loop(g0 >> shift, g_pad >> shift, set_block, 0)
        return g_pad

    g_total = lax.fori_loop(0, N_EXPERTS, per_expert, 0)

    def tail_unit(j, c):
        src_ref[j] = 0
        return c

    lax.fori_loop(g_total, n_blocks * BLOCK_UNITS + PLAN_FAST, tail_unit, 0)
    blk_ref[2, 0] = g_total >> shift


def _plan_call(ku, n_blocks):
    n_tiles = ku.shape[0]
    smem = pl.BlockSpec(memory_space=pltpu.SMEM)
    return pl.pallas_call(
        functools.partial(_plan_kernel, n_tiles=n_tiles, n_blocks=n_blocks),
        in_specs=[smem], out_specs=(smem, smem),
        out_shape=(jax.ShapeDtypeStruct((n_blocks * BLOCK_UNITS + PLAN_FAST,), jnp.int32),
                   jax.ShapeDtypeStruct((3, n_blocks), jnp.int32)),
        scratch_shapes=[pltpu.SMEM((n_tiles,), jnp.int32)],
        name="moe_plan",
    )(ku)


def _experts_kernel(src_ref, blk_ref, x_hbm, wg_ref, wu_ref, wd_ref, y_hbm,
                    xin, yout, wgub, wdb, sem_in, sem_out):
    b = pl.program_id(0)
    used = blk_ref[2, 0]
    slot = b & 1

    def unit_rows(blk, u):
        return pl.ds(pl.multiple_of(src_ref[blk * BLOCK_UNITS + u] * UNIT, UNIT), UNIT)

    def in_copy(blk, u, s):
        return pltpu.make_async_copy(x_hbm.at[unit_rows(blk, u), :], xin.at[s, pl.ds(u * UNIT, UNIT), :],
                                     sem_in.at[s])

    def out_copy(blk, u, s):
        return pltpu.make_async_copy(yout.at[s, pl.ds(u * UNIT, UNIT), :], y_hbm.at[unit_rows(blk, u), :],
                                     sem_out.at[s])

    @pl.when(jnp.logical_and(b == 0, used > 0))
    def _():
        for u in range(BLOCK_UNITS):
            in_copy(0, u, 0).start()

    @pl.when(b < used)
    def _():
        expert = blk_ref[0, b]
        changed = jnp.logical_or(b == 0, expert != blk_ref[0, jnp.maximum(b - 1, 0)])

        @pl.when(changed)
        def _():
            half = D_EXPERT // EXPERT_PHASES
            for q in range(EXPERT_PHASES):
                wgub[:, 2 * q * half:(2 * q + 1) * half] = wg_ref[0][:, q * half:(q + 1) * half].astype(BF16)
                wgub[:, (2 * q + 1) * half:(2 * q + 2) * half] = wu_ref[0][:, q * half:(q + 1) * half].astype(BF16)
            wdb[...] = wd_ref[0].astype(BF16)

        for u in range(BLOCK_UNITS):
            in_copy(b, u, slot).wait()

        nxt = jnp.where(b + 1 < used, b + 1, 0)
        quarter = 2 * D_EXPERT // EXPERT_PHASES
        hid_parts = []
        for q in range(EXPERT_PHASES):
            gate_up = _dot(_unpack_halves(xin[slot]), wgub[:, q * quarter:(q + 1) * quarter])
            for u in range(q * BLOCK_UNITS // EXPERT_PHASES, (q + 1) * BLOCK_UNITS // EXPERT_PHASES):
                in_copy(nxt, u, 1 - slot).start()
            gate = gate_up[:, :quarter // 2]
            hid_parts.append(((gate * _sigmoid(gate)) * gate_up[:, quarter // 2:]).astype(BF16))
        hid = jnp.concatenate(hid_parts, axis=1)
        yout[slot] = _pack_halves(_dot(hid, wdb[...]).astype(BF16).astype(F32))

        n_real = blk_ref[1, b]

        @pl.when(n_real == BLOCK_UNITS)
        def _():
            for u in range(BLOCK_UNITS):
                out_copy(b, u, slot).start()

        @pl.when(n_real < BLOCK_UNITS)
        def _():
            lax.fori_loop(0, n_real, lambda u, c: (out_copy(b, u, slot).start(), c)[1], 0)

        @pl.when(b > 0)
        def _():
            lax.fori_loop(0, blk_ref[1, b - 1], lambda u, c: (out_copy(b - 1, u, 1 - slot).wait(), c)[1], 0)

        @pl.when(b == used - 1)
        def _():
            lax.fori_loop(0, n_real, lambda u, c: (out_copy(b, u, slot).wait(), c)[1], 0)
            for u in range(BLOCK_UNITS):
                in_copy(0, u, 1 - slot).wait()


def _experts_call(src, blk, xloc, wg, wu, wd, n_blocks):
    w_in_spec = pl.BlockSpec((1, D_MODEL, D_EXPERT), lambda b, s, m: (m[0, b], 0, 0))
    grid_spec = pltpu.PrefetchScalarGridSpec(
        num_scalar_prefetch=2,
        grid=(n_blocks,),
        in_specs=[pl.BlockSpec(memory_space=pl.ANY), w_in_spec, w_in_spec,
                  pl.BlockSpec((1, D_EXPERT, D_MODEL), lambda b, s, m: (m[0, b], 0, 0))],
        out_specs=pl.BlockSpec(memory_space=pl.ANY),
        scratch_shapes=[pltpu.VMEM((2, MOE_BLOCK, D_MODEL // 2), jnp.int32),
                        pltpu.VMEM((2, MOE_BLOCK, D_MODEL // 2), jnp.int32),
                        pltpu.VMEM((D_MODEL, 2 * D_EXPERT), BF16), pltpu.VMEM((D_EXPERT, D_MODEL), BF16),
                        pltpu.SemaphoreType.DMA((2,)), pltpu.SemaphoreType.DMA((2,))],
    )
    return pl.pallas_call(
        _experts_kernel, grid_spec=grid_spec,
        out_shape=jax.ShapeDtypeStruct(xloc.shape, jnp.int32),
        input_output_aliases={2: 0},
        compiler_params=pltpu.CompilerParams(dimension_semantics=("arbitrary",), vmem_limit_bytes=VMEM_LIMIT),
        name="moe_experts",
    )(src, blk, xloc, wg, wu, wd)


def _combine_kernel(route_ref, x_ref, y_ref, gfin_ref, out_a_ref, out_b_ref, *, tm, tiles_a):
    t = pl.program_id(0)
    rec = route_ref[0]
    recf = jnp.concatenate([rec[2:4, :].astype(F32), pltpu.bitcast(rec, F32)[4:6, :],
                            jnp.zeros((LANES - 4, tm), F32)], axis=0)
    cols = recf.T
    lrow = lax.broadcasted_iota(jnp.int32, (tm, LOCAL_ROWS), 1).astype(F32)
    select = jnp.where(lrow == cols[:, 0:1], cols[:, 2:3], 0.0) + jnp.where(lrow == cols[:, 1:2], cols[:, 3:4], 0.0)
    xf = x_ref[...] + _dot(select.astype(BF16), _unpack_halves(y_ref[...]))
    out = xf * lax.rsqrt(jnp.mean(xf * xf, axis=-1, keepdims=True) + NORM_EPS) * gfin_ref[...]

    @pl.when(t < tiles_a)
    def _():
        out_a_ref[...] = out

    @pl.when(t >= tiles_a)
    def _():
        out_b_ref[...] = out


def _combine_call(route, xnew, yloc, gfin, tm, tiles_a):
    n_tiles = route.shape[0]
    tiles_b = n_tiles - tiles_a
    return pl.pallas_call(
        functools.partial(_combine_kernel, tm=tm, tiles_a=tiles_a),
        grid=(n_tiles,),
        in_specs=[pl.BlockSpec((1, ROUTE_ROWS, tm), lambda t: (t, 0, 0)),
                  pl.BlockSpec((tm, D_MODEL), lambda t: (t, 0)),
                  pl.BlockSpec((LOCAL_ROWS, D_MODEL // 2), lambda t: (t, 0)),
                  pl.BlockSpec((1, D_MODEL), lambda t: (0, 0))],
        out_specs=(pl.BlockSpec((tm, D_MODEL), lambda t: (jnp.minimum(t, tiles_a - 1), 0)),
                   pl.BlockSpec((tm, D_MODEL), lambda t: (jnp.maximum(t - tiles_a, 0), 0))),
        out_shape=(jax.ShapeDtypeStruct((tiles_a * tm, D_MODEL), F32),
                   jax.ShapeDtypeStruct((tiles_b * tm, D_MODEL), F32)),
        compiler_params=pltpu.CompilerParams(dimension_semantics=("arbitrary",), vmem_limit_bytes=VMEM_LIMIT),
        name="moe_combine",
    )(route, xnew, yloc, gfin)


def _prep_weights(g_mix, w_in, w_gla_gk2, b_gla_gk, g_gla_norm, w_br_ret, w_br_gla, w_out, g_ffn,
                  w_rg, b_rg, w_re, b_re):
    offs = np.cumsum((0, HK, HK, HV, HV, HK, HK, HV, HV, GLA_RANK, D_MODEL, D_MODEL))
    seg = lambda i: w_in[:, offs[i]:offs[i + 1]]
    lr = jnp.pad(seg(8), ((0, 0), (0, LANES - GLA_RANK)))
    wcat = jnp.concatenate([seg(0), seg(1), seg(2), seg(3), seg(4), seg(5), seg(6), seg(7), seg(9), seg(10), lr],
                           axis=1).astype(BF16)
    wgk = jnp.pad(w_gla_gk2, ((0, LANES - GLA_RANK), (0, 0))).astype(BF16)
    wrt = jnp.zeros((D_MODEL, ROUTER_LANES), F32)
    wrt = wrt.at[:, 0:N_GROUPS].set(w_rg).at[:, SUBLANES:SUBLANES + N_EXPERTS].set(w_re)
    brt = jnp.zeros((1, ROUTER_LANES), F32)
    brt = brt.at[0, 0:N_GROUPS].set(b_rg).at[0, SUBLANES:SUBLANES + N_EXPERTS].set(b_re)
    wrt_hi = wrt.astype(BF16)
    wrt = jnp.concatenate([wrt_hi, (wrt - wrt_hi.astype(F32)).astype(BF16)], axis=1)
    return (g_mix.reshape(1, D_MODEL), wcat, wgk, b_gla_gk.reshape(1, HK), g_gla_norm.reshape(1, HEAD_V),
            w_br_ret.astype(BF16), w_br_gla.astype(BF16), w_out.astype(BF16), g_ffn.reshape(1, D_MODEL), wrt, brt)


def _tri_consts(tm):
    r = np.arange(tm)
    low = (r[None, :] <= r[:, None])
    bd = low & ((r[None, :] // CHUNK) == (r[:, None] // CHUNK))
    return jnp.asarray(bd, BF16), jnp.asarray(low, BF16)


def _state_to_kernel(s):
    return jnp.swapaxes(s.reshape(s.shape[0], HK, HEAD_V), 1, 2)


def _state_from_kernel(s):
    return jnp.swapaxes(s, 1, 2).reshape(1, s.shape[0], N_HEADS, HEAD_K, HEAD_V)


def kernel(x_prompt, x_sample, state_ret, state_gla, meta_tokens, g_mix, w_in, w_gla_gk2, b_gla_gk, g_gla_norm, w_br_ret, w_br_gla, w_out, g_ffn, w_router_group, b_router_group, w_router_expert, b_router_expert, w_exp_gate, w_exp_up, w_exp_down, g_final):
    n_b, seq, _ = x_prompt.shape
    n_s, dec_seq, _ = x_sample.shape
    depth = state_ret.shape[0]
    assert depth == 1 and dec_seq == CHUNK and seq % TILE_ROWS == 0 and n_s % TILE_CHUNKS == 0
    cos_t, sin_t, cos_m, sin_m = _rotary_tables(seq, PAST_LEN)
    weights = _prep_weights(g_mix[0], w_in[0], w_gla_gk2[0], b_gla_gk[0], g_gla_norm[0], w_br_ret[0], w_br_gla[0],
                            w_out[0], g_ffn[0], w_router_group[0], b_router_group[0], w_router_expert[0],
                            b_router_expert[0])
    zero_state = jnp.zeros((1, HEAD_V, HK), F32)

    x_meta = jnp.concatenate([jnp.zeros((CHUNK - N_META, D_MODEL), F32), meta_tokens.astype(F32)], axis=0)
    meta_out = _mixer_call(None, x_meta.reshape(1, CHUNK, D_MODEL), cos_m, sin_m, (zero_state, zero_state),
                           (zero_state, zero_state), weights, _tri_consts(CHUNK), n_chunks=1, route=False)
    meta_ret, meta_gla = meta_out[6], meta_out[7]

    xnew, xloc, route, ku, ret_p, gla_p, ret_s, gla_s = _mixer_call(
        x_prompt, x_sample, cos_t, sin_t, (meta_ret, meta_gla),
        (_state_to_kernel(state_ret[0]), _state_to_kernel(state_gla[0])), weights, _tri_consts(TILE_ROWS),
        n_chunks=TILE_CHUNKS, route=True)

    n_tiles = route.shape[0]
    n_blocks = n_tiles * LOCAL_UNITS // BLOCK_UNITS + N_EXPERTS
    assert 2 * n_tiles * TILE_ROWS // MOE_BLOCK >= 3
    src, blk = _plan_call(ku[:, 0, :N_EXPERTS], n_blocks)
    yloc = _experts_call(src, blk, xloc, w_exp_gate[0], w_exp_up[0], w_exp_down[0], n_blocks)
    y_p, y_s = _combine_call(route, xnew, yloc, g_final.reshape(1, D_MODEL), TILE_ROWS, n_b * seq // TILE_ROWS)

    return (y_p.reshape(n_b, seq, D_MODEL), y_s.reshape(n_s, dec_seq, D_MODEL),
            _state_from_kernel(ret_p), _state_from_kernel(gla_p), _state_from_kernel(ret_s), _state_from_kernel(gla_s))
```

```python
import functools
import math

import jax
import jax.numpy as jnp
import numpy as np
from jax import lax
from jax.experimental import pallas as pl
from jax.experimental.pallas import tpu as pltpu

D_MODEL = 1024
CHUNK = 64
PAST_LEN = 1024
N_META = 16
N_HEADS = 4
HEAD_K = 64
HEAD_V = 128
HK = N_HEADS * HEAD_K
HV = N_HEADS * HEAD_V
GLA_RANK = 16
GATE_NORM = 16.0
ROPE_BASE = 10000.0
N_GROUPS = 4
EXPERTS_PER_GROUP = 8
N_EXPERTS = N_GROUPS * EXPERTS_PER_GROUP
D_EXPERT = 512
NORM_EPS = 1e-6

LANES = 128
SUBLANES = 8
TILE_CHUNKS = 8
TILE_ROWS = TILE_CHUNKS * CHUNK
ROUTE_ROWS = 8
ROUTER_LANES = 128
UNIT = SUBLANES
MOE_BLOCK = 512
BLOCK_UNITS = MOE_BLOCK // UNIT
LOCAL_ROWS = 2 * TILE_ROWS + N_EXPERTS * UNIT
LOCAL_UNITS = LOCAL_ROWS // UNIT
PLAN_FAST = 8
VMEM_LIMIT = 56 * 1024 * 1024

C_RQ, C_RK, C_RV, C_RG = 0, 256, 512, 1024
C_GQ, C_GK, C_GV, C_GG = 1536, 1792, 2048, 2560
C_ZR, C_ZG, C_LR = 3072, 4096, 5120
W_CAT = 5248

F32 = jnp.float32
BF16 = jnp.bfloat16
LOG_G = tuple(math.log1p(-(2.0 ** (-5.0 - h))) for h in range(N_HEADS))


def _dot(a, b):
    return jnp.dot(a, b, preferred_element_type=F32)


def _dot_nt(a, b):
    return lax.dot_general(a, b, (((1,), (1,)), ((), ())), preferred_element_type=F32)


def _dot_tn(a, b):
    return lax.dot_general(a, b, (((0,), (0,)), ((), ())), preferred_element_type=F32)


def _sigmoid(x):
    return 1.0 / (1.0 + jnp.exp(-x))


def _pack_halves(x):
    half = x.shape[1] // 2
    bits = pltpu.bitcast(x, jnp.int32)
    return lax.shift_right_logical(bits[:, :half], 16) | (bits[:, half:] & jnp.int32(-65536))


def _unpack_halves(w):
    lo = pltpu.bitcast(lax.shift_left(w, 16), F32)
    hi = pltpu.bitcast(w & jnp.int32(-65536), F32)
    return jnp.concatenate([lo, hi], axis=1).astype(BF16)


def _head_of_lane(shape, width):
    return lax.broadcasted_iota(jnp.int32, shape, len(shape) - 1) >> int(math.log2(width))


def _per_head_lane_const(vals, shape, width):
    hd = _head_of_lane(shape, width)
    out = jnp.full(shape, vals[N_HEADS - 1], F32)
    for h in range(N_HEADS - 2, -1, -1):
        out = jnp.where(hd == h, vals[h], out)
    return out


def _tables_kernel(inv_ref, cp_ref, sp_ref, cm_ref, sm_ref, *, seq, past_len):
    inv = inv_ref[...]

    def fill(c_ref, s_ref, rows, row0, pos_of_row):
        r = lax.broadcasted_iota(jnp.int32, (rows, LANES), 0)
        ang = pos_of_row(r).astype(F32) * inv
        lane = lax.broadcasted_iota(jnp.int32, (rows, LANES), 1)
        first_half = (lane & (HEAD_K - 1)) < (HEAD_K // 2)
        c_ref[pl.ds(row0, rows), :] = jnp.cos(ang)
        s = jnp.sin(ang)
        s_ref[pl.ds(row0, rows), :] = jnp.where(first_half, -s, s)

    fill(cp_ref, sp_ref, seq, 0, lambda r: r)
    fill(cp_ref, sp_ref, TILE_ROWS, seq, lambda r: past_len + (r & (CHUNK - 1)))
    fill(cm_ref, sm_ref, CHUNK, 0, lambda r: r - CHUNK)


def _rotary_tables(seq, past_len):
    half = HEAD_K // 2
    inv = ROPE_BASE ** (-2.0 * jnp.arange(half, dtype=F32) / HEAD_K)
    inv = jnp.tile(inv, LANES // half).reshape(1, LANES)
    shp = lambda r: jax.ShapeDtypeStruct((r, LANES), F32)
    return pl.pallas_call(
        functools.partial(_tables_kernel, seq=seq, past_len=past_len),
        out_shape=(shp(seq + TILE_ROWS), shp(seq + TILE_ROWS), shp(CHUNK), shp(CHUNK)),
        compiler_params=pltpu.CompilerParams(vmem_limit_bytes=VMEM_LIMIT),
        name="rotary_tables",
    )(inv)


def _mixer_kernel(*refs, n_chunks, tiles_p, tiles_per_row, route):
    tm = n_chunks * CHUNK
    if tiles_p:
        xp_ref, refs = refs[0], refs[1:]
    (xs_ref, cos_ref, sin_ref, spi_r_ref, spi_g_ref, ssi_r_ref, ssi_g_ref,
     gmix_ref, wcat_ref, wgk_ref, bgk_ref, gnorm_ref, wbr_ref, wbg_ref, wout_ref,
     gffn_ref, wrt_ref, brt_ref, bdtri_ref, ltri_ref,
     xnew_ref, xloc_ref, route_ref, ku_ref, spo_r_ref, spo_g_ref, sso_r_ref, sso_g_ref,
     qb, qdb, kb, kkb, vb, gqb, gkkb, gvb, ga, o_ret, o_gla, st_ret, st_gla) = refs

    i = pl.program_id(0)
    if tiles_p:
        is_s = i >= tiles_p
        t_idx = jnp.minimum(i, tiles_p - 1) % tiles_per_row
        x = jnp.where(is_s, xs_ref[...].reshape(tm, D_MODEL), xp_ref[...].reshape(tm, D_MODEL))

        @pl.when(jnp.logical_and(jnp.logical_not(is_s), t_idx == 0))
        def _():
            st_ret[...] = spi_r_ref[0]
            st_gla[...] = spi_g_ref[0]
    else:
        is_s = None
        x = xs_ref[...].reshape(tm, D_MODEL)

    h = x * lax.rsqrt(jnp.mean(x * x, axis=-1, keepdims=True) + NORM_EPS) * gmix_ref[...]
    hb = h.astype(BF16)

    def proj(c0, width):
        return _dot(hb, wcat_ref[:, c0:c0 + width])

    cos3 = jnp.concatenate([cos_ref[...]] * 2, axis=1).reshape(n_chunks, CHUNK, HK)
    sin3 = jnp.concatenate([sin_ref[...]] * 2, axis=1).reshape(n_chunks, CHUNK, HK)
    lane_hk = lax.broadcasted_iota(jnp.int32, (tm, HK), 1)
    first_half = (lane_hk & (HEAD_K - 1)) < (HEAD_K // 2)

    def rotary(t):
        swapped = jnp.where(first_half, pltpu.roll(t, HK - HEAD_K // 2, 1), pltpu.roll(t, HEAD_K // 2, 1))
        t3 = t.reshape(n_chunks, CHUNK, HK)
        return (t3 * cos3 + swapped.reshape(n_chunks, CHUNK, HK) * sin3).reshape(tm, HK)

    logg_hk = _per_head_lane_const(LOG_G, (CHUNK, HK), HEAD_K)
    l_idx = lax.broadcasted_iota(jnp.int32, (CHUNK, HK), 0).astype(F32)
    qdec = jnp.exp((l_idx + 1.0) * logg_hk)
    kdec = jnp.exp((CHUNK - 1.0 - l_idx) * logg_hk)
    cdec = jnp.exp(float(CHUNK) * _per_head_lane_const(LOG_G, (1, HK), HEAD_K))
    r_idx = lax.broadcasted_iota(jnp.int32, (N_HEADS * CHUNK, CHUNK), 0)
    m_idx = lax.broadcasted_iota(jnp.int32, (N_HEADS * CHUNK, CHUNK), 1)
    logg_rows = jnp.full((N_HEADS * CHUNK, CHUNK), LOG_G[N_HEADS - 1], F32)
    for hh in range(N_HEADS - 2, -1, -1):
        logg_rows = jnp.where((r_idx >> 6) == hh, LOG_G[hh], logg_rows)
    dmat = jnp.exp(jnp.abs((r_idx & (CHUNK - 1)) - m_idx).astype(F32) * logg_rows)

    glr = proj(C_LR, LANES)
    rq = rotary(proj(C_RQ, HK))
    qb[...] = rq.astype(BF16)
    qdb[...] = (rq.reshape(n_chunks, CHUNK, HK) * qdec).reshape(tm, HK).astype(BF16)
    gl = _dot(glr.astype(BF16), wgk_ref[...]) + bgk_ref[...]
    rk = rotary(proj(C_RK, HK)) * (HEAD_K ** -0.5)
    kb[...] = rk.astype(BF16)
    kkb[...] = (rk.reshape(n_chunks, CHUNK, HK) * kdec).reshape(tm, HK).astype(BF16)
    log_a = (jnp.minimum(gl, 0.0) - jnp.log1p(jnp.exp(-jnp.abs(gl)))) / GATE_NORM
    la_hi = log_a.astype(BF16)
    la_lo = (log_a - la_hi.astype(F32)).astype(BF16)
    vb[...] = proj(C_RV, HV).astype(BF16)
    bdtri = bdtri_ref[...]
    bcum = _dot(bdtri, la_hi) + _dot(bdtri, la_lo)
    gqb[...] = (proj(C_GQ, HK) * (HEAD_K ** -0.5)).astype(BF16)
    gvb[...] = proj(C_GV, HV).astype(BF16)
    gk = proj(C_GK, HK)
    b3 = bcum.reshape(n_chunks, CHUNK, HK)
    bl3 = b3[:, CHUNK - 1:CHUNK, :]
    gkkb[...] = (gk.reshape(n_chunks, CHUNK, HK) * jnp.exp(bl3 - b3)).reshape(tm, HK).astype(BF16)
    ga[...] = jnp.broadcast_to(jnp.exp(bl3), (n_chunks, SUBLANES, HK))

    def stack_masked(a, width):
        head = _head_of_lane(a.shape, width)
        zero = jnp.zeros_like(a)
        return jnp.concatenate([jnp.where(head == hh, a, zero) for hh in range(N_HEADS)], axis=0)

    def heads_to_rows(a):
        return jnp.concatenate([a[:, hh * HEAD_V:(hh + 1) * HEAD_V] for hh in range(N_HEADS)], axis=0)

    def rows_to_heads(a):
        return jnp.concatenate([a[hh * CHUNK:(hh + 1) * CHUNK, :] for hh in range(N_HEADS)], axis=1)

    chunk_rows = [slice(c * CHUNK, (c + 1) * CHUNK) for c in range(n_chunks)]
    probs = [(_dot_nt(stack_masked(qb[r, :], HEAD_K), kb[r, :]) * dmat).astype(BF16) for r in chunk_rows]
    inc_ret = [_dot_tn(heads_to_rows(vb[r, :]), stack_masked(kkb[r, :], HEAD_K)) for r in chunk_rows]
    inc_gla = [_dot_tn(heads_to_rows(gvb[r, :]), stack_masked(gkkb[r, :], HEAD_K)) for r in chunk_rows]
    rg = proj(C_RG, HV)
    gg = proj(C_GG, HV)
    s_in, g_out = [], []
    s_cur, g_cur = st_ret[...], st_gla[...]
    for c in range(n_chunks):
        if is_s is None:
            s_cur, g_cur = ssi_r_ref[c], ssi_g_ref[c]
        else:
            s_cur = jnp.where(is_s, ssi_r_ref[c], s_cur)
            g_cur = jnp.where(is_s, ssi_g_ref[c], g_cur)
        s_in.append(s_cur.astype(BF16))
        s_cur = s_cur * cdec + inc_ret[c]
        g_cur = g_cur * ga[c][0:1, :] + inc_gla[c]
        g_out.append(g_cur.astype(BF16))
        sso_r_ref[c] = s_cur
        sso_g_ref[c] = g_cur
    st_ret[...] = s_cur
    st_gla[...] = g_cur
    for c, r in enumerate(chunk_rows):
        v = vb[r, :]
        intra = jnp.concatenate(
            [_dot(probs[c][hh * CHUNK:(hh + 1) * CHUNK, :], v[:, hh * HEAD_V:(hh + 1) * HEAD_V])
             for hh in range(N_HEADS)], axis=1)
        inter = rows_to_heads(_dot_nt(stack_masked(qdb[r, :], HEAD_K), s_in[c]))
        o_ret[r, :] = intra + inter
        o_gla[r, :] = rows_to_heads(_dot_nt(stack_masked(gqb[r, :], HEAD_K), g_out[c]))

    gnorm = gnorm_ref[...]
    orr = o_ret[...]
    ogg = o_gla[...]
    ret_parts, gla_parts = [], []
    for hh in range(N_HEADS):
        sl = slice(hh * HEAD_V, (hh + 1) * HEAD_V)
        oh = orr[:, sl]
        mu = jnp.mean(oh, axis=-1, keepdims=True)
        dev = oh - mu
        var = jnp.mean(dev * dev, axis=-1, keepdims=True)
        ret_parts.append(dev * lax.rsqrt(var + NORM_EPS))
        og = ogg[:, sl]
        gla_parts.append(og * lax.rsqrt(jnp.mean(og * og, axis=-1, keepdims=True) + NORM_EPS) * gnorm)
    o_r = jnp.concatenate(ret_parts, axis=1) * (rg * _sigmoid(rg))
    o_g = jnp.concatenate(gla_parts, axis=1) * (gg * _sigmoid(gg))
    merged = (_sigmoid(proj(C_ZR, D_MODEL)) * _dot(o_r.astype(BF16), wbr_ref[...])
              + _sigmoid(proj(C_ZG, D_MODEL)) * _dot(o_g.astype(BF16), wbg_ref[...]))
    xn = x + _dot(merged.astype(BF16), wout_ref[...])
    xnew_ref[...] = xn

    if route:
        h2 = xn * lax.rsqrt(jnp.mean(xn * xn, axis=-1, keepdims=True) + NORM_EPS) * gffn_ref[...]
        h2_hi = h2.astype(BF16)
        h2_lo = (h2 - h2_hi.astype(F32)).astype(BF16)
        hi_both = _dot(h2_hi, wrt_ref[...])
        logits = (hi_both[:, :ROUTER_LANES] + _dot(h2_lo, wrt_ref[:, :ROUTER_LANES])
                  + hi_both[:, ROUTER_LANES:]) + brt_ref[...]
        lt = logits.T
        row8 = lax.broadcasted_iota(jnp.int32, (SUBLANES, tm), 0)
        neg_inf = jnp.float32(-jnp.inf)
        glog = jnp.where(row8 < N_GROUPS, lt[0:SUBLANES, :], neg_inf)
        gmax = jnp.max(glog, axis=0, keepdims=True)
        grp = jnp.min(jnp.where(glog == gmax, row8, SUBLANES), axis=0, keepdims=True)
        p_grp = 1.0 / jnp.sum(jnp.exp(glog - gmax), axis=0, keepdims=True)
        le = jnp.zeros((SUBLANES, tm), F32)
        for g in range(N_GROUPS):
            le = jnp.where(grp == g, lt[SUBLANES * (g + 1):SUBLANES * (g + 2), :], le)
        m1 = jnp.max(le, axis=0, keepdims=True)
        i1 = jnp.min(jnp.where(le == m1, row8, SUBLANES), axis=0, keepdims=True)
        le2 = jnp.where(row8 == i1, neg_inf, le)
        m2 = jnp.max(le2, axis=0, keepdims=True)
        i2 = jnp.min(jnp.where(le2 == m2, row8, SUBLANES), axis=0, keepdims=True)
        e0 = grp * EXPERTS_PER_GROUP + i1
        e1 = grp * EXPERTS_PER_GROUP + i2
        t21 = jnp.exp(m2 - m1)
        w0 = p_grp / (1.0 + t21)
        w1 = p_grp * t21 / (1.0 + t21)

        erow = lax.broadcasted_iota(jnp.int32, (N_EXPERTS, tm), 0)
        hit0 = erow == e0
        hit1 = erow == e1
        onehot = jnp.where(jnp.logical_or(hit0, hit1), 1.0, 0.0).astype(BF16)
        cum = _dot_nt(onehot, ltri_ref[...])
        n_run = cum[:, tm - 1:tm]
        n_pad = jnp.ceil(n_run / UNIT) * UNIT
        rank0 = jnp.sum(jnp.where(hit0, cum - 1.0, 0.0), axis=0, keepdims=True)
        rank1 = jnp.sum(jnp.where(hit1, cum - 1.0, 0.0), axis=0, keepdims=True)
        start0 = jnp.sum(jnp.where(erow < e0, n_pad, 0.0), axis=0, keepdims=True)
        start1 = jnp.sum(jnp.where(erow < e1, n_pad, 0.0), axis=0, keepdims=True)
        ld0 = (start0 + rank0).astype(jnp.int32)
        ld1 = (start1 + rank1).astype(jnp.int32)
        lrow = lax.broadcasted_iota(jnp.int32, (LOCAL_ROWS, tm), 0)
        perm = jnp.where(jnp.logical_or(lrow == ld0, lrow == ld1), 1.0, 0.0).astype(BF16)
        xloc_ref[...] = _pack_halves(_dot(perm, h2_hi))
        lane_e = lax.broadcasted_iota(jnp.int32, (N_EXPERTS, LANES), 1)
        erow_l = lax.broadcasted_iota(jnp.int32, (N_EXPERTS, LANES), 0)
        units_row = jnp.sum(jnp.where(erow_l == lane_e, n_pad / UNIT, 0.0), axis=0, keepdims=True)
        ku_ref[...] = jnp.broadcast_to(units_row, (SUBLANES, LANES)).astype(jnp.int32).reshape(ku_ref.shape)
        zero_row = jnp.zeros((1, tm), jnp.int32)
        rec = jnp.concatenate([e0, e1, ld0, ld1, pltpu.bitcast(w0, jnp.int32), pltpu.bitcast(w1, jnp.int32),
                               zero_row, zero_row], axis=0)
        route_ref[...] = rec.reshape(route_ref.shape)
    else:
        xloc_ref[...] = jnp.zeros(xloc_ref.shape, jnp.int32)
        route_ref[...] = jnp.zeros(route_ref.shape, jnp.int32)
        ku_ref[...] = jnp.zeros(ku_ref.shape, jnp.int32)

    if tiles_p:
        @pl.when(jnp.logical_and(jnp.logical_not(is_s), t_idx == tiles_per_row - 1))
        def _():
            spo_r_ref[0] = st_ret[...]
            spo_g_ref[0] = st_gla[...]
    else:
        spo_r_ref[0] = st_ret[...]
        spo_g_ref[0] = st_gla[...]


def _const_spec(shape):
    nd = len(shape)
    return pl.BlockSpec(shape, lambda *_: (0,) * nd, pipeline_mode=pl.Buffered(1))


def _mixer_call(x_prompt, x_streams, cos, sin, st_prompt, st_streams, weights, consts, *, n_chunks, route):
    tm = n_chunks * CHUNK
    n_streams = x_streams.shape[0]
    tiles_s = n_streams // n_chunks
    if x_prompt is not None:
        n_rows, n_seq, _ = x_prompt.shape
        tpr = n_seq // tm
        tiles_p = n_rows * tpr
    else:
        n_rows, tpr, tiles_p = 1, 1, 0
    n_tiles = tiles_p + tiles_s
    p_idx = lambda i: jnp.minimum(i, tiles_p - 1)
    s_idx = lambda i: jnp.maximum(i - tiles_p, 0)

    st_blk = (1, HEAD_V, HK)
    in_specs, args = [], []
    if tiles_p:
        in_specs.append(pl.BlockSpec((1, tm, D_MODEL), lambda i: (p_idx(i) // tpr, p_idx(i) % tpr, 0)))
        args.append(x_prompt)
        cos_spec = pl.BlockSpec((tm, LANES), lambda i: (jnp.where(i < tiles_p, i % tpr, tpr), 0))
    else:
        cos_spec = pl.BlockSpec((tm, LANES), lambda i: (0, 0))
    once = pl.Buffered(1)
    stream_in = pl.BlockSpec((n_chunks, HEAD_V, HK), lambda i: (s_idx(i), 0, 0), pipeline_mode=once)
    in_specs += [pl.BlockSpec((n_chunks, CHUNK, D_MODEL), lambda i: (s_idx(i), 0, 0), pipeline_mode=once),
                 cos_spec, cos_spec, _const_spec(st_blk), _const_spec(st_blk), stream_in, stream_in]
    args += [x_streams, cos, sin, st_prompt[0], st_prompt[1], st_streams[0], st_streams[1]]
    in_specs += [_const_spec(w.shape) for w in weights] + [_const_spec(c.shape) for c in consts]
    args += list(weights) + list(consts)

    out_shape = (jax.ShapeDtypeStruct((n_tiles * tm, D_MODEL), F32),
                 jax.ShapeDtypeStruct((n_tiles * LOCAL_ROWS, D_MODEL // 2), jnp.int32),
                 jax.ShapeDtypeStruct((n_tiles, ROUTE_ROWS, tm), jnp.int32),
                 jax.ShapeDtypeStruct((n_tiles, SUBLANES, LANES), jnp.int32),
                 jax.ShapeDtypeStruct((n_rows, HEAD_V, HK), F32), jax.ShapeDtypeStruct((n_rows, HEAD_V, HK), F32),
                 jax.ShapeDtypeStruct((n_streams, HEAD_V, HK), F32),
                 jax.ShapeDtypeStruct((n_streams, HEAD_V, HK), F32))
    row_spec = pl.BlockSpec(st_blk, lambda i: (p_idx(i) // tpr if tiles_p else 0, 0, 0))
    stream_spec = pl.BlockSpec((n_chunks, HEAD_V, HK), lambda i: (s_idx(i), 0, 0))
    out_specs = (pl.BlockSpec((tm, D_MODEL), lambda i: (i, 0)),
                 pl.BlockSpec((LOCAL_ROWS, D_MODEL // 2), lambda i: (i, 0)),
                 pl.BlockSpec((1, ROUTE_ROWS, tm), lambda i: (i, 0, 0)),
                 pl.BlockSpec((1, SUBLANES, LANES), lambda i: (i, 0, 0)),
                 row_spec, row_spec, stream_spec, stream_spec)

    scratch = [pltpu.VMEM((tm, HK), BF16), pltpu.VMEM((tm, HK), BF16), pltpu.VMEM((tm, HK), BF16),
               pltpu.VMEM((tm, HK), BF16), pltpu.VMEM((tm, HV), BF16),
               pltpu.VMEM((tm, HK), BF16), pltpu.VMEM((tm, HK), BF16), pltpu.VMEM((tm, HV), BF16),
               pltpu.VMEM((n_chunks, SUBLANES, HK), F32),
               pltpu.VMEM((tm, HV), F32), pltpu.VMEM((tm, HV), F32),
               pltpu.VMEM((HEAD_V, HK), F32), pltpu.VMEM((HEAD_V, HK), F32)]

    return pl.pallas_call(
        functools.partial(_mixer_kernel, n_chunks=n_chunks, tiles_p=tiles_p, tiles_per_row=tpr, route=route),
        grid=(n_tiles,), in_specs=in_specs, out_specs=out_specs, out_shape=out_shape, scratch_shapes=scratch,
        compiler_params=pltpu.CompilerParams(dimension_semantics=("arbitrary",), vmem_limit_bytes=VMEM_LIMIT),
        name="mixer" if route else "mixer_meta",
    )(*args)


def _plan_kernel(ku_ref, src_ref, blk_ref, next_unit, *, n_tiles, n_blocks):
    shift = int(math.log2(BLOCK_UNITS))

    def walk(fast):
        def init_tile(t, c):
            next_unit[t] = t * LOCAL_UNITS
            return c

        lax.fori_loop(0, n_tiles, init_tile, 0)

        def init_block(b, c):
            blk_ref[0, b] = N_EXPERTS - 1
            blk_ref[1, b] = 0
            blk_ref[2, b] = 0
            return c

        lax.fori_loop(0, n_blocks, init_block, 0)

        def per_expert(e, carry):
            g0, over0 = carry

            def per_tile(t, c):
                g, over = c
                k = ku_ref[t, e]
                base = next_unit[t]
                if fast:
                    for j in range(PLAN_FAST):
                        src_ref[g + j] = base + j
                    over = jnp.maximum(over, (k > PLAN_FAST).astype(jnp.int32))
                else:
                    def per_unit(j, cc):
                        src_ref[g + j] = base + j
                        return cc

                    lax.fori_loop(0, k, per_unit, 0)
                next_unit[t] = base + k
                return g + k, over

            g_end, over1 = lax.fori_loop(0, n_tiles, per_tile, (g0, over0), unroll=4 if fast else 1)
            g_pad = ((g_end + (BLOCK_UNITS - 1)) >> shift) << shift

            def pad_unit(j, c):
                src_ref[j] = src_ref[g_pad - BLOCK_UNITS]
                return c

            lax.fori_loop(g_end, g_pad, pad_unit, 0)

            def set_block(b, c):
                blk_ref[0, b] = e
                blk_ref[1, b] = jnp.minimum(g_end - (b << shift), BLOCK_UNITS)
                return c

            lax.fori_loop(g0 >> shift, g_pad >> shift, set_block, 0)
            return g_pad, over1

        g_total, over = lax.fori_loop(0, N_EXPERTS, per_expert, (0, 0))

        def tail_unit(j, c):
            src_ref[j] = 0
            return c

        lax.fori_loop(g_total, n_blocks * BLOCK_UNITS + PLAN_FAST, tail_unit, 0)
        blk_ref[2, 0] = g_total >> shift
        return over

    overflow = walk(True)

    @pl.when(overflow > 0)
    def _():
        walk(False)


def _plan_call(ku, n_blocks):
    n_tiles = ku.shape[0]
    smem = pl.BlockSpec(memory_space=pltpu.SMEM)
    return pl.pallas_call(
        functools.partial(_plan_kernel, n_tiles=n_tiles, n_blocks=n_blocks),
        in_specs=[smem], out_specs=(smem, smem),
        out_shape=(jax.ShapeDtypeStruct((n_blocks * BLOCK_UNITS + PLAN_FAST,), jnp.int32),
                   jax.ShapeDtypeStruct((3, n_blocks), jnp.int32)),
        scratch_shapes=[pltpu.SMEM((n_tiles,), jnp.int32)],
        name="moe_plan",
    )(ku)


def _experts_kernel(src_ref, blk_ref, x_hbm, wg_ref, wu_ref, wd_ref, y_hbm,
                    xin, yout, wgub, wdb, sem_in, sem_out):
    b = pl.program_id(0)
    used = blk_ref[2, 0]
    slot = b & 1

    def unit_rows(blk, u):
        return pl.ds(pl.multiple_of(src_ref[blk * BLOCK_UNITS + u] * UNIT, UNIT), UNIT)

    def in_copy(blk, u, s):
        return pltpu.make_async_copy(x_hbm.at[unit_rows(blk, u), :], xin.at[s, pl.ds(u * UNIT, UNIT), :],
                                     sem_in.at[s])

    def out_copy(blk, u, s):
        return pltpu.make_async_copy(yout.at[s, pl.ds(u * UNIT, UNIT), :], y_hbm.at[unit_rows(blk, u), :],
                                     sem_out.at[s])

    @pl.when(jnp.logical_and(b == 0, used > 0))
    def _():
        for u in range(BLOCK_UNITS):
            in_copy(0, u, 0).start()

    @pl.when(b < used)
    def _():
        expert = blk_ref[0, b]
        changed = jnp.logical_or(b == 0, expert != blk_ref[0, jnp.maximum(b - 1, 0)])

        @pl.when(changed)
        def _():
            wgub[:, :D_EXPERT] = wg_ref[0].astype(BF16)
            wgub[:, D_EXPERT:] = wu_ref[0].astype(BF16)
            wdb[...] = wd_ref[0].astype(BF16)

        @pl.when(b + 1 < used)
        def _():
            for u in range(BLOCK_UNITS):
                in_copy(b + 1, u, 1 - slot).start()

        for u in range(BLOCK_UNITS):
            in_copy(b, u, slot).wait()

        xb = _unpack_halves(xin[slot])
        gate_up = _dot(xb, wgub[...])
        gate = gate_up[:, :D_EXPERT]
        hid = (gate * _sigmoid(gate)) * gate_up[:, D_EXPERT:]
        yout[slot] = _pack_halves(_dot(hid.astype(BF16), wdb[...]).astype(BF16).astype(F32))

        n_real = blk_ref[1, b]

        @pl.when(n_real == BLOCK_UNITS)
        def _():
            for u in range(BLOCK_UNITS):
                out_copy(b, u, slot).start()

        @pl.when(n_real < BLOCK_UNITS)
        def _():
            lax.fori_loop(0, n_real, lambda u, c: (out_copy(b, u, slot).start(), c)[1], 0)

        @pl.when(b > 0)
        def _():
            lax.fori_loop(0, blk_ref[1, b - 1], lambda u, c: (out_copy(b - 1, u, 1 - slot).wait(), c)[1], 0)

        @pl.when(b == used - 1)
        def _():
            lax.fori_loop(0, n_real, lambda u, c: (out_copy(b, u, slot).wait(), c)[1], 0)


def _experts_call(src, blk, xloc, wg, wu, wd, n_blocks):
    w_in_spec = pl.BlockSpec((1, D_MODEL, D_EXPERT), lambda b, s, m: (m[0, b], 0, 0))
    grid_spec = pltpu.PrefetchScalarGridSpec(
        num_scalar_prefetch=2,
        grid=(n_blocks,),
        in_specs=[pl.BlockSpec(memory_space=pl.ANY), w_in_spec, w_in_spec,
                  pl.BlockSpec((1, D_EXPERT, D_MODEL), lambda b, s, m: (m[0, b], 0, 0))],
        out_specs=pl.BlockSpec(memory_space=pl.ANY),
        scratch_shapes=[pltpu.VMEM((2, MOE_BLOCK, D_MODEL // 2), jnp.int32),
                        pltpu.VMEM((2, MOE_BLOCK, D_MODEL // 2), jnp.int32),
                        pltpu.VMEM((D_MODEL, 2 * D_EXPERT), BF16), pltpu.VMEM((D_EXPERT, D_MODEL), BF16),
                        pltpu.SemaphoreType.DMA((2,)), pltpu.SemaphoreType.DMA((2,))],
    )
    return pl.pallas_call(
        _experts_kernel, grid_spec=grid_spec,
        out_shape=jax.ShapeDtypeStruct(xloc.shape, jnp.int32),
        input_output_aliases={2: 0},
        compiler_params=pltpu.CompilerParams(dimension_semantics=("arbitrary",), vmem_limit_bytes=VMEM_LIMIT),
        name="moe_experts",
    )(src, blk, xloc, wg, wu, wd)


def _combine_kernel(route_ref, x_ref, y_ref, gfin_ref, out_a_ref, out_b_ref, *, tm, tiles_a):
    t = pl.program_id(0)
    rec = route_ref[0]
    recf = jnp.concatenate([rec[2:4, :].astype(F32), pltpu.bitcast(rec, F32)[4:6, :],
                            jnp.zeros((LANES - 4, tm), F32)], axis=0)
    cols = recf.T
    lrow = lax.broadcasted_iota(jnp.int32, (tm, LOCAL_ROWS), 1).astype(F32)
    select = jnp.where(lrow == cols[:, 0:1], cols[:, 2:3], 0.0) + jnp.where(lrow == cols[:, 1:2], cols[:, 3:4], 0.0)
    xf = x_ref[...] + _dot(select.astype(BF16), _unpack_halves(y_ref[...]))
    out = xf * lax.rsqrt(jnp.mean(xf * xf, axis=-1, keepdims=True) + NORM_EPS) * gfin_ref[...]

    @pl.when(t < tiles_a)
    def _():
        out_a_ref[...] = out

    @pl.when(t >= tiles_a)
    def _():
        out_b_ref[...] = out


def _combine_call(route, xnew, yloc, gfin, tm, tiles_a):
    n_tiles = route.shape[0]
    tiles_b = n_tiles - tiles_a
    return pl.pallas_call(
        functools.partial(_combine_kernel, tm=tm, tiles_a=tiles_a),
        grid=(n_tiles,),
        in_specs=[pl.BlockSpec((1, ROUTE_ROWS, tm), lambda t: (t, 0, 0)),
                  pl.BlockSpec((tm, D_MODEL), lambda t: (t, 0)),
                  pl.BlockSpec((LOCAL_ROWS, D_MODEL // 2), lambda t: (t, 0)),
                  pl.BlockSpec((1, D_MODEL), lambda t: (0, 0))],
        out_specs=(pl.BlockSpec((tm, D_MODEL), lambda t: (jnp.minimum(t, tiles_a - 1), 0)),
                   pl.BlockSpec((tm, D_MODEL), lambda t: (jnp.maximum(t - tiles_a, 0), 0))),
        out_shape=(jax.ShapeDtypeStruct((tiles_a * tm, D_MODEL), F32),
                   jax.ShapeDtypeStruct((tiles_b * tm, D_MODEL), F32)),
        compiler_params=pltpu.CompilerParams(dimension_semantics=("arbitrary",), vmem_limit_bytes=VMEM_LIMIT),
        name="moe_combine",
    )(route, xnew, yloc, gfin)


def _prep_weights(g_mix, w_in, w_gla_gk2, b_gla_gk, g_gla_norm, w_br_ret, w_br_gla, w_out, g_ffn,
                  w_rg, b_rg, w_re, b_re):
    offs = np.cumsum((0, HK, HK, HV, HV, HK, HK, HV, HV, GLA_RANK, D_MODEL, D_MODEL))
    seg = lambda i: w_in[:, offs[i]:offs[i + 1]]
    lr = jnp.pad(seg(8), ((0, 0), (0, LANES - GLA_RANK)))
    wcat = jnp.concatenate([seg(0), seg(1), seg(2), seg(3), seg(4), seg(5), seg(6), seg(7), seg(9), seg(10), lr],
                           axis=1).astype(BF16)
    wgk = jnp.pad(w_gla_gk2, ((0, LANES - GLA_RANK), (0, 0))).astype(BF16)
    wrt = jnp.zeros((D_MODEL, ROUTER_LANES), F32)
    wrt = wrt.at[:, 0:N_GROUPS].set(w_rg).at[:, SUBLANES:SUBLANES + N_EXPERTS].set(w_re)
    brt = jnp.zeros((1, ROUTER_LANES), F32)
    brt = brt.at[0, 0:N_GROUPS].set(b_rg).at[0, SUBLANES:SUBLANES + N_EXPERTS].set(b_re)
    wrt_hi = wrt.astype(BF16)
    wrt = jnp.concatenate([wrt_hi, (wrt - wrt_hi.astype(F32)).astype(BF16)], axis=1)
    return (g_mix.reshape(1, D_MODEL), wcat, wgk, b_gla_gk.reshape(1, HK), g_gla_norm.reshape(1, HEAD_V),
            w_br_ret.astype(BF16), w_br_gla.astype(BF16), w_out.astype(BF16), g_ffn.reshape(1, D_MODEL), wrt, brt)


def _tri_consts(tm):
    r = np.arange(tm)
    low = (r[None, :] <= r[:, None])
    bd = low & ((r[None, :] // CHUNK) == (r[:, None] // CHUNK))
    return jnp.asarray(bd, BF16), jnp.asarray(low, BF16)


def _state_to_kernel(s):
    return jnp.swapaxes(s.reshape(s.shape[0], HK, HEAD_V), 1, 2)


def _state_from_kernel(s):
    return jnp.swapaxes(s, 1, 2).reshape(1, s.shape[0], N_HEADS, HEAD_K, HEAD_V)


def kernel(x_prompt, x_sample, state_ret, state_gla, meta_tokens, g_mix, w_in, w_gla_gk2, b_gla_gk, g_gla_norm, w_br_ret, w_br_gla, w_out, g_ffn, w_router_group, b_router_group, w_router_expert, b_router_expert, w_exp_gate, w_exp_up, w_exp_down, g_final):
    n_b, seq, _ = x_prompt.shape
    n_s, dec_seq, _ = x_sample.shape
    depth = state_ret.shape[0]
    assert depth == 1 and dec_seq == CHUNK and seq % TILE_ROWS == 0 and n_s % TILE_CHUNKS == 0
    cos_t, sin_t, cos_m, sin_m = _rotary_tables(seq, PAST_LEN)
    weights = _prep_weights(g_mix[0], w_in[0], w_gla_gk2[0], b_gla_gk[0], g_gla_norm[0], w_br_ret[0], w_br_gla[0],
                            w_out[0], g_ffn[0], w_router_group[0], b_router_group[0], w_router_expert[0],
                            b_router_expert[0])
    zero_state = jnp.zeros((1, HEAD_V, HK), F32)

    x_meta = jnp.concatenate([jnp.zeros((CHUNK - N_META, D_MODEL), F32), meta_tokens.astype(F32)], axis=0)
    meta_out = _mixer_call(None, x_meta.reshape(1, CHUNK, D_MODEL), cos_m, sin_m, (zero_state, zero_state),
                           (zero_state, zero_state), weights, _tri_consts(CHUNK), n_chunks=1, route=False)
    meta_ret, meta_gla = meta_out[6], meta_out[7]

    xnew, xloc, route, ku, ret_p, gla_p, ret_s, gla_s = _mixer_call(
        x_prompt, x_sample, cos_t, sin_t, (meta_ret, meta_gla),
        (_state_to_kernel(state_ret[0]), _state_to_kernel(state_gla[0])), weights, _tri_consts(TILE_ROWS),
        n_chunks=TILE_CHUNKS, route=True)

    n_tiles = route.shape[0]
    n_blocks = n_tiles * LOCAL_UNITS // BLOCK_UNITS + N_EXPERTS
    src, blk = _plan_call(ku[:, 0, :N_EXPERTS], n_blocks)
    yloc = _experts_call(src, blk, xloc, w_exp_gate[0], w_exp_up[0], w_exp_down[0], n_blocks)
    y_p, y_s = _combine_call(route, xnew, yloc, g_final.reshape(1, D_MODEL), TILE_ROWS, n_b * seq // TILE_ROWS)

    return (y_p.reshape(n_b, seq, D_MODEL), y_s.reshape(n_s, dec_seq, D_MODEL),
            _state_from_kernel(ret_p), _state_from_kernel(gla_p), _state_from_kernel(ret_s), _state_from_kernel(gla_s))
```

```python
import functools
import math

import jax
import jax.numpy as jnp
import numpy as np
from jax import lax
from jax.experimental import pallas as pl
from jax.experimental.pallas import tpu as pltpu

D_MODEL = 1024
CHUNK = 64
PAST_LEN = 1024
N_META = 16
N_HEADS = 4
HEAD_K = 64
HEAD_V = 128
HK = N_HEADS * HEAD_K
HV = N_HEADS * HEAD_V
GLA_RANK = 16
GATE_NORM = 16.0
ROPE_BASE = 10000.0
N_GROUPS = 4
EXPERTS_PER_GROUP = 8
N_EXPERTS = N_GROUPS * EXPERTS_PER_GROUP
D_EXPERT = 512
NORM_EPS = 1e-6

LANES = 128
SUBLANES = 8
TILE_CHUNKS = 8
TILE_ROWS = TILE_CHUNKS * CHUNK
ROUTE_ROWS = 8
ROUTER_LANES = 128
UNIT = SUBLANES
MOE_BLOCK = 512
BLOCK_UNITS = MOE_BLOCK // UNIT
LOCAL_ROWS = 2 * TILE_ROWS + N_EXPERTS * UNIT
LOCAL_UNITS = LOCAL_ROWS // UNIT
PLAN_FAST = 8
VMEM_LIMIT = 56 * 1024 * 1024

C_RQ, C_RK, C_RV, C_RG = 0, 256, 512, 1024
C_GQ, C_GK, C_GV, C_GG = 1536, 1792, 2048, 2560
W_MIX = 3072

F32 = jnp.float32
BF16 = jnp.bfloat16
LOG_G = tuple(math.log1p(-(2.0 ** (-5.0 - h))) for h in range(N_HEADS))


def _dot(a, b):
    return jnp.dot(a, b, preferred_element_type=F32)


def _dot_nt(a, b):
    return lax.dot_general(a, b, (((1,), (1,)), ((), ())), preferred_element_type=F32)


def _dot_tn(a, b):
    return lax.dot_general(a, b, (((0,), (0,)), ((), ())), preferred_element_type=F32)


def _sigmoid(x):
    return 1.0 / (1.0 + jnp.exp(-x))


def _pack_halves(x):
    half = x.shape[1] // 2
    bits = pltpu.bitcast(x, jnp.int32)
    return lax.shift_right_logical(bits[:, :half], 16) | (bits[:, half:] & jnp.int32(-65536))


def _unpack_halves(w):
    lo = pltpu.bitcast(lax.shift_left(w, 16), F32)
    hi = pltpu.bitcast(w & jnp.int32(-65536), F32)
    return jnp.concatenate([lo, hi], axis=1).astype(BF16)


def _head_of_lane(shape, width):
    return lax.broadcasted_iota(jnp.int32, shape, len(shape) - 1) >> int(math.log2(width))


def _per_head_lane_const(vals, shape, width):
    hd = _head_of_lane(shape, width)
    out = jnp.full(shape, vals[N_HEADS - 1], F32)
    for h in range(N_HEADS - 2, -1, -1):
        out = jnp.where(hd == h, vals[h], out)
    return out


def _tables_kernel(inv_ref, cp_ref, sp_ref, cm_ref, sm_ref, *, seq, past_len):
    inv = inv_ref[...].reshape(1, 1, LANES)
    lane = lax.broadcasted_iota(jnp.int32, (1, 1, LANES), 2)
    sign = jnp.where((lane & (HEAD_K - 1)) < (HEAD_K // 2), -1.0, 1.0)
    off = lax.broadcasted_iota(jnp.int32, (1, CHUNK, LANES), 1).astype(F32) * inv
    c_off, s_off = jnp.cos(off), jnp.sin(off)

    def chunks(n, first_pos):
        base = (lax.broadcasted_iota(jnp.int32, (n, 1, LANES), 0) * CHUNK + first_pos).astype(F32) * inv
        c_base, s_base = jnp.cos(base), jnp.sin(base)
        cos = (c_base * c_off - s_base * s_off).reshape(n * CHUNK, LANES)
        sin = ((s_base * c_off + c_base * s_off) * sign).reshape(n * CHUNK, LANES)
        return cos, sin

    cp_ref[0:seq, :], sp_ref[0:seq, :] = chunks(seq // CHUNK, 0)
    cos_s, sin_s = chunks(1, past_len)
    for c in range(TILE_CHUNKS):
        cp_ref[seq + c * CHUNK:seq + (c + 1) * CHUNK, :] = cos_s
        sp_ref[seq + c * CHUNK:seq + (c + 1) * CHUNK, :] = sin_s
    cm_ref[...], sm_ref[...] = chunks(1, -CHUNK)


def _rotary_tables(seq, past_len):
    half = HEAD_K // 2
    inv = ROPE_BASE ** (-2.0 * jnp.arange(half, dtype=F32) / HEAD_K)
    inv = jnp.tile(inv, LANES // half).reshape(1, LANES)
    shp = lambda r: jax.ShapeDtypeStruct((r, LANES), F32)
    return pl.pallas_call(
        functools.partial(_tables_kernel, seq=seq, past_len=past_len),
        out_shape=(shp(seq + TILE_ROWS), shp(seq + TILE_ROWS), shp(CHUNK), shp(CHUNK)),
        compiler_params=pltpu.CompilerParams(vmem_limit_bytes=VMEM_LIMIT),
        name="rotary_tables",
    )(inv)


def _mixer_kernel(*refs, n_chunks, tiles_p, tiles_per_row, route):
    tm = n_chunks * CHUNK
    if tiles_p:
        xp_ref, refs = refs[0], refs[1:]
    (xs_ref, cos_ref, sin_ref, spi_r_ref, spi_g_ref, ssi_r_ref, ssi_g_ref,
     gmix_ref, wmix_ref, wlr_ref, wz_ref, wgk_ref, bgk_ref, gnorm_ref, wbr_ref, wbg_ref, wout_ref,
     gffn_ref, wrt_ref, brt_ref, bdtri_ref, ltri_ref,
     xnew_ref, xloc_ref, route_ref, ku_ref, spo_r_ref, spo_g_ref, sso_r_ref, sso_g_ref,
     qb, qdb, kb, kkb, vb, gqb, gkkb, gvb, ga, o_ret, o_gla, st_ret, st_gla) = refs

    i = pl.program_id(0)
    if tiles_p:
        is_s = i >= tiles_p
        t_idx = jnp.minimum(i, tiles_p - 1) % tiles_per_row
        x = jnp.where(is_s, xs_ref[...].reshape(tm, D_MODEL), xp_ref[...].reshape(tm, D_MODEL))

        @pl.when(jnp.logical_and(jnp.logical_not(is_s), t_idx == 0))
        def _():
            st_ret[...] = spi_r_ref[0]
            st_gla[...] = spi_g_ref[0]
    else:
        is_s = None
        x = xs_ref[...].reshape(tm, D_MODEL)

    h = x * lax.rsqrt(jnp.mean(x * x, axis=-1, keepdims=True) + NORM_EPS) * gmix_ref[...]
    hb = h.astype(BF16)

    def proj(c0, width):
        return _dot(hb, wmix_ref[:, c0:c0 + width])

    cos3 = jnp.concatenate([cos_ref[...]] * 2, axis=1).reshape(n_chunks, CHUNK, HK)
    sin3 = jnp.concatenate([sin_ref[...]] * 2, axis=1).reshape(n_chunks, CHUNK, HK)
    lane_hk = lax.broadcasted_iota(jnp.int32, (tm, HK), 1)
    first_half = (lane_hk & (HEAD_K - 1)) < (HEAD_K // 2)

    def rotary(t):
        swapped = jnp.where(first_half, pltpu.roll(t, HK - HEAD_K // 2, 1), pltpu.roll(t, HEAD_K // 2, 1))
        t3 = t.reshape(n_chunks, CHUNK, HK)
        return (t3 * cos3 + swapped.reshape(n_chunks, CHUNK, HK) * sin3).reshape(tm, HK)

    logg_hk = _per_head_lane_const(LOG_G, (CHUNK, HK), HEAD_K)
    l_idx = lax.broadcasted_iota(jnp.int32, (CHUNK, HK), 0).astype(F32)
    qdec = jnp.exp((l_idx + 1.0) * logg_hk)
    kdec = jnp.exp((CHUNK - 1.0 - l_idx) * logg_hk)
    cdec = jnp.exp(float(CHUNK) * _per_head_lane_const(LOG_G, (1, HK), HEAD_K))
    r_idx = lax.broadcasted_iota(jnp.int32, (N_HEADS * CHUNK, CHUNK), 0)
    m_idx = lax.broadcasted_iota(jnp.int32, (N_HEADS * CHUNK, CHUNK), 1)
    logg_rows = jnp.full((N_HEADS * CHUNK, CHUNK), LOG_G[N_HEADS - 1], F32)
    for hh in range(N_HEADS - 2, -1, -1):
        logg_rows = jnp.where((r_idx >> 6) == hh, LOG_G[hh], logg_rows)
    dmat = jnp.exp(jnp.abs((r_idx & (CHUNK - 1)) - m_idx).astype(F32) * logg_rows)

    glr = _dot(hb, wlr_ref[...])
    rq = rotary(proj(C_RQ, HK))
    qb[...] = rq.astype(BF16)
    qdb[...] = (rq.reshape(n_chunks, CHUNK, HK) * qdec).reshape(tm, HK).astype(BF16)
    gl = _dot(glr.astype(BF16), wgk_ref[...]) + bgk_ref[...]
    rk = rotary(proj(C_RK, HK)) * (HEAD_K ** -0.5)
    kb[...] = rk.astype(BF16)
    kkb[...] = (rk.reshape(n_chunks, CHUNK, HK) * kdec).reshape(tm, HK).astype(BF16)
    log_a = (jnp.minimum(gl, 0.0) - jnp.log1p(jnp.exp(-jnp.abs(gl)))) / GATE_NORM
    la_hi = log_a.astype(BF16)
    la_lo = (log_a - la_hi.astype(F32)).astype(BF16)
    vb[...] = proj(C_RV, HV).astype(BF16)
    bdtri = bdtri_ref[...]
    bcum = _dot(bdtri, la_hi) + _dot(bdtri, la_lo)
    gqb[...] = (proj(C_GQ, HK) * (HEAD_K ** -0.5)).astype(BF16)
    gvb[...] = proj(C_GV, HV).astype(BF16)
    gk = proj(C_GK, HK)
    b3 = bcum.reshape(n_chunks, CHUNK, HK)
    bl3 = b3[:, CHUNK - 1:CHUNK, :]
    gkkb[...] = (gk.reshape(n_chunks, CHUNK, HK) * jnp.exp(bl3 - b3)).reshape(tm, HK).astype(BF16)
    ga[...] = jnp.broadcast_to(jnp.exp(bl3), (n_chunks, SUBLANES, HK))

    def stack_masked(a, width):
        head = _head_of_lane(a.shape, width)
        zero = jnp.zeros_like(a)
        return jnp.concatenate([jnp.where(head == hh, a, zero) for hh in range(N_HEADS)], axis=0)

    def heads_to_rows(a):
        return jnp.concatenate([a[:, hh * HEAD_V:(hh + 1) * HEAD_V] for hh in range(N_HEADS)], axis=0)

    def rows_to_heads(a):
        return jnp.concatenate([a[hh * CHUNK:(hh + 1) * CHUNK, :] for hh in range(N_HEADS)], axis=1)

    chunk_rows = [slice(c * CHUNK, (c + 1) * CHUNK) for c in range(n_chunks)]
    probs = [(_dot_nt(stack_masked(qb[r, :], HEAD_K), kb[r, :]) * dmat).astype(BF16) for r in chunk_rows]
    inc_ret = [_dot_tn(heads_to_rows(vb[r, :]), stack_masked(kkb[r, :], HEAD_K)) for r in chunk_rows]
    inc_gla = [_dot_tn(heads_to_rows(gvb[r, :]), stack_masked(gkkb[r, :], HEAD_K)) for r in chunk_rows]
    rg = proj(C_RG, HV)
    gg = proj(C_GG, HV)
    s_in, g_out = [], []
    s_cur, g_cur = st_ret[...], st_gla[...]
    for c in range(n_chunks):
        if is_s is None:
            s_cur, g_cur = ssi_r_ref[c], ssi_g_ref[c]
        else:
            s_cur = jnp.where(is_s, ssi_r_ref[c], s_cur)
            g_cur = jnp.where(is_s, ssi_g_ref[c], g_cur)
        s_in.append(s_cur.astype(BF16))
        s_cur = s_cur * cdec + inc_ret[c]
        g_cur = g_cur * ga[c][0:1, :] + inc_gla[c]
        g_out.append(g_cur.astype(BF16))
        sso_r_ref[c] = s_cur
        sso_g_ref[c] = g_cur
    st_ret[...] = s_cur
    st_gla[...] = g_cur
    for c, r in enumerate(chunk_rows):
        v = vb[r, :]
        intra = jnp.concatenate(
            [_dot(probs[c][hh * CHUNK:(hh + 1) * CHUNK, :], v[:, hh * HEAD_V:(hh + 1) * HEAD_V])
             for hh in range(N_HEADS)], axis=1)
        inter = rows_to_heads(_dot_nt(stack_masked(qdb[r, :], HEAD_K), s_in[c]))
        o_ret[r, :] = intra + inter
        o_gla[r, :] = rows_to_heads(_dot_nt(stack_masked(gqb[r, :], HEAD_K), g_out[c]))

    gnorm = gnorm_ref[...]
    orr = o_ret[...]
    ogg = o_gla[...]
    ret_parts, gla_parts = [], []
    for hh in range(N_HEADS):
        sl = slice(hh * HEAD_V, (hh + 1) * HEAD_V)
        oh = orr[:, sl]
        mu = jnp.mean(oh, axis=-1, keepdims=True)
        dev = oh - mu
        var = jnp.mean(dev * dev, axis=-1, keepdims=True)
        ret_parts.append(dev * lax.rsqrt(var + NORM_EPS))
        og = ogg[:, sl]
        gla_parts.append(og * lax.rsqrt(jnp.mean(og * og, axis=-1, keepdims=True) + NORM_EPS) * gnorm)
    o_r = jnp.concatenate(ret_parts, axis=1) * (rg * _sigmoid(rg))
    o_g = jnp.concatenate(gla_parts, axis=1) * (gg * _sigmoid(gg))
    merged = (_sigmoid(_dot(hb, wz_ref[:, :D_MODEL])) * _dot(o_r.astype(BF16), wbr_ref[...])
              + _sigmoid(_dot(hb, wz_ref[:, D_MODEL:])) * _dot(o_g.astype(BF16), wbg_ref[...]))
    xn = x + _dot(merged.astype(BF16), wout_ref[...])
    xnew_ref[...] = xn

    if route:
        h2 = xn * lax.rsqrt(jnp.mean(xn * xn, axis=-1, keepdims=True) + NORM_EPS) * gffn_ref[...]
        h2_hi = h2.astype(BF16)
        h2_lo = (h2 - h2_hi.astype(F32)).astype(BF16)
        hi_both = _dot(h2_hi, wrt_ref[...])
        logits = (hi_both[:, :ROUTER_LANES] + _dot(h2_lo, wrt_ref[:, :ROUTER_LANES])
                  + hi_both[:, ROUTER_LANES:]) + brt_ref[...]
        lt = logits.T
        row8 = lax.broadcasted_iota(jnp.int32, (SUBLANES, tm), 0)
        neg_inf = jnp.float32(-jnp.inf)
        glog = jnp.where(row8 < N_GROUPS, lt[0:SUBLANES, :], neg_inf)
        gmax = jnp.max(glog, axis=0, keepdims=True)
        grp = jnp.min(jnp.where(glog == gmax, row8, SUBLANES), axis=0, keepdims=True)
        p_grp = 1.0 / jnp.sum(jnp.exp(glog - gmax), axis=0, keepdims=True)
        le = jnp.zeros((SUBLANES, tm), F32)
        for g in range(N_GROUPS):
            le = jnp.where(grp == g, lt[SUBLANES * (g + 1):SUBLANES * (g + 2), :], le)
        m1 = jnp.max(le, axis=0, keepdims=True)
        i1 = jnp.min(jnp.where(le == m1, row8, SUBLANES), axis=0, keepdims=True)
        le2 = jnp.where(row8 == i1, neg_inf, le)
        m2 = jnp.max(le2, axis=0, keepdims=True)
        i2 = jnp.min(jnp.where(le2 == m2, row8, SUBLANES), axis=0, keepdims=True)
        e0 = grp * EXPERTS_PER_GROUP + i1
        e1 = grp * EXPERTS_PER_GROUP + i2
        t21 = jnp.exp(m2 - m1)
        w0 = p_grp / (1.0 + t21)
        w1 = p_grp * t21 / (1.0 + t21)

        erow = lax.broadcasted_iota(jnp.int32, (N_EXPERTS, tm), 0)
        hit0 = erow == e0
        hit1 = erow == e1
        onehot = jnp.where(jnp.logical_or(hit0, hit1), 1.0, 0.0).astype(BF16)
        cum = _dot_nt(onehot, ltri_ref[...])
        n_run = cum[:, tm - 1:tm]
        n_pad = jnp.ceil(n_run / UNIT) * UNIT
        rank0 = jnp.sum(jnp.where(hit0, cum - 1.0, 0.0), axis=0, keepdims=True)
        rank1 = jnp.sum(jnp.where(hit1, cum - 1.0, 0.0), axis=0, keepdims=True)
        start0 = jnp.sum(jnp.where(erow < e0, n_pad, 0.0), axis=0, keepdims=True)
        start1 = jnp.sum(jnp.where(erow < e1, n_pad, 0.0), axis=0, keepdims=True)
        ld0 = (start0 + rank0).astype(jnp.int32)
        ld1 = (start1 + rank1).astype(jnp.int32)
        lrow = lax.broadcasted_iota(jnp.int32, (LOCAL_ROWS, tm), 0)
        perm = jnp.where(jnp.logical_or(lrow == ld0, lrow == ld1), 1.0, 0.0).astype(BF16)
        xloc_ref[...] = _pack_halves(_dot(perm, h2_hi))
        lane_e = lax.broadcasted_iota(jnp.int32, (N_EXPERTS, LANES), 1)
        erow_l = lax.broadcasted_iota(jnp.int32, (N_EXPERTS, LANES), 0)
        units_row = jnp.sum(jnp.where(erow_l == lane_e, n_pad / UNIT, 0.0), axis=0, keepdims=True)
        ku_ref[...] = jnp.broadcast_to(units_row, (SUBLANES, LANES)).astype(jnp.int32).reshape(ku_ref.shape)
        zero_row = jnp.zeros((1, tm), jnp.int32)
        rec = jnp.concatenate([e0, e1, ld0, ld1, pltpu.bitcast(w0, jnp.int32), pltpu.bitcast(w1, jnp.int32),
                               zero_row, zero_row], axis=0)
        route_ref[...] = rec.reshape(route_ref.shape)
    else:
        xloc_ref[...] = jnp.zeros(xloc_ref.shape, jnp.int32)
        route_ref[...] = jnp.zeros(route_ref.shape, jnp.int32)
        ku_ref[...] = jnp.zeros(ku_ref.shape, jnp.int32)

    if tiles_p:
        @pl.when(jnp.logical_and(jnp.logical_not(is_s), t_idx == tiles_per_row - 1))
        def _():
            spo_r_ref[0] = st_ret[...]
            spo_g_ref[0] = st_gla[...]
    else:
        spo_r_ref[0] = st_ret[...]
        spo_g_ref[0] = st_gla[...]


def _const_spec(shape):
    nd = len(shape)
    return pl.BlockSpec(shape, lambda *_: (0,) * nd, pipeline_mode=pl.Buffered(1))


def _mixer_call(x_prompt, x_streams, cos, sin, st_prompt, st_streams, weights, consts, *, n_chunks, route):
    tm = n_chunks * CHUNK
    n_streams = x_streams.shape[0]
    tiles_s = n_streams // n_chunks
    if x_prompt is not None:
        n_rows, n_seq, _ = x_prompt.shape
        tpr = n_seq // tm
        tiles_p = n_rows * tpr
    else:
        n_rows, tpr, tiles_p = 1, 1, 0
    n_tiles = tiles_p + tiles_s
    p_idx = lambda i: jnp.minimum(i, tiles_p - 1)
    s_idx = lambda i: jnp.maximum(i - tiles_p, 0)

    st_blk = (1, HEAD_V, HK)
    in_specs, args = [], []
    if tiles_p:
        in_specs.append(pl.BlockSpec((1, tm, D_MODEL), lambda i: (p_idx(i) // tpr, p_idx(i) % tpr, 0)))
        args.append(x_prompt)
        cos_spec = pl.BlockSpec((tm, LANES), lambda i: (jnp.where(i < tiles_p, i % tpr, tpr), 0))
    else:
        cos_spec = pl.BlockSpec((tm, LANES), lambda i: (0, 0))
    once = pl.Buffered(1)
    stream_in = pl.BlockSpec((n_chunks, HEAD_V, HK), lambda i: (s_idx(i), 0, 0), pipeline_mode=once)
    in_specs += [pl.BlockSpec((n_chunks, CHUNK, D_MODEL), lambda i: (s_idx(i), 0, 0), pipeline_mode=once),
                 cos_spec, cos_spec, _const_spec(st_blk), _const_spec(st_blk), stream_in, stream_in]
    args += [x_streams, cos, sin, st_prompt[0], st_prompt[1], st_streams[0], st_streams[1]]
    in_specs += [_const_spec(w.shape) for w in weights] + [_const_spec(c.shape) for c in consts]
    args += list(weights) + list(consts)

    out_shape = (jax.ShapeDtypeStruct((n_tiles * tm, D_MODEL), F32),
                 jax.ShapeDtypeStruct((n_tiles * LOCAL_ROWS, D_MODEL // 2), jnp.int32),
                 jax.ShapeDtypeStruct((n_tiles, ROUTE_ROWS, tm), jnp.int32),
                 jax.ShapeDtypeStruct((n_tiles, SUBLANES, LANES), jnp.int32),
                 jax.ShapeDtypeStruct((n_rows, HEAD_V, HK), F32), jax.ShapeDtypeStruct((n_rows, HEAD_V, HK), F32),
                 jax.ShapeDtypeStruct((n_streams, HEAD_V, HK), F32),
                 jax.ShapeDtypeStruct((n_streams, HEAD_V, HK), F32))
    row_spec = pl.BlockSpec(st_blk, lambda i: (p_idx(i) // tpr if tiles_p else 0, 0, 0))
    stream_spec = pl.BlockSpec((n_chunks, HEAD_V, HK), lambda i: (s_idx(i), 0, 0))
    out_specs = (pl.BlockSpec((tm, D_MODEL), lambda i: (i, 0)),
                 pl.BlockSpec((LOCAL_ROWS, D_MODEL // 2), lambda i: (i, 0)),
                 pl.BlockSpec((1, ROUTE_ROWS, tm), lambda i: (i, 0, 0)),
                 pl.BlockSpec((1, SUBLANES, LANES), lambda i: (i, 0, 0)),
                 row_spec, row_spec, stream_spec, stream_spec)

    scratch = [pltpu.VMEM((tm, HK), BF16), pltpu.VMEM((tm, HK), BF16), pltpu.VMEM((tm, HK), BF16),
               pltpu.VMEM((tm, HK), BF16), pltpu.VMEM((tm, HV), BF16),
               pltpu.VMEM((tm, HK), BF16), pltpu.VMEM((tm, HK), BF16), pltpu.VMEM((tm, HV), BF16),
               pltpu.VMEM((n_chunks, SUBLANES, HK), F32),
               pltpu.VMEM((tm, HV), F32), pltpu.VMEM((tm, HV), F32),
               pltpu.VMEM((HEAD_V, HK), F32), pltpu.VMEM((HEAD_V, HK), F32)]

    return pl.pallas_call(
        functools.partial(_mixer_kernel, n_chunks=n_chunks, tiles_p=tiles_p, tiles_per_row=tpr, route=route),
        grid=(n_tiles,), in_specs=in_specs, out_specs=out_specs, out_shape=out_shape, scratch_shapes=scratch,
        compiler_params=pltpu.CompilerParams(dimension_semantics=("arbitrary",), vmem_limit_bytes=VMEM_LIMIT),
        name="mixer" if route else "mixer_meta",
    )(*args)


def _plan_kernel(ku_ref, src_ref, blk_ref, next_unit, *, n_tiles, n_blocks):
    shift = int(math.log2(BLOCK_UNITS))

    def walk(fast):
        def init_tile(t, c):
            next_unit[t] = t * LOCAL_UNITS
            return c

        lax.fori_loop(0, n_tiles, init_tile, 0)

        def init_block(b, c):
            blk_ref[0, b] = N_EXPERTS - 1
            blk_ref[1, b] = 0
            blk_ref[2, b] = 0
            return c

        lax.fori_loop(0, n_blocks, init_block, 0)

        def per_expert(e, carry):
            g0, over0 = carry

            def per_tile(t, c):
                g, over = c
                k = ku_ref[t, e]
                base = next_unit[t]
                if fast:
                    for j in range(PLAN_FAST):
                        src_ref[g + j] = base + j
                    over = jnp.maximum(over, (k > PLAN_FAST).astype(jnp.int32))
                else:
                    def per_unit(j, cc):
                        src_ref[g + j] = base + j
                        return cc

                    lax.fori_loop(0, k, per_unit, 0)
                next_unit[t] = base + k
                return g + k, over

            g_end, over1 = lax.fori_loop(0, n_tiles, per_tile, (g0, over0), unroll=4 if fast else 1)
            g_pad = ((g_end + (BLOCK_UNITS - 1)) >> shift) << shift

            def pad_unit(j, c):
                src_ref[j] = src_ref[g_pad - BLOCK_UNITS]
                return c

            lax.fori_loop(g_end, g_pad, pad_unit, 0)

            def set_block(b, c):
                blk_ref[0, b] = e
                blk_ref[1, b] = jnp.minimum(g_end - (b << shift), BLOCK_UNITS)
                return c

            lax.fori_loop(g0 >> shift, g_pad >> shift, set_block, 0)
            return g_pad, over1

        g_total, over = lax.fori_loop(0, N_EXPERTS, per_expert, (0, 0))

        def tail_unit(j, c):
            src_ref[j] = 0
            return c

        lax.fori_loop(g_total, n_blocks * BLOCK_UNITS + PLAN_FAST, tail_unit, 0)
        blk_ref[2, 0] = g_total >> shift
        return over

    overflow = walk(True)

    @pl.when(overflow > 0)
    def _():
        walk(False)


def _plan_call(ku, n_blocks):
    n_tiles = ku.shape[0]
    smem = pl.BlockSpec(memory_space=pltpu.SMEM)
    return pl.pallas_call(
        functools.partial(_plan_kernel, n_tiles=n_tiles, n_blocks=n_blocks),
        in_specs=[smem], out_specs=(smem, smem),
        out_shape=(jax.ShapeDtypeStruct((n_blocks * BLOCK_UNITS + PLAN_FAST,), jnp.int32),
                   jax.ShapeDtypeStruct((3, n_blocks), jnp.int32)),
        scratch_shapes=[pltpu.SMEM((n_tiles,), jnp.int32)],
        name="moe_plan",
    )(ku)


def _experts_kernel(src_ref, blk_ref, x_hbm, wg_ref, wu_ref, wd_ref, y_hbm,
                    xin, yout, wgub, wdb, sem_in, sem_out):
    b = pl.program_id(0)
    used = blk_ref[2, 0]
    slot = b & 1

    def unit_rows(blk, u):
        return pl.ds(pl.multiple_of(src_ref[blk * BLOCK_UNITS + u] * UNIT, UNIT), UNIT)

    def in_copy(blk, u, s):
        return pltpu.make_async_copy(x_hbm.at[unit_rows(blk, u), :], xin.at[s, pl.ds(u * UNIT, UNIT), :],
                                     sem_in.at[s])

    def out_copy(blk, u, s):
        return pltpu.make_async_copy(yout.at[s, pl.ds(u * UNIT, UNIT), :], y_hbm.at[unit_rows(blk, u), :],
                                     sem_out.at[s])

    @pl.when(jnp.logical_and(b == 0, used > 0))
    def _():
        for u in range(BLOCK_UNITS):
            in_copy(0, u, 0).start()

    @pl.when(b < used)
    def _():
        expert = blk_ref[0, b]
        changed = jnp.logical_or(b == 0, expert != blk_ref[0, jnp.maximum(b - 1, 0)])

        @pl.when(changed)
        def _():
            wgub[:, :D_EXPERT] = wg_ref[0].astype(BF16)
            wgub[:, D_EXPERT:] = wu_ref[0].astype(BF16)
            wdb[...] = wd_ref[0].astype(BF16)

        @pl.when(b + 1 < used)
        def _():
            for u in range(BLOCK_UNITS):
                in_copy(b + 1, u, 1 - slot).start()

        for u in range(BLOCK_UNITS):
            in_copy(b, u, slot).wait()

        xb = _unpack_halves(xin[slot])
        gate_up = _dot(xb, wgub[...])
        gate = gate_up[:, :D_EXPERT]
        hid = (gate * _sigmoid(gate)) * gate_up[:, D_EXPERT:]
        yout[slot] = _pack_halves(_dot(hid.astype(BF16), wdb[...]).astype(BF16).astype(F32))

        n_real = blk_ref[1, b]

        @pl.when(n_real == BLOCK_UNITS)
        def _():
            for u in range(BLOCK_UNITS):
                out_copy(b, u, slot).start()

        @pl.when(n_real < BLOCK_UNITS)
        def _():
            lax.fori_loop(0, n_real, lambda u, c: (out_copy(b, u, slot).start(), c)[1], 0)

        @pl.when(b > 0)
        def _():
            lax.fori_loop(0, blk_ref[1, b - 1], lambda u, c: (out_copy(b - 1, u, 1 - slot).wait(), c)[1], 0)

        @pl.when(b == used - 1)
        def _():
            lax.fori_loop(0, n_real, lambda u, c: (out_copy(b, u, slot).wait(), c)[1], 0)


def _experts_call(src, blk, xloc, wg, wu, wd, n_blocks):
    w_in_spec = pl.BlockSpec((1, D_MODEL, D_EXPERT), lambda b, s, m: (m[0, b], 0, 0))
    grid_spec = pltpu.PrefetchScalarGridSpec(
        num_scalar_prefetch=2,
        grid=(n_blocks,),
        in_specs=[pl.BlockSpec(memory_space=pl.ANY), w_in_spec, w_in_spec,
                  pl.BlockSpec((1, D_EXPERT, D_MODEL), lambda b, s, m: (m[0, b], 0, 0))],
        out_specs=pl.BlockSpec(memory_space=pl.ANY),
        scratch_shapes=[pltpu.VMEM((2, MOE_BLOCK, D_MODEL // 2), jnp.int32),
                        pltpu.VMEM((2, MOE_BLOCK, D_MODEL // 2), jnp.int32),
                        pltpu.VMEM((D_MODEL, 2 * D_EXPERT), BF16), pltpu.VMEM((D_EXPERT, D_MODEL), BF16),
                        pltpu.SemaphoreType.DMA((2,)), pltpu.SemaphoreType.DMA((2,))],
    )
    return pl.pallas_call(
        _experts_kernel, grid_spec=grid_spec,
        out_shape=jax.ShapeDtypeStruct(xloc.shape, jnp.int32),
        input_output_aliases={2: 0},
        compiler_params=pltpu.CompilerParams(dimension_semantics=("arbitrary",), vmem_limit_bytes=VMEM_LIMIT),
        name="moe_experts",
    )(src, blk, xloc, wg, wu, wd)


def _combine_kernel(route_ref, x_ref, y_ref, gfin_ref, out_a_ref, out_b_ref, *, tm, tiles_a):
    t = pl.program_id(0)
    rec = route_ref[0]
    recf = jnp.concatenate([rec[2:4, :].astype(F32), pltpu.bitcast(rec, F32)[4:6, :],
                            jnp.zeros((LANES - 4, tm), F32)], axis=0)
    cols = recf.T
    lrow = lax.broadcasted_iota(jnp.int32, (tm, LOCAL_ROWS), 1).astype(F32)
    select = jnp.where(lrow == cols[:, 0:1], cols[:, 2:3], 0.0) + jnp.where(lrow == cols[:, 1:2], cols[:, 3:4], 0.0)
    xf = x_ref[...] + _dot(select.astype(BF16), _unpack_halves(y_ref[...]))
    out = xf * lax.rsqrt(jnp.mean(xf * xf, axis=-1, keepdims=True) + NORM_EPS) * gfin_ref[...]

    @pl.when(t < tiles_a)
    def _():
        out_a_ref[...] = out

    @pl.when(t >= tiles_a)
    def _():
        out_b_ref[...] = out


def _combine_call(route, xnew, yloc, gfin, tm, tiles_a):
    n_tiles = route.shape[0]
    tiles_b = n_tiles - tiles_a
    return pl.pallas_call(
        functools.partial(_combine_kernel, tm=tm, tiles_a=tiles_a),
        grid=(n_tiles,),
        in_specs=[pl.BlockSpec((1, ROUTE_ROWS, tm), lambda t: (t, 0, 0)),
                  pl.BlockSpec((tm, D_MODEL), lambda t: (t, 0)),
                  pl.BlockSpec((LOCAL_ROWS, D_MODEL // 2), lambda t: (t, 0)),
                  pl.BlockSpec((1, D_MODEL), lambda t: (0, 0))],
        out_specs=(pl.BlockSpec((tm, D_MODEL), lambda t: (jnp.minimum(t, tiles_a - 1), 0)),
                   pl.BlockSpec((tm, D_MODEL), lambda t: (jnp.maximum(t - tiles_a, 0), 0))),
        out_shape=(jax.ShapeDtypeStruct((tiles_a * tm, D_MODEL), F32),
                   jax.ShapeDtypeStruct((tiles_b * tm, D_MODEL), F32)),
        compiler_params=pltpu.CompilerParams(dimension_semantics=("arbitrary",), vmem_limit_bytes=VMEM_LIMIT),
        name="moe_combine",
    )(route, xnew, yloc, gfin)


def _prep_weights(g_mix, w_in, w_gla_gk2, b_gla_gk, g_gla_norm, w_br_ret, w_br_gla, w_out, g_ffn,
                  w_rg, b_rg, w_re, b_re):
    w_mix = w_in[:, :W_MIX].astype(BF16)
    w_lr = jnp.pad(w_in[:, W_MIX:W_MIX + GLA_RANK], ((0, 0), (0, LANES - GLA_RANK))).astype(BF16)
    w_z = w_in[:, W_MIX + GLA_RANK:].astype(BF16)
    wgk = jnp.pad(w_gla_gk2, ((0, LANES - GLA_RANK), (0, 0))).astype(BF16)
    wrt = jnp.zeros((D_MODEL, ROUTER_LANES), F32)
    wrt = wrt.at[:, 0:N_GROUPS].set(w_rg).at[:, SUBLANES:SUBLANES + N_EXPERTS].set(w_re)
    brt = jnp.zeros((1, ROUTER_LANES), F32)
    brt = brt.at[0, 0:N_GROUPS].set(b_rg).at[0, SUBLANES:SUBLANES + N_EXPERTS].set(b_re)
    wrt_hi = wrt.astype(BF16)
    wrt = jnp.concatenate([wrt_hi, (wrt - wrt_hi.astype(F32)).astype(BF16)], axis=1)
    return (g_mix.reshape(1, D_MODEL), w_mix, w_lr, w_z, wgk, b_gla_gk.reshape(1, HK), g_gla_norm.reshape(1, HEAD_V),
            w_br_ret.astype(BF16), w_br_gla.astype(BF16), w_out.astype(BF16), g_ffn.reshape(1, D_MODEL), wrt, brt)


def _tri_consts(tm):
    r = np.arange(tm)
    low = (r[None, :] <= r[:, None])
    bd = low & ((r[None, :] // CHUNK) == (r[:, None] // CHUNK))
    return jnp.asarray(bd, BF16), jnp.asarray(low, BF16)


def _state_to_kernel(s):
    return jnp.swapaxes(s.reshape(s.shape[0], HK, HEAD_V), 1, 2)


def _state_from_kernel(s):
    return jnp.swapaxes(s, 1, 2).reshape(1, s.shape[0], N_HEADS, HEAD_K, HEAD_V)


def kernel(x_prompt, x_sample, state_ret, state_gla, meta_tokens, g_mix, w_in, w_gla_gk2, b_gla_gk, g_gla_norm, w_br_ret, w_br_gla, w_out, g_ffn, w_router_group, b_router_group, w_router_expert, b_router_expert, w_exp_gate, w_exp_up, w_exp_down, g_final):
    n_b, seq, _ = x_prompt.shape
    n_s, dec_seq, _ = x_sample.shape
    depth = state_ret.shape[0]
    assert depth == 1 and dec_seq == CHUNK and seq % TILE_ROWS == 0 and n_s % TILE_CHUNKS == 0
    cos_t, sin_t, cos_m, sin_m = _rotary_tables(seq, PAST_LEN)
    weights = _prep_weights(g_mix[0], w_in[0], w_gla_gk2[0], b_gla_gk[0], g_gla_norm[0], w_br_ret[0], w_br_gla[0],
                            w_out[0], g_ffn[0], w_router_group[0], b_router_group[0], w_router_expert[0],
                            b_router_expert[0])
    zero_state = jnp.zeros((1, HEAD_V, HK), F32)

    x_meta = jnp.concatenate([jnp.zeros((CHUNK - N_META, D_MODEL), F32), meta_tokens.astype(F32)], axis=0)
    meta_out = _mixer_call(None, x_meta.reshape(1, CHUNK, D_MODEL), cos_m, sin_m, (zero_state, zero_state),
                           (zero_state, zero_state), weights, _tri_consts(CHUNK), n_chunks=1, route=False)
    meta_ret, meta_gla = meta_out[6], meta_out[7]

    xnew, xloc, route, ku, ret_p, gla_p, ret_s, gla_s = _mixer_call(
        x_prompt, x_sample, cos_t, sin_t, (meta_ret, meta_gla),
        (_state_to_kernel(state_ret[0]), _state_to_kernel(state_gla[0])), weights, _tri_consts(TILE_ROWS),
        n_chunks=TILE_CHUNKS, route=True)

    n_tiles = route.shape[0]
    n_blocks = n_tiles * LOCAL_UNITS // BLOCK_UNITS + N_EXPERTS
    src, blk = _plan_call(ku[:, 0, :N_EXPERTS], n_blocks)
    yloc = _experts_call(src, blk, xloc, w_exp_gate[0], w_exp_up[0], w_exp_down[0], n_blocks)
    y_p, y_s = _combine_call(route, xnew, yloc, g_final.reshape(1, D_MODEL), TILE_ROWS, n_b * seq // TILE_ROWS)

    return (y_p.reshape(n_b, seq, D_MODEL), y_s.reshape(n_s, dec_seq, D_MODEL),
            _state_from_kernel(ret_p), _state_from_kernel(gla_p), _state_from_kernel(ret_s), _state_from_kernel(gla_s))
```

```python
import functools
import math

import jax
import jax.numpy as jnp
import numpy as np
from jax import lax
from jax.experimental import pallas as pl
from jax.experimental.pallas import tpu as pltpu

D_MODEL = 1024
CHUNK = 64
PAST_LEN = 1024
N_META = 16
N_HEADS = 4
HEAD_K = 64
HEAD_V = 128
HK = N_HEADS * HEAD_K
HV = N_HEADS * HEAD_V
GLA_RANK = 16
GATE_NORM = 16.0
ROPE_BASE = 10000.0
N_GROUPS = 4
EXPERTS_PER_GROUP = 8
N_EXPERTS = N_GROUPS * EXPERTS_PER_GROUP
D_EXPERT = 512
NORM_EPS = 1e-6

LANES = 128
SUBLANES = 8
TILE_CHUNKS = 8
TILE_ROWS = TILE_CHUNKS * CHUNK
ROUTE_ROWS = 8
ROUTER_LANES = 128
UNIT = SUBLANES
MOE_BLOCK = 512
BLOCK_UNITS = MOE_BLOCK // UNIT
LOCAL_ROWS = 2 * TILE_ROWS + N_EXPERTS * UNIT
LOCAL_UNITS = LOCAL_ROWS // UNIT
VMEM_LIMIT = 56 * 1024 * 1024

C_RQ, C_RK, C_RV, C_RG = 0, 256, 512, 1024
C_GQ, C_GK, C_GV, C_GG = 1536, 1792, 2048, 2560
W_MIX = 3072

F32 = jnp.float32
BF16 = jnp.bfloat16
LOG_G = tuple(math.log1p(-(2.0 ** (-5.0 - h))) for h in range(N_HEADS))


def _dot(a, b):
    return jnp.dot(a, b, preferred_element_type=F32)


def _dot_nt(a, b):
    return lax.dot_general(a, b, (((1,), (1,)), ((), ())), preferred_element_type=F32)


def _dot_tn(a, b):
    return lax.dot_general(a, b, (((0,), (0,)), ((), ())), preferred_element_type=F32)


def _sigmoid(x):
    return 1.0 / (1.0 + jnp.exp(-x))


def _pack_halves(x):
    half = x.shape[1] // 2
    bits = pltpu.bitcast(x, jnp.int32)
    return lax.shift_right_logical(bits[:, :half], 16) | (bits[:, half:] & jnp.int32(-65536))


def _unpack_halves(w):
    lo = pltpu.bitcast(lax.shift_left(w, 16), F32)
    hi = pltpu.bitcast(w & jnp.int32(-65536), F32)
    return jnp.concatenate([lo, hi], axis=1).astype(BF16)


def _head_of_lane(shape, width):
    return lax.broadcasted_iota(jnp.int32, shape, len(shape) - 1) >> int(math.log2(width))


def _per_head_lane_const(vals, shape, width):
    hd = _head_of_lane(shape, width)
    out = jnp.full(shape, vals[N_HEADS - 1], F32)
    for h in range(N_HEADS - 2, -1, -1):
        out = jnp.where(hd == h, vals[h], out)
    return out


def _tables_kernel(inv_ref, cp_ref, sp_ref, cm_ref, sm_ref, *, seq, past_len):
    inv = inv_ref[...].reshape(1, 1, LANES)
    lane = lax.broadcasted_iota(jnp.int32, (1, 1, LANES), 2)
    sign = jnp.where((lane & (HEAD_K - 1)) < (HEAD_K // 2), -1.0, 1.0)
    off = lax.broadcasted_iota(jnp.int32, (1, CHUNK, LANES), 1).astype(F32) * inv
    c_off, s_off = jnp.cos(off), jnp.sin(off)

    def chunks(n, first_pos):
        base = (lax.broadcasted_iota(jnp.int32, (n, 1, LANES), 0) * CHUNK + first_pos).astype(F32) * inv
        c_base, s_base = jnp.cos(base), jnp.sin(base)
        cos = (c_base * c_off - s_base * s_off).reshape(n * CHUNK, LANES)
        sin = ((s_base * c_off + c_base * s_off) * sign).reshape(n * CHUNK, LANES)
        return cos, sin

    cp_ref[0:seq, :], sp_ref[0:seq, :] = chunks(seq // CHUNK, 0)
    cos_s, sin_s = chunks(1, past_len)
    for c in range(TILE_CHUNKS):
        cp_ref[seq + c * CHUNK:seq + (c + 1) * CHUNK, :] = cos_s
        sp_ref[seq + c * CHUNK:seq + (c + 1) * CHUNK, :] = sin_s
    cm_ref[...], sm_ref[...] = chunks(1, -CHUNK)


def _rotary_tables(seq, past_len):
    half = HEAD_K // 2
    inv = ROPE_BASE ** (-2.0 * jnp.arange(half, dtype=F32) / HEAD_K)
    inv = jnp.tile(inv, LANES // half).reshape(1, LANES)
    shp = lambda r: jax.ShapeDtypeStruct((r, LANES), F32)
    return pl.pallas_call(
        functools.partial(_tables_kernel, seq=seq, past_len=past_len),
        out_shape=(shp(seq + TILE_ROWS), shp(seq + TILE_ROWS), shp(CHUNK), shp(CHUNK)),
        compiler_params=pltpu.CompilerParams(vmem_limit_bytes=VMEM_LIMIT),
        name="rotary_tables",
    )(inv)


def _mixer_kernel(*refs, n_chunks, tiles_p, tiles_per_row, route):
    tm = n_chunks * CHUNK
    if tiles_p:
        xp_ref, refs = refs[0], refs[1:]
    (xs_ref, cos_ref, sin_ref, spi_r_ref, spi_g_ref, ssi_r_ref, ssi_g_ref,
     gmix_ref, wmix_ref, wlr_ref, wz_ref, wgk_ref, bgk_ref, gnorm_ref, wbr_ref, wbg_ref, wout_ref,
     gffn_ref, wrt_ref, brt_ref, bdtri_ref, ltri_ref,
     xnew_ref, xloc_ref, route_ref, ku_ref, spo_r_ref, spo_g_ref, sso_r_ref, sso_g_ref,
     qb, qdb, kb, kkb, vb, gqb, gkkb, gvb, ga, o_ret, o_gla, st_ret, st_gla) = refs

    i = pl.program_id(0)
    if tiles_p:
        is_s = i >= tiles_p
        t_idx = jnp.minimum(i, tiles_p - 1) % tiles_per_row
        x = jnp.where(is_s, xs_ref[...].reshape(tm, D_MODEL), xp_ref[...].reshape(tm, D_MODEL))

        @pl.when(jnp.logical_and(jnp.logical_not(is_s), t_idx == 0))
        def _():
            st_ret[...] = spi_r_ref[0]
            st_gla[...] = spi_g_ref[0]
    else:
        is_s = None
        x = xs_ref[...].reshape(tm, D_MODEL)

    h = x * lax.rsqrt(jnp.mean(x * x, axis=-1, keepdims=True) + NORM_EPS) * gmix_ref[...]
    hb = h.astype(BF16)

    def proj(c0, width):
        return _dot(hb, wmix_ref[:, c0:c0 + width])

    cos3 = jnp.concatenate([cos_ref[...]] * 2, axis=1).reshape(n_chunks, CHUNK, HK)
    sin3 = jnp.concatenate([sin_ref[...]] * 2, axis=1).reshape(n_chunks, CHUNK, HK)
    lane_hk = lax.broadcasted_iota(jnp.int32, (tm, HK), 1)
    first_half = (lane_hk & (HEAD_K - 1)) < (HEAD_K // 2)

    def rotary(t):
        swapped = jnp.where(first_half, pltpu.roll(t, HK - HEAD_K // 2, 1), pltpu.roll(t, HEAD_K // 2, 1))
        t3 = t.reshape(n_chunks, CHUNK, HK)
        return (t3 * cos3 + swapped.reshape(n_chunks, CHUNK, HK) * sin3).reshape(tm, HK)

    logg_hk = _per_head_lane_const(LOG_G, (CHUNK, HK), HEAD_K)
    l_idx = lax.broadcasted_iota(jnp.int32, (CHUNK, HK), 0).astype(F32)
    qdec = jnp.exp((l_idx + 1.0) * logg_hk)
    kdec = jnp.exp((CHUNK - 1.0 - l_idx) * logg_hk)
    cdec = jnp.exp(float(CHUNK) * _per_head_lane_const(LOG_G, (1, HK), HEAD_K))
    r_idx = lax.broadcasted_iota(jnp.int32, (N_HEADS * CHUNK, CHUNK), 0)
    m_idx = lax.broadcasted_iota(jnp.int32, (N_HEADS * CHUNK, CHUNK), 1)
    logg_rows = jnp.full((N_HEADS * CHUNK, CHUNK), LOG_G[N_HEADS - 1], F32)
    for hh in range(N_HEADS - 2, -1, -1):
        logg_rows = jnp.where((r_idx >> int(math.log2(CHUNK))) == hh, LOG_G[hh], logg_rows)
    dmat = jnp.exp(jnp.abs((r_idx & (CHUNK - 1)) - m_idx).astype(F32) * logg_rows)

    glr = _dot(hb, wlr_ref[...])
    rq = rotary(proj(C_RQ, HK))
    qb[...] = rq.astype(BF16)
    qdb[...] = (rq.reshape(n_chunks, CHUNK, HK) * qdec).reshape(tm, HK).astype(BF16)
    gl = _dot(glr.astype(BF16), wgk_ref[...]) + bgk_ref[...]
    rk = rotary(proj(C_RK, HK)) * (HEAD_K ** -0.5)
    kb[...] = rk.astype(BF16)
    kkb[...] = (rk.reshape(n_chunks, CHUNK, HK) * kdec).reshape(tm, HK).astype(BF16)
    log_a = (jnp.minimum(gl, 0.0) - jnp.log1p(jnp.exp(-jnp.abs(gl)))) / GATE_NORM
    la_hi = log_a.astype(BF16)
    la_lo = (log_a - la_hi.astype(F32)).astype(BF16)
    vb[...] = proj(C_RV, HV).astype(BF16)
    bdtri = bdtri_ref[...]
    bcum = _dot(bdtri, la_hi) + _dot(bdtri, la_lo)
    gqb[...] = (proj(C_GQ, HK) * (HEAD_K ** -0.5)).astype(BF16)
    gvb[...] = proj(C_GV, HV).astype(BF16)
    gk = proj(C_GK, HK)
    b3 = bcum.reshape(n_chunks, CHUNK, HK)
    bl3 = b3[:, CHUNK - 1:CHUNK, :]
    gkkb[...] = (gk.reshape(n_chunks, CHUNK, HK) * jnp.exp(bl3 - b3)).reshape(tm, HK).astype(BF16)
    ga[...] = jnp.broadcast_to(jnp.exp(bl3), (n_chunks, SUBLANES, HK))

    def stack_masked(a, width):
        head = _head_of_lane(a.shape, width)
        zero = jnp.zeros_like(a)
        return jnp.concatenate([jnp.where(head == hh, a, zero) for hh in range(N_HEADS)], axis=0)

    def heads_to_rows(a):
        return jnp.concatenate([a[:, hh * HEAD_V:(hh + 1) * HEAD_V] for hh in range(N_HEADS)], axis=0)

    def rows_to_heads(a):
        return jnp.concatenate([a[hh * CHUNK:(hh + 1) * CHUNK, :] for hh in range(N_HEADS)], axis=1)

    chunk_rows = [slice(c * CHUNK, (c + 1) * CHUNK) for c in range(n_chunks)]
    probs = [(_dot_nt(stack_masked(qb[r, :], HEAD_K), kb[r, :]) * dmat).astype(BF16) for r in chunk_rows]
    inc_ret = [_dot_tn(heads_to_rows(vb[r, :]), stack_masked(kkb[r, :], HEAD_K)) for r in chunk_rows]
    inc_gla = [_dot_tn(heads_to_rows(gvb[r, :]), stack_masked(gkkb[r, :], HEAD_K)) for r in chunk_rows]
    rg = proj(C_RG, HV)
    gg = proj(C_GG, HV)
    s_in, g_out = [], []
    s_cur, g_cur = st_ret[...], st_gla[...]
    for c in range(n_chunks):
        if is_s is None:
            s_cur, g_cur = ssi_r_ref[c], ssi_g_ref[c]
        else:
            s_cur = jnp.where(is_s, ssi_r_ref[c], s_cur)
            g_cur = jnp.where(is_s, ssi_g_ref[c], g_cur)
        s_in.append(s_cur.astype(BF16))
        s_cur = s_cur * cdec + inc_ret[c]
        g_cur = g_cur * ga[c][0:1, :] + inc_gla[c]
        g_out.append(g_cur.astype(BF16))
        sso_r_ref[c] = s_cur
        sso_g_ref[c] = g_cur
    st_ret[...] = s_cur
    st_gla[...] = g_cur
    for c, r in enumerate(chunk_rows):
        v = vb[r, :]
        intra = jnp.concatenate(
            [_dot(probs[c][hh * CHUNK:(hh + 1) * CHUNK, :], v[:, hh * HEAD_V:(hh + 1) * HEAD_V])
             for hh in range(N_HEADS)], axis=1)
        inter = rows_to_heads(_dot_nt(stack_masked(qdb[r, :], HEAD_K), s_in[c]))
        o_ret[r, :] = intra + inter
        o_gla[r, :] = rows_to_heads(_dot_nt(stack_masked(gqb[r, :], HEAD_K), g_out[c]))

    gnorm = gnorm_ref[...]
    orr = o_ret[...]
    ogg = o_gla[...]
    ret_parts, gla_parts = [], []
    for hh in range(N_HEADS):
        sl = slice(hh * HEAD_V, (hh + 1) * HEAD_V)
        oh = orr[:, sl]
        mu = jnp.mean(oh, axis=-1, keepdims=True)
        dev = oh - mu
        var = jnp.mean(dev * dev, axis=-1, keepdims=True)
        ret_parts.append(dev * lax.rsqrt(var + NORM_EPS))
        og = ogg[:, sl]
        gla_parts.append(og * lax.rsqrt(jnp.mean(og * og, axis=-1, keepdims=True) + NORM_EPS) * gnorm)
    o_r = jnp.concatenate(ret_parts, axis=1) * (rg * _sigmoid(rg))
    o_g = jnp.concatenate(gla_parts, axis=1) * (gg * _sigmoid(gg))
    merged = (_sigmoid(_dot(hb, wz_ref[:, :D_MODEL])) * _dot(o_r.astype(BF16), wbr_ref[...])
              + _sigmoid(_dot(hb, wz_ref[:, D_MODEL:])) * _dot(o_g.astype(BF16), wbg_ref[...]))
    xn = x + _dot(merged.astype(BF16), wout_ref[...])
    xnew_ref[...] = xn

    if route:
        h2 = xn * lax.rsqrt(jnp.mean(xn * xn, axis=-1, keepdims=True) + NORM_EPS) * gffn_ref[...]
        h2_hi = h2.astype(BF16)
        h2_lo = (h2 - h2_hi.astype(F32)).astype(BF16)
        hi_both = _dot(h2_hi, wrt_ref[...])
        logits = (hi_both[:, :ROUTER_LANES] + _dot(h2_lo, wrt_ref[:, :ROUTER_LANES])
                  + hi_both[:, ROUTER_LANES:]) + brt_ref[...]
        lt = logits.T
        row8 = lax.broadcasted_iota(jnp.int32, (SUBLANES, tm), 0)
        neg_inf = jnp.float32(-jnp.inf)
        glog = jnp.where(row8 < N_GROUPS, lt[0:SUBLANES, :], neg_inf)
        gmax = jnp.max(glog, axis=0, keepdims=True)
        grp = jnp.min(jnp.where(glog == gmax, row8, SUBLANES), axis=0, keepdims=True)
        p_grp = 1.0 / jnp.sum(jnp.exp(glog - gmax), axis=0, keepdims=True)
        le = jnp.zeros((SUBLANES, tm), F32)
        for g in range(N_GROUPS):
            le = jnp.where(grp == g, lt[SUBLANES * (g + 1):SUBLANES * (g + 2), :], le)
        m1 = jnp.max(le, axis=0, keepdims=True)
        i1 = jnp.min(jnp.where(le == m1, row8, SUBLANES), axis=0, keepdims=True)
        le2 = jnp.where(row8 == i1, neg_inf, le)
        m2 = jnp.max(le2, axis=0, keepdims=True)
        i2 = jnp.min(jnp.where(le2 == m2, row8, SUBLANES), axis=0, keepdims=True)
        e0 = grp * EXPERTS_PER_GROUP + i1
        e1 = grp * EXPERTS_PER_GROUP + i2
        t21 = jnp.exp(m2 - m1)
        w0 = p_grp / (1.0 + t21)
        w1 = p_grp * t21 / (1.0 + t21)

        erow = lax.broadcasted_iota(jnp.int32, (N_EXPERTS, tm), 0)
        hit0 = erow == e0
        hit1 = erow == e1
        onehot = jnp.where(jnp.logical_or(hit0, hit1), 1.0, 0.0).astype(BF16)
        cum = _dot_nt(onehot, ltri_ref[...])
        n_run = cum[:, tm - 1:tm]
        n_pad = jnp.ceil(n_run / UNIT) * UNIT
        rank0 = jnp.sum(jnp.where(hit0, cum - 1.0, 0.0), axis=0, keepdims=True)
        rank1 = jnp.sum(jnp.where(hit1, cum - 1.0, 0.0), axis=0, keepdims=True)
        start0 = jnp.sum(jnp.where(erow < e0, n_pad, 0.0), axis=0, keepdims=True)
        start1 = jnp.sum(jnp.where(erow < e1, n_pad, 0.0), axis=0, keepdims=True)
        ld0 = (start0 + rank0).astype(jnp.int32)
        ld1 = (start1 + rank1).astype(jnp.int32)
        lrow = lax.broadcasted_iota(jnp.int32, (LOCAL_ROWS, tm), 0)
        perm = jnp.where(jnp.logical_or(lrow == ld0, lrow == ld1), 1.0, 0.0).astype(BF16)
        xloc_ref[...] = _pack_halves(_dot(perm, h2_hi))
        lane_e = lax.broadcasted_iota(jnp.int32, (N_EXPERTS, LANES), 1)
        erow_l = lax.broadcasted_iota(jnp.int32, (N_EXPERTS, LANES), 0)
        units_row = jnp.sum(jnp.where(erow_l == lane_e, n_pad / UNIT, 0.0), axis=0, keepdims=True)
        ku_ref[...] = jnp.broadcast_to(units_row, (SUBLANES, LANES)).astype(jnp.int32).reshape(ku_ref.shape)
        zero_row = jnp.zeros((1, tm), jnp.int32)
        rec = jnp.concatenate([e0, e1, ld0, ld1, pltpu.bitcast(w0, jnp.int32), pltpu.bitcast(w1, jnp.int32),
                               zero_row, zero_row], axis=0)
        route_ref[...] = rec.reshape(route_ref.shape)
    else:
        xloc_ref[...] = jnp.zeros(xloc_ref.shape, jnp.int32)
        route_ref[...] = jnp.zeros(route_ref.shape, jnp.int32)
        ku_ref[...] = jnp.zeros(ku_ref.shape, jnp.int32)

    if tiles_p:
        @pl.when(jnp.logical_and(jnp.logical_not(is_s), t_idx == tiles_per_row - 1))
        def _():
            spo_r_ref[0] = st_ret[...]
            spo_g_ref[0] = st_gla[...]
    else:
        spo_r_ref[0] = st_ret[...]
        spo_g_ref[0] = st_gla[...]


def _const_spec(shape):
    nd = len(shape)
    return pl.BlockSpec(shape, lambda *_: (0,) * nd, pipeline_mode=pl.Buffered(1))


def _mixer_call(x_prompt, x_streams, cos, sin, st_prompt, st_streams, weights, consts, *, n_chunks, route):
    tm = n_chunks * CHUNK
    n_streams = x_streams.shape[0]
    tiles_s = n_streams // n_chunks
    if x_prompt is not None:
        n_rows, n_seq, _ = x_prompt.shape
        tpr = n_seq // tm
        tiles_p = n_rows * tpr
    else:
        n_rows, tpr, tiles_p = 1, 1, 0
    n_tiles = tiles_p + tiles_s
    p_idx = lambda i: jnp.minimum(i, tiles_p - 1)
    s_idx = lambda i: jnp.maximum(i - tiles_p, 0)

    st_blk = (1, HEAD_V, HK)
    in_specs, args = [], []
    if tiles_p:
        in_specs.append(pl.BlockSpec((1, tm, D_MODEL), lambda i: (p_idx(i) // tpr, p_idx(i) % tpr, 0)))
        args.append(x_prompt)
        cos_spec = pl.BlockSpec((tm, LANES), lambda i: (jnp.where(i < tiles_p, i % tpr, tpr), 0))
    else:
        cos_spec = pl.BlockSpec((tm, LANES), lambda i: (0, 0))
    once = pl.Buffered(1)
    stream_in = pl.BlockSpec((n_chunks, HEAD_V, HK), lambda i: (s_idx(i), 0, 0), pipeline_mode=once)
    in_specs += [pl.BlockSpec((n_chunks, CHUNK, D_MODEL), lambda i: (s_idx(i), 0, 0), pipeline_mode=once),
                 cos_spec, cos_spec, _const_spec(st_blk), _const_spec(st_blk), stream_in, stream_in]
    args += [x_streams, cos, sin, st_prompt[0], st_prompt[1], st_streams[0], st_streams[1]]
    in_specs += [_const_spec(w.shape) for w in weights] + [_const_spec(c.shape) for c in consts]
    args += list(weights) + list(consts)

    out_shape = (jax.ShapeDtypeStruct((n_tiles * tm, D_MODEL), F32),
                 jax.ShapeDtypeStruct((n_tiles * LOCAL_ROWS, D_MODEL // 2), jnp.int32),
                 jax.ShapeDtypeStruct((n_tiles, ROUTE_ROWS, tm), jnp.int32),
                 jax.ShapeDtypeStruct((n_tiles, SUBLANES, LANES), jnp.int32),
                 jax.ShapeDtypeStruct((n_rows, HEAD_V, HK), F32), jax.ShapeDtypeStruct((n_rows, HEAD_V, HK), F32),
                 jax.ShapeDtypeStruct((n_streams, HEAD_V, HK), F32),
                 jax.ShapeDtypeStruct((n_streams, HEAD_V, HK), F32))
    row_spec = pl.BlockSpec(st_blk, lambda i: (p_idx(i) // tpr if tiles_p else 0, 0, 0))
    stream_spec = pl.BlockSpec((n_chunks, HEAD_V, HK), lambda i: (s_idx(i), 0, 0))
    out_specs = (pl.BlockSpec((tm, D_MODEL), lambda i: (i, 0)),
                 pl.BlockSpec((LOCAL_ROWS, D_MODEL // 2), lambda i: (i, 0)),
                 pl.BlockSpec((1, ROUTE_ROWS, tm), lambda i: (i, 0, 0)),
                 pl.BlockSpec((1, SUBLANES, LANES), lambda i: (i, 0, 0)),
                 row_spec, row_spec, stream_spec, stream_spec)

    scratch = [pltpu.VMEM((tm, HK), BF16), pltpu.VMEM((tm, HK), BF16), pltpu.VMEM((tm, HK), BF16),
               pltpu.VMEM((tm, HK), BF16), pltpu.VMEM((tm, HV), BF16),
               pltpu.VMEM((tm, HK), BF16), pltpu.VMEM((tm, HK), BF16), pltpu.VMEM((tm, HV), BF16),
               pltpu.VMEM((n_chunks, SUBLANES, HK), F32),
               pltpu.VMEM((tm, HV), F32), pltpu.VMEM((tm, HV), F32),
               pltpu.VMEM((HEAD_V, HK), F32), pltpu.VMEM((HEAD_V, HK), F32)]

    return pl.pallas_call(
        functools.partial(_mixer_kernel, n_chunks=n_chunks, tiles_p=tiles_p, tiles_per_row=tpr, route=route),
        grid=(n_tiles,), in_specs=in_specs, out_specs=out_specs, out_shape=out_shape, scratch_shapes=scratch,
        compiler_params=pltpu.CompilerParams(dimension_semantics=("arbitrary",), vmem_limit_bytes=VMEM_LIMIT),
        name="mixer" if route else "mixer_meta",
    )(*args)


def _plan_kernel(ku_ref, src_ref, blk_ref, run_first, run_step, next_unit, seg_start, seg_units, *, n_tiles, n_blocks):
    shift = int(math.log2(BLOCK_UNITS))
    group_shift = int(math.log2(SUBLANES * LANES))

    def init_tile(t, c):
        next_unit[t] = t * LOCAL_UNITS
        return c

    lax.fori_loop(0, n_tiles, init_tile, 0)

    def init_block(b, c):
        blk_ref[0, b] = N_EXPERTS - 1
        blk_ref[1, b] = 0
        blk_ref[2, b] = 0
        return c

    lax.fori_loop(0, n_blocks, init_block, 0)

    def per_expert(e, g0):
        def per_tile(t, c):
            units, d_prev = c
            base = next_unit[t]
            d = base - units
            run_first[e * n_tiles + t] = units
            run_step[e * n_tiles + t] = d - d_prev
            k = ku_ref[t, e]
            next_unit[t] = base + k
            return units + k, d

        units, _ = lax.fori_loop(0, n_tiles, per_tile, (0, 0), unroll=4)
        g_pad = g0 + (((units + (BLOCK_UNITS - 1)) >> shift) << shift)
        seg_start[e] = g0
        seg_units[e] = units

        def set_block(b, c):
            blk_ref[0, b] = e
            blk_ref[1, b] = jnp.minimum(g0 + units - (b << shift), BLOCK_UNITS)
            return c

        lax.fori_loop(g0 >> shift, g_pad >> shift, set_block, 0)
        return g_pad

    g_total = lax.fori_loop(0, N_EXPERTS, per_expert, 0)
    blk_ref[2, 0] = g_total >> shift

    src_ref[...] = jnp.zeros(src_ref.shape, jnp.int32)
    in_group = (lax.broadcasted_iota(jnp.int32, (SUBLANES, LANES), 0) * LANES
                + lax.broadcasted_iota(jnp.int32, (SUBLANES, LANES), 1))

    def expert_units(e, c):
        g0 = seg_start[e]
        units = seg_units[e]
        g_pad = g0 + (((units + (BLOCK_UNITS - 1)) >> shift) << shift)

        def per_group(grp, c2):
            g = in_group + (grp << group_shift)
            o = g - g0
            mine = jnp.logical_and(o >= 0, g < g_pad)
            o_eff = jnp.where(o < units, o, (o >> shift) << shift)

            def per_tile(t, acc):
                return acc + jnp.where(o_eff >= run_first[e * n_tiles + t], run_step[e * n_tiles + t], 0)

            offset = lax.fori_loop(0, n_tiles, per_tile, jnp.zeros((SUBLANES, LANES), jnp.int32), unroll=4)
            rows = pl.ds(pl.multiple_of(grp * SUBLANES, SUBLANES), SUBLANES)
            src_ref[rows, :] = jnp.where(mine, o_eff + offset, src_ref[rows, :])
            return c2

        lax.fori_loop(g0 >> group_shift, (g_pad + (SUBLANES * LANES - 1)) >> group_shift, per_group, 0)
        return c

    lax.fori_loop(0, N_EXPERTS, expert_units, 0)


def _plan_call(ku, n_blocks):
    n_tiles = ku.shape[0]
    smem = pl.BlockSpec(memory_space=pltpu.SMEM)
    group = SUBLANES * LANES
    src_rows = -(-n_blocks * BLOCK_UNITS // group) * SUBLANES
    src, blk = pl.pallas_call(
        functools.partial(_plan_kernel, n_tiles=n_tiles, n_blocks=n_blocks),
        in_specs=[smem], out_specs=(pl.BlockSpec(memory_space=pltpu.VMEM), smem),
        out_shape=(jax.ShapeDtypeStruct((src_rows, LANES), jnp.int32),
                   jax.ShapeDtypeStruct((3, n_blocks), jnp.int32)),
        scratch_shapes=[pltpu.SMEM((N_EXPERTS * n_tiles,), jnp.int32), pltpu.SMEM((N_EXPERTS * n_tiles,), jnp.int32),
                        pltpu.SMEM((n_tiles,), jnp.int32), pltpu.SMEM((N_EXPERTS,), jnp.int32),
                        pltpu.SMEM((N_EXPERTS,), jnp.int32)],
        name="moe_plan",
    )(ku)
    return src.reshape(-1), blk


def _experts_kernel(src_ref, blk_ref, x_hbm, wg_ref, wu_ref, wd_ref, y_hbm,
                    xin, yout, wgub, wdb, sem_in, sem_out):
    b = pl.program_id(0)
    used = blk_ref[2, 0]
    slot = b & 1

    def unit_rows(blk, u):
        return pl.ds(pl.multiple_of(src_ref[blk * BLOCK_UNITS + u] * UNIT, UNIT), UNIT)

    def in_copy(blk, u, s):
        return pltpu.make_async_copy(x_hbm.at[unit_rows(blk, u), :], xin.at[s, pl.ds(u * UNIT, UNIT), :],
                                     sem_in.at[s])

    def out_copy(blk, u, s):
        return pltpu.make_async_copy(yout.at[s, pl.ds(u * UNIT, UNIT), :], y_hbm.at[unit_rows(blk, u), :],
                                     sem_out.at[s])

    @pl.when(jnp.logical_and(b == 0, used > 0))
    def _():
        for u in range(BLOCK_UNITS):
            in_copy(0, u, 0).start()

    @pl.when(b < used)
    def _():
        expert = blk_ref[0, b]
        changed = jnp.logical_or(b == 0, expert != blk_ref[0, jnp.maximum(b - 1, 0)])

        @pl.when(changed)
        def _():
            wgub[:, :D_EXPERT] = wg_ref[0].astype(BF16)
            wgub[:, D_EXPERT:] = wu_ref[0].astype(BF16)
            wdb[...] = wd_ref[0].astype(BF16)

        @pl.when(b + 1 < used)
        def _():
            for u in range(BLOCK_UNITS):
                in_copy(b + 1, u, 1 - slot).start()

        for u in range(BLOCK_UNITS):
            in_copy(b, u, slot).wait()

        xb = _unpack_halves(xin[slot])
        gate_up = _dot(xb, wgub[...])
        gate = gate_up[:, :D_EXPERT]
        hid = (gate * _sigmoid(gate)) * gate_up[:, D_EXPERT:]
        yout[slot] = _pack_halves(_dot(hid.astype(BF16), wdb[...]).astype(BF16).astype(F32))

        n_real = blk_ref[1, b]

        @pl.when(n_real == BLOCK_UNITS)
        def _():
            for u in range(BLOCK_UNITS):
                out_copy(b, u, slot).start()

        @pl.when(n_real < BLOCK_UNITS)
        def _():
            lax.fori_loop(0, n_real, lambda u, c: (out_copy(b, u, slot).start(), c)[1], 0)

        @pl.when(b > 0)
        def _():
            lax.fori_loop(0, blk_ref[1, b - 1], lambda u, c: (out_copy(b - 1, u, 1 - slot).wait(), c)[1], 0)

        @pl.when(b == used - 1)
        def _():
            lax.fori_loop(0, n_real, lambda u, c: (out_copy(b, u, slot).wait(), c)[1], 0)


def _experts_call(src, blk, xloc, wg, wu, wd, n_blocks):
    w_in_spec = pl.BlockSpec((1, D_MODEL, D_EXPERT), lambda b, s, m: (m[0, b], 0, 0))
    grid_spec = pltpu.PrefetchScalarGridSpec(
        num_scalar_prefetch=2,
        grid=(n_blocks,),
        in_specs=[pl.BlockSpec(memory_space=pl.ANY), w_in_spec, w_in_spec,
                  pl.BlockSpec((1, D_EXPERT, D_MODEL), lambda b, s, m: (m[0, b], 0, 0))],
        out_specs=pl.BlockSpec(memory_space=pl.ANY),
        scratch_shapes=[pltpu.VMEM((2, MOE_BLOCK, D_MODEL // 2), jnp.int32),
                        pltpu.VMEM((2, MOE_BLOCK, D_MODEL // 2), jnp.int32),
                        pltpu.VMEM((D_MODEL, 2 * D_EXPERT), BF16), pltpu.VMEM((D_EXPERT, D_MODEL), BF16),
                        pltpu.SemaphoreType.DMA((2,)), pltpu.SemaphoreType.DMA((2,))],
    )
    return pl.pallas_call(
        _experts_kernel, grid_spec=grid_spec,
        out_shape=jax.ShapeDtypeStruct(xloc.shape, jnp.int32),
        input_output_aliases={2: 0},
        compiler_params=pltpu.CompilerParams(dimension_semantics=("arbitrary",), vmem_limit_bytes=VMEM_LIMIT),
        name="moe_experts",
    )(src, blk, xloc, wg, wu, wd)


def _combine_kernel(route_ref, x_ref, y_ref, gfin_ref, out_a_ref, out_b_ref, *, tm, tiles_a):
    t = pl.program_id(0)
    rec = route_ref[0]
    recf = jnp.concatenate([rec[2:4, :].astype(F32), pltpu.bitcast(rec, F32)[4:6, :],
                            jnp.zeros((LANES - 4, tm), F32)], axis=0)
    cols = recf.T
    lrow = lax.broadcasted_iota(jnp.int32, (tm, LOCAL_ROWS), 1).astype(F32)
    select = jnp.where(lrow == cols[:, 0:1], cols[:, 2:3], 0.0) + jnp.where(lrow == cols[:, 1:2], cols[:, 3:4], 0.0)
    xf = x_ref[...] + _dot(select.astype(BF16), _unpack_halves(y_ref[...]))
    out = xf * lax.rsqrt(jnp.mean(xf * xf, axis=-1, keepdims=True) + NORM_EPS) * gfin_ref[...]

    @pl.when(t < tiles_a)
    def _():
        out_a_ref[...] = out

    @pl.when(t >= tiles_a)
    def _():
        out_b_ref[...] = out


def _combine_call(route, xnew, yloc, gfin, tm, tiles_a):
    n_tiles = route.shape[0]
    tiles_b = n_tiles - tiles_a
    return pl.pallas_call(
        functools.partial(_combine_kernel, tm=tm, tiles_a=tiles_a),
        grid=(n_tiles,),
        in_specs=[pl.BlockSpec((1, ROUTE_ROWS, tm), lambda t: (t, 0, 0)),
                  pl.BlockSpec((tm, D_MODEL), lambda t: (t, 0)),
                  pl.BlockSpec((LOCAL_ROWS, D_MODEL // 2), lambda t: (t, 0)),
                  pl.BlockSpec((1, D_MODEL), lambda t: (0, 0))],
        out_specs=(pl.BlockSpec((tm, D_MODEL), lambda t: (jnp.minimum(t, tiles_a - 1), 0)),
                   pl.BlockSpec((tm, D_MODEL), lambda t: (jnp.maximum(t - tiles_a, 0), 0))),
        out_shape=(jax.ShapeDtypeStruct((tiles_a * tm, D_MODEL), F32),
                   jax.ShapeDtypeStruct((tiles_b * tm, D_MODEL), F32)),
        compiler_params=pltpu.CompilerParams(dimension_semantics=("arbitrary",), vmem_limit_bytes=VMEM_LIMIT),
        name="moe_combine",
    )(route, xnew, yloc, gfin)


def _prep_weights(g_mix, w_in, w_gla_gk2, b_gla_gk, g_gla_norm, w_br_ret, w_br_gla, w_out, g_ffn,
                  w_rg, b_rg, w_re, b_re):
    w_mix = w_in[:, :W_MIX].astype(BF16)
    w_lr = jnp.pad(w_in[:, W_MIX:W_MIX + GLA_RANK], ((0, 0), (0, LANES - GLA_RANK))).astype(BF16)
    w_z = w_in[:, W_MIX + GLA_RANK:].astype(BF16)
    wgk = jnp.pad(w_gla_gk2, ((0, LANES - GLA_RANK), (0, 0))).astype(BF16)
    wrt = jnp.zeros((D_MODEL, ROUTER_LANES), F32)
    wrt = wrt.at[:, 0:N_GROUPS].set(w_rg).at[:, SUBLANES:SUBLANES + N_EXPERTS].set(w_re)
    brt = jnp.zeros((1, ROUTER_LANES), F32)
    brt = brt.at[0, 0:N_GROUPS].set(b_rg).at[0, SUBLANES:SUBLANES + N_EXPERTS].set(b_re)
    wrt_hi = wrt.astype(BF16)
    wrt = jnp.concatenate([wrt_hi, (wrt - wrt_hi.astype(F32)).astype(BF16)], axis=1)
    return (g_mix.reshape(1, D_MODEL), w_mix, w_lr, w_z, wgk, b_gla_gk.reshape(1, HK), g_gla_norm.reshape(1, HEAD_V),
            w_br_ret.astype(BF16), w_br_gla.astype(BF16), w_out.astype(BF16), g_ffn.reshape(1, D_MODEL), wrt, brt)


def _tri_consts(tm):
    r = np.arange(tm)
    low = (r[None, :] <= r[:, None])
    bd = low & ((r[None, :] // CHUNK) == (r[:, None] // CHUNK))
    return jnp.asarray(bd, BF16), jnp.asarray(low, BF16)


def _state_to_kernel(s):
    return jnp.swapaxes(s.reshape(s.shape[0], HK, HEAD_V), 1, 2)


def _state_from_kernel(s):
    return jnp.swapaxes(s, 1, 2).reshape(1, s.shape[0], N_HEADS, HEAD_K, HEAD_V)


def kernel(x_prompt, x_sample, state_ret, state_gla, meta_tokens, g_mix, w_in, w_gla_gk2, b_gla_gk, g_gla_norm, w_br_ret, w_br_gla, w_out, g_ffn, w_router_group, b_router_group, w_router_expert, b_router_expert, w_exp_gate, w_exp_up, w_exp_down, g_final):
    n_b, seq, _ = x_prompt.shape
    n_s, dec_seq, _ = x_sample.shape
    depth = state_ret.shape[0]
    assert depth == 1 and dec_seq == CHUNK and seq % TILE_ROWS == 0 and n_s % TILE_CHUNKS == 0
    cos_t, sin_t, cos_m, sin_m = _rotary_tables(seq, PAST_LEN)
    weights = _prep_weights(g_mix[0], w_in[0], w_gla_gk2[0], b_gla_gk[0], g_gla_norm[0], w_br_ret[0], w_br_gla[0],
                            w_out[0], g_ffn[0], w_router_group[0], b_router_group[0], w_router_expert[0],
                            b_router_expert[0])
    zero_state = jnp.zeros((1, HEAD_V, HK), F32)

    x_meta = jnp.concatenate([jnp.zeros((CHUNK - N_META, D_MODEL), F32), meta_tokens.astype(F32)], axis=0)
    meta_out = _mixer_call(None, x_meta.reshape(1, CHUNK, D_MODEL), cos_m, sin_m, (zero_state, zero_state),
                           (zero_state, zero_state), weights, _tri_consts(CHUNK), n_chunks=1, route=False)
    meta_ret, meta_gla = meta_out[6], meta_out[7]

    xnew, xloc, route, ku, ret_p, gla_p, ret_s, gla_s = _mixer_call(
        x_prompt, x_sample, cos_t, sin_t, (meta_ret, meta_gla),
        (_state_to_kernel(state_ret[0]), _state_to_kernel(state_gla[0])), weights, _tri_consts(TILE_ROWS),
        n_chunks=TILE_CHUNKS, route=True)

    n_tiles = route.shape[0]
    n_blocks = n_tiles * LOCAL_UNITS // BLOCK_UNITS + N_EXPERTS
    src, blk = _plan_call(ku[:, 0, :N_EXPERTS], n_blocks)
    yloc = _experts_call(src, blk, xloc, w_exp_gate[0], w_exp_up[0], w_exp_down[0], n_blocks)
    y_p, y_s = _combine_call(route, xnew, yloc, g_final.reshape(1, D_MODEL), TILE_ROWS, n_b * seq // TILE_ROWS)

    return (y_p.reshape(n_b, seq, D_MODEL), y_s.reshape(n_s, dec_seq, D_MODEL),
            _state_from_kernel(ret_p), _state_from_kernel(gla_p), _state_from_kernel(ret_s), _state_from_kernel(gla_s))
```

```python
import functools
import math

import jax
import jax.numpy as jnp
import numpy as np
from jax import lax
from jax.experimental import pallas as pl
from jax.experimental.pallas import tpu as pltpu

D_MODEL = 1024
CHUNK = 64
PAST_LEN = 1024
N_META = 16
N_HEADS = 4
HEAD_K = 64
HEAD_V = 128
HK = N_HEADS * HEAD_K
HV = N_HEADS * HEAD_V
GLA_RANK = 16
GATE_NORM = 16.0
ROPE_BASE = 10000.0
N_GROUPS = 4
EXPERTS_PER_GROUP = 8
N_EXPERTS = N_GROUPS * EXPERTS_PER_GROUP
D_EXPERT = 512
NORM_EPS = 1e-6

LANES = 128
SUBLANES = 8
TILE_CHUNKS = 8
TILE_ROWS = TILE_CHUNKS * CHUNK
ROUTE_ROWS = 8
ROUTER_LANES = 128
UNIT = SUBLANES
MOE_BLOCK = 512
BLOCK_UNITS = MOE_BLOCK // UNIT
LOCAL_ROWS = 2 * TILE_ROWS + N_EXPERTS * UNIT
LOCAL_UNITS = LOCAL_ROWS // UNIT
VMEM_LIMIT = 56 * 1024 * 1024

C_RQ, C_RK, C_RV, C_RG = 0, 256, 512, 1024
C_GQ, C_GK, C_GV, C_GG = 1536, 1792, 2048, 2560
W_MIX = 3072

F32 = jnp.float32
BF16 = jnp.bfloat16
LOG_G = tuple(math.log1p(-(2.0 ** (-5.0 - h))) for h in range(N_HEADS))


def _dot(a, b):
    return jnp.dot(a, b, preferred_element_type=F32)


def _dot_nt(a, b):
    return lax.dot_general(a, b, (((1,), (1,)), ((), ())), preferred_element_type=F32)


def _dot_tn(a, b):
    return lax.dot_general(a, b, (((0,), (0,)), ((), ())), preferred_element_type=F32)


def _sigmoid(x):
    return 1.0 / (1.0 + jnp.exp(-x))


def _pack_halves(x):
    half = x.shape[1] // 2
    bits = pltpu.bitcast(x, jnp.int32)
    return lax.shift_right_logical(bits[:, :half], 16) | (bits[:, half:] & jnp.int32(-65536))


def _unpack_halves(w):
    lo = pltpu.bitcast(lax.shift_left(w, 16), F32)
    hi = pltpu.bitcast(w & jnp.int32(-65536), F32)
    return jnp.concatenate([lo, hi], axis=1).astype(BF16)


def _head_of_lane(shape, width):
    return lax.broadcasted_iota(jnp.int32, shape, len(shape) - 1) >> int(math.log2(width))


def _per_head_lane_const(vals, shape, width):
    hd = _head_of_lane(shape, width)
    out = jnp.full(shape, vals[N_HEADS - 1], F32)
    for h in range(N_HEADS - 2, -1, -1):
        out = jnp.where(hd == h, vals[h], out)
    return out


def _tables_kernel(inv_ref, cp_ref, sp_ref, cm_ref, sm_ref, *, seq, past_len):
    inv = inv_ref[...].reshape(1, 1, LANES)
    lane = lax.broadcasted_iota(jnp.int32, (1, 1, LANES), 2)
    sign = jnp.where((lane & (HEAD_K - 1)) < (HEAD_K // 2), -1.0, 1.0)
    off = lax.broadcasted_iota(jnp.int32, (1, CHUNK, LANES), 1).astype(F32) * inv
    c_off, s_off = jnp.cos(off), jnp.sin(off)

    def chunks(n, first_pos):
        base = (lax.broadcasted_iota(jnp.int32, (n, 1, LANES), 0) * CHUNK + first_pos).astype(F32) * inv
        c_base, s_base = jnp.cos(base), jnp.sin(base)
        cos = (c_base * c_off - s_base * s_off).reshape(n * CHUNK, LANES)
        sin = ((s_base * c_off + c_base * s_off) * sign).reshape(n * CHUNK, LANES)
        return cos, sin

    cp_ref[0:seq, :], sp_ref[0:seq, :] = chunks(seq // CHUNK, 0)
    cos_s, sin_s = chunks(1, past_len)
    for c in range(TILE_CHUNKS):
        cp_ref[seq + c * CHUNK:seq + (c + 1) * CHUNK, :] = cos_s
        sp_ref[seq + c * CHUNK:seq + (c + 1) * CHUNK, :] = sin_s
    cm_ref[...], sm_ref[...] = chunks(1, -CHUNK)


def _rotary_tables(seq, past_len):
    half = HEAD_K // 2
    inv = ROPE_BASE ** (-2.0 * jnp.arange(half, dtype=F32) / HEAD_K)
    inv = jnp.tile(inv, LANES // half).reshape(1, LANES)
    shp = lambda r: jax.ShapeDtypeStruct((r, LANES), F32)
    return pl.pallas_call(
        functools.partial(_tables_kernel, seq=seq, past_len=past_len),
        out_shape=(shp(seq + TILE_ROWS), shp(seq + TILE_ROWS), shp(CHUNK), shp(CHUNK)),
        compiler_params=pltpu.CompilerParams(vmem_limit_bytes=VMEM_LIMIT),
        name="rotary_tables",
    )(inv)


def _mixer_kernel(*refs, n_chunks, tiles_p, tiles_per_row, route):
    tm = n_chunks * CHUNK
    if tiles_p:
        xp_ref, refs = refs[0], refs[1:]
    (xs_ref, cos_ref, sin_ref, spi_r_ref, spi_g_ref, ssi_r_ref, ssi_g_ref,
     gmix_ref, wmix_ref, wlr_ref, wz_ref, wgk_ref, bgk_ref, gnorm_ref, wbr_ref, wbg_ref, wout_ref,
     gffn_ref, wrt_ref, brt_ref, bdtri_ref, ltri_ref,
     xnew_ref, xloc_ref, route_ref, ku_ref, spo_r_ref, spo_g_ref, sso_r_ref, sso_g_ref,
     qb, qdb, kb, kkb, vb, gqb, gkkb, gvb, ga, o_ret, o_gla, st_ret, st_gla) = refs

    i = pl.program_id(0)
    if tiles_p:
        is_s = i >= tiles_p
        t_idx = jnp.minimum(i, tiles_p - 1) % tiles_per_row
        x = jnp.where(is_s, xs_ref[...].reshape(tm, D_MODEL), xp_ref[...].reshape(tm, D_MODEL))

        @pl.when(jnp.logical_and(jnp.logical_not(is_s), t_idx == 0))
        def _():
            st_ret[...] = spi_r_ref[0]
            st_gla[...] = spi_g_ref[0]
    else:
        is_s = None
        x = xs_ref[...].reshape(tm, D_MODEL)

    h = x * lax.rsqrt(jnp.mean(x * x, axis=-1, keepdims=True) + NORM_EPS) * gmix_ref[...]
    hb = h.astype(BF16)

    def proj(c0, width):
        return _dot(hb, wmix_ref[:, c0:c0 + width])

    cos3 = jnp.concatenate([cos_ref[...]] * 2, axis=1).reshape(n_chunks, CHUNK, HK)
    sin3 = jnp.concatenate([sin_ref[...]] * 2, axis=1).reshape(n_chunks, CHUNK, HK)
    lane_hk = lax.broadcasted_iota(jnp.int32, (tm, HK), 1)
    first_half = (lane_hk & (HEAD_K - 1)) < (HEAD_K // 2)

    def rotary(t):
        swapped = jnp.where(first_half, pltpu.roll(t, HK - HEAD_K // 2, 1), pltpu.roll(t, HEAD_K // 2, 1))
        t3 = t.reshape(n_chunks, CHUNK, HK)
        return (t3 * cos3 + swapped.reshape(n_chunks, CHUNK, HK) * sin3).reshape(tm, HK)

    logg_hk = _per_head_lane_const(LOG_G, (CHUNK, HK), HEAD_K)
    l_idx = lax.broadcasted_iota(jnp.int32, (CHUNK, HK), 0).astype(F32)
    qdec = jnp.exp((l_idx + 1.0) * logg_hk)
    kdec = jnp.exp((CHUNK - 1.0 - l_idx) * logg_hk)
    cdec = jnp.exp(float(CHUNK) * _per_head_lane_const(LOG_G, (1, HK), HEAD_K))
    r_idx = lax.broadcasted_iota(jnp.int32, (N_HEADS * CHUNK, CHUNK), 0)
    m_idx = lax.broadcasted_iota(jnp.int32, (N_HEADS * CHUNK, CHUNK), 1)
    logg_rows = jnp.full((N_HEADS * CHUNK, CHUNK), LOG_G[N_HEADS - 1], F32)
    for hh in range(N_HEADS - 2, -1, -1):
        logg_rows = jnp.where((r_idx >> int(math.log2(CHUNK))) == hh, LOG_G[hh], logg_rows)
    dmat = jnp.exp(jnp.abs((r_idx & (CHUNK - 1)) - m_idx).astype(F32) * logg_rows)

    glr = _dot(hb, wlr_ref[...])
    rq = rotary(proj(C_RQ, HK))
    qb[...] = rq.astype(BF16)
    qdb[...] = (rq.reshape(n_chunks, CHUNK, HK) * qdec).reshape(tm, HK).astype(BF16)
    gl = _dot(glr.astype(BF16), wgk_ref[...]) + bgk_ref[...]
    rk = rotary(proj(C_RK, HK)) * (HEAD_K ** -0.5)
    kb[...] = rk.astype(BF16)
    kkb[...] = (rk.reshape(n_chunks, CHUNK, HK) * kdec).reshape(tm, HK).astype(BF16)
    log_a = (jnp.minimum(gl, 0.0) - jnp.log1p(jnp.exp(-jnp.abs(gl)))) / GATE_NORM
    la_hi = log_a.astype(BF16)
    la_lo = (log_a - la_hi.astype(F32)).astype(BF16)
    vb[...] = proj(C_RV, HV).astype(BF16)
    bdtri = bdtri_ref[...]
    bcum = _dot(bdtri, la_hi) + _dot(bdtri, la_lo)
    gqb[...] = (proj(C_GQ, HK) * (HEAD_K ** -0.5)).astype(BF16)
    gvb[...] = proj(C_GV, HV).astype(BF16)
    gk = proj(C_GK, HK)
    b3 = bcum.reshape(n_chunks, CHUNK, HK)
    bl3 = b3[:, CHUNK - 1:CHUNK, :]
    gkkb[...] = (gk.reshape(n_chunks, CHUNK, HK) * jnp.exp(bl3 - b3)).reshape(tm, HK).astype(BF16)
    ga[...] = jnp.broadcast_to(jnp.exp(bl3), (n_chunks, SUBLANES, HK))

    def stack_masked(a, width):
        head = _head_of_lane(a.shape, width)
        zero = jnp.zeros_like(a)
        return jnp.concatenate([jnp.where(head == hh, a, zero) for hh in range(N_HEADS)], axis=0)

    def heads_to_rows(a):
        return jnp.concatenate([a[:, hh * HEAD_V:(hh + 1) * HEAD_V] for hh in range(N_HEADS)], axis=0)

    def rows_to_heads(a):
        return jnp.concatenate([a[hh * CHUNK:(hh + 1) * CHUNK, :] for hh in range(N_HEADS)], axis=1)

    chunk_rows = [slice(c * CHUNK, (c + 1) * CHUNK) for c in range(n_chunks)]
    probs = [(_dot_nt(stack_masked(qb[r, :], HEAD_K), kb[r, :]) * dmat).astype(BF16) for r in chunk_rows]
    inc_ret = [_dot_tn(heads_to_rows(vb[r, :]), stack_masked(kkb[r, :], HEAD_K)) for r in chunk_rows]
    inc_gla = [_dot_tn(heads_to_rows(gvb[r, :]), stack_masked(gkkb[r, :], HEAD_K)) for r in chunk_rows]
    rg = proj(C_RG, HV)
    gg = proj(C_GG, HV)
    s_in, g_out = [], []
    s_cur, g_cur = st_ret[...], st_gla[...]
    for c in range(n_chunks):
        if is_s is None:
            s_cur, g_cur = ssi_r_ref[c], ssi_g_ref[c]
        else:
            s_cur = jnp.where(is_s, ssi_r_ref[c], s_cur)
            g_cur = jnp.where(is_s, ssi_g_ref[c], g_cur)
        s_in.append(s_cur.astype(BF16))
        s_cur = s_cur * cdec + inc_ret[c]
        g_cur = g_cur * ga[c][0:1, :] + inc_gla[c]
        g_out.append(g_cur.astype(BF16))
        sso_r_ref[c] = s_cur
        sso_g_ref[c] = g_cur
    st_ret[...] = s_cur
    st_gla[...] = g_cur
    for c, r in enumerate(chunk_rows):
        v = vb[r, :]
        intra = jnp.concatenate(
            [_dot(probs[c][hh * CHUNK:(hh + 1) * CHUNK, :], v[:, hh * HEAD_V:(hh + 1) * HEAD_V])
             for hh in range(N_HEADS)], axis=1)
        inter = rows_to_heads(_dot_nt(stack_masked(qdb[r, :], HEAD_K), s_in[c]))
        o_ret[r, :] = intra + inter
        o_gla[r, :] = rows_to_heads(_dot_nt(stack_masked(gqb[r, :], HEAD_K), g_out[c]))

    gnorm = gnorm_ref[...]
    orr = o_ret[...]
    ogg = o_gla[...]
    ret_parts, gla_parts = [], []
    for hh in range(N_HEADS):
        sl = slice(hh * HEAD_V, (hh + 1) * HEAD_V)
        oh = orr[:, sl]
        mu = jnp.mean(oh, axis=-1, keepdims=True)
        dev = oh - mu
        var = jnp.mean(dev * dev, axis=-1, keepdims=True)
        ret_parts.append(dev * lax.rsqrt(var + NORM_EPS))
        og = ogg[:, sl]
        gla_parts.append(og * lax.rsqrt(jnp.mean(og * og, axis=-1, keepdims=True) + NORM_EPS) * gnorm)
    o_r = jnp.concatenate(ret_parts, axis=1) * (rg * _sigmoid(rg))
    o_g = jnp.concatenate(gla_parts, axis=1) * (gg * _sigmoid(gg))
    merged = (_sigmoid(_dot(hb, wz_ref[:, :D_MODEL])) * _dot(o_r.astype(BF16), wbr_ref[...])
              + _sigmoid(_dot(hb, wz_ref[:, D_MODEL:])) * _dot(o_g.astype(BF16), wbg_ref[...]))
    xn = x + _dot(merged.astype(BF16), wout_ref[...])
    xnew_ref[...] = xn

    if route:
        h2 = xn * lax.rsqrt(jnp.mean(xn * xn, axis=-1, keepdims=True) + NORM_EPS) * gffn_ref[...]
        h2_hi = h2.astype(BF16)
        h2_lo = (h2 - h2_hi.astype(F32)).astype(BF16)
        hi_both = _dot(h2_hi, wrt_ref[...])
        logits = (hi_both[:, :ROUTER_LANES] + _dot(h2_lo, wrt_ref[:, :ROUTER_LANES])
                  + hi_both[:, ROUTER_LANES:]) + brt_ref[...]
        lt = logits.T
        row8 = lax.broadcasted_iota(jnp.int32, (SUBLANES, tm), 0)
        neg_inf = jnp.float32(-jnp.inf)
        glog = jnp.where(row8 < N_GROUPS, lt[0:SUBLANES, :], neg_inf)
        gmax = jnp.max(glog, axis=0, keepdims=True)
        grp = jnp.min(jnp.where(glog == gmax, row8, SUBLANES), axis=0, keepdims=True)
        p_grp = 1.0 / jnp.sum(jnp.exp(glog - gmax), axis=0, keepdims=True)
        le = jnp.zeros((SUBLANES, tm), F32)
        for g in range(N_GROUPS):
            le = jnp.where(grp == g, lt[SUBLANES * (g + 1):SUBLANES * (g + 2), :], le)
        m1 = jnp.max(le, axis=0, keepdims=True)
        i1 = jnp.min(jnp.where(le == m1, row8, SUBLANES), axis=0, keepdims=True)
        le2 = jnp.where(row8 == i1, neg_inf, le)
        m2 = jnp.max(le2, axis=0, keepdims=True)
        i2 = jnp.min(jnp.where(le2 == m2, row8, SUBLANES), axis=0, keepdims=True)
        e0 = grp * EXPERTS_PER_GROUP + i1
        e1 = grp * EXPERTS_PER_GROUP + i2
        t21 = jnp.exp(m2 - m1)
        w0 = p_grp / (1.0 + t21)
        w1 = p_grp * t21 / (1.0 + t21)

        erow = lax.broadcasted_iota(jnp.int32, (N_EXPERTS, tm), 0)
        hit0 = erow == e0
        hit1 = erow == e1
        onehot = jnp.where(jnp.logical_or(hit0, hit1), 1.0, 0.0).astype(BF16)
        cum = _dot_nt(onehot, ltri_ref[...])
        n_run = cum[:, tm - 1:tm]
        n_pad = jnp.ceil(n_run / UNIT) * UNIT
        rank0 = jnp.sum(jnp.where(hit0, cum - 1.0, 0.0), axis=0, keepdims=True)
        rank1 = jnp.sum(jnp.where(hit1, cum - 1.0, 0.0), axis=0, keepdims=True)
        start0 = jnp.sum(jnp.where(erow < e0, n_pad, 0.0), axis=0, keepdims=True)
        start1 = jnp.sum(jnp.where(erow < e1, n_pad, 0.0), axis=0, keepdims=True)
        ld0 = (start0 + rank0).astype(jnp.int32)
        ld1 = (start1 + rank1).astype(jnp.int32)
        lrow = lax.broadcasted_iota(jnp.int32, (LOCAL_ROWS, tm), 0)
        perm = jnp.where(jnp.logical_or(lrow == ld0, lrow == ld1), 1.0, 0.0).astype(BF16)
        xloc_ref[...] = _pack_halves(_dot(perm, h2_hi))
        lane_e = lax.broadcasted_iota(jnp.int32, (N_EXPERTS, LANES), 1)
        erow_l = lax.broadcasted_iota(jnp.int32, (N_EXPERTS, LANES), 0)
        units_row = jnp.sum(jnp.where(erow_l == lane_e, n_pad / UNIT, 0.0), axis=0, keepdims=True)
        ku_ref[...] = jnp.broadcast_to(units_row, (SUBLANES, LANES)).astype(jnp.int32).reshape(ku_ref.shape)
        zero_row = jnp.zeros((1, tm), jnp.int32)
        rec = jnp.concatenate([e0, e1, ld0, ld1, pltpu.bitcast(w0, jnp.int32), pltpu.bitcast(w1, jnp.int32),
                               zero_row, zero_row], axis=0)
        route_ref[...] = rec.reshape(route_ref.shape)
    else:
        xloc_ref[...] = jnp.zeros(xloc_ref.shape, jnp.int32)
        route_ref[...] = jnp.zeros(route_ref.shape, jnp.int32)
        ku_ref[...] = jnp.zeros(ku_ref.shape, jnp.int32)

    if tiles_p:
        @pl.when(jnp.logical_and(jnp.logical_not(is_s), t_idx == tiles_per_row - 1))
        def _():
            spo_r_ref[0] = st_ret[...]
            spo_g_ref[0] = st_gla[...]
    else:
        spo_r_ref[0] = st_ret[...]
        spo_g_ref[0] = st_gla[...]


def _const_spec(shape):
    nd = len(shape)
    return pl.BlockSpec(shape, lambda *_: (0,) * nd, pipeline_mode=pl.Buffered(1))


def _mixer_call(x_prompt, x_streams, cos, sin, st_prompt, st_streams, weights, consts, *, n_chunks, route):
    tm = n_chunks * CHUNK
    n_streams = x_streams.shape[0]
    tiles_s = n_streams // n_chunks
    if x_prompt is not None:
        n_rows, n_seq, _ = x_prompt.shape
        tpr = n_seq // tm
        tiles_p = n_rows * tpr
    else:
        n_rows, tpr, tiles_p = 1, 1, 0
    n_tiles = tiles_p + tiles_s
    p_idx = lambda i: jnp.minimum(i, tiles_p - 1)
    s_idx = lambda i: jnp.maximum(i - tiles_p, 0)

    st_blk = (1, HEAD_V, HK)
    in_specs, args = [], []
    if tiles_p:
        in_specs.append(pl.BlockSpec((1, tm, D_MODEL), lambda i: (p_idx(i) // tpr, p_idx(i) % tpr, 0)))
        args.append(x_prompt)
        cos_spec = pl.BlockSpec((tm, LANES), lambda i: (jnp.where(i < tiles_p, i % tpr, tpr), 0))
    else:
        cos_spec = pl.BlockSpec((tm, LANES), lambda i: (0, 0))
    once = pl.Buffered(1)
    stream_in = pl.BlockSpec((n_chunks, HEAD_V, HK), lambda i: (s_idx(i), 0, 0), pipeline_mode=once)
    in_specs += [pl.BlockSpec((n_chunks, CHUNK, D_MODEL), lambda i: (s_idx(i), 0, 0), pipeline_mode=once),
                 cos_spec, cos_spec, _const_spec(st_blk), _const_spec(st_blk), stream_in, stream_in]
    args += [x_streams, cos, sin, st_prompt[0], st_prompt[1], st_streams[0], st_streams[1]]
    in_specs += [_const_spec(w.shape) for w in weights] + [_const_spec(c.shape) for c in consts]
    args += list(weights) + list(consts)

    out_shape = (jax.ShapeDtypeStruct((n_tiles * tm, D_MODEL), F32),
                 jax.ShapeDtypeStruct((n_tiles * LOCAL_ROWS, D_MODEL // 2), jnp.int32),
                 jax.ShapeDtypeStruct((n_tiles, ROUTE_ROWS, tm), jnp.int32),
                 jax.ShapeDtypeStruct((n_tiles, SUBLANES, LANES), jnp.int32),
                 jax.ShapeDtypeStruct((n_rows, HEAD_V, HK), F32), jax.ShapeDtypeStruct((n_rows, HEAD_V, HK), F32),
                 jax.ShapeDtypeStruct((n_streams, HEAD_V, HK), F32),
                 jax.ShapeDtypeStruct((n_streams, HEAD_V, HK), F32))
    row_spec = pl.BlockSpec(st_blk, lambda i: (p_idx(i) // tpr if tiles_p else 0, 0, 0))
    stream_spec = pl.BlockSpec((n_chunks, HEAD_V, HK), lambda i: (s_idx(i), 0, 0))
    out_specs = (pl.BlockSpec((tm, D_MODEL), lambda i: (i, 0)),
                 pl.BlockSpec((LOCAL_ROWS, D_MODEL // 2), lambda i: (i, 0)),
                 pl.BlockSpec((1, ROUTE_ROWS, tm), lambda i: (i, 0, 0)),
                 pl.BlockSpec((1, SUBLANES, LANES), lambda i: (i, 0, 0)),
                 row_spec, row_spec, stream_spec, stream_spec)

    scratch = [pltpu.VMEM((tm, HK), BF16), pltpu.VMEM((tm, HK), BF16), pltpu.VMEM((tm, HK), BF16),
               pltpu.VMEM((tm, HK), BF16), pltpu.VMEM((tm, HV), BF16),
               pltpu.VMEM((tm, HK), BF16), pltpu.VMEM((tm, HK), BF16), pltpu.VMEM((tm, HV), BF16),
               pltpu.VMEM((n_chunks, SUBLANES, HK), F32),
               pltpu.VMEM((tm, HV), F32), pltpu.VMEM((tm, HV), F32),
               pltpu.VMEM((HEAD_V, HK), F32), pltpu.VMEM((HEAD_V, HK), F32)]

    return pl.pallas_call(
        functools.partial(_mixer_kernel, n_chunks=n_chunks, tiles_p=tiles_p, tiles_per_row=tpr, route=route),
        grid=(n_tiles,), in_specs=in_specs, out_specs=out_specs, out_shape=out_shape, scratch_shapes=scratch,
        compiler_params=pltpu.CompilerParams(dimension_semantics=("arbitrary",), vmem_limit_bytes=VMEM_LIMIT),
        name="mixer" if route else "mixer_meta",
    )(*args)


def _plan_kernel(ku_ref, src_ref, blk_ref, run_first, run_step, next_unit, seg_start, seg_units, *, n_tiles, n_blocks):
    shift = int(math.log2(BLOCK_UNITS))
    group_shift = int(math.log2(SUBLANES * LANES))

    def init_tile(t, c):
        next_unit[t] = t * LOCAL_UNITS
        return c

    lax.fori_loop(0, n_tiles, init_tile, 0)

    def init_block(b, c):
        blk_ref[0, b] = N_EXPERTS - 1
        blk_ref[1, b] = 0
        blk_ref[2, b] = 0
        return c

    lax.fori_loop(0, n_blocks, init_block, 0)

    def per_expert(e, g0):
        def per_tile(t, c):
            units, d_prev = c
            base = next_unit[t]
            d = base - units
            run_first[e * n_tiles + t] = units
            run_step[e * n_tiles + t] = d - d_prev
            k = ku_ref[t, e]
            next_unit[t] = base + k
            return units + k, d

        units, _ = lax.fori_loop(0, n_tiles, per_tile, (0, 0), unroll=4)
        g_pad = g0 + (((units + (BLOCK_UNITS - 1)) >> shift) << shift)
        seg_start[e] = g0
        seg_units[e] = units

        def set_block(b, c):
            blk_ref[0, b] = e
            blk_ref[1, b] = jnp.minimum(g0 + units - (b << shift), BLOCK_UNITS)
            return c

        lax.fori_loop(g0 >> shift, g_pad >> shift, set_block, 0)
        return g_pad

    g_total = lax.fori_loop(0, N_EXPERTS, per_expert, 0)
    blk_ref[2, 0] = g_total >> shift

    src_ref[...] = jnp.zeros(src_ref.shape, jnp.int32)
    in_group = (lax.broadcasted_iota(jnp.int32, (SUBLANES, LANES), 0) * LANES
                + lax.broadcasted_iota(jnp.int32, (SUBLANES, LANES), 1))

    def expert_units(e, c):
        g0 = seg_start[e]
        units = seg_units[e]
        g_pad = g0 + (((units + (BLOCK_UNITS - 1)) >> shift) << shift)

        def per_group(grp, c2):
            g = in_group + (grp << group_shift)
            o = g - g0
            mine = jnp.logical_and(o >= 0, g < g_pad)
            o_eff = jnp.where(o < units, o, (o >> shift) << shift)

            def per_tile(t, acc):
                return acc + jnp.where(o_eff >= run_first[e * n_tiles + t], run_step[e * n_tiles + t], 0)

            offset = lax.fori_loop(0, n_tiles, per_tile, jnp.zeros((SUBLANES, LANES), jnp.int32), unroll=4)
            rows = pl.ds(pl.multiple_of(grp * SUBLANES, SUBLANES), SUBLANES)
            src_ref[rows, :] = jnp.where(mine, o_eff + offset, src_ref[rows, :])
            return c2

        lax.fori_loop(g0 >> group_shift, (g_pad + (SUBLANES * LANES - 1)) >> group_shift, per_group, 0)
        return c

    lax.fori_loop(0, N_EXPERTS, expert_units, 0)


def _plan_call(ku, n_blocks):
    n_tiles = ku.shape[0]
    smem = pl.BlockSpec(memory_space=pltpu.SMEM)
    group = SUBLANES * LANES
    src_rows = -(-n_blocks * BLOCK_UNITS // group) * SUBLANES
    src, blk = pl.pallas_call(
        functools.partial(_plan_kernel, n_tiles=n_tiles, n_blocks=n_blocks),
        in_specs=[smem], out_specs=(pl.BlockSpec(memory_space=pltpu.VMEM), smem),
        out_shape=(jax.ShapeDtypeStruct((src_rows, LANES), jnp.int32),
                   jax.ShapeDtypeStruct((3, n_blocks), jnp.int32)),
        scratch_shapes=[pltpu.SMEM((N_EXPERTS * n_tiles,), jnp.int32), pltpu.SMEM((N_EXPERTS * n_tiles,), jnp.int32),
                        pltpu.SMEM((n_tiles,), jnp.int32), pltpu.SMEM((N_EXPERTS,), jnp.int32),
                        pltpu.SMEM((N_EXPERTS,), jnp.int32)],
        name="moe_plan",
    )(ku)
    return src.reshape(-1), blk


def _experts_kernel(src_ref, blk_ref, x_hbm, wg_ref, wu_ref, wd_ref, y_hbm,
                    xin, yout, wgub, wdb, sem_in, sem_out):
    b = pl.program_id(0)
    used = blk_ref[2, 0]
    slot = b & 1

    def unit_rows(blk, u):
        return pl.ds(pl.multiple_of(src_ref[blk * BLOCK_UNITS + u] * UNIT, UNIT), UNIT)

    def in_copy(blk, u, s):
        return pltpu.make_async_copy(x_hbm.at[unit_rows(blk, u), :], xin.at[s, pl.ds(u * UNIT, UNIT), :],
                                     sem_in.at[s])

    def out_copy(blk, u, s):
        return pltpu.make_async_copy(yout.at[s, pl.ds(u * UNIT, UNIT), :], y_hbm.at[unit_rows(blk, u), :],
                                     sem_out.at[s])

    @pl.when(jnp.logical_and(b == 0, used > 0))
    def _():
        for u in range(BLOCK_UNITS):
            in_copy(0, u, 0).start()

    @pl.when(b < used)
    def _():
        expert = blk_ref[0, b]
        changed = jnp.logical_or(b == 0, expert != blk_ref[0, jnp.maximum(b - 1, 0)])

        @pl.when(changed)
        def _():
            wgub[:, :D_EXPERT] = wg_ref[0].astype(BF16)
            wgub[:, D_EXPERT:] = wu_ref[0].astype(BF16)
            wdb[...] = wd_ref[0].astype(BF16)

        @pl.when(b + 1 < used)
        def _():
            for u in range(BLOCK_UNITS):
                in_copy(b + 1, u, 1 - slot).start()

        for u in range(BLOCK_UNITS):
            in_copy(b, u, slot).wait()

        n_real = blk_ref[1, b]

        def mlp(rows):
            xb = _unpack_halves(xin[slot, :rows, :])
            gate_up = _dot(xb, wgub[...])
            gate = gate_up[:, :D_EXPERT]
            hid = (gate * _sigmoid(gate)) * gate_up[:, D_EXPERT:]
            yout[slot, :rows, :] = _pack_halves(_dot(hid.astype(BF16), wdb[...]).astype(BF16).astype(F32))

        @pl.when(n_real > BLOCK_UNITS // 2)
        def _():
            mlp(MOE_BLOCK)

        @pl.when(n_real <= BLOCK_UNITS // 2)
        def _():
            mlp(MOE_BLOCK // 2)


        @pl.when(n_real == BLOCK_UNITS)
        def _():
            for u in range(BLOCK_UNITS):
                out_copy(b, u, slot).start()

        @pl.when(n_real < BLOCK_UNITS)
        def _():
            lax.fori_loop(0, n_real, lambda u, c: (out_copy(b, u, slot).start(), c)[1], 0)

        @pl.when(b > 0)
        def _():
            lax.fori_loop(0, blk_ref[1, b - 1], lambda u, c: (out_copy(b - 1, u, 1 - slot).wait(), c)[1], 0)

        @pl.when(b == used - 1)
        def _():
            lax.fori_loop(0, n_real, lambda u, c: (out_copy(b, u, slot).wait(), c)[1], 0)


def _experts_call(src, blk, xloc, wg, wu, wd, n_blocks):
    w_in_spec = pl.BlockSpec((1, D_MODEL, D_EXPERT), lambda b, s, m: (m[0, b], 0, 0))
    grid_spec = pltpu.PrefetchScalarGridSpec(
        num_scalar_prefetch=2,
        grid=(n_blocks,),
        in_specs=[pl.BlockSpec(memory_space=pl.ANY), w_in_spec, w_in_spec,
                  pl.BlockSpec((1, D_EXPERT, D_MODEL), lambda b, s, m: (m[0, b], 0, 0))],
        out_specs=pl.BlockSpec(memory_space=pl.ANY),
        scratch_shapes=[pltpu.VMEM((2, MOE_BLOCK, D_MODEL // 2), jnp.int32),
                        pltpu.VMEM((2, MOE_BLOCK, D_MODEL // 2), jnp.int32),
                        pltpu.VMEM((D_MODEL, 2 * D_EXPERT), BF16), pltpu.VMEM((D_EXPERT, D_MODEL), BF16),
                        pltpu.SemaphoreType.DMA((2,)), pltpu.SemaphoreType.DMA((2,))],
    )
    return pl.pallas_call(
        _experts_kernel, grid_spec=grid_spec,
        out_shape=jax.ShapeDtypeStruct(xloc.shape, jnp.int32),
        input_output_aliases={2: 0},
        compiler_params=pltpu.CompilerParams(dimension_semantics=("arbitrary",), vmem_limit_bytes=VMEM_LIMIT),
        name="moe_experts",
    )(src, blk, xloc, wg, wu, wd)


def _combine_kernel(route_ref, x_ref, y_ref, gfin_ref, out_a_ref, out_b_ref, *, tm, tiles_a):
    t = pl.program_id(0)
    rec = route_ref[0]
    recf = jnp.concatenate([rec[2:4, :].astype(F32), pltpu.bitcast(rec, F32)[4:6, :],
                            jnp.zeros((LANES - 4, tm), F32)], axis=0)
    cols = recf.T
    lrow = lax.broadcasted_iota(jnp.int32, (tm, LOCAL_ROWS), 1).astype(F32)
    select = jnp.where(lrow == cols[:, 0:1], cols[:, 2:3], 0.0) + jnp.where(lrow == cols[:, 1:2], cols[:, 3:4], 0.0)
    xf = x_ref[...] + _dot(select.astype(BF16), _unpack_halves(y_ref[...]))
    out = xf * lax.rsqrt(jnp.mean(xf * xf, axis=-1, keepdims=True) + NORM_EPS) * gfin_ref[...]

    @pl.when(t < tiles_a)
    def _():
        out_a_ref[...] = out

    @pl.when(t >= tiles_a)
    def _():
        out_b_ref[...] = out


def _combine_call(route, xnew, yloc, gfin, tm, tiles_a):
    n_tiles = route.shape[0]
    tiles_b = n_tiles - tiles_a
    return pl.pallas_call(
        functools.partial(_combine_kernel, tm=tm, tiles_a=tiles_a),
        grid=(n_tiles,),
        in_specs=[pl.BlockSpec((1, ROUTE_ROWS, tm), lambda t: (t, 0, 0)),
                  pl.BlockSpec((tm, D_MODEL), lambda t: (t, 0)),
                  pl.BlockSpec((LOCAL_ROWS, D_MODEL // 2), lambda t: (t, 0)),
                  pl.BlockSpec((1, D_MODEL), lambda t: (0, 0))],
        out_specs=(pl.BlockSpec((tm, D_MODEL), lambda t: (jnp.minimum(t, tiles_a - 1), 0)),
                   pl.BlockSpec((tm, D_MODEL), lambda t: (jnp.maximum(t - tiles_a, 0), 0))),
        out_shape=(jax.ShapeDtypeStruct((tiles_a * tm, D_MODEL), F32),
                   jax.ShapeDtypeStruct((tiles_b * tm, D_MODEL), F32)),
        compiler_params=pltpu.CompilerParams(dimension_semantics=("arbitrary",), vmem_limit_bytes=VMEM_LIMIT),
        name="moe_combine",
    )(route, xnew, yloc, gfin)


def _prep_weights(g_mix, w_in, w_gla_gk2, b_gla_gk, g_gla_norm, w_br_ret, w_br_gla, w_out, g_ffn,
                  w_rg, b_rg, w_re, b_re):
    w_mix = w_in[:, :W_MIX].astype(BF16)
    w_lr = jnp.pad(w_in[:, W_MIX:W_MIX + GLA_RANK], ((0, 0), (0, LANES - GLA_RANK))).astype(BF16)
    w_z = w_in[:, W_MIX + GLA_RANK:].astype(BF16)
    wgk = jnp.pad(w_gla_gk2, ((0, LANES - GLA_RANK), (0, 0))).astype(BF16)
    wrt = jnp.zeros((D_MODEL, ROUTER_LANES), F32)
    wrt = wrt.at[:, 0:N_GROUPS].set(w_rg).at[:, SUBLANES:SUBLANES + N_EXPERTS].set(w_re)
    brt = jnp.zeros((1, ROUTER_LANES), F32)
    brt = brt.at[0, 0:N_GROUPS].set(b_rg).at[0, SUBLANES:SUBLANES + N_EXPERTS].set(b_re)
    wrt_hi = wrt.astype(BF16)
    wrt = jnp.concatenate([wrt_hi, (wrt - wrt_hi.astype(F32)).astype(BF16)], axis=1)
    return (g_mix.reshape(1, D_MODEL), w_mix, w_lr, w_z, wgk, b_gla_gk.reshape(1, HK), g_gla_norm.reshape(1, HEAD_V),
            w_br_ret.astype(BF16), w_br_gla.astype(BF16), w_out.astype(BF16), g_ffn.reshape(1, D_MODEL), wrt, brt)


def _tri_consts(tm):
    r = np.arange(tm)
    low = (r[None, :] <= r[:, None])
    bd = low & ((r[None, :] // CHUNK) == (r[:, None] // CHUNK))
    return jnp.asarray(bd, BF16), jnp.asarray(low, BF16)


def _state_to_kernel(s):
    return jnp.swapaxes(s.reshape(s.shape[0], HK, HEAD_V), 1, 2)


def _state_from_kernel(s):
    return jnp.swapaxes(s, 1, 2).reshape(1, s.shape[0], N_HEADS, HEAD_K, HEAD_V)


def kernel(x_prompt, x_sample, state_ret, state_gla, meta_tokens, g_mix, w_in, w_gla_gk2, b_gla_gk, g_gla_norm, w_br_ret, w_br_gla, w_out, g_ffn, w_router_group, b_router_group, w_router_expert, b_router_expert, w_exp_gate, w_exp_up, w_exp_down, g_final):
    n_b, seq, _ = x_prompt.shape
    n_s, dec_seq, _ = x_sample.shape
    depth = state_ret.shape[0]
    assert depth == 1 and dec_seq == CHUNK and seq % TILE_ROWS == 0 and n_s % TILE_CHUNKS == 0
    cos_t, sin_t, cos_m, sin_m = _rotary_tables(seq, PAST_LEN)
    weights = _prep_weights(g_mix[0], w_in[0], w_gla_gk2[0], b_gla_gk[0], g_gla_norm[0], w_br_ret[0], w_br_gla[0],
                            w_out[0], g_ffn[0], w_router_group[0], b_router_group[0], w_router_expert[0],
                            b_router_expert[0])
    zero_state = jnp.zeros((1, HEAD_V, HK), F32)

    x_meta = jnp.concatenate([jnp.zeros((CHUNK - N_META, D_MODEL), F32), meta_tokens.astype(F32)], axis=0)
    meta_out = _mixer_call(None, x_meta.reshape(1, CHUNK, D_MODEL), cos_m, sin_m, (zero_state, zero_state),
                           (zero_state, zero_state), weights, _tri_consts(CHUNK), n_chunks=1, route=False)
    meta_ret, meta_gla = meta_out[6], meta_out[7]

    xnew, xloc, route, ku, ret_p, gla_p, ret_s, gla_s = _mixer_call(
        x_prompt, x_sample, cos_t, sin_t, (meta_ret, meta_gla),
        (_state_to_kernel(state_ret[0]), _state_to_kernel(state_gla[0])), weights, _tri_consts(TILE_ROWS),
        n_chunks=TILE_CHUNKS, route=True)

    n_tiles = route.shape[0]
    n_blocks = n_tiles * LOCAL_UNITS // BLOCK_UNITS + N_EXPERTS
    src, blk = _plan_call(ku[:, 0, :N_EXPERTS], n_blocks)
    yloc = _experts_call(src, blk, xloc, w_exp_gate[0], w_exp_up[0], w_exp_down[0], n_blocks)
    y_p, y_s = _combine_call(route, xnew, yloc, g_final.reshape(1, D_MODEL), TILE_ROWS, n_b * seq // TILE_ROWS)

    return (y_p.reshape(n_b, seq, D_MODEL), y_s.reshape(n_s, dec_seq, D_MODEL),
            _state_from_kernel(ret_p), _state_from_kernel(gla_p), _state_from_kernel(ret_s), _state_from_kernel(gla_s))
```

```python
import functools
import math

import jax
import jax.numpy as jnp
import numpy as np
from jax import lax
from jax.experimental import pallas as pl
from jax.experimental.pallas import tpu as pltpu

D_MODEL = 1024
CHUNK = 64
PAST_LEN = 1024
N_META = 16
N_HEADS = 4
HEAD_K = 64
HEAD_V = 128
HK = N_HEADS * HEAD_K
HV = N_HEADS * HEAD_V
GLA_RANK = 16
GATE_NORM = 16.0
ROPE_BASE = 10000.0
N_GROUPS = 4
EXPERTS_PER_GROUP = 8
N_EXPERTS = N_GROUPS * EXPERTS_PER_GROUP
D_EXPERT = 512
NORM_EPS = 1e-6

LANES = 128
SUBLANES = 8
TILE_CHUNKS = 8
TILE_ROWS = TILE_CHUNKS * CHUNK
ROUTE_ROWS = 8
ROUTER_LANES = 128
UNIT = SUBLANES
MOE_BLOCK = 512
BLOCK_UNITS = MOE_BLOCK // UNIT
BLK_ROWS = 5
LOCAL_ROWS = 2 * TILE_ROWS + N_EXPERTS * UNIT
LOCAL_UNITS = LOCAL_ROWS // UNIT
VMEM_LIMIT = 56 * 1024 * 1024

C_RQ, C_RK, C_RV, C_RG = 0, 256, 512, 1024
C_GQ, C_GK, C_GV, C_GG = 1536, 1792, 2048, 2560
W_MIX = 3072

F32 = jnp.float32
BF16 = jnp.bfloat16
LOG_G = tuple(math.log1p(-(2.0 ** (-5.0 - h))) for h in range(N_HEADS))


def _dot(a, b):
    return jnp.dot(a, b, preferred_element_type=F32)


def _dot_nt(a, b):
    return lax.dot_general(a, b, (((1,), (1,)), ((), ())), preferred_element_type=F32)


def _dot_tn(a, b):
    return lax.dot_general(a, b, (((0,), (0,)), ((), ())), preferred_element_type=F32)


def _sigmoid(x):
    return 1.0 / (1.0 + jnp.exp(-x))


def _pack_halves(x):
    half = x.shape[1] // 2
    bits = pltpu.bitcast(x, jnp.int32)
    return lax.shift_right_logical(bits[:, :half], 16) | (bits[:, half:] & jnp.int32(-65536))


def _unpack_halves(w):
    lo = pltpu.bitcast(lax.shift_left(w, 16), F32)
    hi = pltpu.bitcast(w & jnp.int32(-65536), F32)
    return jnp.concatenate([lo, hi], axis=1).astype(BF16)


def _head_of_lane(shape, width):
    return lax.broadcasted_iota(jnp.int32, shape, len(shape) - 1) >> int(math.log2(width))


def _per_head_lane_const(vals, shape, width):
    hd = _head_of_lane(shape, width)
    out = jnp.full(shape, vals[N_HEADS - 1], F32)
    for h in range(N_HEADS - 2, -1, -1):
        out = jnp.where(hd == h, vals[h], out)
    return out


def _tables_kernel(inv_ref, cp_ref, sp_ref, cm_ref, sm_ref, *, seq, past_len):
    inv = inv_ref[...].reshape(1, 1, LANES)
    lane = lax.broadcasted_iota(jnp.int32, (1, 1, LANES), 2)
    sign = jnp.where((lane & (HEAD_K - 1)) < (HEAD_K // 2), -1.0, 1.0)
    off = lax.broadcasted_iota(jnp.int32, (1, CHUNK, LANES), 1).astype(F32) * inv
    c_off, s_off = jnp.cos(off), jnp.sin(off)

    def chunks(n, first_pos):
        base = (lax.broadcasted_iota(jnp.int32, (n, 1, LANES), 0) * CHUNK + first_pos).astype(F32) * inv
        c_base, s_base = jnp.cos(base), jnp.sin(base)
        cos = (c_base * c_off - s_base * s_off).reshape(n * CHUNK, LANES)
        sin = ((s_base * c_off + c_base * s_off) * sign).reshape(n * CHUNK, LANES)
        return cos, sin

    cp_ref[0:seq, :], sp_ref[0:seq, :] = chunks(seq // CHUNK, 0)
    cos_s, sin_s = chunks(1, past_len)
    for c in range(TILE_CHUNKS):
        cp_ref[seq + c * CHUNK:seq + (c + 1) * CHUNK, :] = cos_s
        sp_ref[seq + c * CHUNK:seq + (c + 1) * CHUNK, :] = sin_s
    cm_ref[...], sm_ref[...] = chunks(1, -CHUNK)


def _rotary_tables(seq, past_len):
    half = HEAD_K // 2
    inv = ROPE_BASE ** (-2.0 * jnp.arange(half, dtype=F32) / HEAD_K)
    inv = jnp.tile(inv, LANES // half).reshape(1, LANES)
    shp = lambda r: jax.ShapeDtypeStruct((r, LANES), F32)
    return pl.pallas_call(
        functools.partial(_tables_kernel, seq=seq, past_len=past_len),
        out_shape=(shp(seq + TILE_ROWS), shp(seq + TILE_ROWS), shp(CHUNK), shp(CHUNK)),
        compiler_params=pltpu.CompilerParams(vmem_limit_bytes=VMEM_LIMIT),
        name="rotary_tables",
    )(inv)


def _mixer_kernel(*refs, n_chunks, tiles_p, tiles_per_row, route):
    tm = n_chunks * CHUNK
    if tiles_p:
        xp_ref, refs = refs[0], refs[1:]
    (xs_ref, cos_ref, sin_ref, spi_r_ref, spi_g_ref, ssi_r_ref, ssi_g_ref,
     gmix_ref, wmix_ref, wlr_ref, wz_ref, wgk_ref, bgk_ref, gnorm_ref, wbr_ref, wbg_ref, wout_ref,
     gffn_ref, wrt_ref, brt_ref, bdtri_ref, ltri_ref,
     xnew_ref, xloc_ref, route_ref, ku_ref, spo_r_ref, spo_g_ref, sso_r_ref, sso_g_ref,
     qb, qdb, kb, kkb, vb, gqb, gkkb, gvb, ga, o_ret, o_gla, st_ret, st_gla) = refs

    i = pl.program_id(0)
    if tiles_p:
        is_s = i >= tiles_p
        t_idx = jnp.minimum(i, tiles_p - 1) % tiles_per_row
        x = jnp.where(is_s, xs_ref[...].reshape(tm, D_MODEL), xp_ref[...].reshape(tm, D_MODEL))

        @pl.when(jnp.logical_and(jnp.logical_not(is_s), t_idx == 0))
        def _():
            st_ret[...] = spi_r_ref[0]
            st_gla[...] = spi_g_ref[0]
    else:
        is_s = None
        x = xs_ref[...].reshape(tm, D_MODEL)

    h = x * lax.rsqrt(jnp.mean(x * x, axis=-1, keepdims=True) + NORM_EPS) * gmix_ref[...]
    hb = h.astype(BF16)

    def proj(c0, width):
        return _dot(hb, wmix_ref[:, c0:c0 + width])

    cos3 = jnp.concatenate([cos_ref[...]] * 2, axis=1).reshape(n_chunks, CHUNK, HK)
    sin3 = jnp.concatenate([sin_ref[...]] * 2, axis=1).reshape(n_chunks, CHUNK, HK)
    lane_hk = lax.broadcasted_iota(jnp.int32, (tm, HK), 1)
    first_half = (lane_hk & (HEAD_K - 1)) < (HEAD_K // 2)

    def rotary(t):
        swapped = jnp.where(first_half, pltpu.roll(t, HK - HEAD_K // 2, 1), pltpu.roll(t, HEAD_K // 2, 1))
        t3 = t.reshape(n_chunks, CHUNK, HK)
        return (t3 * cos3 + swapped.reshape(n_chunks, CHUNK, HK) * sin3).reshape(tm, HK)

    logg_hk = _per_head_lane_const(LOG_G, (CHUNK, HK), HEAD_K)
    l_idx = lax.broadcasted_iota(jnp.int32, (CHUNK, HK), 0).astype(F32)
    qdec = jnp.exp((l_idx + 1.0) * logg_hk)
    kdec = jnp.exp((CHUNK - 1.0 - l_idx) * logg_hk)
    cdec = jnp.exp(float(CHUNK) * _per_head_lane_const(LOG_G, (1, HK), HEAD_K))
    r_idx = lax.broadcasted_iota(jnp.int32, (N_HEADS * CHUNK, CHUNK), 0)
    m_idx = lax.broadcasted_iota(jnp.int32, (N_HEADS * CHUNK, CHUNK), 1)
    logg_rows = jnp.full((N_HEADS * CHUNK, CHUNK), LOG_G[N_HEADS - 1], F32)
    for hh in range(N_HEADS - 2, -1, -1):
        logg_rows = jnp.where((r_idx >> int(math.log2(CHUNK))) == hh, LOG_G[hh], logg_rows)
    dmat = jnp.exp(jnp.abs((r_idx & (CHUNK - 1)) - m_idx).astype(F32) * logg_rows)

    glr = _dot(hb, wlr_ref[...])
    rq = rotary(proj(C_RQ, HK))
    qb[...] = rq.astype(BF16)
    qdb[...] = (rq.reshape(n_chunks, CHUNK, HK) * qdec).reshape(tm, HK).astype(BF16)
    gl = _dot(glr.astype(BF16), wgk_ref[...]) + bgk_ref[...]
    rk = rotary(proj(C_RK, HK)) * (HEAD_K ** -0.5)
    kb[...] = rk.astype(BF16)
    kkb[...] = (rk.reshape(n_chunks, CHUNK, HK) * kdec).reshape(tm, HK).astype(BF16)
    log_a = (jnp.minimum(gl, 0.0) - jnp.log1p(jnp.exp(-jnp.abs(gl)))) / GATE_NORM
    la_hi = log_a.astype(BF16)
    la_lo = (log_a - la_hi.astype(F32)).astype(BF16)
    vb[...] = proj(C_RV, HV).astype(BF16)
    bdtri = bdtri_ref[...]
    bcum = _dot(bdtri, la_hi) + _dot(bdtri, la_lo)
    gqb[...] = (proj(C_GQ, HK) * (HEAD_K ** -0.5)).astype(BF16)
    gvb[...] = proj(C_GV, HV).astype(BF16)
    gk = proj(C_GK, HK)
    b3 = bcum.reshape(n_chunks, CHUNK, HK)
    bl3 = b3[:, CHUNK - 1:CHUNK, :]
    gkkb[...] = (gk.reshape(n_chunks, CHUNK, HK) * jnp.exp(bl3 - b3)).reshape(tm, HK).astype(BF16)
    ga[...] = jnp.broadcast_to(jnp.exp(bl3), (n_chunks, SUBLANES, HK))

    def stack_masked(a, width):
        head = _head_of_lane(a.shape, width)
        zero = jnp.zeros_like(a)
        return jnp.concatenate([jnp.where(head == hh, a, zero) for hh in range(N_HEADS)], axis=0)

    def heads_to_rows(a):
        return jnp.concatenate([a[:, hh * HEAD_V:(hh + 1) * HEAD_V] for hh in range(N_HEADS)], axis=0)

    def rows_to_heads(a):
        return jnp.concatenate([a[hh * CHUNK:(hh + 1) * CHUNK, :] for hh in range(N_HEADS)], axis=1)

    chunk_rows = [slice(c * CHUNK, (c + 1) * CHUNK) for c in range(n_chunks)]
    probs = [(_dot_nt(stack_masked(qb[r, :], HEAD_K), kb[r, :]) * dmat).astype(BF16) for r in chunk_rows]
    inc_ret = [_dot_tn(heads_to_rows(vb[r, :]), stack_masked(kkb[r, :], HEAD_K)) for r in chunk_rows]
    inc_gla = [_dot_tn(heads_to_rows(gvb[r, :]), stack_masked(gkkb[r, :], HEAD_K)) for r in chunk_rows]
    rg = proj(C_RG, HV)
    gg = proj(C_GG, HV)
    s_in, g_out = [], []
    s_cur, g_cur = st_ret[...], st_gla[...]
    for c in range(n_chunks):
        if is_s is None:
            s_cur, g_cur = ssi_r_ref[c], ssi_g_ref[c]
        else:
            s_cur = jnp.where(is_s, ssi_r_ref[c], s_cur)
            g_cur = jnp.where(is_s, ssi_g_ref[c], g_cur)
        s_in.append(s_cur.astype(BF16))
        s_cur = s_cur * cdec + inc_ret[c]
        g_cur = g_cur * ga[c][0:1, :] + inc_gla[c]
        g_out.append(g_cur.astype(BF16))
        sso_r_ref[c] = s_cur
        sso_g_ref[c] = g_cur
    st_ret[...] = s_cur
    st_gla[...] = g_cur
    for c, r in enumerate(chunk_rows):
        v = vb[r, :]
        intra = jnp.concatenate(
            [_dot(probs[c][hh * CHUNK:(hh + 1) * CHUNK, :], v[:, hh * HEAD_V:(hh + 1) * HEAD_V])
             for hh in range(N_HEADS)], axis=1)
        inter = rows_to_heads(_dot_nt(stack_masked(qdb[r, :], HEAD_K), s_in[c]))
        o_ret[r, :] = intra + inter
        o_gla[r, :] = rows_to_heads(_dot_nt(stack_masked(gqb[r, :], HEAD_K), g_out[c]))

    gnorm = gnorm_ref[...]
    orr = o_ret[...]
    ogg = o_gla[...]
    ret_parts, gla_parts = [], []
    for hh in range(N_HEADS):
        sl = slice(hh * HEAD_V, (hh + 1) * HEAD_V)
        oh = orr[:, sl]
        mu = jnp.mean(oh, axis=-1, keepdims=True)
        dev = oh - mu
        var = jnp.mean(dev * dev, axis=-1, keepdims=True)
        ret_parts.append(dev * lax.rsqrt(var + NORM_EPS))
        og = ogg[:, sl]
        gla_parts.append(og * lax.rsqrt(jnp.mean(og * og, axis=-1, keepdims=True) + NORM_EPS) * gnorm)
    o_r = jnp.concatenate(ret_parts, axis=1) * (rg * _sigmoid(rg))
    o_g = jnp.concatenate(gla_parts, axis=1) * (gg * _sigmoid(gg))
    merged = (_sigmoid(_dot(hb, wz_ref[:, :D_MODEL])) * _dot(o_r.astype(BF16), wbr_ref[...])
              + _sigmoid(_dot(hb, wz_ref[:, D_MODEL:])) * _dot(o_g.astype(BF16), wbg_ref[...]))
    xn = x + _dot(merged.astype(BF16), wout_ref[...])
    xnew_ref[...] = xn

    if route:
        h2 = xn * lax.rsqrt(jnp.mean(xn * xn, axis=-1, keepdims=True) + NORM_EPS) * gffn_ref[...]
        h2_hi = h2.astype(BF16)
        h2_lo = (h2 - h2_hi.astype(F32)).astype(BF16)
        hi_both = _dot(h2_hi, wrt_ref[...])
        logits = (hi_both[:, :ROUTER_LANES] + _dot(h2_lo, wrt_ref[:, :ROUTER_LANES])
                  + hi_both[:, ROUTER_LANES:]) + brt_ref[...]
        lt = logits.T
        row8 = lax.broadcasted_iota(jnp.int32, (SUBLANES, tm), 0)
        neg_inf = jnp.float32(-jnp.inf)
        glog = jnp.where(row8 < N_GROUPS, lt[0:SUBLANES, :], neg_inf)
        gmax = jnp.max(glog, axis=0, keepdims=True)
        grp = jnp.min(jnp.where(glog == gmax, row8, SUBLANES), axis=0, keepdims=True)
        p_grp = 1.0 / jnp.sum(jnp.exp(glog - gmax), axis=0, keepdims=True)
        le = jnp.zeros((SUBLANES, tm), F32)
        for g in range(N_GROUPS):
            le = jnp.where(grp == g, lt[SUBLANES * (g + 1):SUBLANES * (g + 2), :], le)
        m1 = jnp.max(le, axis=0, keepdims=True)
        i1 = jnp.min(jnp.where(le == m1, row8, SUBLANES), axis=0, keepdims=True)
        le2 = jnp.where(row8 == i1, neg_inf, le)
        m2 = jnp.max(le2, axis=0, keepdims=True)
        i2 = jnp.min(jnp.where(le2 == m2, row8, SUBLANES), axis=0, keepdims=True)
        e0 = grp * EXPERTS_PER_GROUP + i1
        e1 = grp * EXPERTS_PER_GROUP + i2
        t21 = jnp.exp(m2 - m1)
        w0 = p_grp / (1.0 + t21)
        w1 = p_grp * t21 / (1.0 + t21)

        erow = lax.broadcasted_iota(jnp.int32, (N_EXPERTS, tm), 0)
        hit0 = erow == e0
        hit1 = erow == e1
        onehot = jnp.where(jnp.logical_or(hit0, hit1), 1.0, 0.0).astype(BF16)
        cum = _dot_nt(onehot, ltri_ref[...])
        n_run = cum[:, tm - 1:tm]
        n_pad = jnp.ceil(n_run / UNIT) * UNIT
        rank0 = jnp.sum(jnp.where(hit0, cum - 1.0, 0.0), axis=0, keepdims=True)
        rank1 = jnp.sum(jnp.where(hit1, cum - 1.0, 0.0), axis=0, keepdims=True)
        start0 = jnp.sum(jnp.where(erow < e0, n_pad, 0.0), axis=0, keepdims=True)
        start1 = jnp.sum(jnp.where(erow < e1, n_pad, 0.0), axis=0, keepdims=True)
        ld0 = (start0 + rank0).astype(jnp.int32)
        ld1 = (start1 + rank1).astype(jnp.int32)
        lrow = lax.broadcasted_iota(jnp.int32, (LOCAL_ROWS, tm), 0)
        perm = jnp.where(jnp.logical_or(lrow == ld0, lrow == ld1), 1.0, 0.0).astype(BF16)
        xloc_ref[...] = _pack_halves(_dot(perm, h2_hi))
        lane_e = lax.broadcasted_iota(jnp.int32, (N_EXPERTS, LANES), 1)
        erow_l = lax.broadcasted_iota(jnp.int32, (N_EXPERTS, LANES), 0)
        units_row = jnp.sum(jnp.where(erow_l == lane_e, n_pad / UNIT, 0.0), axis=0, keepdims=True)
        ku_ref[...] = jnp.broadcast_to(units_row, (SUBLANES, LANES)).astype(jnp.int32).reshape(ku_ref.shape)
        zero_row = jnp.zeros((1, tm), jnp.int32)
        rec = jnp.concatenate([e0, e1, ld0, ld1, pltpu.bitcast(w0, jnp.int32), pltpu.bitcast(w1, jnp.int32),
                               zero_row, zero_row], axis=0)
        route_ref[...] = rec.reshape(route_ref.shape)
    else:
        xloc_ref[...] = jnp.zeros(xloc_ref.shape, jnp.int32)
        route_ref[...] = jnp.zeros(route_ref.shape, jnp.int32)
        ku_ref[...] = jnp.zeros(ku_ref.shape, jnp.int32)

    if tiles_p:
        @pl.when(jnp.logical_and(jnp.logical_not(is_s), t_idx == tiles_per_row - 1))
        def _():
            spo_r_ref[0] = st_ret[...]
            spo_g_ref[0] = st_gla[...]
    else:
        spo_r_ref[0] = st_ret[...]
        spo_g_ref[0] = st_gla[...]


def _const_spec(shape):
    nd = len(shape)
    return pl.BlockSpec(shape, lambda *_: (0,) * nd, pipeline_mode=pl.Buffered(1))


def _mixer_call(x_prompt, x_streams, cos, sin, st_prompt, st_streams, weights, consts, *, n_chunks, route):
    tm = n_chunks * CHUNK
    n_streams = x_streams.shape[0]
    tiles_s = n_streams // n_chunks
    if x_prompt is not None:
        n_rows, n_seq, _ = x_prompt.shape
        tpr = n_seq // tm
        tiles_p = n_rows * tpr
    else:
        n_rows, tpr, tiles_p = 1, 1, 0
    n_tiles = tiles_p + tiles_s
    p_idx = lambda i: jnp.minimum(i, tiles_p - 1)
    s_idx = lambda i: jnp.maximum(i - tiles_p, 0)

    st_blk = (1, HEAD_V, HK)
    in_specs, args = [], []
    if tiles_p:
        in_specs.append(pl.BlockSpec((1, tm, D_MODEL), lambda i: (p_idx(i) // tpr, p_idx(i) % tpr, 0)))
        args.append(x_prompt)
        cos_spec = pl.BlockSpec((tm, LANES), lambda i: (jnp.where(i < tiles_p, i % tpr, tpr), 0))
    else:
        cos_spec = pl.BlockSpec((tm, LANES), lambda i: (0, 0))
    once = pl.Buffered(1)
    stream_in = pl.BlockSpec((n_chunks, HEAD_V, HK), lambda i: (s_idx(i), 0, 0), pipeline_mode=once)
    in_specs += [pl.BlockSpec((n_chunks, CHUNK, D_MODEL), lambda i: (s_idx(i), 0, 0), pipeline_mode=once),
                 cos_spec, cos_spec, _const_spec(st_blk), _const_spec(st_blk), stream_in, stream_in]
    args += [x_streams, cos, sin, st_prompt[0], st_prompt[1], st_streams[0], st_streams[1]]
    in_specs += [_const_spec(w.shape) for w in weights] + [_const_spec(c.shape) for c in consts]
    args += list(weights) + list(consts)

    out_shape = (jax.ShapeDtypeStruct((n_tiles * tm, D_MODEL), F32),
                 jax.ShapeDtypeStruct((n_tiles * LOCAL_ROWS, D_MODEL // 2), jnp.int32),
                 jax.ShapeDtypeStruct((n_tiles, ROUTE_ROWS, tm), jnp.int32),
                 jax.ShapeDtypeStruct((n_tiles, SUBLANES, LANES), jnp.int32),
                 jax.ShapeDtypeStruct((n_rows, HEAD_V, HK), F32), jax.ShapeDtypeStruct((n_rows, HEAD_V, HK), F32),
                 jax.ShapeDtypeStruct((n_streams, HEAD_V, HK), F32),
                 jax.ShapeDtypeStruct((n_streams, HEAD_V, HK), F32))
    row_spec = pl.BlockSpec(st_blk, lambda i: (p_idx(i) // tpr if tiles_p else 0, 0, 0))
    stream_spec = pl.BlockSpec((n_chunks, HEAD_V, HK), lambda i: (s_idx(i), 0, 0))
    out_specs = (pl.BlockSpec((tm, D_MODEL), lambda i: (i, 0)),
                 pl.BlockSpec((LOCAL_ROWS, D_MODEL // 2), lambda i: (i, 0)),
                 pl.BlockSpec((1, ROUTE_ROWS, tm), lambda i: (i, 0, 0)),
                 pl.BlockSpec((1, SUBLANES, LANES), lambda i: (i, 0, 0)),
                 row_spec, row_spec, stream_spec, stream_spec)

    scratch = [pltpu.VMEM((tm, HK), BF16), pltpu.VMEM((tm, HK), BF16), pltpu.VMEM((tm, HK), BF16),
               pltpu.VMEM((tm, HK), BF16), pltpu.VMEM((tm, HV), BF16),
               pltpu.VMEM((tm, HK), BF16), pltpu.VMEM((tm, HK), BF16), pltpu.VMEM((tm, HV), BF16),
               pltpu.VMEM((n_chunks, SUBLANES, HK), F32),
               pltpu.VMEM((tm, HV), F32), pltpu.VMEM((tm, HV), F32),
               pltpu.VMEM((HEAD_V, HK), F32), pltpu.VMEM((HEAD_V, HK), F32)]

    return pl.pallas_call(
        functools.partial(_mixer_kernel, n_chunks=n_chunks, tiles_p=tiles_p, tiles_per_row=tpr, route=route),
        grid=(n_tiles,), in_specs=in_specs, out_specs=out_specs, out_shape=out_shape, scratch_shapes=scratch,
        compiler_params=pltpu.CompilerParams(dimension_semantics=("arbitrary",), vmem_limit_bytes=VMEM_LIMIT),
        name="mixer" if route else "mixer_meta",
    )(*args)


def _plan_kernel(ku_ref, src_ref, blk_ref, run_first, run_step, next_unit, seg_start, seg_units, *, n_tiles, n_blocks):
    shift = int(math.log2(BLOCK_UNITS))
    group_shift = int(math.log2(SUBLANES * LANES))

    def init_tile(t, c):
        next_unit[t] = t * LOCAL_UNITS
        return c

    lax.fori_loop(0, n_tiles, init_tile, 0)

    def init_block(b, c):
        blk_ref[0, b] = N_EXPERTS - 1
        blk_ref[1, b] = 0
        blk_ref[2, b] = 0
        blk_ref[3, b] = -1
        blk_ref[4, b] = 0
        return c

    lax.fori_loop(0, n_blocks, init_block, 0)

    def per_expert(e, carry):
        g0, position = carry

        def per_tile(t, c):
            units, d_prev = c
            base = next_unit[t]
            d = base - units
            run_first[e * n_tiles + t] = units
            run_step[e * n_tiles + t] = d - d_prev
            k = ku_ref[t, e]
            next_unit[t] = base + k
            return units + k, d

        units, _ = lax.fori_loop(0, n_tiles, per_tile, (0, 0), unroll=4)
        g_pad = g0 + (((units + (BLOCK_UNITS - 1)) >> shift) << shift)
        seg_start[e] = g0
        seg_units[e] = units

        def set_block(b, c):
            blk_ref[0, b] = e
            blk_ref[1, b] = jnp.minimum(g0 + units - (b << shift), BLOCK_UNITS)
            blk_ref[4, b] = position & 1
            return c

        lax.fori_loop(g0 >> shift, g_pad >> shift, set_block, 0)
        return g_pad, position + (units > 0).astype(jnp.int32)

    g_total, _ = lax.fori_loop(0, N_EXPERTS, per_expert, (0, 0))
    used = g_total >> shift
    blk_ref[2, 0] = used

    def set_next(i, c):
        later, following = c
        b = used - 1 - i
        e = blk_ref[0, b]
        following = jnp.where(e != later, later, following)
        blk_ref[3, b] = following
        return e, following

    lax.fori_loop(0, used, set_next, (-1, -1))

    src_ref[...] = jnp.zeros(src_ref.shape, jnp.int32)
    in_group = (lax.broadcasted_iota(jnp.int32, (SUBLANES, LANES), 0) * LANES
                + lax.broadcasted_iota(jnp.int32, (SUBLANES, LANES), 1))

    def expert_units(e, c):
        g0 = seg_start[e]
        units = seg_units[e]
        g_pad = g0 + (((units + (BLOCK_UNITS - 1)) >> shift) << shift)

        def per_group(grp, c2):
            g = in_group + (grp << group_shift)
            o = g - g0
            mine = jnp.logical_and(o >= 0, g < g_pad)
            o_eff = jnp.where(o < units, o, (o >> shift) << shift)

            def per_tile(t, acc):
                return acc + jnp.where(o_eff >= run_first[e * n_tiles + t], run_step[e * n_tiles + t], 0)

            offset = lax.fori_loop(0, n_tiles, per_tile, jnp.zeros((SUBLANES, LANES), jnp.int32), unroll=4)
            rows = pl.ds(pl.multiple_of(grp * SUBLANES, SUBLANES), SUBLANES)
            src_ref[rows, :] = jnp.where(mine, o_eff + offset, src_ref[rows, :])
            return c2

        lax.fori_loop(g0 >> group_shift, (g_pad + (SUBLANES * LANES - 1)) >> group_shift, per_group, 0)
        return c

    lax.fori_loop(0, N_EXPERTS, expert_units, 0)


def _plan_call(ku, n_blocks):
    n_tiles = ku.shape[0]
    smem = pl.BlockSpec(memory_space=pltpu.SMEM)
    group = SUBLANES * LANES
    src_rows = -(-n_blocks * BLOCK_UNITS // group) * SUBLANES
    src, blk = pl.pallas_call(
        functools.partial(_plan_kernel, n_tiles=n_tiles, n_blocks=n_blocks),
        in_specs=[smem], out_specs=(pl.BlockSpec(memory_space=pltpu.VMEM), smem),
        out_shape=(jax.ShapeDtypeStruct((src_rows, LANES), jnp.int32),
                   jax.ShapeDtypeStruct((BLK_ROWS, n_blocks), jnp.int32)),
        scratch_shapes=[pltpu.SMEM((N_EXPERTS * n_tiles,), jnp.int32), pltpu.SMEM((N_EXPERTS * n_tiles,), jnp.int32),
                        pltpu.SMEM((n_tiles,), jnp.int32), pltpu.SMEM((N_EXPERTS,), jnp.int32),
                        pltpu.SMEM((N_EXPERTS,), jnp.int32)],
        name="moe_plan",
    )(ku)
    return src.reshape(-1), blk


def _experts_kernel(src_ref, blk_ref, x_hbm, wg_hbm, wu_hbm, wd_hbm, y_hbm,
                    xin, yout, wgf, wuf, wdf, wgub, wdb, sem_in, sem_out, sem_w):
    b = pl.program_id(0)
    used = blk_ref[2, 0]
    slot = b & 1

    def weight_copies(e, s):
        return [pltpu.make_async_copy(hbm.at[e], buf.at[s], sem_w.at[s])
                for hbm, buf in ((wg_hbm, wgf), (wu_hbm, wuf), (wd_hbm, wdf))]

    def unit_rows(blk, u):
        return pl.ds(pl.multiple_of(src_ref[blk * BLOCK_UNITS + u] * UNIT, UNIT), UNIT)

    def in_copy(blk, u, s):
        return pltpu.make_async_copy(x_hbm.at[unit_rows(blk, u), :], xin.at[s, pl.ds(u * UNIT, UNIT), :],
                                     sem_in.at[s])

    def out_copy(blk, u, s):
        return pltpu.make_async_copy(yout.at[s, pl.ds(u * UNIT, UNIT), :], y_hbm.at[unit_rows(blk, u), :],
                                     sem_out.at[s])

    @pl.when(jnp.logical_and(b == 0, used > 0))
    def _():
        for copy in weight_copies(blk_ref[0, 0], blk_ref[4, 0]):
            copy.start()
        for u in range(BLOCK_UNITS):
            in_copy(0, u, 0).start()

    @pl.when(b < used)
    def _():
        expert = blk_ref[0, b]
        changed = jnp.logical_or(b == 0, expert != blk_ref[0, jnp.maximum(b - 1, 0)])

        @pl.when(changed)
        def _():
            wslot = blk_ref[4, b]
            following = blk_ref[3, b]
            for copy in weight_copies(expert, wslot):
                copy.wait()

            @pl.when(following >= 0)
            def _():
                for copy in weight_copies(following, 1 - wslot):
                    copy.start()

            wgub[:, :D_EXPERT] = wgf[wslot].astype(BF16)
            wgub[:, D_EXPERT:] = wuf[wslot].astype(BF16)
            wdb[...] = wdf[wslot].astype(BF16)

        @pl.when(b + 1 < used)
        def _():
            for u in range(BLOCK_UNITS):
                in_copy(b + 1, u, 1 - slot).start()

        for u in range(BLOCK_UNITS):
            in_copy(b, u, slot).wait()

        xb = _unpack_halves(xin[slot])
        gate_up = _dot(xb, wgub[...])
        gate = gate_up[:, :D_EXPERT]
        hid = (gate * _sigmoid(gate)) * gate_up[:, D_EXPERT:]
        yout[slot] = _pack_halves(_dot(hid.astype(BF16), wdb[...]).astype(BF16).astype(F32))

        n_real = blk_ref[1, b]

        @pl.when(n_real == BLOCK_UNITS)
        def _():
            for u in range(BLOCK_UNITS):
                out_copy(b, u, slot).start()

        @pl.when(n_real < BLOCK_UNITS)
        def _():
            lax.fori_loop(0, n_real, lambda u, c: (out_copy(b, u, slot).start(), c)[1], 0)

        @pl.when(b > 0)
        def _():
            lax.fori_loop(0, blk_ref[1, b - 1], lambda u, c: (out_copy(b - 1, u, 1 - slot).wait(), c)[1], 0)

        @pl.when(b == used - 1)
        def _():
            lax.fori_loop(0, n_real, lambda u, c: (out_copy(b, u, slot).wait(), c)[1], 0)


def _experts_call(src, blk, xloc, wg, wu, wd, n_blocks):
    hbm = pl.BlockSpec(memory_space=pl.ANY)
    grid_spec = pltpu.PrefetchScalarGridSpec(
        num_scalar_prefetch=2,
        grid=(n_blocks,),
        in_specs=[hbm, hbm, hbm, hbm],
        out_specs=hbm,
        scratch_shapes=[pltpu.VMEM((2, MOE_BLOCK, D_MODEL // 2), jnp.int32),
                        pltpu.VMEM((2, MOE_BLOCK, D_MODEL // 2), jnp.int32),
                        pltpu.VMEM((2, D_MODEL, D_EXPERT), F32), pltpu.VMEM((2, D_MODEL, D_EXPERT), F32),
                        pltpu.VMEM((2, D_EXPERT, D_MODEL), F32),
                        pltpu.VMEM((D_MODEL, 2 * D_EXPERT), BF16), pltpu.VMEM((D_EXPERT, D_MODEL), BF16),
                        pltpu.SemaphoreType.DMA((2,)), pltpu.SemaphoreType.DMA((2,)),
                        pltpu.SemaphoreType.DMA((2,))],
    )
    return pl.pallas_call(
        _experts_kernel, grid_spec=grid_spec,
        out_shape=jax.ShapeDtypeStruct(xloc.shape, jnp.int32),
        input_output_aliases={2: 0},
        compiler_params=pltpu.CompilerParams(dimension_semantics=("arbitrary",), vmem_limit_bytes=VMEM_LIMIT),
        name="moe_experts",
    )(src, blk, xloc, wg, wu, wd)


def _combine_kernel(route_ref, x_ref, y_ref, gfin_ref, out_a_ref, out_b_ref, *, tm, tiles_a):
    t = pl.program_id(0)
    rec = route_ref[0]
    recf = jnp.concatenate([rec[2:4, :].astype(F32), pltpu.bitcast(rec, F32)[4:6, :],
                            jnp.zeros((LANES - 4, tm), F32)], axis=0)
    cols = recf.T
    lrow = lax.broadcasted_iota(jnp.int32, (tm, LOCAL_ROWS), 1).astype(F32)
    select = jnp.where(lrow == cols[:, 0:1], cols[:, 2:3], 0.0) + jnp.where(lrow == cols[:, 1:2], cols[:, 3:4], 0.0)
    xf = x_ref[...] + _dot(select.astype(BF16), _unpack_halves(y_ref[...]))
    out = xf * lax.rsqrt(jnp.mean(xf * xf, axis=-1, keepdims=True) + NORM_EPS) * gfin_ref[...]

    @pl.when(t < tiles_a)
    def _():
        out_a_ref[...] = out

    @pl.when(t >= tiles_a)
    def _():
        out_b_ref[...] = out


def _combine_call(route, xnew, yloc, gfin, tm, tiles_a):
    n_tiles = route.shape[0]
    tiles_b = n_tiles - tiles_a
    return pl.pallas_call(
        functools.partial(_combine_kernel, tm=tm, tiles_a=tiles_a),
        grid=(n_tiles,),
        in_specs=[pl.BlockSpec((1, ROUTE_ROWS, tm), lambda t: (t, 0, 0)),
                  pl.BlockSpec((tm, D_MODEL), lambda t: (t, 0)),
                  pl.BlockSpec((LOCAL_ROWS, D_MODEL // 2), lambda t: (t, 0)),
                  pl.BlockSpec((1, D_MODEL), lambda t: (0, 0))],
        out_specs=(pl.BlockSpec((tm, D_MODEL), lambda t: (jnp.minimum(t, tiles_a - 1), 0)),
                   pl.BlockSpec((tm, D_MODEL), lambda t: (jnp.maximum(t - tiles_a, 0), 0))),
        out_shape=(jax.ShapeDtypeStruct((tiles_a * tm, D_MODEL), F32),
                   jax.ShapeDtypeStruct((tiles_b * tm, D_MODEL), F32)),
        compiler_params=pltpu.CompilerParams(dimension_semantics=("arbitrary",), vmem_limit_bytes=VMEM_LIMIT),
        name="moe_combine",
    )(route, xnew, yloc, gfin)


def _prep_weights(g_mix, w_in, w_gla_gk2, b_gla_gk, g_gla_norm, w_br_ret, w_br_gla, w_out, g_ffn,
                  w_rg, b_rg, w_re, b_re):
    w_mix = w_in[:, :W_MIX].astype(BF16)
    w_lr = jnp.pad(w_in[:, W_MIX:W_MIX + GLA_RANK], ((0, 0), (0, LANES - GLA_RANK))).astype(BF16)
    w_z = w_in[:, W_MIX + GLA_RANK:].astype(BF16)
    wgk = jnp.pad(w_gla_gk2, ((0, LANES - GLA_RANK), (0, 0))).astype(BF16)
    wrt = jnp.zeros((D_MODEL, ROUTER_LANES), F32)
    wrt = wrt.at[:, 0:N_GROUPS].set(w_rg).at[:, SUBLANES:SUBLANES + N_EXPERTS].set(w_re)
    brt = jnp.zeros((1, ROUTER_LANES), F32)
    brt = brt.at[0, 0:N_GROUPS].set(b_rg).at[0, SUBLANES:SUBLANES + N_EXPERTS].set(b_re)
    wrt_hi = wrt.astype(BF16)
    wrt = jnp.concatenate([wrt_hi, (wrt - wrt_hi.astype(F32)).astype(BF16)], axis=1)
    return (g_mix.reshape(1, D_MODEL), w_mix, w_lr, w_z, wgk, b_gla_gk.reshape(1, HK), g_gla_norm.reshape(1, HEAD_V),
            w_br_ret.astype(BF16), w_br_gla.astype(BF16), w_out.astype(BF16), g_ffn.reshape(1, D_MODEL), wrt, brt)


def _tri_consts(tm):
    r = np.arange(tm)
    low = (r[None, :] <= r[:, None])
    bd = low & ((r[None, :] // CHUNK) == (r[:, None] // CHUNK))
    return jnp.asarray(bd, BF16), jnp.asarray(low, BF16)


def _state_to_kernel(s):
    return jnp.swapaxes(s.reshape(s.shape[0], HK, HEAD_V), 1, 2)


def _state_from_kernel(s):
    return jnp.swapaxes(s, 1, 2).reshape(1, s.shape[0], N_HEADS, HEAD_K, HEAD_V)


def kernel(x_prompt, x_sample, state_ret, state_gla, meta_tokens, g_mix, w_in, w_gla_gk2, b_gla_gk, g_gla_norm, w_br_ret, w_br_gla, w_out, g_ffn, w_router_group, b_router_group, w_router_expert, b_router_expert, w_exp_gate, w_exp_up, w_exp_down, g_final):
    n_b, seq, _ = x_prompt.shape
    n_s, dec_seq, _ = x_sample.shape
    depth = state_ret.shape[0]
    assert depth == 1 and dec_seq == CHUNK and seq % TILE_ROWS == 0 and n_s % TILE_CHUNKS == 0
    cos_t, sin_t, cos_m, sin_m = _rotary_tables(seq, PAST_LEN)
    weights = _prep_weights(g_mix[0], w_in[0], w_gla_gk2[0], b_gla_gk[0], g_gla_norm[0], w_br_ret[0], w_br_gla[0],
                            w_out[0], g_ffn[0], w_router_group[0], b_router_group[0], w_router_expert[0],
                            b_router_expert[0])
    zero_state = jnp.zeros((1, HEAD_V, HK), F32)

    x_meta = jnp.concatenate([jnp.zeros((CHUNK - N_META, D_MODEL), F32), meta_tokens.astype(F32)], axis=0)
    meta_out = _mixer_call(None, x_meta.reshape(1, CHUNK, D_MODEL), cos_m, sin_m, (zero_state, zero_state),
                           (zero_state, zero_state), weights, _tri_consts(CHUNK), n_chunks=1, route=False)
    meta_ret, meta_gla = meta_out[6], meta_out[7]

    xnew, xloc, route, ku, ret_p, gla_p, ret_s, gla_s = _mixer_call(
        x_prompt, x_sample, cos_t, sin_t, (meta_ret, meta_gla),
        (_state_to_kernel(state_ret[0]), _state_to_kernel(state_gla[0])), weights, _tri_consts(TILE_ROWS),
        n_chunks=TILE_CHUNKS, route=True)

    n_tiles = route.shape[0]
    n_blocks = n_tiles * LOCAL_UNITS // BLOCK_UNITS + N_EXPERTS
    src, blk = _plan_call(ku[:, 0, :N_EXPERTS], n_blocks)
    yloc = _experts_call(src, blk, xloc, w_exp_gate[0], w_exp_up[0], w_exp_down[0], n_blocks)
    y_p, y_s = _combine_call(route, xnew, yloc, g_final.reshape(1, D_MODEL), TILE_ROWS, n_b * seq // TILE_ROWS)

    return (y_p.reshape(n_b, seq, D_MODEL), y_s.reshape(n_s, dec_seq, D_MODEL),
            _state_from_kernel(ret_p), _state_from_kernel(gla_p), _state_from_kernel(ret_s), _state_from_kernel(gla_s))
```

```python
import functools
import math

import jax
import jax.numpy as jnp
import numpy as np
from jax import lax
from jax.experimental import pallas as pl
from jax.experimental.pallas import tpu as pltpu

D_MODEL = 1024
CHUNK = 64
PAST_LEN = 1024
N_META = 16
N_HEADS = 4
HEAD_K = 64
HEAD_V = 128
HK = N_HEADS * HEAD_K
HV = N_HEADS * HEAD_V
GLA_RANK = 16
GATE_NORM = 16.0
ROPE_BASE = 10000.0
N_GROUPS = 4
EXPERTS_PER_GROUP = 8
N_EXPERTS = N_GROUPS * EXPERTS_PER_GROUP
D_EXPERT = 512
NORM_EPS = 1e-6

LANES = 128
SUBLANES = 8
TILE_CHUNKS = 8
TILE_ROWS = TILE_CHUNKS * CHUNK
ROUTE_ROWS = 8
ROUTER_LANES = 128
UNIT = SUBLANES
MOE_BLOCK = 512
BLOCK_UNITS = MOE_BLOCK // UNIT
BLK_ROWS = 5
LOCAL_ROWS = 2 * TILE_ROWS + N_EXPERTS * UNIT
LOCAL_UNITS = LOCAL_ROWS // UNIT
VMEM_LIMIT = 56 * 1024 * 1024

C_RQ, C_RK, C_RV, C_RG = 0, 256, 512, 1024
C_GQ, C_GK, C_GV, C_GG = 1536, 1792, 2048, 2560
W_MIX = 3072

F32 = jnp.float32
BF16 = jnp.bfloat16
LOG_G = tuple(math.log1p(-(2.0 ** (-5.0 - h))) for h in range(N_HEADS))


def _dot(a, b):
    return jnp.dot(a, b, preferred_element_type=F32)


def _dot_nt(a, b):
    return lax.dot_general(a, b, (((1,), (1,)), ((), ())), preferred_element_type=F32)


def _dot_tn(a, b):
    return lax.dot_general(a, b, (((0,), (0,)), ((), ())), preferred_element_type=F32)


def _sigmoid(x):
    return 1.0 / (1.0 + jnp.exp(-x))


def _pack_halves(x):
    half = x.shape[1] // 2
    bits = pltpu.bitcast(x, jnp.int32)
    return lax.shift_right_logical(bits[:, :half], 16) | (bits[:, half:] & jnp.int32(-65536))


def _unpack_halves(w):
    lo = pltpu.bitcast(lax.shift_left(w, 16), F32)
    hi = pltpu.bitcast(w & jnp.int32(-65536), F32)
    return jnp.concatenate([lo, hi], axis=1).astype(BF16)


def _head_of_lane(shape, width):
    return lax.broadcasted_iota(jnp.int32, shape, len(shape) - 1) >> int(math.log2(width))


def _per_head_lane_const(vals, shape, width):
    hd = _head_of_lane(shape, width)
    out = jnp.full(shape, vals[N_HEADS - 1], F32)
    for h in range(N_HEADS - 2, -1, -1):
        out = jnp.where(hd == h, vals[h], out)
    return out


def _tables_kernel(inv_ref, cp_ref, sp_ref, cm_ref, sm_ref, *, seq, past_len):
    inv = inv_ref[...].reshape(1, 1, LANES)
    lane = lax.broadcasted_iota(jnp.int32, (1, 1, LANES), 2)
    sign = jnp.where((lane & (HEAD_K - 1)) < (HEAD_K // 2), -1.0, 1.0)
    off = lax.broadcasted_iota(jnp.int32, (1, CHUNK, LANES), 1).astype(F32) * inv
    c_off, s_off = jnp.cos(off), jnp.sin(off)

    def chunks(n, first_pos):
        base = (lax.broadcasted_iota(jnp.int32, (n, 1, LANES), 0) * CHUNK + first_pos).astype(F32) * inv
        c_base, s_base = jnp.cos(base), jnp.sin(base)
        cos = (c_base * c_off - s_base * s_off).reshape(n * CHUNK, LANES)
        sin = ((s_base * c_off + c_base * s_off) * sign).reshape(n * CHUNK, LANES)
        return cos, sin

    cp_ref[0:seq, :], sp_ref[0:seq, :] = chunks(seq // CHUNK, 0)
    cos_s, sin_s = chunks(1, past_len)
    for c in range(TILE_CHUNKS):
        cp_ref[seq + c * CHUNK:seq + (c + 1) * CHUNK, :] = cos_s
        sp_ref[seq + c * CHUNK:seq + (c + 1) * CHUNK, :] = sin_s
    cm_ref[...], sm_ref[...] = chunks(1, -CHUNK)


def _rotary_tables(seq, past_len):
    half = HEAD_K // 2
    inv = ROPE_BASE ** (-2.0 * jnp.arange(half, dtype=F32) / HEAD_K)
    inv = jnp.tile(inv, LANES // half).reshape(1, LANES)
    shp = lambda r: jax.ShapeDtypeStruct((r, LANES), F32)
    return pl.pallas_call(
        functools.partial(_tables_kernel, seq=seq, past_len=past_len),
        out_shape=(shp(seq + TILE_ROWS), shp(seq + TILE_ROWS), shp(CHUNK), shp(CHUNK)),
        compiler_params=pltpu.CompilerParams(vmem_limit_bytes=VMEM_LIMIT),
        name="rotary_tables",
    )(inv)


def _mixer_kernel(*refs, n_chunks, tiles_p, tiles_per_row, route):
    tm = n_chunks * CHUNK
    if tiles_p:
        xp_ref, refs = refs[0], refs[1:]
    (xs_ref, cos_ref, sin_ref, spi_r_ref, spi_g_ref, ssi_r_ref, ssi_g_ref,
     gmix_ref, wmix_ref, wlr_ref, wz_ref, wgk_ref, bgk_ref, gnorm_ref, wbr_ref, wbg_ref, wout_ref,
     gffn_ref, wrt_ref, brt_ref, bdtri_ref, ltri_ref,
     xnew_ref, xloc_ref, route_ref, ku_ref, spo_r_ref, spo_g_ref, sso_r_ref, sso_g_ref,
     qb, qdb, kb, kkb, vb, gqb, gkkb, gvb, ga, o_ret, o_gla, st_ret, st_gla) = refs

    i = pl.program_id(0)
    if tiles_p:
        is_s = i >= tiles_p
        t_idx = jnp.minimum(i, tiles_p - 1) % tiles_per_row
        x = jnp.where(is_s, xs_ref[...].reshape(tm, D_MODEL), xp_ref[...].reshape(tm, D_MODEL))

        @pl.when(jnp.logical_and(jnp.logical_not(is_s), t_idx == 0))
        def _():
            st_ret[...] = spi_r_ref[0]
            st_gla[...] = spi_g_ref[0]
    else:
        is_s = None
        x = xs_ref[...].reshape(tm, D_MODEL)

    h = x * lax.rsqrt(jnp.mean(x * x, axis=-1, keepdims=True) + NORM_EPS) * gmix_ref[...]
    hb = h.astype(BF16)

    def proj(c0, width):
        return _dot(hb, wmix_ref[:, c0:c0 + width])

    cos3 = jnp.concatenate([cos_ref[...]] * 2, axis=1).reshape(n_chunks, CHUNK, HK)
    sin3 = jnp.concatenate([sin_ref[...]] * 2, axis=1).reshape(n_chunks, CHUNK, HK)
    lane_hk = lax.broadcasted_iota(jnp.int32, (tm, HK), 1)
    first_half = (lane_hk & (HEAD_K - 1)) < (HEAD_K // 2)

    def rotary(t):
        swapped = jnp.where(first_half, pltpu.roll(t, HK - HEAD_K // 2, 1), pltpu.roll(t, HEAD_K // 2, 1))
        t3 = t.reshape(n_chunks, CHUNK, HK)
        return (t3 * cos3 + swapped.reshape(n_chunks, CHUNK, HK) * sin3).reshape(tm, HK)

    logg_hk = _per_head_lane_const(LOG_G, (CHUNK, HK), HEAD_K)
    l_idx = lax.broadcasted_iota(jnp.int32, (CHUNK, HK), 0).astype(F32)
    qdec = jnp.exp((l_idx + 1.0) * logg_hk)
    kdec = jnp.exp((CHUNK - 1.0 - l_idx) * logg_hk)
    cdec = jnp.exp(float(CHUNK) * _per_head_lane_const(LOG_G, (1, HK), HEAD_K))
    r_idx = lax.broadcasted_iota(jnp.int32, (N_HEADS * CHUNK, CHUNK), 0)
    m_idx = lax.broadcasted_iota(jnp.int32, (N_HEADS * CHUNK, CHUNK), 1)
    logg_rows = jnp.full((N_HEADS * CHUNK, CHUNK), LOG_G[N_HEADS - 1], F32)
    for hh in range(N_HEADS - 2, -1, -1):
        logg_rows = jnp.where((r_idx >> int(math.log2(CHUNK))) == hh, LOG_G[hh], logg_rows)
    dmat = jnp.exp(jnp.abs((r_idx & (CHUNK - 1)) - m_idx).astype(F32) * logg_rows)

    glr = _dot(hb, wlr_ref[...])
    rq = rotary(proj(C_RQ, HK))
    qb[...] = rq.astype(BF16)
    qdb[...] = (rq.reshape(n_chunks, CHUNK, HK) * qdec).reshape(tm, HK).astype(BF16)
    gl = _dot(glr.astype(BF16), wgk_ref[...]) + bgk_ref[...]
    rk = rotary(proj(C_RK, HK)) * (HEAD_K ** -0.5)
    kb[...] = rk.astype(BF16)
    kkb[...] = (rk.reshape(n_chunks, CHUNK, HK) * kdec).reshape(tm, HK).astype(BF16)
    log_a = (jnp.minimum(gl, 0.0) - jnp.log1p(jnp.exp(-jnp.abs(gl)))) / GATE_NORM
    la_hi = log_a.astype(BF16)
    la_lo = (log_a - la_hi.astype(F32)).astype(BF16)
    vb[...] = proj(C_RV, HV).astype(BF16)
    bdtri = bdtri_ref[...]
    bcum = _dot(bdtri, la_hi) + _dot(bdtri, la_lo)
    gqb[...] = (proj(C_GQ, HK) * (HEAD_K ** -0.5)).astype(BF16)
    gvb[...] = proj(C_GV, HV).astype(BF16)
    gk = proj(C_GK, HK)
    b3 = bcum.reshape(n_chunks, CHUNK, HK)
    bl3 = b3[:, CHUNK - 1:CHUNK, :]
    gkkb[...] = (gk.reshape(n_chunks, CHUNK, HK) * jnp.exp(bl3 - b3)).reshape(tm, HK).astype(BF16)
    ga[...] = jnp.broadcast_to(jnp.exp(bl3), (n_chunks, SUBLANES, HK))

    def stack_masked(a, width):
        head = _head_of_lane(a.shape, width)
        zero = jnp.zeros_like(a)
        return jnp.concatenate([jnp.where(head == hh, a, zero) for hh in range(N_HEADS)], axis=0)

    def heads_to_rows(a):
        return jnp.concatenate([a[:, hh * HEAD_V:(hh + 1) * HEAD_V] for hh in range(N_HEADS)], axis=0)

    def rows_to_heads(a):
        return jnp.concatenate([a[hh * CHUNK:(hh + 1) * CHUNK, :] for hh in range(N_HEADS)], axis=1)

    chunk_rows = [slice(c * CHUNK, (c + 1) * CHUNK) for c in range(n_chunks)]
    probs = [(_dot_nt(stack_masked(qb[r, :], HEAD_K), kb[r, :]) * dmat).astype(BF16) for r in chunk_rows]
    inc_ret = [_dot_tn(heads_to_rows(vb[r, :]), stack_masked(kkb[r, :], HEAD_K)) for r in chunk_rows]
    inc_gla = [_dot_tn(heads_to_rows(gvb[r, :]), stack_masked(gkkb[r, :], HEAD_K)) for r in chunk_rows]
    rg = proj(C_RG, HV)
    gg = proj(C_GG, HV)
    s_in, g_out = [], []
    s_cur, g_cur = st_ret[...], st_gla[...]
    for c in range(n_chunks):
        if is_s is None:
            s_cur, g_cur = ssi_r_ref[c], ssi_g_ref[c]
        else:
            s_cur = jnp.where(is_s, ssi_r_ref[c], s_cur)
            g_cur = jnp.where(is_s, ssi_g_ref[c], g_cur)
        s_in.append(s_cur.astype(BF16))
        s_cur = s_cur * cdec + inc_ret[c]
        g_cur = g_cur * ga[c][0:1, :] + inc_gla[c]
        g_out.append(g_cur.astype(BF16))
        sso_r_ref[c] = s_cur
        sso_g_ref[c] = g_cur
    st_ret[...] = s_cur
    st_gla[...] = g_cur
    for c, r in enumerate(chunk_rows):
        v = vb[r, :]
        intra = jnp.concatenate(
            [_dot(probs[c][hh * CHUNK:(hh + 1) * CHUNK, :], v[:, hh * HEAD_V:(hh + 1) * HEAD_V])
             for hh in range(N_HEADS)], axis=1)
        inter = rows_to_heads(_dot_nt(stack_masked(qdb[r, :], HEAD_K), s_in[c]))
        o_ret[r, :] = intra + inter
        o_gla[r, :] = rows_to_heads(_dot_nt(stack_masked(gqb[r, :], HEAD_K), g_out[c]))

    gnorm = gnorm_ref[...]
    orr = o_ret[...]
    ogg = o_gla[...]
    ret_parts, gla_parts = [], []
    for hh in range(N_HEADS):
        sl = slice(hh * HEAD_V, (hh + 1) * HEAD_V)
        oh = orr[:, sl]
        mu = jnp.mean(oh, axis=-1, keepdims=True)
        dev = oh - mu
        var = jnp.mean(dev * dev, axis=-1, keepdims=True)
        ret_parts.append(dev * lax.rsqrt(var + NORM_EPS))
        og = ogg[:, sl]
        gla_parts.append(og * lax.rsqrt(jnp.mean(og * og, axis=-1, keepdims=True) + NORM_EPS) * gnorm)
    o_r = jnp.concatenate(ret_parts, axis=1) * (rg * _sigmoid(rg))
    o_g = jnp.concatenate(gla_parts, axis=1) * (gg * _sigmoid(gg))
    merged = (_sigmoid(_dot(hb, wz_ref[:, :D_MODEL])) * _dot(o_r.astype(BF16), wbr_ref[...])
              + _sigmoid(_dot(hb, wz_ref[:, D_MODEL:])) * _dot(o_g.astype(BF16), wbg_ref[...]))
    xn = x + _dot(merged.astype(BF16), wout_ref[...])
    xnew_ref[...] = xn

    if route:
        h2 = xn * lax.rsqrt(jnp.mean(xn * xn, axis=-1, keepdims=True) + NORM_EPS) * gffn_ref[...]
        h2_hi = h2.astype(BF16)
        h2_lo = (h2 - h2_hi.astype(F32)).astype(BF16)
        hi_both = _dot(h2_hi, wrt_ref[...])
        logits = (hi_both[:, :ROUTER_LANES] + _dot(h2_lo, wrt_ref[:, :ROUTER_LANES])
                  + hi_both[:, ROUTER_LANES:]) + brt_ref[...]
        lt = logits.T
        row8 = lax.broadcasted_iota(jnp.int32, (SUBLANES, tm), 0)
        neg_inf = jnp.float32(-jnp.inf)
        glog = jnp.where(row8 < N_GROUPS, lt[0:SUBLANES, :], neg_inf)
        gmax = jnp.max(glog, axis=0, keepdims=True)
        grp = jnp.min(jnp.where(glog == gmax, row8, SUBLANES), axis=0, keepdims=True)
        p_grp = 1.0 / jnp.sum(jnp.exp(glog - gmax), axis=0, keepdims=True)
        le = jnp.zeros((SUBLANES, tm), F32)
        for g in range(N_GROUPS):
            le = jnp.where(grp == g, lt[SUBLANES * (g + 1):SUBLANES * (g + 2), :], le)
        m1 = jnp.max(le, axis=0, keepdims=True)
        i1 = jnp.min(jnp.where(le == m1, row8, SUBLANES), axis=0, keepdims=True)
        le2 = jnp.where(row8 == i1, neg_inf, le)
        m2 = jnp.max(le2, axis=0, keepdims=True)
        i2 = jnp.min(jnp.where(le2 == m2, row8, SUBLANES), axis=0, keepdims=True)
        e0 = grp * EXPERTS_PER_GROUP + i1
        e1 = grp * EXPERTS_PER_GROUP + i2
        t21 = jnp.exp(m2 - m1)
        w0 = p_grp / (1.0 + t21)
        w1 = p_grp * t21 / (1.0 + t21)

        erow = lax.broadcasted_iota(jnp.int32, (N_EXPERTS, tm), 0)
        hit0 = erow == e0
        hit1 = erow == e1
        onehot = jnp.where(jnp.logical_or(hit0, hit1), 1.0, 0.0).astype(BF16)
        cum = _dot_nt(onehot, ltri_ref[...])
        n_run = cum[:, tm - 1:tm]
        n_pad = jnp.ceil(n_run / UNIT) * UNIT
        rank0 = jnp.sum(jnp.where(hit0, cum - 1.0, 0.0), axis=0, keepdims=True)
        rank1 = jnp.sum(jnp.where(hit1, cum - 1.0, 0.0), axis=0, keepdims=True)
        start0 = jnp.sum(jnp.where(erow < e0, n_pad, 0.0), axis=0, keepdims=True)
        start1 = jnp.sum(jnp.where(erow < e1, n_pad, 0.0), axis=0, keepdims=True)
        ld0 = (start0 + rank0).astype(jnp.int32)
        ld1 = (start1 + rank1).astype(jnp.int32)
        lrow = lax.broadcasted_iota(jnp.int32, (LOCAL_ROWS, tm), 0)
        perm = jnp.where(jnp.logical_or(lrow == ld0, lrow == ld1), 1.0, 0.0).astype(BF16)
        xloc_ref[...] = _pack_halves(_dot(perm, h2_hi))
        lane_e = lax.broadcasted_iota(jnp.int32, (N_EXPERTS, LANES), 1)
        erow_l = lax.broadcasted_iota(jnp.int32, (N_EXPERTS, LANES), 0)
        units_row = jnp.sum(jnp.where(erow_l == lane_e, n_pad / UNIT, 0.0), axis=0, keepdims=True)
        ku_ref[...] = jnp.broadcast_to(units_row, (SUBLANES, LANES)).astype(jnp.int32).reshape(ku_ref.shape)
        zero_row = jnp.zeros((1, tm), jnp.int32)
        rec = jnp.concatenate([e0, e1, ld0, ld1, pltpu.bitcast(w0, jnp.int32), pltpu.bitcast(w1, jnp.int32),
                               zero_row, zero_row], axis=0)
        route_ref[...] = rec.reshape(route_ref.shape)
    else:
        xloc_ref[...] = jnp.zeros(xloc_ref.shape, jnp.int32)
        route_ref[...] = jnp.zeros(route_ref.shape, jnp.int32)
        ku_ref[...] = jnp.zeros(ku_ref.shape, jnp.int32)

    if tiles_p:
        @pl.when(jnp.logical_and(jnp.logical_not(is_s), t_idx == tiles_per_row - 1))
        def _():
            spo_r_ref[0] = st_ret[...]
            spo_g_ref[0] = st_gla[...]
    else:
        spo_r_ref[0] = st_ret[...]
        spo_g_ref[0] = st_gla[...]


def _const_spec(shape):
    nd = len(shape)
    return pl.BlockSpec(shape, lambda *_: (0,) * nd, pipeline_mode=pl.Buffered(1))


def _mixer_call(x_prompt, x_streams, cos, sin, st_prompt, st_streams, weights, consts, *, n_chunks, route):
    tm = n_chunks * CHUNK
    n_streams = x_streams.shape[0]
    tiles_s = n_streams // n_chunks
    if x_prompt is not None:
        n_rows, n_seq, _ = x_prompt.shape
        tpr = n_seq // tm
        tiles_p = n_rows * tpr
    else:
        n_rows, tpr, tiles_p = 1, 1, 0
    n_tiles = tiles_p + tiles_s
    p_idx = lambda i: jnp.minimum(i, tiles_p - 1)
    s_idx = lambda i: jnp.maximum(i - tiles_p, 0)

    st_blk = (1, HEAD_V, HK)
    in_specs, args = [], []
    if tiles_p:
        in_specs.append(pl.BlockSpec((1, tm, D_MODEL), lambda i: (p_idx(i) // tpr, p_idx(i) % tpr, 0)))
        args.append(x_prompt)
        cos_spec = pl.BlockSpec((tm, LANES), lambda i: (jnp.where(i < tiles_p, i % tpr, tpr), 0))
    else:
        cos_spec = pl.BlockSpec((tm, LANES), lambda i: (0, 0))
    once = pl.Buffered(1)
    stream_in = pl.BlockSpec((n_chunks, HEAD_V, HK), lambda i: (s_idx(i), 0, 0), pipeline_mode=once)
    in_specs += [pl.BlockSpec((n_chunks, CHUNK, D_MODEL), lambda i: (s_idx(i), 0, 0), pipeline_mode=once),
                 cos_spec, cos_spec, _const_spec(st_blk), _const_spec(st_blk), stream_in, stream_in]
    args += [x_streams, cos, sin, st_prompt[0], st_prompt[1], st_streams[0], st_streams[1]]
    in_specs += [_const_spec(w.shape) for w in weights] + [_const_spec(c.shape) for c in consts]
    args += list(weights) + list(consts)

    out_shape = (jax.ShapeDtypeStruct((n_tiles * tm, D_MODEL), F32),
                 jax.ShapeDtypeStruct((n_tiles * LOCAL_ROWS, D_MODEL // 2), jnp.int32),
                 jax.ShapeDtypeStruct((n_tiles, ROUTE_ROWS, tm), jnp.int32),
                 jax.ShapeDtypeStruct((n_tiles, SUBLANES, LANES), jnp.int32),
                 jax.ShapeDtypeStruct((n_rows, HEAD_V, HK), F32), jax.ShapeDtypeStruct((n_rows, HEAD_V, HK), F32),
                 jax.ShapeDtypeStruct((n_streams, HEAD_V, HK), F32),
                 jax.ShapeDtypeStruct((n_streams, HEAD_V, HK), F32))
    row_spec = pl.BlockSpec(st_blk, lambda i: (p_idx(i) // tpr if tiles_p else 0, 0, 0))
    stream_spec = pl.BlockSpec((n_chunks, HEAD_V, HK), lambda i: (s_idx(i), 0, 0))
    out_specs = (pl.BlockSpec((tm, D_MODEL), lambda i: (i, 0)),
                 pl.BlockSpec((LOCAL_ROWS, D_MODEL // 2), lambda i: (i, 0)),
                 pl.BlockSpec((1, ROUTE_ROWS, tm), lambda i: (i, 0, 0)),
                 pl.BlockSpec((1, SUBLANES, LANES), lambda i: (i, 0, 0)),
                 row_spec, row_spec, stream_spec, stream_spec)

    scratch = [pltpu.VMEM((tm, HK), BF16), pltpu.VMEM((tm, HK), BF16), pltpu.VMEM((tm, HK), BF16),
               pltpu.VMEM((tm, HK), BF16), pltpu.VMEM((tm, HV), BF16),
               pltpu.VMEM((tm, HK), BF16), pltpu.VMEM((tm, HK), BF16), pltpu.VMEM((tm, HV), BF16),
               pltpu.VMEM((n_chunks, SUBLANES, HK), F32),
               pltpu.VMEM((tm, HV), F32), pltpu.VMEM((tm, HV), F32),
               pltpu.VMEM((HEAD_V, HK), F32), pltpu.VMEM((HEAD_V, HK), F32)]

    return pl.pallas_call(
        functools.partial(_mixer_kernel, n_chunks=n_chunks, tiles_p=tiles_p, tiles_per_row=tpr, route=route),
        grid=(n_tiles,), in_specs=in_specs, out_specs=out_specs, out_shape=out_shape, scratch_shapes=scratch,
        compiler_params=pltpu.CompilerParams(dimension_semantics=("arbitrary",), vmem_limit_bytes=VMEM_LIMIT),
        name="mixer" if route else "mixer_meta",
    )(*args)


def _plan_kernel(ku_ref, src_ref, blk_ref, run_first, run_step, next_unit, seg_start, seg_units, *, n_tiles, n_blocks):
    shift = int(math.log2(BLOCK_UNITS))
    group_shift = int(math.log2(SUBLANES * LANES))

    def init_tile(t, c):
        next_unit[t] = t * LOCAL_UNITS
        return c

    lax.fori_loop(0, n_tiles, init_tile, 0)

    def init_block(b, c):
        blk_ref[0, b] = N_EXPERTS - 1
        blk_ref[1, b] = 0
        blk_ref[2, b] = 0
        blk_ref[3, b] = -1
        blk_ref[4, b] = 0
        return c

    lax.fori_loop(0, n_blocks, init_block, 0)

    def per_expert(e, carry):
        g0, position = carry

        def per_tile(t, c):
            units, d_prev = c
            base = next_unit[t]
            d = base - units
            run_first[e * n_tiles + t] = units
            run_step[e * n_tiles + t] = d - d_prev
            k = ku_ref[t, e]
            next_unit[t] = base + k
            return units + k, d

        units, _ = lax.fori_loop(0, n_tiles, per_tile, (0, 0), unroll=4)
        g_pad = g0 + (((units + (BLOCK_UNITS - 1)) >> shift) << shift)
        seg_start[e] = g0
        seg_units[e] = units

        def set_block(b, c):
            blk_ref[0, b] = e
            blk_ref[1, b] = jnp.minimum(g0 + units - (b << shift), BLOCK_UNITS)
            blk_ref[4, b] = position & 1
            return c

        lax.fori_loop(g0 >> shift, g_pad >> shift, set_block, 0)
        return g_pad, position + (units > 0).astype(jnp.int32)

    g_total, _ = lax.fori_loop(0, N_EXPERTS, per_expert, (0, 0))
    used = g_total >> shift
    blk_ref[2, 0] = used

    def set_next(i, c):
        later, following = c
        b = used - 1 - i
        e = blk_ref[0, b]
        following = jnp.where(e != later, later, following)
        blk_ref[3, b] = following
        return e, following

    lax.fori_loop(0, used, set_next, (-1, -1))

    src_ref[...] = jnp.zeros(src_ref.shape, jnp.int32)
    in_group = (lax.broadcasted_iota(jnp.int32, (SUBLANES, LANES), 0) * LANES
                + lax.broadcasted_iota(jnp.int32, (SUBLANES, LANES), 1))

    def expert_units(e, c):
        g0 = seg_start[e]
        units = seg_units[e]
        g_pad = g0 + (((units + (BLOCK_UNITS - 1)) >> shift) << shift)

        def per_group(grp, c2):
            g = in_group + (grp << group_shift)
            o = g - g0
            mine = jnp.logical_and(o >= 0, g < g_pad)
            o_eff = jnp.where(o < units, o, (o >> shift) << shift)

            def per_tile(t, acc):
                return acc + jnp.where(o_eff >= run_first[e * n_tiles + t], run_step[e * n_tiles + t], 0)

            offset = lax.fori_loop(0, n_tiles, per_tile, jnp.zeros((SUBLANES, LANES), jnp.int32), unroll=4)
            rows = pl.ds(pl.multiple_of(grp * SUBLANES, SUBLANES), SUBLANES)
            src_ref[rows, :] = jnp.where(mine, o_eff + offset, src_ref[rows, :])
            return c2

        lax.fori_loop(g0 >> group_shift, (g_pad + (SUBLANES * LANES - 1)) >> group_shift, per_group, 0)
        return c

    lax.fori_loop(0, N_EXPERTS, expert_units, 0)


def _plan_call(ku, n_blocks):
    n_tiles = ku.shape[0]
    smem = pl.BlockSpec(memory_space=pltpu.SMEM)
    group = SUBLANES * LANES
    src_rows = -(-n_blocks * BLOCK_UNITS // group) * SUBLANES
    src, blk = pl.pallas_call(
        functools.partial(_plan_kernel, n_tiles=n_tiles, n_blocks=n_blocks),
        in_specs=[smem], out_specs=(pl.BlockSpec(memory_space=pltpu.VMEM), smem),
        out_shape=(jax.ShapeDtypeStruct((src_rows, LANES), jnp.int32),
                   jax.ShapeDtypeStruct((BLK_ROWS, n_blocks), jnp.int32)),
        scratch_shapes=[pltpu.SMEM((N_EXPERTS * n_tiles,), jnp.int32), pltpu.SMEM((N_EXPERTS * n_tiles,), jnp.int32),
                        pltpu.SMEM((n_tiles,), jnp.int32), pltpu.SMEM((N_EXPERTS,), jnp.int32),
                        pltpu.SMEM((N_EXPERTS,), jnp.int32)],
        name="moe_plan",
    )(ku)
    return src.reshape(-1), blk


def _experts_kernel(src_ref, blk_ref, x_hbm, wg_hbm, wu_hbm, wd_hbm, y_hbm,
                    xin, yout, wgf, wuf, wdf, wgub, wdb, sem_in, sem_out, sem_w):
    b = pl.program_id(0)
    used = blk_ref[2, 0]
    slot = b & 1

    def weight_copies(e, s):
        return [pltpu.make_async_copy(hbm.at[e], buf.at[s], sem_w.at[s])
                for hbm, buf in ((wg_hbm, wgf), (wu_hbm, wuf), (wd_hbm, wdf))]

    def unit_rows(blk, u):
        return pl.ds(pl.multiple_of(src_ref[blk * BLOCK_UNITS + u] * UNIT, UNIT), UNIT)

    def in_copy(blk, u, s):
        return pltpu.make_async_copy(x_hbm.at[unit_rows(blk, u), :], xin.at[s, pl.ds(u * UNIT, UNIT), :],
                                     sem_in.at[s])

    def out_copy(blk, u, s):
        return pltpu.make_async_copy(yout.at[s, pl.ds(u * UNIT, UNIT), :], y_hbm.at[unit_rows(blk, u), :],
                                     sem_out.at[s])

    @pl.when(jnp.logical_and(b == 0, used > 0))
    def _():
        for copy in weight_copies(blk_ref[0, 0], blk_ref[4, 0]):
            copy.start()
        for u in range(BLOCK_UNITS):
            in_copy(0, u, 0).start()

    @pl.when(b < used)
    def _():
        expert = blk_ref[0, b]
        changed = jnp.logical_or(b == 0, expert != blk_ref[0, jnp.maximum(b - 1, 0)])

        @pl.when(changed)
        def _():
            wslot = blk_ref[4, b]
            following = blk_ref[3, b]
            for copy in weight_copies(expert, wslot):
                copy.wait()

            @pl.when(following >= 0)
            def _():
                for copy in weight_copies(following, 1 - wslot):
                    copy.start()

            wgub[:, :D_EXPERT] = wgf[wslot].astype(BF16)
            wgub[:, D_EXPERT:] = wuf[wslot].astype(BF16)
            wdb[...] = wdf[wslot].astype(BF16)

        @pl.when(b + 1 < used)
        def _():
            for u in range(BLOCK_UNITS):
                in_copy(b + 1, u, 1 - slot).start()

        for u in range(BLOCK_UNITS):
            in_copy(b, u, slot).wait()

        n_real = blk_ref[1, b]

        def mlp(rows):
            xb = _unpack_halves(xin[slot, :rows, :])
            gate_up = _dot(xb, wgub[...])
            gate = gate_up[:, :D_EXPERT]
            hid = (gate * _sigmoid(gate)) * gate_up[:, D_EXPERT:]
            yout[slot, :rows, :] = _pack_halves(_dot(hid.astype(BF16), wdb[...]).astype(BF16).astype(F32))

        @pl.when(n_real > BLOCK_UNITS // 2)
        def _():
            mlp(MOE_BLOCK)

        @pl.when(n_real <= BLOCK_UNITS // 2)
        def _():
            mlp(MOE_BLOCK // 2)


        @pl.when(n_real == BLOCK_UNITS)
        def _():
            for u in range(BLOCK_UNITS):
                out_copy(b, u, slot).start()

        @pl.when(n_real < BLOCK_UNITS)
        def _():
            lax.fori_loop(0, n_real, lambda u, c: (out_copy(b, u, slot).start(), c)[1], 0)

        @pl.when(b > 0)
        def _():
            lax.fori_loop(0, blk_ref[1, b - 1], lambda u, c: (out_copy(b - 1, u, 1 - slot).wait(), c)[1], 0)

        @pl.when(b == used - 1)
        def _():
            lax.fori_loop(0, n_real, lambda u, c: (out_copy(b, u, slot).wait(), c)[1], 0)


def _experts_call(src, blk, xloc, wg, wu, wd, n_blocks):
    hbm = pl.BlockSpec(memory_space=pl.ANY)
    grid_spec = pltpu.PrefetchScalarGridSpec(
        num_scalar_prefetch=2,
        grid=(n_blocks,),
        in_specs=[hbm, hbm, hbm, hbm],
        out_specs=hbm,
        scratch_shapes=[pltpu.VMEM((2, MOE_BLOCK, D_MODEL // 2), jnp.int32),
                        pltpu.VMEM((2, MOE_BLOCK, D_MODEL // 2), jnp.int32),
                        pltpu.VMEM((2, D_MODEL, D_EXPERT), F32), pltpu.VMEM((2, D_MODEL, D_EXPERT), F32),
                        pltpu.VMEM((2, D_EXPERT, D_MODEL), F32),
                        pltpu.VMEM((D_MODEL, 2 * D_EXPERT), BF16), pltpu.VMEM((D_EXPERT, D_MODEL), BF16),
                        pltpu.SemaphoreType.DMA((2,)), pltpu.SemaphoreType.DMA((2,)),
                        pltpu.SemaphoreType.DMA((2,))],
    )
    return pl.pallas_call(
        _experts_kernel, grid_spec=grid_spec,
        out_shape=jax.ShapeDtypeStruct(xloc.shape, jnp.int32),
        input_output_aliases={2: 0},
        compiler_params=pltpu.CompilerParams(dimension_semantics=("arbitrary",), vmem_limit_bytes=VMEM_LIMIT),
        name="moe_experts",
    )(src, blk, xloc, wg, wu, wd)


def _combine_kernel(route_ref, x_ref, y_ref, gfin_ref, out_a_ref, out_b_ref, *, tm, tiles_a):
    t = pl.program_id(0)
    rec = route_ref[0]
    recf = jnp.concatenate([rec[2:4, :].astype(F32), pltpu.bitcast(rec, F32)[4:6, :],
                            jnp.zeros((LANES - 4, tm), F32)], axis=0)
    cols = recf.T
    lrow = lax.broadcasted_iota(jnp.int32, (tm, LOCAL_ROWS), 1).astype(F32)
    select = jnp.where(lrow == cols[:, 0:1], cols[:, 2:3], jnp.where(lrow == cols[:, 1:2], cols[:, 3:4], 0.0))
    xf = x_ref[...] + _dot(select.astype(BF16), _unpack_halves(y_ref[...]))
    out = xf * lax.rsqrt(jnp.mean(xf * xf, axis=-1, keepdims=True) + NORM_EPS) * gfin_ref[...]

    @pl.when(t < tiles_a)
    def _():
        out_a_ref[...] = out

    @pl.when(t >= tiles_a)
    def _():
        out_b_ref[...] = out


def _combine_call(route, xnew, yloc, gfin, tm, tiles_a):
    n_tiles = route.shape[0]
    tiles_b = n_tiles - tiles_a
    return pl.pallas_call(
        functools.partial(_combine_kernel, tm=tm, tiles_a=tiles_a),
        grid=(n_tiles,),
        in_specs=[pl.BlockSpec((1, ROUTE_ROWS, tm), lambda t: (t, 0, 0)),
                  pl.BlockSpec((tm, D_MODEL), lambda t: (t, 0)),
                  pl.BlockSpec((LOCAL_ROWS, D_MODEL // 2), lambda t: (t, 0)),
                  pl.BlockSpec((1, D_MODEL), lambda t: (0, 0))],
        out_specs=(pl.BlockSpec((tm, D_MODEL), lambda t: (jnp.minimum(t, tiles_a - 1), 0)),
                   pl.BlockSpec((tm, D_MODEL), lambda t: (jnp.maximum(t - tiles_a, 0), 0))),
        out_shape=(jax.ShapeDtypeStruct((tiles_a * tm, D_MODEL), F32),
                   jax.ShapeDtypeStruct((tiles_b * tm, D_MODEL), F32)),
        compiler_params=pltpu.CompilerParams(dimension_semantics=("arbitrary",), vmem_limit_bytes=VMEM_LIMIT),
        name="moe_combine",
    )(route, xnew, yloc, gfin)


def _prep_weights(g_mix, w_in, w_gla_gk2, b_gla_gk, g_gla_norm, w_br_ret, w_br_gla, w_out, g_ffn,
                  w_rg, b_rg, w_re, b_re):
    w_mix = w_in[:, :W_MIX].astype(BF16)
    w_lr = jnp.pad(w_in[:, W_MIX:W_MIX + GLA_RANK], ((0, 0), (0, LANES - GLA_RANK))).astype(BF16)
    w_z = w_in[:, W_MIX + GLA_RANK:].astype(BF16)
    wgk = jnp.pad(w_gla_gk2, ((0, LANES - GLA_RANK), (0, 0))).astype(BF16)
    wrt = jnp.zeros((D_MODEL, ROUTER_LANES), F32)
    wrt = wrt.at[:, 0:N_GROUPS].set(w_rg).at[:, SUBLANES:SUBLANES + N_EXPERTS].set(w_re)
    brt = jnp.zeros((1, ROUTER_LANES), F32)
    brt = brt.at[0, 0:N_GROUPS].set(b_rg).at[0, SUBLANES:SUBLANES + N_EXPERTS].set(b_re)
    wrt_hi = wrt.astype(BF16)
    wrt = jnp.concatenate([wrt_hi, (wrt - wrt_hi.astype(F32)).astype(BF16)], axis=1)
    return (g_mix.reshape(1, D_MODEL), w_mix, w_lr, w_z, wgk, b_gla_gk.reshape(1, HK), g_gla_norm.reshape(1, HEAD_V),
            w_br_ret.astype(BF16), w_br_gla.astype(BF16), w_out.astype(BF16), g_ffn.reshape(1, D_MODEL), wrt, brt)


def _tri_consts(tm):
    r = np.arange(tm)
    low = (r[None, :] <= r[:, None])
    bd = low & ((r[None, :] // CHUNK) == (r[:, None] // CHUNK))
    return jnp.asarray(bd, BF16), jnp.asarray(low, BF16)


def _state_to_kernel(s):
    return jnp.swapaxes(s.reshape(s.shape[0], HK, HEAD_V), 1, 2)


def _state_from_kernel(s):
    return jnp.swapaxes(s, 1, 2).reshape(1, s.shape[0], N_HEADS, HEAD_K, HEAD_V)


def kernel(x_prompt, x_sample, state_ret, state_gla, meta_tokens, g_mix, w_in, w_gla_gk2, b_gla_gk, g_gla_norm, w_br_ret, w_br_gla, w_out, g_ffn, w_router_group, b_router_group, w_router_expert, b_router_expert, w_exp_gate, w_exp_up, w_exp_down, g_final):
    n_b, seq, _ = x_prompt.shape
    n_s, dec_seq, _ = x_sample.shape
    depth = state_ret.shape[0]
    assert depth == 1 and dec_seq == CHUNK and seq % TILE_ROWS == 0 and n_s % TILE_CHUNKS == 0
    cos_t, sin_t, cos_m, sin_m = _rotary_tables(seq, PAST_LEN)
    weights = _prep_weights(g_mix[0], w_in[0], w_gla_gk2[0], b_gla_gk[0], g_gla_norm[0], w_br_ret[0], w_br_gla[0],
                            w_out[0], g_ffn[0], w_router_group[0], b_router_group[0], w_router_expert[0],
                            b_router_expert[0])
    zero_state = jnp.zeros((1, HEAD_V, HK), F32)

    x_meta = jnp.concatenate([jnp.zeros((CHUNK - N_META, D_MODEL), F32), meta_tokens.astype(F32)], axis=0)
    meta_out = _mixer_call(None, x_meta.reshape(1, CHUNK, D_MODEL), cos_m, sin_m, (zero_state, zero_state),
                           (zero_state, zero_state), weights, _tri_consts(CHUNK), n_chunks=1, route=False)
    meta_ret, meta_gla = meta_out[6], meta_out[7]

    xnew, xloc, route, ku, ret_p, gla_p, ret_s, gla_s = _mixer_call(
        x_prompt, x_sample, cos_t, sin_t, (meta_ret, meta_gla),
        (_state_to_kernel(state_ret[0]), _state_to_kernel(state_gla[0])), weights, _tri_consts(TILE_ROWS),
        n_chunks=TILE_CHUNKS, route=True)

    n_tiles = route.shape[0]
    n_blocks = n_tiles * LOCAL_UNITS // BLOCK_UNITS + N_EXPERTS
    src, blk = _plan_call(ku[:, 0, :N_EXPERTS], n_blocks)
    yloc = _experts_call(src, blk, xloc, w_exp_gate[0], w_exp_up[0], w_exp_down[0], n_blocks)
    y_p, y_s = _combine_call(route, xnew, yloc, g_final.reshape(1, D_MODEL), TILE_ROWS, n_b * seq // TILE_ROWS)

    return (y_p.reshape(n_b, seq, D_MODEL), y_s.reshape(n_s, dec_seq, D_MODEL),
            _state_from_kernel(ret_p), _state_from_kernel(gla_p), _state_from_kernel(ret_s), _state_from_kernel(gla_s))
```

```python
import functools
import math

import jax
import jax.numpy as jnp
import numpy as np
from jax import lax
from jax.experimental import pallas as pl
from jax.experimental.pallas import tpu as pltpu

D_MODEL = 1024
CHUNK = 64
PAST_LEN = 1024
N_META = 16
N_HEADS = 4
HEAD_K = 64
HEAD_V = 128
HK = N_HEADS * HEAD_K
HV = N_HEADS * HEAD_V
GLA_RANK = 16
GATE_NORM = 16.0
ROPE_BASE = 10000.0
N_GROUPS = 4
EXPERTS_PER_GROUP = 8
N_EXPERTS = N_GROUPS * EXPERTS_PER_GROUP
D_EXPERT = 512
NORM_EPS = 1e-6

LANES = 128
SUBLANES = 8
TILE_CHUNKS = 8
TILE_ROWS = TILE_CHUNKS * CHUNK
ROUTE_ROWS = 8
ROUTER_LANES = 128
UNIT = SUBLANES
MOE_BLOCK = 512
BLOCK_UNITS = MOE_BLOCK // UNIT
BLK_ROWS = 5
LOCAL_ROWS = 2 * TILE_ROWS + N_EXPERTS * UNIT
LOCAL_UNITS = LOCAL_ROWS // UNIT
VMEM_LIMIT = 56 * 1024 * 1024

C_RQ, C_RK, C_RV, C_RG = 0, 256, 512, 1024
C_GQ, C_GK, C_GV, C_GG = 1536, 1792, 2048, 2560
W_MIX = 3072
C_Z = W_MIX + GLA_RANK

F32 = jnp.float32
BF16 = jnp.bfloat16
LOG_G = tuple(math.log1p(-(2.0 ** (-5.0 - h))) for h in range(N_HEADS))


def _dot(a, b):
    return jnp.dot(a, b, preferred_element_type=F32)


def _dot_nt(a, b):
    return lax.dot_general(a, b, (((1,), (1,)), ((), ())), preferred_element_type=F32)


def _dot_tn(a, b):
    return lax.dot_general(a, b, (((0,), (0,)), ((), ())), preferred_element_type=F32)


def _sigmoid(x):
    return 1.0 / (1.0 + jnp.exp(-x))


def _pack_halves(x):
    half = x.shape[1] // 2
    bits = pltpu.bitcast(x, jnp.int32)
    return lax.shift_right_logical(bits[:, :half], 16) | (bits[:, half:] & jnp.int32(-65536))


def _unpack_halves(w):
    lo = pltpu.bitcast(lax.shift_left(w, 16), F32)
    hi = pltpu.bitcast(w & jnp.int32(-65536), F32)
    return jnp.concatenate([lo, hi], axis=1).astype(BF16)


def _head_of_lane(shape, width):
    return lax.broadcasted_iota(jnp.int32, shape, len(shape) - 1) >> int(math.log2(width))


def _per_head_lane_const(vals, shape, width):
    hd = _head_of_lane(shape, width)
    out = jnp.full(shape, vals[N_HEADS - 1], F32)
    for h in range(N_HEADS - 2, -1, -1):
        out = jnp.where(hd == h, vals[h], out)
    return out


def _tables_kernel(inv_ref, cp_ref, sp_ref, cm_ref, sm_ref, *, seq, past_len):
    inv = inv_ref[...].reshape(1, 1, LANES)
    lane = lax.broadcasted_iota(jnp.int32, (1, 1, LANES), 2)
    sign = jnp.where((lane & (HEAD_K - 1)) < (HEAD_K // 2), -1.0, 1.0)
    off = lax.broadcasted_iota(jnp.int32, (1, CHUNK, LANES), 1).astype(F32) * inv
    c_off, s_off = jnp.cos(off), jnp.sin(off)

    def chunks(n, first_pos):
        base = (lax.broadcasted_iota(jnp.int32, (n, 1, LANES), 0) * CHUNK + first_pos).astype(F32) * inv
        c_base, s_base = jnp.cos(base), jnp.sin(base)
        cos = (c_base * c_off - s_base * s_off).reshape(n * CHUNK, LANES)
        sin = ((s_base * c_off + c_base * s_off) * sign).reshape(n * CHUNK, LANES)
        return cos, sin

    cp_ref[0:seq, :], sp_ref[0:seq, :] = chunks(seq // CHUNK, 0)
    cos_s, sin_s = chunks(1, past_len)
    for c in range(TILE_CHUNKS):
        cp_ref[seq + c * CHUNK:seq + (c + 1) * CHUNK, :] = cos_s
        sp_ref[seq + c * CHUNK:seq + (c + 1) * CHUNK, :] = sin_s
    cm_ref[...], sm_ref[...] = chunks(1, -CHUNK)


def _rotary_tables(seq, past_len):
    half = HEAD_K // 2
    inv = ROPE_BASE ** (-2.0 * jnp.arange(half, dtype=F32) / HEAD_K)
    inv = jnp.tile(inv, LANES // half).reshape(1, LANES)
    shp = lambda r: jax.ShapeDtypeStruct((r, LANES), F32)
    return pl.pallas_call(
        functools.partial(_tables_kernel, seq=seq, past_len=past_len),
        out_shape=(shp(seq + TILE_ROWS), shp(seq + TILE_ROWS), shp(CHUNK), shp(CHUNK)),
        compiler_params=pltpu.CompilerParams(vmem_limit_bytes=VMEM_LIMIT),
        name="rotary_tables",
    )(inv)


def _mixer_kernel(*refs, n_chunks, tiles_p, tiles_per_row, route):
    tm = n_chunks * CHUNK
    if tiles_p:
        xp_ref, refs = refs[0], refs[1:]
    (xs_ref, cos_ref, sin_ref, spi_r_ref, spi_g_ref, ssi_r_ref, ssi_g_ref,
     gmix_ref, wint_ref, wgk_ref, bgk_ref, gnorm_ref, wbr_ref, wbg_ref, wout_ref,
     gffn_ref, wrt_ref, brt_ref, bdtri_ref, ltri_ref,
     xnew_ref, xloc_ref, route_ref, ku_ref, spo_r_ref, spo_g_ref, sso_r_ref, sso_g_ref,
     qb, qdb, kb, kkb, vb, gqb, gkkb, gvb, ga, o_ret, o_gla, st_ret, st_gla) = refs

    i = pl.program_id(0)
    if tiles_p:
        is_s = i >= tiles_p
        t_idx = jnp.minimum(i, tiles_p - 1) % tiles_per_row
        x = jnp.where(is_s, xs_ref[...].reshape(tm, D_MODEL), xp_ref[...].reshape(tm, D_MODEL))

        @pl.when(jnp.logical_and(jnp.logical_not(is_s), t_idx == 0))
        def _():
            st_ret[...] = spi_r_ref[0]
            st_gla[...] = spi_g_ref[0]
    else:
        is_s = None
        x = xs_ref[...].reshape(tm, D_MODEL)

    h = x * lax.rsqrt(jnp.mean(x * x, axis=-1, keepdims=True) + NORM_EPS) * gmix_ref[...]
    hb = h.astype(BF16)

    def proj(c0, width):
        return _dot_nt(hb, wint_ref[c0:c0 + width, :])

    cos3 = jnp.concatenate([cos_ref[...]] * 2, axis=1).reshape(n_chunks, CHUNK, HK)
    sin3 = jnp.concatenate([sin_ref[...]] * 2, axis=1).reshape(n_chunks, CHUNK, HK)
    lane_hk = lax.broadcasted_iota(jnp.int32, (tm, HK), 1)
    first_half = (lane_hk & (HEAD_K - 1)) < (HEAD_K // 2)

    def rotary(t):
        swapped = jnp.where(first_half, pltpu.roll(t, HK - HEAD_K // 2, 1), pltpu.roll(t, HEAD_K // 2, 1))
        t3 = t.reshape(n_chunks, CHUNK, HK)
        return (t3 * cos3 + swapped.reshape(n_chunks, CHUNK, HK) * sin3).reshape(tm, HK)

    logg_hk = _per_head_lane_const(LOG_G, (CHUNK, HK), HEAD_K)
    l_idx = lax.broadcasted_iota(jnp.int32, (CHUNK, HK), 0).astype(F32)
    qdec = jnp.exp((l_idx + 1.0) * logg_hk)
    kdec = jnp.exp((CHUNK - 1.0 - l_idx) * logg_hk)
    cdec = jnp.exp(float(CHUNK) * _per_head_lane_const(LOG_G, (1, HK), HEAD_K))
    r_idx = lax.broadcasted_iota(jnp.int32, (N_HEADS * CHUNK, CHUNK), 0)
    m_idx = lax.broadcasted_iota(jnp.int32, (N_HEADS * CHUNK, CHUNK), 1)
    logg_rows = jnp.full((N_HEADS * CHUNK, CHUNK), LOG_G[N_HEADS - 1], F32)
    for hh in range(N_HEADS - 2, -1, -1):
        logg_rows = jnp.where((r_idx >> int(math.log2(CHUNK))) == hh, LOG_G[hh], logg_rows)
    dmat = jnp.exp(jnp.abs((r_idx & (CHUNK - 1)) - m_idx).astype(F32) * logg_rows)

    glr = jnp.where(lax.broadcasted_iota(jnp.int32, (tm, LANES), 1) < GLA_RANK, proj(W_MIX, LANES), 0.0)
    rq = rotary(proj(C_RQ, HK))
    qb[...] = rq.astype(BF16)
    qdb[...] = (rq.reshape(n_chunks, CHUNK, HK) * qdec).reshape(tm, HK).astype(BF16)
    gl = _dot(glr.astype(BF16), wgk_ref[...]) + bgk_ref[...]
    rk = rotary(proj(C_RK, HK)) * (HEAD_K ** -0.5)
    kb[...] = rk.astype(BF16)
    kkb[...] = (rk.reshape(n_chunks, CHUNK, HK) * kdec).reshape(tm, HK).astype(BF16)
    log_a = (jnp.minimum(gl, 0.0) - jnp.log1p(jnp.exp(-jnp.abs(gl)))) / GATE_NORM
    la_hi = log_a.astype(BF16)
    la_lo = (log_a - la_hi.astype(F32)).astype(BF16)
    vb[...] = proj(C_RV, HV).astype(BF16)
    bdtri = bdtri_ref[...]
    bcum = _dot(bdtri, la_hi) + _dot(bdtri, la_lo)
    gqb[...] = (proj(C_GQ, HK) * (HEAD_K ** -0.5)).astype(BF16)
    gvb[...] = proj(C_GV, HV).astype(BF16)
    gk = proj(C_GK, HK)
    b3 = bcum.reshape(n_chunks, CHUNK, HK)
    bl3 = b3[:, CHUNK - 1:CHUNK, :]
    gkkb[...] = (gk.reshape(n_chunks, CHUNK, HK) * jnp.exp(bl3 - b3)).reshape(tm, HK).astype(BF16)
    ga[...] = jnp.broadcast_to(jnp.exp(bl3), (n_chunks, SUBLANES, HK))

    def stack_masked(a, width):
        head = _head_of_lane(a.shape, width)
        zero = jnp.zeros_like(a)
        return jnp.concatenate([jnp.where(head == hh, a, zero) for hh in range(N_HEADS)], axis=0)

    def heads_to_rows(a):
        return jnp.concatenate([a[:, hh * HEAD_V:(hh + 1) * HEAD_V] for hh in range(N_HEADS)], axis=0)

    def rows_to_heads(a):
        return jnp.concatenate([a[hh * CHUNK:(hh + 1) * CHUNK, :] for hh in range(N_HEADS)], axis=1)

    chunk_rows = [slice(c * CHUNK, (c + 1) * CHUNK) for c in range(n_chunks)]
    probs = [(_dot_nt(stack_masked(qb[r, :], HEAD_K), kb[r, :]) * dmat).astype(BF16) for r in chunk_rows]
    inc_ret = [_dot_tn(heads_to_rows(vb[r, :]), stack_masked(kkb[r, :], HEAD_K)) for r in chunk_rows]
    inc_gla = [_dot_tn(heads_to_rows(gvb[r, :]), stack_masked(gkkb[r, :], HEAD_K)) for r in chunk_rows]
    rg = proj(C_RG, HV)
    gg = proj(C_GG, HV)
    s_in, g_out = [], []
    s_cur, g_cur = st_ret[...], st_gla[...]
    for c in range(n_chunks):
        if is_s is None:
            s_cur, g_cur = ssi_r_ref[c], ssi_g_ref[c]
        else:
            s_cur = jnp.where(is_s, ssi_r_ref[c], s_cur)
            g_cur = jnp.where(is_s, ssi_g_ref[c], g_cur)
        s_in.append(s_cur.astype(BF16))
        s_cur = s_cur * cdec + inc_ret[c]
        g_cur = g_cur * ga[c][0:1, :] + inc_gla[c]
        g_out.append(g_cur.astype(BF16))
        sso_r_ref[c] = s_cur
        sso_g_ref[c] = g_cur
    st_ret[...] = s_cur
    st_gla[...] = g_cur
    for c, r in enumerate(chunk_rows):
        v = vb[r, :]
        intra = jnp.concatenate(
            [_dot(probs[c][hh * CHUNK:(hh + 1) * CHUNK, :], v[:, hh * HEAD_V:(hh + 1) * HEAD_V])
             for hh in range(N_HEADS)], axis=1)
        inter = rows_to_heads(_dot_nt(stack_masked(qdb[r, :], HEAD_K), s_in[c]))
        o_ret[r, :] = intra + inter
        o_gla[r, :] = rows_to_heads(_dot_nt(stack_masked(gqb[r, :], HEAD_K), g_out[c]))

    gnorm = gnorm_ref[...]
    orr = o_ret[...]
    ogg = o_gla[...]
    ret_parts, gla_parts = [], []
    for hh in range(N_HEADS):
        sl = slice(hh * HEAD_V, (hh + 1) * HEAD_V)
        oh = orr[:, sl]
        mu = jnp.mean(oh, axis=-1, keepdims=True)
        dev = oh - mu
        var = jnp.mean(dev * dev, axis=-1, keepdims=True)
        ret_parts.append(dev * lax.rsqrt(var + NORM_EPS))
        og = ogg[:, sl]
        gla_parts.append(og * lax.rsqrt(jnp.mean(og * og, axis=-1, keepdims=True) + NORM_EPS) * gnorm)
    o_r = jnp.concatenate(ret_parts, axis=1) * (rg * _sigmoid(rg))
    o_g = jnp.concatenate(gla_parts, axis=1) * (gg * _sigmoid(gg))
    merged = (_sigmoid(proj(C_Z, D_MODEL)) * _dot(o_r.astype(BF16), wbr_ref[...])
              + _sigmoid(proj(C_Z + D_MODEL, D_MODEL)) * _dot(o_g.astype(BF16), wbg_ref[...]))
    xn = x + _dot(merged.astype(BF16), wout_ref[...])
    xnew_ref[...] = xn

    if route:
        h2 = xn * lax.rsqrt(jnp.mean(xn * xn, axis=-1, keepdims=True) + NORM_EPS) * gffn_ref[...]
        h2_hi = h2.astype(BF16)
        h2_lo = (h2 - h2_hi.astype(F32)).astype(BF16)
        hi_both = _dot(h2_hi, wrt_ref[...])
        logits = (hi_both[:, :ROUTER_LANES] + _dot(h2_lo, wrt_ref[:, :ROUTER_LANES])
                  + hi_both[:, ROUTER_LANES:]) + brt_ref[...]
        lt = logits.T
        row8 = lax.broadcasted_iota(jnp.int32, (SUBLANES, tm), 0)
        neg_inf = jnp.float32(-jnp.inf)
        glog = jnp.where(row8 < N_GROUPS, lt[0:SUBLANES, :], neg_inf)
        gmax = jnp.max(glog, axis=0, keepdims=True)
        grp = jnp.min(jnp.where(glog == gmax, row8, SUBLANES), axis=0, keepdims=True)
        p_grp = 1.0 / jnp.sum(jnp.exp(glog - gmax), axis=0, keepdims=True)
        le = jnp.zeros((SUBLANES, tm), F32)
        for g in range(N_GROUPS):
            le = jnp.where(grp == g, lt[SUBLANES * (g + 1):SUBLANES * (g + 2), :], le)
        m1 = jnp.max(le, axis=0, keepdims=True)
        i1 = jnp.min(jnp.where(le == m1, row8, SUBLANES), axis=0, keepdims=True)
        le2 = jnp.where(row8 == i1, neg_inf, le)
        m2 = jnp.max(le2, axis=0, keepdims=True)
        i2 = jnp.min(jnp.where(le2 == m2, row8, SUBLANES), axis=0, keepdims=True)
        e0 = grp * EXPERTS_PER_GROUP + i1
        e1 = grp * EXPERTS_PER_GROUP + i2
        t21 = jnp.exp(m2 - m1)
        w0 = p_grp / (1.0 + t21)
        w1 = p_grp * t21 / (1.0 + t21)

        erow = lax.broadcasted_iota(jnp.int32, (N_EXPERTS, tm), 0)
        hit0 = erow == e0
        hit1 = erow == e1
        onehot = jnp.where(jnp.logical_or(hit0, hit1), 1.0, 0.0).astype(BF16)
        cum = _dot_nt(onehot, ltri_ref[...])
        n_run = cum[:, tm - 1:tm]
        n_pad = jnp.ceil(n_run / UNIT) * UNIT
        rank0 = jnp.sum(jnp.where(hit0, cum - 1.0, 0.0), axis=0, keepdims=True)
        rank1 = jnp.sum(jnp.where(hit1, cum - 1.0, 0.0), axis=0, keepdims=True)
        start0 = jnp.sum(jnp.where(erow < e0, n_pad, 0.0), axis=0, keepdims=True)
        start1 = jnp.sum(jnp.where(erow < e1, n_pad, 0.0), axis=0, keepdims=True)
        ld0 = (start0 + rank0).astype(jnp.int32)
        ld1 = (start1 + rank1).astype(jnp.int32)
        lrow = lax.broadcasted_iota(jnp.int32, (LOCAL_ROWS, tm), 0)
        perm = jnp.where(jnp.logical_or(lrow == ld0, lrow == ld1), 1.0, 0.0).astype(BF16)
        xloc_ref[...] = _pack_halves(_dot(perm, h2_hi))
        lane_e = lax.broadcasted_iota(jnp.int32, (N_EXPERTS, LANES), 1)
        erow_l = lax.broadcasted_iota(jnp.int32, (N_EXPERTS, LANES), 0)
        units_row = jnp.sum(jnp.where(erow_l == lane_e, n_pad / UNIT, 0.0), axis=0, keepdims=True)
        ku_ref[...] = jnp.broadcast_to(units_row, (SUBLANES, LANES)).astype(jnp.int32).reshape(ku_ref.shape)
        zero_row = jnp.zeros((1, tm), jnp.int32)
        rec = jnp.concatenate([e0, e1, ld0, ld1, pltpu.bitcast(w0, jnp.int32), pltpu.bitcast(w1, jnp.int32),
                               zero_row, zero_row], axis=0)
        route_ref[...] = rec.reshape(route_ref.shape)
    else:
        xloc_ref[...] = jnp.zeros(xloc_ref.shape, jnp.int32)
        route_ref[...] = jnp.zeros(route_ref.shape, jnp.int32)
        ku_ref[...] = jnp.zeros(ku_ref.shape, jnp.int32)

    if tiles_p:
        @pl.when(jnp.logical_and(jnp.logical_not(is_s), t_idx == tiles_per_row - 1))
        def _():
            spo_r_ref[0] = st_ret[...]
            spo_g_ref[0] = st_gla[...]
    else:
        spo_r_ref[0] = st_ret[...]
        spo_g_ref[0] = st_gla[...]


def _const_spec(shape):
    nd = len(shape)
    return pl.BlockSpec(shape, lambda *_: (0,) * nd, pipeline_mode=pl.Buffered(1))


def _mixer_call(x_prompt, x_streams, cos, sin, st_prompt, st_streams, weights, consts, *, n_chunks, route):
    tm = n_chunks * CHUNK
    n_streams = x_streams.shape[0]
    tiles_s = n_streams // n_chunks
    if x_prompt is not None:
        n_rows, n_seq, _ = x_prompt.shape
        tpr = n_seq // tm
        tiles_p = n_rows * tpr
    else:
        n_rows, tpr, tiles_p = 1, 1, 0
    n_tiles = tiles_p + tiles_s
    p_idx = lambda i: jnp.minimum(i, tiles_p - 1)
    s_idx = lambda i: jnp.maximum(i - tiles_p, 0)

    st_blk = (1, HEAD_V, HK)
    in_specs, args = [], []
    if tiles_p:
        in_specs.append(pl.BlockSpec((1, tm, D_MODEL), lambda i: (p_idx(i) // tpr, p_idx(i) % tpr, 0)))
        args.append(x_prompt)
        cos_spec = pl.BlockSpec((tm, LANES), lambda i: (jnp.where(i < tiles_p, i % tpr, tpr), 0))
    else:
        cos_spec = pl.BlockSpec((tm, LANES), lambda i: (0, 0))
    once = pl.Buffered(1)
    stream_in = pl.BlockSpec((n_chunks, HEAD_V, HK), lambda i: (s_idx(i), 0, 0), pipeline_mode=once)
    in_specs += [pl.BlockSpec((n_chunks, CHUNK, D_MODEL), lambda i: (s_idx(i), 0, 0), pipeline_mode=once),
                 cos_spec, cos_spec, _const_spec(st_blk), _const_spec(st_blk), stream_in, stream_in]
    args += [x_streams, cos, sin, st_prompt[0], st_prompt[1], st_streams[0], st_streams[1]]
    in_specs += [_const_spec(w.shape) for w in weights] + [_const_spec(c.shape) for c in consts]
    args += list(weights) + list(consts)

    out_shape = (jax.ShapeDtypeStruct((n_tiles * tm, D_MODEL), F32),
                 jax.ShapeDtypeStruct((n_tiles * LOCAL_ROWS, D_MODEL // 2), jnp.int32),
                 jax.ShapeDtypeStruct((n_tiles, ROUTE_ROWS, tm), jnp.int32),
                 jax.ShapeDtypeStruct((n_tiles, SUBLANES, LANES), jnp.int32),
                 jax.ShapeDtypeStruct((n_rows, HEAD_V, HK), F32), jax.ShapeDtypeStruct((n_rows, HEAD_V, HK), F32),
                 jax.ShapeDtypeStruct((n_streams, HEAD_V, HK), F32),
                 jax.ShapeDtypeStruct((n_streams, HEAD_V, HK), F32))
    row_spec = pl.BlockSpec(st_blk, lambda i: (p_idx(i) // tpr if tiles_p else 0, 0, 0))
    stream_spec = pl.BlockSpec((n_chunks, HEAD_V, HK), lambda i: (s_idx(i), 0, 0))
    out_specs = (pl.BlockSpec((tm, D_MODEL), lambda i: (i, 0)),
                 pl.BlockSpec((LOCAL_ROWS, D_MODEL // 2), lambda i: (i, 0)),
                 pl.BlockSpec((1, ROUTE_ROWS, tm), lambda i: (i, 0, 0)),
                 pl.BlockSpec((1, SUBLANES, LANES), lambda i: (i, 0, 0)),
                 row_spec, row_spec, stream_spec, stream_spec)

    scratch = [pltpu.VMEM((tm, HK), BF16), pltpu.VMEM((tm, HK), BF16), pltpu.VMEM((tm, HK), BF16),
               pltpu.VMEM((tm, HK), BF16), pltpu.VMEM((tm, HV), BF16),
               pltpu.VMEM((tm, HK), BF16), pltpu.VMEM((tm, HK), BF16), pltpu.VMEM((tm, HV), BF16),
               pltpu.VMEM((n_chunks, SUBLANES, HK), F32),
               pltpu.VMEM((tm, HV), F32), pltpu.VMEM((tm, HV), F32),
               pltpu.VMEM((HEAD_V, HK), F32), pltpu.VMEM((HEAD_V, HK), F32)]

    return pl.pallas_call(
        functools.partial(_mixer_kernel, n_chunks=n_chunks, tiles_p=tiles_p, tiles_per_row=tpr, route=route),
        grid=(n_tiles,), in_specs=in_specs, out_specs=out_specs, out_shape=out_shape, scratch_shapes=scratch,
        compiler_params=pltpu.CompilerParams(dimension_semantics=("arbitrary",), vmem_limit_bytes=VMEM_LIMIT),
        name="mixer" if route else "mixer_meta",
    )(*args)


def _plan_kernel(ku_ref, src_ref, blk_ref, run_first, run_step, next_unit, seg_start, seg_units, *, n_tiles, n_blocks):
    shift = int(math.log2(BLOCK_UNITS))
    group_shift = int(math.log2(SUBLANES * LANES))

    def init_tile(t, c):
        next_unit[t] = t * LOCAL_UNITS
        return c

    lax.fori_loop(0, n_tiles, init_tile, 0)

    def init_block(b, c):
        blk_ref[0, b] = N_EXPERTS - 1
        blk_ref[1, b] = 0
        blk_ref[2, b] = 0
        blk_ref[3, b] = -1
        blk_ref[4, b] = 0
        return c

    lax.fori_loop(0, n_blocks, init_block, 0)

    def per_expert(e, carry):
        g0, position = carry

        def per_tile(t, c):
            units, d_prev = c
            base = next_unit[t]
            d = base - units
            run_first[e * n_tiles + t] = units
            run_step[e * n_tiles + t] = d - d_prev
            k = ku_ref[t, e]
            next_unit[t] = base + k
            return units + k, d

        units, _ = lax.fori_loop(0, n_tiles, per_tile, (0, 0), unroll=4)
        g_pad = g0 + (((units + (BLOCK_UNITS - 1)) >> shift) << shift)
        seg_start[e] = g0
        seg_units[e] = units

        def set_block(b, c):
            blk_ref[0, b] = e
            blk_ref[1, b] = jnp.minimum(g0 + units - (b << shift), BLOCK_UNITS)
            blk_ref[4, b] = position & 1
            return c

        lax.fori_loop(g0 >> shift, g_pad >> shift, set_block, 0)
        return g_pad, position + (units > 0).astype(jnp.int32)

    g_total, _ = lax.fori_loop(0, N_EXPERTS, per_expert, (0, 0))
    used = g_total >> shift
    blk_ref[2, 0] = used

    def set_next(i, c):
        later, following = c
        b = used - 1 - i
        e = blk_ref[0, b]
        following = jnp.where(e != later, later, following)
        blk_ref[3, b] = following
        return e, following

    lax.fori_loop(0, used, set_next, (-1, -1))

    src_ref[...] = jnp.zeros(src_ref.shape, jnp.int32)
    in_group = (lax.broadcasted_iota(jnp.int32, (SUBLANES, LANES), 0) * LANES
                + lax.broadcasted_iota(jnp.int32, (SUBLANES, LANES), 1))

    def expert_units(e, c):
        g0 = seg_start[e]
        units = seg_units[e]
        g_pad = g0 + (((units + (BLOCK_UNITS - 1)) >> shift) << shift)

        def per_group(grp, c2):
            g = in_group + (grp << group_shift)
            o = g - g0
            mine = jnp.logical_and(o >= 0, g < g_pad)
            o_eff = jnp.where(o < units, o, (o >> shift) << shift)

            def per_tile(t, acc):
                return acc + jnp.where(o_eff >= run_first[e * n_tiles + t], run_step[e * n_tiles + t], 0)

            offset = lax.fori_loop(0, n_tiles, per_tile, jnp.zeros((SUBLANES, LANES), jnp.int32), unroll=4)
            rows = pl.ds(pl.multiple_of(grp * SUBLANES, SUBLANES), SUBLANES)
            src_ref[rows, :] = jnp.where(mine, o_eff + offset, src_ref[rows, :])
            return c2

        lax.fori_loop(g0 >> group_shift, (g_pad + (SUBLANES * LANES - 1)) >> group_shift, per_group, 0)
        return c

    lax.fori_loop(0, N_EXPERTS, expert_units, 0)


def _plan_call(ku, n_blocks):
    n_tiles = ku.shape[0]
    smem = pl.BlockSpec(memory_space=pltpu.SMEM)
    group = SUBLANES * LANES
    src_rows = -(-n_blocks * BLOCK_UNITS // group) * SUBLANES
    src, blk = pl.pallas_call(
        functools.partial(_plan_kernel, n_tiles=n_tiles, n_blocks=n_blocks),
        in_specs=[smem], out_specs=(pl.BlockSpec(memory_space=pltpu.VMEM), smem),
        out_shape=(jax.ShapeDtypeStruct((src_rows, LANES), jnp.int32),
                   jax.ShapeDtypeStruct((BLK_ROWS, n_blocks), jnp.int32)),
        scratch_shapes=[pltpu.SMEM((N_EXPERTS * n_tiles,), jnp.int32), pltpu.SMEM((N_EXPERTS * n_tiles,), jnp.int32),
                        pltpu.SMEM((n_tiles,), jnp.int32), pltpu.SMEM((N_EXPERTS,), jnp.int32),
                        pltpu.SMEM((N_EXPERTS,), jnp.int32)],
        name="moe_plan",
    )(ku)
    return src.reshape(-1), blk


def _experts_kernel(src_ref, blk_ref, x_hbm, wg_hbm, wu_hbm, wd_hbm, y_hbm,
                    xin, yout, wgf, wuf, wdf, wgub, wdb, sem_in, sem_out, sem_w):
    b = pl.program_id(0)
    used = blk_ref[2, 0]
    slot = b & 1

    def weight_copies(e, s):
        return [pltpu.make_async_copy(hbm.at[e], buf.at[s], sem_w.at[s])
                for hbm, buf in ((wg_hbm, wgf), (wu_hbm, wuf), (wd_hbm, wdf))]

    def unit_rows(blk, u):
        return pl.ds(pl.multiple_of(src_ref[blk * BLOCK_UNITS + u] * UNIT, UNIT), UNIT)

    def in_copy(blk, u, s):
        return pltpu.make_async_copy(x_hbm.at[unit_rows(blk, u), :], xin.at[s, pl.ds(u * UNIT, UNIT), :],
                                     sem_in.at[s])

    def out_copy(blk, u, s):
        return pltpu.make_async_copy(yout.at[s, pl.ds(u * UNIT, UNIT), :], y_hbm.at[unit_rows(blk, u), :],
                                     sem_out.at[s])

    @pl.when(jnp.logical_and(b == 0, used > 0))
    def _():
        for copy in weight_copies(blk_ref[0, 0], blk_ref[4, 0]):
            copy.start()
        for u in range(BLOCK_UNITS):
            in_copy(0, u, 0).start()

    @pl.when(b < used)
    def _():
        expert = blk_ref[0, b]
        changed = jnp.logical_or(b == 0, expert != blk_ref[0, jnp.maximum(b - 1, 0)])

        @pl.when(changed)
        def _():
            wslot = blk_ref[4, b]
            following = blk_ref[3, b]
            for copy in weight_copies(expert, wslot):
                copy.wait()

            @pl.when(following >= 0)
            def _():
                for copy in weight_copies(following, 1 - wslot):
                    copy.start()

            wgub[:, :D_EXPERT] = wgf[wslot].astype(BF16)
            wgub[:, D_EXPERT:] = wuf[wslot].astype(BF16)
            wdb[...] = wdf[wslot].astype(BF16)

        @pl.when(b + 1 < used)
        def _():
            for u in range(BLOCK_UNITS):
                in_copy(b + 1, u, 1 - slot).start()

        for u in range(BLOCK_UNITS):
            in_copy(b, u, slot).wait()

        n_real = blk_ref[1, b]

        def mlp(rows):
            xb = _unpack_halves(xin[slot, :rows, :])
            gate_up = _dot(xb, wgub[...])
            gate = gate_up[:, :D_EXPERT]
            hid = (gate * _sigmoid(gate)) * gate_up[:, D_EXPERT:]
            yout[slot, :rows, :] = _pack_halves(_dot(hid.astype(BF16), wdb[...]).astype(BF16).astype(F32))

        @pl.when(n_real > BLOCK_UNITS // 2)
        def _():
            mlp(MOE_BLOCK)

        @pl.when(n_real <= BLOCK_UNITS // 2)
        def _():
            mlp(MOE_BLOCK // 2)


        @pl.when(n_real == BLOCK_UNITS)
        def _():
            for u in range(BLOCK_UNITS):
                out_copy(b, u, slot).start()

        @pl.when(n_real < BLOCK_UNITS)
        def _():
            lax.fori_loop(0, n_real, lambda u, c: (out_copy(b, u, slot).start(), c)[1], 0)

        @pl.when(b > 0)
        def _():
            lax.fori_loop(0, blk_ref[1, b - 1], lambda u, c: (out_copy(b - 1, u, 1 - slot).wait(), c)[1], 0)

        @pl.when(b == used - 1)
        def _():
            lax.fori_loop(0, n_real, lambda u, c: (out_copy(b, u, slot).wait(), c)[1], 0)


def _experts_call(src, blk, xloc, wg, wu, wd, n_blocks):
    hbm = pl.BlockSpec(memory_space=pl.ANY)
    grid_spec = pltpu.PrefetchScalarGridSpec(
        num_scalar_prefetch=2,
        grid=(n_blocks,),
        in_specs=[hbm, hbm, hbm, hbm],
        out_specs=hbm,
        scratch_shapes=[pltpu.VMEM((2, MOE_BLOCK, D_MODEL // 2), jnp.int32),
                        pltpu.VMEM((2, MOE_BLOCK, D_MODEL // 2), jnp.int32),
                        pltpu.VMEM((2, D_MODEL, D_EXPERT), F32), pltpu.VMEM((2, D_MODEL, D_EXPERT), F32),
                        pltpu.VMEM((2, D_EXPERT, D_MODEL), F32),
                        pltpu.VMEM((D_MODEL, 2 * D_EXPERT), BF16), pltpu.VMEM((D_EXPERT, D_MODEL), BF16),
                        pltpu.SemaphoreType.DMA((2,)), pltpu.SemaphoreType.DMA((2,)),
                        pltpu.SemaphoreType.DMA((2,))],
    )
    return pl.pallas_call(
        _experts_kernel, grid_spec=grid_spec,
        out_shape=jax.ShapeDtypeStruct(xloc.shape, jnp.int32),
        input_output_aliases={2: 0},
        compiler_params=pltpu.CompilerParams(dimension_semantics=("arbitrary",), vmem_limit_bytes=VMEM_LIMIT),
        name="moe_experts",
    )(src, blk, xloc, wg, wu, wd)


def _combine_kernel(route_ref, x_ref, y_ref, gfin_ref, out_a_ref, out_b_ref, *, tm, tiles_a):
    t = pl.program_id(0)
    rec = route_ref[0]
    recf = jnp.concatenate([rec[2:4, :].astype(F32), pltpu.bitcast(rec, F32)[4:6, :],
                            jnp.zeros((LANES - 4, tm), F32)], axis=0)
    cols = recf.T
    lrow = lax.broadcasted_iota(jnp.int32, (tm, LOCAL_ROWS), 1).astype(F32)
    select = jnp.where(lrow == cols[:, 0:1], cols[:, 2:3], jnp.where(lrow == cols[:, 1:2], cols[:, 3:4], 0.0))
    xf = x_ref[...] + _dot(select.astype(BF16), _unpack_halves(y_ref[...]))
    out = xf * lax.rsqrt(jnp.mean(xf * xf, axis=-1, keepdims=True) + NORM_EPS) * gfin_ref[...]

    @pl.when(t < tiles_a)
    def _():
        out_a_ref[...] = out

    @pl.when(t >= tiles_a)
    def _():
        out_b_ref[...] = out


def _combine_call(route, xnew, yloc, gfin, tm, tiles_a):
    n_tiles = route.shape[0]
    tiles_b = n_tiles - tiles_a
    return pl.pallas_call(
        functools.partial(_combine_kernel, tm=tm, tiles_a=tiles_a),
        grid=(n_tiles,),
        in_specs=[pl.BlockSpec((1, ROUTE_ROWS, tm), lambda t: (t, 0, 0)),
                  pl.BlockSpec((tm, D_MODEL), lambda t: (t, 0)),
                  pl.BlockSpec((LOCAL_ROWS, D_MODEL // 2), lambda t: (t, 0)),
                  pl.BlockSpec((1, D_MODEL), lambda t: (0, 0))],
        out_specs=(pl.BlockSpec((tm, D_MODEL), lambda t: (jnp.minimum(t, tiles_a - 1), 0)),
                   pl.BlockSpec((tm, D_MODEL), lambda t: (jnp.maximum(t - tiles_a, 0), 0))),
        out_shape=(jax.ShapeDtypeStruct((tiles_a * tm, D_MODEL), F32),
                   jax.ShapeDtypeStruct((tiles_b * tm, D_MODEL), F32)),
        compiler_params=pltpu.CompilerParams(dimension_semantics=("arbitrary",), vmem_limit_bytes=VMEM_LIMIT),
        name="moe_combine",
    )(route, xnew, yloc, gfin)


def _prep_weights(g_mix, w_in, w_gla_gk2, b_gla_gk, g_gla_norm, w_br_ret, w_br_gla, w_out, g_ffn,
                  w_rg, b_rg, w_re, b_re):
    w_int = jnp.swapaxes(w_in, 0, 1).astype(BF16)
    wgk = jnp.pad(w_gla_gk2, ((0, LANES - GLA_RANK), (0, 0))).astype(BF16)
    wrt = jnp.zeros((D_MODEL, ROUTER_LANES), F32)
    wrt = wrt.at[:, 0:N_GROUPS].set(w_rg).at[:, SUBLANES:SUBLANES + N_EXPERTS].set(w_re)
    brt = jnp.zeros((1, ROUTER_LANES), F32)
    brt = brt.at[0, 0:N_GROUPS].set(b_rg).at[0, SUBLANES:SUBLANES + N_EXPERTS].set(b_re)
    wrt_hi = wrt.astype(BF16)
    wrt = jnp.concatenate([wrt_hi, (wrt - wrt_hi.astype(F32)).astype(BF16)], axis=1)
    return (g_mix.reshape(1, D_MODEL), w_int, wgk, b_gla_gk.reshape(1, HK), g_gla_norm.reshape(1, HEAD_V),
            w_br_ret.astype(BF16), w_br_gla.astype(BF16), w_out.astype(BF16), g_ffn.reshape(1, D_MODEL), wrt, brt)


def _tri_consts(tm):
    r = np.arange(tm)
    low = (r[None, :] <= r[:, None])
    bd = low & ((r[None, :] // CHUNK) == (r[:, None] // CHUNK))
    return jnp.asarray(bd, BF16), jnp.asarray(low, BF16)


def _state_to_kernel(s):
    return jnp.swapaxes(s.reshape(s.shape[0], HK, HEAD_V), 1, 2)


def _state_from_kernel(s):
    return jnp.swapaxes(s, 1, 2).reshape(1, s.shape[0], N_HEADS, HEAD_K, HEAD_V)


def kernel(x_prompt, x_sample, state_ret, state_gla, meta_tokens, g_mix, w_in, w_gla_gk2, b_gla_gk, g_gla_norm, w_br_ret, w_br_gla, w_out, g_ffn, w_router_group, b_router_group, w_router_expert, b_router_expert, w_exp_gate, w_exp_up, w_exp_down, g_final):
    n_b, seq, _ = x_prompt.shape
    n_s, dec_seq, _ = x_sample.shape
    depth = state_ret.shape[0]
    assert depth == 1 and dec_seq == CHUNK and seq % TILE_ROWS == 0 and n_s % TILE_CHUNKS == 0
    cos_t, sin_t, cos_m, sin_m = _rotary_tables(seq, PAST_LEN)
    weights = _prep_weights(g_mix[0], w_in[0], w_gla_gk2[0], b_gla_gk[0], g_gla_norm[0], w_br_ret[0], w_br_gla[0],
                            w_out[0], g_ffn[0], w_router_group[0], b_router_group[0], w_router_expert[0],
                            b_router_expert[0])
    zero_state = jnp.zeros((1, HEAD_V, HK), F32)

    x_meta = jnp.concatenate([jnp.zeros((CHUNK - N_META, D_MODEL), F32), meta_tokens.astype(F32)], axis=0)
    meta_out = _mixer_call(None, x_meta.reshape(1, CHUNK, D_MODEL), cos_m, sin_m, (zero_state, zero_state),
                           (zero_state, zero_state), weights, _tri_consts(CHUNK), n_chunks=1, route=False)
    meta_ret, meta_gla = meta_out[6], meta_out[7]

    xnew, xloc, route, ku, ret_p, gla_p, ret_s, gla_s = _mixer_call(
        x_prompt, x_sample, cos_t, sin_t, (meta_ret, meta_gla),
        (_state_to_kernel(state_ret[0]), _state_to_kernel(state_gla[0])), weights, _tri_consts(TILE_ROWS),
        n_chunks=TILE_CHUNKS, route=True)

    n_tiles = route.shape[0]
    n_blocks = n_tiles * LOCAL_UNITS // BLOCK_UNITS + N_EXPERTS
    src, blk = _plan_call(ku[:, 0, :N_EXPERTS], n_blocks)
    yloc = _experts_call(src, blk, xloc, w_exp_gate[0], w_exp_up[0], w_exp_down[0], n_blocks)
    y_p, y_s = _combine_call(route, xnew, yloc, g_final.reshape(1, D_MODEL), TILE_ROWS, n_b * seq // TILE_ROWS)

    return (y_p.reshape(n_b, seq, D_MODEL), y_s.reshape(n_s, dec_seq, D_MODEL),
            _state_from_kernel(ret_p), _state_from_kernel(gla_p), _state_from_kernel(ret_s), _state_from_kernel(gla_s))
```

```python
import functools
import math

import jax
import jax.numpy as jnp
import numpy as np
from jax import lax
from jax.experimental import pallas as pl
from jax.experimental.pallas import tpu as pltpu

D_MODEL = 1024
CHUNK = 64
PAST_LEN = 1024
N_META = 16
N_HEADS = 4
HEAD_K = 64
HEAD_V = 128
HK = N_HEADS * HEAD_K
HV = N_HEADS * HEAD_V
GLA_RANK = 16
GATE_NORM = 16.0
ROPE_BASE = 10000.0
N_GROUPS = 4
EXPERTS_PER_GROUP = 8
N_EXPERTS = N_GROUPS * EXPERTS_PER_GROUP
D_EXPERT = 512
NORM_EPS = 1e-6

LANES = 128
SUBLANES = 8
TILE_CHUNKS = 8
TILE_ROWS = TILE_CHUNKS * CHUNK
ROUTE_ROWS = 8
ROUTER_LANES = 128
UNIT = SUBLANES
MOE_BLOCK = 512
BLOCK_UNITS = MOE_BLOCK // UNIT
BLK_ROWS = 5
LOCAL_ROWS = 2 * TILE_ROWS + N_EXPERTS * UNIT
LOCAL_UNITS = LOCAL_ROWS // UNIT
VMEM_LIMIT = 56 * 1024 * 1024

C_RQ, C_RK, C_RV, C_RG = 0, 256, 512, 1024
C_GQ, C_GK, C_GV, C_GG = 1536, 1792, 2048, 2560
W_MIX = 3072
C_Z = W_MIX + GLA_RANK

F32 = jnp.float32
BF16 = jnp.bfloat16
LOG_G = tuple(math.log1p(-(2.0 ** (-5.0 - h))) for h in range(N_HEADS))


def _dot(a, b):
    return jnp.dot(a, b, preferred_element_type=F32)


def _dot_nt(a, b):
    return lax.dot_general(a, b, (((1,), (1,)), ((), ())), preferred_element_type=F32)


def _dot_tn(a, b):
    return lax.dot_general(a, b, (((0,), (0,)), ((), ())), preferred_element_type=F32)


def _sigmoid(x):
    return 1.0 / (1.0 + jnp.exp(-x))


def _pack_halves(x):
    half = x.shape[1] // 2
    bits = pltpu.bitcast(x, jnp.int32)
    return lax.shift_right_logical(bits[:, :half], 16) | (bits[:, half:] & jnp.int32(-65536))


def _unpack_halves(w):
    lo = pltpu.bitcast(lax.shift_left(w, 16), F32)
    hi = pltpu.bitcast(w & jnp.int32(-65536), F32)
    return jnp.concatenate([lo, hi], axis=1).astype(BF16)


def _head_of_lane(shape, width):
    return lax.broadcasted_iota(jnp.int32, shape, len(shape) - 1) >> int(math.log2(width))


def _per_head_lane_const(vals, shape, width):
    hd = _head_of_lane(shape, width)
    out = jnp.full(shape, vals[N_HEADS - 1], F32)
    for h in range(N_HEADS - 2, -1, -1):
        out = jnp.where(hd == h, vals[h], out)
    return out


def _tables_kernel(inv_ref, cp_ref, sp_ref, cm_ref, sm_ref, *, seq, past_len):
    inv = inv_ref[...].reshape(1, 1, LANES)
    lane = lax.broadcasted_iota(jnp.int32, (1, 1, LANES), 2)
    sign = jnp.where((lane & (HEAD_K - 1)) < (HEAD_K // 2), -1.0, 1.0)
    off = lax.broadcasted_iota(jnp.int32, (1, CHUNK, LANES), 1).astype(F32) * inv
    c_off, s_off = jnp.cos(off), jnp.sin(off)

    def chunks(n, first_pos):
        base = (lax.broadcasted_iota(jnp.int32, (n, 1, LANES), 0) * CHUNK + first_pos).astype(F32) * inv
        c_base, s_base = jnp.cos(base), jnp.sin(base)
        cos = (c_base * c_off - s_base * s_off).reshape(n * CHUNK, LANES)
        sin = ((s_base * c_off + c_base * s_off) * sign).reshape(n * CHUNK, LANES)
        return cos, sin

    cp_ref[0:seq, :], sp_ref[0:seq, :] = chunks(seq // CHUNK, 0)
    cos_s, sin_s = chunks(1, past_len)
    for c in range(TILE_CHUNKS):
        cp_ref[seq + c * CHUNK:seq + (c + 1) * CHUNK, :] = cos_s
        sp_ref[seq + c * CHUNK:seq + (c + 1) * CHUNK, :] = sin_s
    cm_ref[...], sm_ref[...] = chunks(1, -CHUNK)


def _rotary_tables(seq, past_len):
    half = HEAD_K // 2
    inv = ROPE_BASE ** (-2.0 * jnp.arange(half, dtype=F32) / HEAD_K)
    inv = jnp.tile(inv, LANES // half).reshape(1, LANES)
    shp = lambda r: jax.ShapeDtypeStruct((r, LANES), F32)
    return pl.pallas_call(
        functools.partial(_tables_kernel, seq=seq, past_len=past_len),
        out_shape=(shp(seq + TILE_ROWS), shp(seq + TILE_ROWS), shp(CHUNK), shp(CHUNK)),
        compiler_params=pltpu.CompilerParams(vmem_limit_bytes=VMEM_LIMIT),
        name="rotary_tables",
    )(inv)


def _mixer_kernel(*refs, n_chunks, tiles_p, tiles_per_row, route):
    tm = n_chunks * CHUNK
    if tiles_p:
        xp_ref, refs = refs[0], refs[1:]
    (xs_ref, cos_ref, sin_ref, spi_r_ref, spi_g_ref, ssi_r_ref, ssi_g_ref,
     gmix_ref, wint_ref, wgk_ref, bgk_ref, gnorm_ref, wbr_ref, wbg_ref, wout_ref,
     gffn_ref, wrt_ref, brt_ref, bdtri_ref, ltri_ref,
     xnew_ref, xloc_ref, route_ref, ku_ref, spo_r_ref, spo_g_ref, sso_r_ref, sso_g_ref,
     qb, qdb, kb, kkb, vb, gqb, gkkb, gvb, ga, o_ret, o_gla, st_ret, st_gla,
     sin_r, sin_g, sout_r, sout_g) = refs

    i = pl.program_id(0)
    if tiles_p:
        is_s = i >= tiles_p
        t_idx = jnp.minimum(i, tiles_p - 1) % tiles_per_row
        x = jnp.where(is_s, xs_ref[...].reshape(tm, D_MODEL), xp_ref[...].reshape(tm, D_MODEL))

        @pl.when(jnp.logical_and(jnp.logical_not(is_s), t_idx == 0))
        def _():
            st_ret[...] = spi_r_ref[0].T
            st_gla[...] = spi_g_ref[0].T

        @pl.when(i == 0)
        def _():
            sin_r[...] = jnp.zeros(sin_r.shape, F32)
            sin_g[...] = jnp.zeros(sin_g.shape, F32)

        @pl.when(is_s)
        def _():
            for c in range(n_chunks):
                sin_r[c] = ssi_r_ref[c].T
                sin_g[c] = ssi_g_ref[c].T
    else:
        is_s = None
        x = xs_ref[...].reshape(tm, D_MODEL)

    h = x * lax.rsqrt(jnp.mean(x * x, axis=-1, keepdims=True) + NORM_EPS) * gmix_ref[...]
    hb = h.astype(BF16)

    def proj(c0, width):
        return _dot_nt(hb, wint_ref[c0:c0 + width, :])

    cos3 = jnp.concatenate([cos_ref[...]] * 2, axis=1).reshape(n_chunks, CHUNK, HK)
    sin3 = jnp.concatenate([sin_ref[...]] * 2, axis=1).reshape(n_chunks, CHUNK, HK)
    lane_hk = lax.broadcasted_iota(jnp.int32, (tm, HK), 1)
    first_half = (lane_hk & (HEAD_K - 1)) < (HEAD_K // 2)

    def rotary(t):
        swapped = jnp.where(first_half, pltpu.roll(t, HK - HEAD_K // 2, 1), pltpu.roll(t, HEAD_K // 2, 1))
        t3 = t.reshape(n_chunks, CHUNK, HK)
        return (t3 * cos3 + swapped.reshape(n_chunks, CHUNK, HK) * sin3).reshape(tm, HK)

    logg_hk = _per_head_lane_const(LOG_G, (CHUNK, HK), HEAD_K)
    l_idx = lax.broadcasted_iota(jnp.int32, (CHUNK, HK), 0).astype(F32)
    qdec = jnp.exp((l_idx + 1.0) * logg_hk)
    kdec = jnp.exp((CHUNK - 1.0 - l_idx) * logg_hk)
    cdec = jnp.exp(float(CHUNK) * _per_head_lane_const(LOG_G, (1, HK), HEAD_K))
    r_idx = lax.broadcasted_iota(jnp.int32, (N_HEADS * CHUNK, CHUNK), 0)
    m_idx = lax.broadcasted_iota(jnp.int32, (N_HEADS * CHUNK, CHUNK), 1)
    logg_rows = jnp.full((N_HEADS * CHUNK, CHUNK), LOG_G[N_HEADS - 1], F32)
    for hh in range(N_HEADS - 2, -1, -1):
        logg_rows = jnp.where((r_idx >> int(math.log2(CHUNK))) == hh, LOG_G[hh], logg_rows)
    dmat = jnp.exp(jnp.abs((r_idx & (CHUNK - 1)) - m_idx).astype(F32) * logg_rows)

    glr = jnp.where(lax.broadcasted_iota(jnp.int32, (tm, LANES), 1) < GLA_RANK, proj(W_MIX, LANES), 0.0)
    rq = rotary(proj(C_RQ, HK))
    qb[...] = rq.astype(BF16)
    qdb[...] = (rq.reshape(n_chunks, CHUNK, HK) * qdec).reshape(tm, HK).astype(BF16)
    gl = _dot(glr.astype(BF16), wgk_ref[...]) + bgk_ref[...]
    rk = rotary(proj(C_RK, HK)) * (HEAD_K ** -0.5)
    kb[...] = rk.astype(BF16)
    kkb[...] = (rk.reshape(n_chunks, CHUNK, HK) * kdec).reshape(tm, HK).astype(BF16)
    log_a = (jnp.minimum(gl, 0.0) - jnp.log1p(jnp.exp(-jnp.abs(gl)))) / GATE_NORM
    la_hi = log_a.astype(BF16)
    la_lo = (log_a - la_hi.astype(F32)).astype(BF16)
    vb[...] = proj(C_RV, HV).astype(BF16)
    bdtri = bdtri_ref[...]
    bcum = _dot(bdtri, la_hi) + _dot(bdtri, la_lo)
    gqb[...] = (proj(C_GQ, HK) * (HEAD_K ** -0.5)).astype(BF16)
    gvb[...] = proj(C_GV, HV).astype(BF16)
    gk = proj(C_GK, HK)
    b3 = bcum.reshape(n_chunks, CHUNK, HK)
    bl3 = b3[:, CHUNK - 1:CHUNK, :]
    gkkb[...] = (gk.reshape(n_chunks, CHUNK, HK) * jnp.exp(bl3 - b3)).reshape(tm, HK).astype(BF16)
    ga[...] = jnp.broadcast_to(jnp.exp(bl3), (n_chunks, SUBLANES, HK))

    def stack_masked(a, width):
        head = _head_of_lane(a.shape, width)
        zero = jnp.zeros_like(a)
        return jnp.concatenate([jnp.where(head == hh, a, zero) for hh in range(N_HEADS)], axis=0)

    def heads_to_rows(a):
        return jnp.concatenate([a[:, hh * HEAD_V:(hh + 1) * HEAD_V] for hh in range(N_HEADS)], axis=0)

    def rows_to_heads(a):
        return jnp.concatenate([a[hh * CHUNK:(hh + 1) * CHUNK, :] for hh in range(N_HEADS)], axis=1)

    chunk_rows = [slice(c * CHUNK, (c + 1) * CHUNK) for c in range(n_chunks)]
    probs = [(_dot_nt(stack_masked(qb[r, :], HEAD_K), kb[r, :]) * dmat).astype(BF16) for r in chunk_rows]
    inc_ret = [_dot_tn(heads_to_rows(vb[r, :]), stack_masked(kkb[r, :], HEAD_K)) for r in chunk_rows]
    inc_gla = [_dot_tn(heads_to_rows(gvb[r, :]), stack_masked(gkkb[r, :], HEAD_K)) for r in chunk_rows]
    rg = proj(C_RG, HV)
    gg = proj(C_GG, HV)
    s_in, g_out = [], []
    s_cur, g_cur = st_ret[...], st_gla[...]
    for c in range(n_chunks):
        if is_s is None:
            s_cur, g_cur = ssi_r_ref[c].T, ssi_g_ref[c].T
        else:
            s_cur = jnp.where(is_s, sin_r[c], s_cur)
            g_cur = jnp.where(is_s, sin_g[c], g_cur)
        s_in.append(s_cur.astype(BF16))
        s_cur = s_cur * cdec + inc_ret[c]
        g_cur = g_cur * ga[c][0:1, :] + inc_gla[c]
        g_out.append(g_cur.astype(BF16))
        sout_r[c] = s_cur
        sout_g[c] = g_cur
    st_ret[...] = s_cur
    st_gla[...] = g_cur

    def stream_states_out():
        for c in range(n_chunks):
            sso_r_ref[c] = sout_r[c].T
            sso_g_ref[c] = sout_g[c].T

    if is_s is None:
        stream_states_out()
    else:
        pl.when(is_s)(stream_states_out)
    for c, r in enumerate(chunk_rows):
        v = vb[r, :]
        intra = jnp.concatenate(
            [_dot(probs[c][hh * CHUNK:(hh + 1) * CHUNK, :], v[:, hh * HEAD_V:(hh + 1) * HEAD_V])
             for hh in range(N_HEADS)], axis=1)
        inter = rows_to_heads(_dot_nt(stack_masked(qdb[r, :], HEAD_K), s_in[c]))
        o_ret[r, :] = intra + inter
        o_gla[r, :] = rows_to_heads(_dot_nt(stack_masked(gqb[r, :], HEAD_K), g_out[c]))

    gnorm = gnorm_ref[...]
    orr = o_ret[...]
    ogg = o_gla[...]
    ret_parts, gla_parts = [], []
    for hh in range(N_HEADS):
        sl = slice(hh * HEAD_V, (hh + 1) * HEAD_V)
        oh = orr[:, sl]
        mu = jnp.mean(oh, axis=-1, keepdims=True)
        dev = oh - mu
        var = jnp.mean(dev * dev, axis=-1, keepdims=True)
        ret_parts.append(dev * lax.rsqrt(var + NORM_EPS))
        og = ogg[:, sl]
        gla_parts.append(og * lax.rsqrt(jnp.mean(og * og, axis=-1, keepdims=True) + NORM_EPS) * gnorm)
    o_r = jnp.concatenate(ret_parts, axis=1) * (rg * _sigmoid(rg))
    o_g = jnp.concatenate(gla_parts, axis=1) * (gg * _sigmoid(gg))
    merged = (_sigmoid(proj(C_Z, D_MODEL)) * _dot(o_r.astype(BF16), wbr_ref[...])
              + _sigmoid(proj(C_Z + D_MODEL, D_MODEL)) * _dot(o_g.astype(BF16), wbg_ref[...]))
    xn = x + _dot(merged.astype(BF16), wout_ref[...])
    xnew_ref[...] = xn

    if route:
        h2 = xn * lax.rsqrt(jnp.mean(xn * xn, axis=-1, keepdims=True) + NORM_EPS) * gffn_ref[...]
        h2_hi = h2.astype(BF16)
        h2_lo = (h2 - h2_hi.astype(F32)).astype(BF16)
        hi_both = _dot_nt(h2_hi, wrt_ref[...])
        logits = (hi_both[:, :ROUTER_LANES] + _dot_nt(h2_lo, wrt_ref[:ROUTER_LANES, :])
                  + hi_both[:, ROUTER_LANES:]) + brt_ref[...]
        lt = logits.T
        row8 = lax.broadcasted_iota(jnp.int32, (SUBLANES, tm), 0)
        neg_inf = jnp.float32(-jnp.inf)
        glog = jnp.where(row8 < N_GROUPS, lt[0:SUBLANES, :], neg_inf)
        gmax = jnp.max(glog, axis=0, keepdims=True)
        grp = jnp.min(jnp.where(glog == gmax, row8, SUBLANES), axis=0, keepdims=True)
        p_grp = 1.0 / jnp.sum(jnp.exp(glog - gmax), axis=0, keepdims=True)
        le = jnp.zeros((SUBLANES, tm), F32)
        for g in range(N_GROUPS):
            le = jnp.where(grp == g, lt[SUBLANES * (g + 1):SUBLANES * (g + 2), :], le)
        m1 = jnp.max(le, axis=0, keepdims=True)
        i1 = jnp.min(jnp.where(le == m1, row8, SUBLANES), axis=0, keepdims=True)
        le2 = jnp.where(row8 == i1, neg_inf, le)
        m2 = jnp.max(le2, axis=0, keepdims=True)
        i2 = jnp.min(jnp.where(le2 == m2, row8, SUBLANES), axis=0, keepdims=True)
        e0 = grp * EXPERTS_PER_GROUP + i1
        e1 = grp * EXPERTS_PER_GROUP + i2
        t21 = jnp.exp(m2 - m1)
        w0 = p_grp / (1.0 + t21)
        w1 = p_grp * t21 / (1.0 + t21)

        erow = lax.broadcasted_iota(jnp.int32, (N_EXPERTS, tm), 0)
        hit0 = erow == e0
        hit1 = erow == e1
        onehot = jnp.where(jnp.logical_or(hit0, hit1), 1.0, 0.0).astype(BF16)
        cum = _dot_nt(onehot, ltri_ref[...])
        n_run = cum[:, tm - 1:tm]
        n_pad = jnp.ceil(n_run / UNIT) * UNIT
        rank0 = jnp.sum(jnp.where(hit0, cum - 1.0, 0.0), axis=0, keepdims=True)
        rank1 = jnp.sum(jnp.where(hit1, cum - 1.0, 0.0), axis=0, keepdims=True)
        start0 = jnp.sum(jnp.where(erow < e0, n_pad, 0.0), axis=0, keepdims=True)
        start1 = jnp.sum(jnp.where(erow < e1, n_pad, 0.0), axis=0, keepdims=True)
        ld0 = (start0 + rank0).astype(jnp.int32)
        ld1 = (start1 + rank1).astype(jnp.int32)
        lrow = lax.broadcasted_iota(jnp.int32, (LOCAL_ROWS, tm), 0)
        perm = jnp.where(jnp.logical_or(lrow == ld0, lrow == ld1), 1.0, 0.0).astype(BF16)
        xloc_ref[...] = _pack_halves(_dot(perm, h2_hi))
        lane_e = lax.broadcasted_iota(jnp.int32, (N_EXPERTS, LANES), 1)
        erow_l = lax.broadcasted_iota(jnp.int32, (N_EXPERTS, LANES), 0)
        units_row = jnp.sum(jnp.where(erow_l == lane_e, n_pad / UNIT, 0.0), axis=0, keepdims=True)
        ku_ref[...] = jnp.broadcast_to(units_row, (SUBLANES, LANES)).astype(jnp.int32).reshape(ku_ref.shape)
        zero_row = jnp.zeros((1, tm), jnp.int32)
        rec = jnp.concatenate([e0, e1, ld0, ld1, pltpu.bitcast(w0, jnp.int32), pltpu.bitcast(w1, jnp.int32),
                               zero_row, zero_row], axis=0)
        route_ref[...] = rec.reshape(route_ref.shape)
    else:
        xloc_ref[...] = jnp.zeros(xloc_ref.shape, jnp.int32)
        route_ref[...] = jnp.zeros(route_ref.shape, jnp.int32)
        ku_ref[...] = jnp.zeros(ku_ref.shape, jnp.int32)

    if tiles_p:
        @pl.when(jnp.logical_and(jnp.logical_not(is_s), t_idx == tiles_per_row - 1))
        def _():
            spo_r_ref[0] = st_ret[...].T
            spo_g_ref[0] = st_gla[...].T
    else:
        spo_r_ref[0] = st_ret[...].T
        spo_g_ref[0] = st_gla[...].T


def _const_spec(shape):
    nd = len(shape)
    return pl.BlockSpec(shape, lambda *_: (0,) * nd, pipeline_mode=pl.Buffered(1))


def _mixer_call(x_prompt, x_streams, cos, sin, st_prompt, st_streams, weights, consts, *, n_chunks, route):
    tm = n_chunks * CHUNK
    n_streams = x_streams.shape[0]
    tiles_s = n_streams // n_chunks
    if x_prompt is not None:
        n_rows, n_seq, _ = x_prompt.shape
        tpr = n_seq // tm
        tiles_p = n_rows * tpr
    else:
        n_rows, tpr, tiles_p = 1, 1, 0
    n_tiles = tiles_p + tiles_s
    p_idx = lambda i: jnp.minimum(i, tiles_p - 1)
    s_idx = lambda i: jnp.maximum(i - tiles_p, 0)

    st_blk = (1, HK, HEAD_V)
    in_specs, args = [], []
    if tiles_p:
        in_specs.append(pl.BlockSpec((1, tm, D_MODEL), lambda i: (p_idx(i) // tpr, p_idx(i) % tpr, 0)))
        args.append(x_prompt)
        cos_spec = pl.BlockSpec((tm, LANES), lambda i: (jnp.where(i < tiles_p, i % tpr, tpr), 0))
    else:
        cos_spec = pl.BlockSpec((tm, LANES), lambda i: (0, 0))
    once = pl.Buffered(1)
    stream_in = pl.BlockSpec((n_chunks, HK, HEAD_V), lambda i: (s_idx(i), 0, 0), pipeline_mode=once)
    in_specs += [pl.BlockSpec((n_chunks, CHUNK, D_MODEL), lambda i: (s_idx(i), 0, 0), pipeline_mode=once),
                 cos_spec, cos_spec, _const_spec(st_blk), _const_spec(st_blk), stream_in, stream_in]
    args += [x_streams, cos, sin, st_prompt[0], st_prompt[1], st_streams[0], st_streams[1]]
    in_specs += [_const_spec(w.shape) for w in weights] + [_const_spec(c.shape) for c in consts]
    args += list(weights) + list(consts)

    out_shape = (jax.ShapeDtypeStruct((n_tiles * tm, D_MODEL), F32),
                 jax.ShapeDtypeStruct((n_tiles * LOCAL_ROWS, D_MODEL // 2), jnp.int32),
                 jax.ShapeDtypeStruct((n_tiles, ROUTE_ROWS, tm), jnp.int32),
                 jax.ShapeDtypeStruct((n_tiles, SUBLANES, LANES), jnp.int32),
                 jax.ShapeDtypeStruct((n_rows, HK, HEAD_V), F32), jax.ShapeDtypeStruct((n_rows, HK, HEAD_V), F32),
                 jax.ShapeDtypeStruct((n_streams, HK, HEAD_V), F32),
                 jax.ShapeDtypeStruct((n_streams, HK, HEAD_V), F32))
    row_spec = pl.BlockSpec(st_blk, lambda i: (p_idx(i) // tpr if tiles_p else 0, 0, 0))
    stream_spec = pl.BlockSpec((n_chunks, HK, HEAD_V), lambda i: (s_idx(i), 0, 0))
    out_specs = (pl.BlockSpec((tm, D_MODEL), lambda i: (i, 0)),
                 pl.BlockSpec((LOCAL_ROWS, D_MODEL // 2), lambda i: (i, 0)),
                 pl.BlockSpec((1, ROUTE_ROWS, tm), lambda i: (i, 0, 0)),
                 pl.BlockSpec((1, SUBLANES, LANES), lambda i: (i, 0, 0)),
                 row_spec, row_spec, stream_spec, stream_spec)

    scratch = [pltpu.VMEM((tm, HK), BF16), pltpu.VMEM((tm, HK), BF16), pltpu.VMEM((tm, HK), BF16),
               pltpu.VMEM((tm, HK), BF16), pltpu.VMEM((tm, HV), BF16),
               pltpu.VMEM((tm, HK), BF16), pltpu.VMEM((tm, HK), BF16), pltpu.VMEM((tm, HV), BF16),
               pltpu.VMEM((n_chunks, SUBLANES, HK), F32),
               pltpu.VMEM((tm, HV), F32), pltpu.VMEM((tm, HV), F32),
               pltpu.VMEM((HEAD_V, HK), F32), pltpu.VMEM((HEAD_V, HK), F32)]
    scratch += [pltpu.VMEM((n_chunks, HEAD_V, HK), F32)] * 4

    return pl.pallas_call(
        functools.partial(_mixer_kernel, n_chunks=n_chunks, tiles_p=tiles_p, tiles_per_row=tpr, route=route),
        grid=(n_tiles,), in_specs=in_specs, out_specs=out_specs, out_shape=out_shape, scratch_shapes=scratch,
        compiler_params=pltpu.CompilerParams(dimension_semantics=("arbitrary",), vmem_limit_bytes=VMEM_LIMIT),
        name="mixer" if route else "mixer_meta",
    )(*args)


def _plan_kernel(ku_ref, src_ref, blk_ref, run_first, run_step, next_unit, seg_start, seg_units, *, n_tiles, n_blocks):
    shift = int(math.log2(BLOCK_UNITS))
    group_shift = int(math.log2(SUBLANES * LANES))

    def init_tile(t, c):
        next_unit[t] = t * LOCAL_UNITS
        return c

    lax.fori_loop(0, n_tiles, init_tile, 0)

    def init_block(b, c):
        blk_ref[0, b] = N_EXPERTS - 1
        blk_ref[1, b] = 0
        blk_ref[2, b] = 0
        blk_ref[3, b] = -1
        blk_ref[4, b] = 0
        return c

    lax.fori_loop(0, n_blocks, init_block, 0)

    def per_expert(e, carry):
        g0, position = carry

        def per_tile(t, c):
            units, d_prev = c
            base = next_unit[t]
            d = base - units
            run_first[e * n_tiles + t] = units
            run_step[e * n_tiles + t] = d - d_prev
            k = ku_ref[t, e]
            next_unit[t] = base + k
            return units + k, d

        units, _ = lax.fori_loop(0, n_tiles, per_tile, (0, 0), unroll=4)
        g_pad = g0 + (((units + (BLOCK_UNITS - 1)) >> shift) << shift)
        seg_start[e] = g0
        seg_units[e] = units

        def set_block(b, c):
            blk_ref[0, b] = e
            blk_ref[1, b] = jnp.minimum(g0 + units - (b << shift), BLOCK_UNITS)
            blk_ref[4, b] = position & 1
            return c

        lax.fori_loop(g0 >> shift, g_pad >> shift, set_block, 0)
        return g_pad, position + (units > 0).astype(jnp.int32)

    g_total, _ = lax.fori_loop(0, N_EXPERTS, per_expert, (0, 0))
    used = g_total >> shift
    blk_ref[2, 0] = used

    def set_next(i, c):
        later, following = c
        b = used - 1 - i
        e = blk_ref[0, b]
        following = jnp.where(e != later, later, following)
        blk_ref[3, b] = following
        return e, following

    lax.fori_loop(0, used, set_next, (-1, -1))

    src_ref[...] = jnp.zeros(src_ref.shape, jnp.int32)
    in_group = (lax.broadcasted_iota(jnp.int32, (SUBLANES, LANES), 0) * LANES
                + lax.broadcasted_iota(jnp.int32, (SUBLANES, LANES), 1))

    def expert_units(e, c):
        g0 = seg_start[e]
        units = seg_units[e]
        g_pad = g0 + (((units + (BLOCK_UNITS - 1)) >> shift) << shift)

        def per_group(grp, c2):
            g = in_group + (grp << group_shift)
            o = g - g0
            mine = jnp.logical_and(o >= 0, g < g_pad)
            o_eff = jnp.where(o < units, o, (o >> shift) << shift)

            def per_tile(t, acc):
                return acc + jnp.where(o_eff >= run_first[e * n_tiles + t], run_step[e * n_tiles + t], 0)

            offset = lax.fori_loop(0, n_tiles, per_tile, jnp.zeros((SUBLANES, LANES), jnp.int32), unroll=4)
            rows = pl.ds(pl.multiple_of(grp * SUBLANES, SUBLANES), SUBLANES)
            src_ref[rows, :] = jnp.where(mine, o_eff + offset, src_ref[rows, :])
            return c2

        lax.fori_loop(g0 >> group_shift, (g_pad + (SUBLANES * LANES - 1)) >> group_shift, per_group, 0)
        return c

    lax.fori_loop(0, N_EXPERTS, expert_units, 0)


def _plan_call(ku, n_blocks):
    n_tiles = ku.shape[0]
    smem = pl.BlockSpec(memory_space=pltpu.SMEM)
    group = SUBLANES * LANES
    src_rows = -(-n_blocks * BLOCK_UNITS // group) * SUBLANES
    src, blk = pl.pallas_call(
        functools.partial(_plan_kernel, n_tiles=n_tiles, n_blocks=n_blocks),
        in_specs=[smem], out_specs=(pl.BlockSpec(memory_space=pltpu.VMEM), smem),
        out_shape=(jax.ShapeDtypeStruct((src_rows, LANES), jnp.int32),
                   jax.ShapeDtypeStruct((BLK_ROWS, n_blocks), jnp.int32)),
        scratch_shapes=[pltpu.SMEM((N_EXPERTS * n_tiles,), jnp.int32), pltpu.SMEM((N_EXPERTS * n_tiles,), jnp.int32),
                        pltpu.SMEM((n_tiles,), jnp.int32), pltpu.SMEM((N_EXPERTS,), jnp.int32),
                        pltpu.SMEM((N_EXPERTS,), jnp.int32)],
        name="moe_plan",
    )(ku)
    return src.reshape(-1), blk


def _experts_kernel(src_ref, blk_ref, x_hbm, wg_hbm, wu_hbm, wd_hbm, y_hbm,
                    xin, yout, wgf, wuf, wdf, wgub, wdb, sem_in, sem_out, sem_w):
    b = pl.program_id(0)
    used = blk_ref[2, 0]
    slot = b & 1

    def weight_copies(e, s):
        return [pltpu.make_async_copy(hbm.at[e], buf.at[s], sem_w.at[s])
                for hbm, buf in ((wg_hbm, wgf), (wu_hbm, wuf), (wd_hbm, wdf))]

    def unit_rows(blk, u):
        return pl.ds(pl.multiple_of(src_ref[blk * BLOCK_UNITS + u] * UNIT, UNIT), UNIT)

    def in_copy(blk, u, s):
        return pltpu.make_async_copy(x_hbm.at[unit_rows(blk, u), :], xin.at[s, pl.ds(u * UNIT, UNIT), :],
                                     sem_in.at[s])

    def out_copy(blk, u, s):
        return pltpu.make_async_copy(yout.at[s, pl.ds(u * UNIT, UNIT), :], y_hbm.at[unit_rows(blk, u), :],
                                     sem_out.at[s])

    @pl.when(jnp.logical_and(b == 0, used > 0))
    def _():
        for copy in weight_copies(blk_ref[0, 0], blk_ref[4, 0]):
            copy.start()
        for u in range(BLOCK_UNITS):
            in_copy(0, u, 0).start()

    @pl.when(b < used)
    def _():
        expert = blk_ref[0, b]
        changed = jnp.logical_or(b == 0, expert != blk_ref[0, jnp.maximum(b - 1, 0)])

        @pl.when(changed)
        def _():
            wslot = blk_ref[4, b]
            following = blk_ref[3, b]
            for copy in weight_copies(expert, wslot):
                copy.wait()

            @pl.when(following >= 0)
            def _():
                for copy in weight_copies(following, 1 - wslot):
                    copy.start()

            wgub[:, :D_EXPERT] = wgf[wslot].astype(BF16)
            wgub[:, D_EXPERT:] = wuf[wslot].astype(BF16)
            wdb[...] = wdf[wslot].astype(BF16)

        @pl.when(b + 1 < used)
        def _():
            for u in range(BLOCK_UNITS):
                in_copy(b + 1, u, 1 - slot).start()

        for u in range(BLOCK_UNITS):
            in_copy(b, u, slot).wait()

        n_real = blk_ref[1, b]

        def mlp(rows):
            xb = _unpack_halves(xin[slot, :rows, :])
            gate_up = _dot(xb, wgub[...])
            gate = gate_up[:, :D_EXPERT]
            hid = (gate * _sigmoid(gate)) * gate_up[:, D_EXPERT:]
            yout[slot, :rows, :] = _pack_halves(_dot(hid.astype(BF16), wdb[...]).astype(BF16).astype(F32))

        @pl.when(n_real > BLOCK_UNITS // 2)
        def _():
            mlp(MOE_BLOCK)

        @pl.when(n_real <= BLOCK_UNITS // 2)
        def _():
            mlp(MOE_BLOCK // 2)


        @pl.when(n_real == BLOCK_UNITS)
        def _():
            for u in range(BLOCK_UNITS):
                out_copy(b, u, slot).start()

        @pl.when(n_real < BLOCK_UNITS)
        def _():
            lax.fori_loop(0, n_real, lambda u, c: (out_copy(b, u, slot).start(), c)[1], 0)

        @pl.when(b > 0)
        def _():
            lax.fori_loop(0, blk_ref[1, b - 1], lambda u, c: (out_copy(b - 1, u, 1 - slot).wait(), c)[1], 0)

        @pl.when(b == used - 1)
        def _():
            lax.fori_loop(0, n_real, lambda u, c: (out_copy(b, u, slot).wait(), c)[1], 0)


def _experts_call(src, blk, xloc, wg, wu, wd, n_blocks):
    hbm = pl.BlockSpec(memory_space=pl.ANY)
    grid_spec = pltpu.PrefetchScalarGridSpec(
        num_scalar_prefetch=2,
        grid=(n_blocks,),
        in_specs=[hbm, hbm, hbm, hbm],
        out_specs=hbm,
        scratch_shapes=[pltpu.VMEM((2, MOE_BLOCK, D_MODEL // 2), jnp.int32),
                        pltpu.VMEM((2, MOE_BLOCK, D_MODEL // 2), jnp.int32),
                        pltpu.VMEM((2, D_MODEL, D_EXPERT), F32), pltpu.VMEM((2, D_MODEL, D_EXPERT), F32),
                        pltpu.VMEM((2, D_EXPERT, D_MODEL), F32),
                        pltpu.VMEM((D_MODEL, 2 * D_EXPERT), BF16), pltpu.VMEM((D_EXPERT, D_MODEL), BF16),
                        pltpu.SemaphoreType.DMA((2,)), pltpu.SemaphoreType.DMA((2,)),
                        pltpu.SemaphoreType.DMA((2,))],
    )
    return pl.pallas_call(
        _experts_kernel, grid_spec=grid_spec,
        out_shape=jax.ShapeDtypeStruct(xloc.shape, jnp.int32),
        input_output_aliases={2: 0},
        compiler_params=pltpu.CompilerParams(dimension_semantics=("arbitrary",), vmem_limit_bytes=VMEM_LIMIT),
        name="moe_experts",
    )(src, blk, xloc, wg, wu, wd)


def _combine_kernel(route_ref, x_ref, y_ref, gfin_ref, out_a_ref, out_b_ref, *, tm, tiles_a):
    t = pl.program_id(0)
    rec = route_ref[0]
    recf = jnp.concatenate([rec[2:4, :].astype(F32), pltpu.bitcast(rec, F32)[4:6, :],
                            jnp.zeros((LANES - 4, tm), F32)], axis=0)
    cols = recf.T
    lrow = lax.broadcasted_iota(jnp.int32, (tm, LOCAL_ROWS), 1).astype(F32)
    select = jnp.where(lrow == cols[:, 0:1], cols[:, 2:3], jnp.where(lrow == cols[:, 1:2], cols[:, 3:4], 0.0))
    xf = x_ref[...] + _dot(select.astype(BF16), _unpack_halves(y_ref[...]))
    out = xf * lax.rsqrt(jnp.mean(xf * xf, axis=-1, keepdims=True) + NORM_EPS) * gfin_ref[...]

    @pl.when(t < tiles_a)
    def _():
        out_a_ref[...] = out

    @pl.when(t >= tiles_a)
    def _():
        out_b_ref[...] = out


def _combine_call(route, xnew, yloc, gfin, tm, tiles_a):
    n_tiles = route.shape[0]
    tiles_b = n_tiles - tiles_a
    return pl.pallas_call(
        functools.partial(_combine_kernel, tm=tm, tiles_a=tiles_a),
        grid=(n_tiles,),
        in_specs=[pl.BlockSpec((1, ROUTE_ROWS, tm), lambda t: (t, 0, 0)),
                  pl.BlockSpec((tm, D_MODEL), lambda t: (t, 0)),
                  pl.BlockSpec((LOCAL_ROWS, D_MODEL // 2), lambda t: (t, 0)),
                  pl.BlockSpec((1, D_MODEL), lambda t: (0, 0))],
        out_specs=(pl.BlockSpec((tm, D_MODEL), lambda t: (jnp.minimum(t, tiles_a - 1), 0)),
                   pl.BlockSpec((tm, D_MODEL), lambda t: (jnp.maximum(t - tiles_a, 0), 0))),
        out_shape=(jax.ShapeDtypeStruct((tiles_a * tm, D_MODEL), F32),
                   jax.ShapeDtypeStruct((tiles_b * tm, D_MODEL), F32)),
        compiler_params=pltpu.CompilerParams(dimension_semantics=("arbitrary",), vmem_limit_bytes=VMEM_LIMIT),
        name="moe_combine",
    )(route, xnew, yloc, gfin)


def _prep_weights(g_mix, w_in, w_gla_gk2, b_gla_gk, g_gla_norm, w_br_ret, w_br_gla, w_out, g_ffn,
                  w_rg, b_rg, w_re, b_re):
    w_int = jnp.swapaxes(w_in, 0, 1).astype(BF16)
    wgk = jnp.pad(w_gla_gk2, ((0, LANES - GLA_RANK), (0, 0))).astype(BF16)
    def router_rows(groups, experts):
        return jnp.concatenate([groups, jnp.zeros((SUBLANES - N_GROUPS,) + groups.shape[1:], F32), experts,
                                jnp.zeros((ROUTER_LANES - SUBLANES - N_EXPERTS,) + experts.shape[1:], F32)], axis=0)

    wrt = router_rows(w_rg.T, w_re.T)
    brt = router_rows(b_rg, b_re).reshape(1, ROUTER_LANES)
    wrt_hi = wrt.astype(BF16)
    wrt = jnp.concatenate([wrt_hi, (wrt - wrt_hi.astype(F32)).astype(BF16)], axis=0)
    return (g_mix.reshape(1, D_MODEL), w_int, wgk, b_gla_gk.reshape(1, HK), g_gla_norm.reshape(1, HEAD_V),
            w_br_ret.astype(BF16), w_br_gla.astype(BF16), w_out.astype(BF16), g_ffn.reshape(1, D_MODEL), wrt, brt)


def _tri_consts(tm):
    r = np.arange(tm)
    low = (r[None, :] <= r[:, None])
    bd = low & ((r[None, :] // CHUNK) == (r[:, None] // CHUNK))
    return jnp.asarray(bd, BF16), jnp.asarray(low, BF16)


def _state_to_kernel(s):
    return s.reshape(s.shape[0], HK, HEAD_V)


def _state_from_kernel(s):
    return s.reshape(1, s.shape[0], N_HEADS, HEAD_K, HEAD_V)


def kernel(x_prompt, x_sample, state_ret, state_gla, meta_tokens, g_mix, w_in, w_gla_gk2, b_gla_gk, g_gla_norm, w_br_ret, w_br_gla, w_out, g_ffn, w_router_group, b_router_group, w_router_expert, b_router_expert, w_exp_gate, w_exp_up, w_exp_down, g_final):
    n_b, seq, _ = x_prompt.shape
    n_s, dec_seq, _ = x_sample.shape
    depth = state_ret.shape[0]
    assert depth == 1 and dec_seq == CHUNK and seq % TILE_ROWS == 0 and n_s % TILE_CHUNKS == 0
    cos_t, sin_t, cos_m, sin_m = _rotary_tables(seq, PAST_LEN)
    weights = _prep_weights(g_mix[0], w_in[0], w_gla_gk2[0], b_gla_gk[0], g_gla_norm[0], w_br_ret[0], w_br_gla[0],
                            w_out[0], g_ffn[0], w_router_group[0], b_router_group[0], w_router_expert[0],
                            b_router_expert[0])
    zero_state = jnp.zeros((1, HK, HEAD_V), F32)

    x_meta = jnp.concatenate([jnp.zeros((CHUNK - N_META, D_MODEL), F32), meta_tokens.astype(F32)], axis=0)
    meta_out = _mixer_call(None, x_meta.reshape(1, CHUNK, D_MODEL), cos_m, sin_m, (zero_state, zero_state),
                           (zero_state, zero_state), weights, _tri_consts(CHUNK), n_chunks=1, route=False)
    meta_ret, meta_gla = meta_out[6], meta_out[7]

    xnew, xloc, route, ku, ret_p, gla_p, ret_s, gla_s = _mixer_call(
        x_prompt, x_sample, cos_t, sin_t, (meta_ret, meta_gla),
        (_state_to_kernel(state_ret[0]), _state_to_kernel(state_gla[0])), weights, _tri_consts(TILE_ROWS),
        n_chunks=TILE_CHUNKS, route=True)

    n_tiles = route.shape[0]
    n_blocks = n_tiles * LOCAL_UNITS // BLOCK_UNITS + N_EXPERTS
    src, blk = _plan_call(ku[:, 0, :N_EXPERTS], n_blocks)
    yloc = _experts_call(src, blk, xloc, w_exp_gate[0], w_exp_up[0], w_exp_down[0], n_blocks)
    y_p, y_s = _combine_call(route, xnew, yloc, g_final.reshape(1, D_MODEL), TILE_ROWS, n_b * seq // TILE_ROWS)

    return (y_p.reshape(n_b, seq, D_MODEL), y_s.reshape(n_s, dec_seq, D_MODEL),
            _state_from_kernel(ret_p), _state_from_kernel(gla_p), _state_from_kernel(ret_s), _state_from_kernel(gla_s))
```

```python
import functools
import math

import jax
import jax.numpy as jnp
import numpy as np
from jax import lax
from jax.experimental import pallas as pl
from jax.experimental.pallas import tpu as pltpu

D_MODEL = 1024
CHUNK = 64
PAST_LEN = 1024
N_META = 16
N_HEADS = 4
HEAD_K = 64
HEAD_V = 128
HK = N_HEADS * HEAD_K
HV = N_HEADS * HEAD_V
GLA_RANK = 16
GATE_NORM = 16.0
ROPE_BASE = 10000.0
N_GROUPS = 4
EXPERTS_PER_GROUP = 8
N_EXPERTS = N_GROUPS * EXPERTS_PER_GROUP
D_EXPERT = 512
NORM_EPS = 1e-6

LANES = 128
SUBLANES = 8
TILE_CHUNKS = 8
TILE_ROWS = TILE_CHUNKS * CHUNK
ROUTE_ROWS = 8
ROUTER_LANES = 128
UNIT = SUBLANES
MOE_BLOCK = 512
BLOCK_UNITS = MOE_BLOCK // UNIT
BLK_ROWS = 5
LOCAL_ROWS = 2 * TILE_ROWS + N_EXPERTS * UNIT
LOCAL_UNITS = LOCAL_ROWS // UNIT
VMEM_LIMIT = 56 * 1024 * 1024

C_RQ, C_RK, C_RV, C_RG = 0, 256, 512, 1024
C_GQ, C_GK, C_GV, C_GG = 1536, 1792, 2048, 2560
W_MIX = 3072
C_Z = W_MIX + GLA_RANK

F32 = jnp.float32
BF16 = jnp.bfloat16
LOG_G = tuple(math.log1p(-(2.0 ** (-5.0 - h))) for h in range(N_HEADS))


def _dot(a, b):
    return jnp.dot(a, b, preferred_element_type=F32)


def _dot_nt(a, b):
    return lax.dot_general(a, b, (((1,), (1,)), ((), ())), preferred_element_type=F32)


def _dot_tn(a, b):
    return lax.dot_general(a, b, (((0,), (0,)), ((), ())), preferred_element_type=F32)


def _sigmoid(x):
    return 1.0 / (1.0 + jnp.exp(-x))


def _pack_halves(x):
    half = x.shape[1] // 2
    bits = pltpu.bitcast(x, jnp.int32)
    return lax.shift_right_logical(bits[:, :half], 16) | (bits[:, half:] & jnp.int32(-65536))


def _unpack_halves(w):
    lo = pltpu.bitcast(lax.shift_left(w, 16), F32)
    hi = pltpu.bitcast(w & jnp.int32(-65536), F32)
    return jnp.concatenate([lo, hi], axis=1).astype(BF16)


def _head_of_lane(shape, width):
    return lax.broadcasted_iota(jnp.int32, shape, len(shape) - 1) >> int(math.log2(width))


def _per_head_lane_const(vals, shape, width):
    hd = _head_of_lane(shape, width)
    out = jnp.full(shape, vals[N_HEADS - 1], F32)
    for h in range(N_HEADS - 2, -1, -1):
        out = jnp.where(hd == h, vals[h], out)
    return out


def _tables_kernel(inv_ref, cp_ref, sp_ref, cm_ref, sm_ref, *, seq, past_len):
    inv = inv_ref[...].reshape(1, 1, LANES)
    lane = lax.broadcasted_iota(jnp.int32, (1, 1, LANES), 2)
    sign = jnp.where((lane & (HEAD_K - 1)) < (HEAD_K // 2), -1.0, 1.0)
    off = lax.broadcasted_iota(jnp.int32, (1, CHUNK, LANES), 1).astype(F32) * inv
    c_off, s_off = jnp.cos(off), jnp.sin(off)

    def chunks(n, first_pos):
        base = (lax.broadcasted_iota(jnp.int32, (n, 1, LANES), 0) * CHUNK + first_pos).astype(F32) * inv
        c_base, s_base = jnp.cos(base), jnp.sin(base)
        cos = (c_base * c_off - s_base * s_off).reshape(n * CHUNK, LANES)
        sin = ((s_base * c_off + c_base * s_off) * sign).reshape(n * CHUNK, LANES)
        return cos, sin

    cp_ref[0:seq, :], sp_ref[0:seq, :] = chunks(seq // CHUNK, 0)
    cos_s, sin_s = chunks(1, past_len)
    for c in range(TILE_CHUNKS):
        cp_ref[seq + c * CHUNK:seq + (c + 1) * CHUNK, :] = cos_s
        sp_ref[seq + c * CHUNK:seq + (c + 1) * CHUNK, :] = sin_s
    cm_ref[...], sm_ref[...] = chunks(1, -CHUNK)


def _rotary_tables(seq, past_len):
    half = HEAD_K // 2
    inv = ROPE_BASE ** (-2.0 * jnp.arange(half, dtype=F32) / HEAD_K)
    inv = jnp.tile(inv, LANES // half).reshape(1, LANES)
    shp = lambda r: jax.ShapeDtypeStruct((r, LANES), F32)
    return pl.pallas_call(
        functools.partial(_tables_kernel, seq=seq, past_len=past_len),
        out_shape=(shp(seq + TILE_ROWS), shp(seq + TILE_ROWS), shp(CHUNK), shp(CHUNK)),
        compiler_params=pltpu.CompilerParams(vmem_limit_bytes=VMEM_LIMIT),
        name="rotary_tables",
    )(inv)


def _mixer_kernel(*refs, n_chunks, tiles_p, tiles_per_row, route):
    tm = n_chunks * CHUNK
    if tiles_p:
        xp_ref, refs = refs[0], refs[1:]
    (xs_ref, cos_ref, sin_ref, spi_r_ref, spi_g_ref, ssi_r_ref, ssi_g_ref,
     gmix_ref, wint_ref, wgk_ref, bgk_ref, gnorm_ref, wbr_ref, wbg_ref, wout_ref,
     gffn_ref, wrt_ref, brt_ref, bdtri_ref, ltri_ref,
     xnew_ref, xloc_ref, route_ref, ku_ref, spo_r_ref, spo_g_ref, sso_r_ref, sso_g_ref,
     qb, qdb, kb, kkb, vb, gqb, gkkb, gvb, ga, o_ret, o_gla, st_ret, st_gla,
     sin_r, sin_g, sout_r, sout_g) = refs

    i = pl.program_id(0)
    if tiles_p:
        is_s = i >= tiles_p
        t_idx = jnp.minimum(i, tiles_p - 1) % tiles_per_row
        x = jnp.where(is_s, xs_ref[...].reshape(tm, D_MODEL), xp_ref[...].reshape(tm, D_MODEL))

        @pl.when(jnp.logical_and(jnp.logical_not(is_s), t_idx == 0))
        def _():
            st_ret[...] = spi_r_ref[0].T
            st_gla[...] = spi_g_ref[0].T

        @pl.when(i == 0)
        def _():
            sin_r[...] = jnp.zeros(sin_r.shape, F32)
            sin_g[...] = jnp.zeros(sin_g.shape, F32)

        @pl.when(is_s)
        def _():
            for c in range(n_chunks):
                sin_r[c] = ssi_r_ref[c].T
                sin_g[c] = ssi_g_ref[c].T
    else:
        is_s = None
        x = xs_ref[...].reshape(tm, D_MODEL)

    h = x * lax.rsqrt(jnp.mean(x * x, axis=-1, keepdims=True) + NORM_EPS) * gmix_ref[...]
    hb = h.astype(BF16)

    def proj(c0, width):
        return _dot_nt(hb, wint_ref[c0:c0 + width, :])

    cos3 = jnp.concatenate([cos_ref[...]] * 2, axis=1).reshape(n_chunks, CHUNK, HK)
    sin3 = jnp.concatenate([sin_ref[...]] * 2, axis=1).reshape(n_chunks, CHUNK, HK)
    lane_hk = lax.broadcasted_iota(jnp.int32, (tm, HK), 1)
    first_half = (lane_hk & (HEAD_K - 1)) < (HEAD_K // 2)

    def rotary(t):
        swapped = jnp.where(first_half, pltpu.roll(t, HK - HEAD_K // 2, 1), pltpu.roll(t, HEAD_K // 2, 1))
        t3 = t.reshape(n_chunks, CHUNK, HK)
        return (t3 * cos3 + swapped.reshape(n_chunks, CHUNK, HK) * sin3).reshape(tm, HK)

    logg_hk = _per_head_lane_const(LOG_G, (CHUNK, HK), HEAD_K)
    l_idx = lax.broadcasted_iota(jnp.int32, (CHUNK, HK), 0).astype(F32)
    qdec = jnp.exp((l_idx + 1.0) * logg_hk)
    kdec = jnp.exp((CHUNK - 1.0 - l_idx) * logg_hk)
    cdec = jnp.exp(float(CHUNK) * _per_head_lane_const(LOG_G, (1, HK), HEAD_K))
    r_idx = lax.broadcasted_iota(jnp.int32, (N_HEADS * CHUNK, CHUNK), 0)
    m_idx = lax.broadcasted_iota(jnp.int32, (N_HEADS * CHUNK, CHUNK), 1)
    logg_rows = jnp.full((N_HEADS * CHUNK, CHUNK), LOG_G[N_HEADS - 1], F32)
    for hh in range(N_HEADS - 2, -1, -1):
        logg_rows = jnp.where((r_idx >> int(math.log2(CHUNK))) == hh, LOG_G[hh], logg_rows)
    dmat = jnp.exp(jnp.abs((r_idx & (CHUNK - 1)) - m_idx).astype(F32) * logg_rows)

    glr = jnp.where(lax.broadcasted_iota(jnp.int32, (tm, LANES), 1) < GLA_RANK, proj(W_MIX, LANES), 0.0)
    rq = rotary(proj(C_RQ, HK))
    qb[...] = rq.astype(BF16)
    qdb[...] = (rq.reshape(n_chunks, CHUNK, HK) * qdec).reshape(tm, HK).astype(BF16)
    gl = _dot(glr.astype(BF16), wgk_ref[...]) + bgk_ref[...]
    rk = rotary(proj(C_RK, HK)) * (HEAD_K ** -0.5)
    kb[...] = rk.astype(BF16)
    kkb[...] = (rk.reshape(n_chunks, CHUNK, HK) * kdec).reshape(tm, HK).astype(BF16)
    log_a = (jnp.minimum(gl, 0.0) - jnp.log1p(jnp.exp(-jnp.abs(gl)))) / GATE_NORM
    la_hi = log_a.astype(BF16)
    la_lo = (log_a - la_hi.astype(F32)).astype(BF16)
    vb[...] = proj(C_RV, HV).astype(BF16)
    bdtri = bdtri_ref[...]
    bcum = _dot(bdtri, la_hi) + _dot(bdtri, la_lo)
    gqb[...] = (proj(C_GQ, HK) * (HEAD_K ** -0.5)).astype(BF16)
    gvb[...] = proj(C_GV, HV).astype(BF16)
    gk = proj(C_GK, HK)
    b3 = bcum.reshape(n_chunks, CHUNK, HK)
    bl3 = b3[:, CHUNK - 1:CHUNK, :]
    gkkb[...] = (gk.reshape(n_chunks, CHUNK, HK) * jnp.exp(bl3 - b3)).reshape(tm, HK).astype(BF16)
    ga[...] = jnp.broadcast_to(jnp.exp(bl3), (n_chunks, SUBLANES, HK))

    def stack_masked(a, width):
        head = _head_of_lane(a.shape, width)
        zero = jnp.zeros_like(a)
        return jnp.concatenate([jnp.where(head == hh, a, zero) for hh in range(N_HEADS)], axis=0)

    def heads_to_rows(a):
        return jnp.concatenate([a[:, hh * HEAD_V:(hh + 1) * HEAD_V] for hh in range(N_HEADS)], axis=0)

    def rows_to_heads(a):
        return jnp.concatenate([a[hh * CHUNK:(hh + 1) * CHUNK, :] for hh in range(N_HEADS)], axis=1)

    chunk_rows = [slice(c * CHUNK, (c + 1) * CHUNK) for c in range(n_chunks)]
    probs = [(_dot_nt(stack_masked(qb[r, :], HEAD_K), kb[r, :]) * dmat).astype(BF16) for r in chunk_rows]
    inc_ret = [_dot_tn(heads_to_rows(vb[r, :]), stack_masked(kkb[r, :], HEAD_K)) for r in chunk_rows]
    inc_gla = [_dot_tn(heads_to_rows(gvb[r, :]), stack_masked(gkkb[r, :], HEAD_K)) for r in chunk_rows]
    rg = proj(C_RG, HV)
    gg = proj(C_GG, HV)
    s_in, g_out = [], []
    s_cur, g_cur = st_ret[...], st_gla[...]
    for c in range(n_chunks):
        if is_s is None:
            s_cur, g_cur = ssi_r_ref[c].T, ssi_g_ref[c].T
        else:
            s_cur = jnp.where(is_s, sin_r[c], s_cur)
            g_cur = jnp.where(is_s, sin_g[c], g_cur)
        s_in.append(s_cur.astype(BF16))
        s_cur = s_cur * cdec + inc_ret[c]
        g_cur = g_cur * ga[c][0:1, :] + inc_gla[c]
        g_out.append(g_cur.astype(BF16))
        sout_r[c] = s_cur
        sout_g[c] = g_cur
    st_ret[...] = s_cur
    st_gla[...] = g_cur
    for c, r in enumerate(chunk_rows):
        v = vb[r, :]
        intra = jnp.concatenate(
            [_dot(probs[c][hh * CHUNK:(hh + 1) * CHUNK, :], v[:, hh * HEAD_V:(hh + 1) * HEAD_V])
             for hh in range(N_HEADS)], axis=1)
        inter = rows_to_heads(_dot_nt(stack_masked(qdb[r, :], HEAD_K), s_in[c]))
        o_ret[r, :] = intra + inter
        o_gla[r, :] = rows_to_heads(_dot_nt(stack_masked(gqb[r, :], HEAD_K), g_out[c]))

    gnorm = gnorm_ref[...]
    orr = o_ret[...]
    ogg = o_gla[...]
    ret_parts, gla_parts = [], []
    for hh in range(N_HEADS):
        sl = slice(hh * HEAD_V, (hh + 1) * HEAD_V)
        oh = orr[:, sl]
        mu = jnp.mean(oh, axis=-1, keepdims=True)
        dev = oh - mu
        var = jnp.mean(dev * dev, axis=-1, keepdims=True)
        ret_parts.append(dev * lax.rsqrt(var + NORM_EPS))
        og = ogg[:, sl]
        gla_parts.append(og * lax.rsqrt(jnp.mean(og * og, axis=-1, keepdims=True) + NORM_EPS) * gnorm)
    o_r = jnp.concatenate(ret_parts, axis=1) * (rg * _sigmoid(rg))
    o_g = jnp.concatenate(gla_parts, axis=1) * (gg * _sigmoid(gg))
    merged = (_sigmoid(proj(C_Z, D_MODEL)) * _dot(o_r.astype(BF16), wbr_ref[...])
              + _sigmoid(proj(C_Z + D_MODEL, D_MODEL)) * _dot(o_g.astype(BF16), wbg_ref[...]))
    xn = x + _dot(merged.astype(BF16), wout_ref[...])
    xnew_ref[...] = xn

    if route:
        h2 = xn * lax.rsqrt(jnp.mean(xn * xn, axis=-1, keepdims=True) + NORM_EPS) * gffn_ref[...]
        h2_hi = h2.astype(BF16)
        h2_lo = (h2 - h2_hi.astype(F32)).astype(BF16)
        hi_both = _dot_nt(h2_hi, wrt_ref[...])
        logits = (hi_both[:, :ROUTER_LANES] + _dot_nt(h2_lo, wrt_ref[:ROUTER_LANES, :])
                  + hi_both[:, ROUTER_LANES:]) + brt_ref[...]
        lt = logits.T
        row8 = lax.broadcasted_iota(jnp.int32, (SUBLANES, tm), 0)
        neg_inf = jnp.float32(-jnp.inf)
        glog = jnp.where(row8 < N_GROUPS, lt[0:SUBLANES, :], neg_inf)
        gmax = jnp.max(glog, axis=0, keepdims=True)
        grp = jnp.min(jnp.where(glog == gmax, row8, SUBLANES), axis=0, keepdims=True)
        p_grp = 1.0 / jnp.sum(jnp.exp(glog - gmax), axis=0, keepdims=True)
        le = jnp.zeros((SUBLANES, tm), F32)
        for g in range(N_GROUPS):
            le = jnp.where(grp == g, lt[SUBLANES * (g + 1):SUBLANES * (g + 2), :], le)
        m1 = jnp.max(le, axis=0, keepdims=True)
        i1 = jnp.min(jnp.where(le == m1, row8, SUBLANES), axis=0, keepdims=True)
        le2 = jnp.where(row8 == i1, neg_inf, le)
        m2 = jnp.max(le2, axis=0, keepdims=True)
        i2 = jnp.min(jnp.where(le2 == m2, row8, SUBLANES), axis=0, keepdims=True)
        e0 = grp * EXPERTS_PER_GROUP + i1
        e1 = grp * EXPERTS_PER_GROUP + i2
        t21 = jnp.exp(m2 - m1)
        w0 = p_grp / (1.0 + t21)
        w1 = p_grp * t21 / (1.0 + t21)

        erow = lax.broadcasted_iota(jnp.int32, (N_EXPERTS, tm), 0)
        hit0 = erow == e0
        hit1 = erow == e1
        onehot = jnp.where(jnp.logical_or(hit0, hit1), 1.0, 0.0).astype(BF16)
        cum = _dot_nt(onehot, ltri_ref[...])
        n_run = cum[:, tm - 1:tm]
        n_pad = jnp.ceil(n_run / UNIT) * UNIT
        rank0 = jnp.sum(jnp.where(hit0, cum - 1.0, 0.0), axis=0, keepdims=True)
        rank1 = jnp.sum(jnp.where(hit1, cum - 1.0, 0.0), axis=0, keepdims=True)
        start0 = jnp.sum(jnp.where(erow < e0, n_pad, 0.0), axis=0, keepdims=True)
        start1 = jnp.sum(jnp.where(erow < e1, n_pad, 0.0), axis=0, keepdims=True)
        ld0 = (start0 + rank0).astype(jnp.int32)
        ld1 = (start1 + rank1).astype(jnp.int32)
        lrow = lax.broadcasted_iota(jnp.int32, (LOCAL_ROWS, tm), 0)
        perm = jnp.where(jnp.logical_or(lrow == ld0, lrow == ld1), 1.0, 0.0).astype(BF16)
        xloc_ref[...] = _pack_halves(_dot(perm, h2_hi))
        lane_e = lax.broadcasted_iota(jnp.int32, (N_EXPERTS, LANES), 1)
        erow_l = lax.broadcasted_iota(jnp.int32, (N_EXPERTS, LANES), 0)
        units_row = jnp.sum(jnp.where(erow_l == lane_e, n_pad / UNIT, 0.0), axis=0, keepdims=True)
        ku_ref[...] = jnp.broadcast_to(units_row, (SUBLANES, LANES)).astype(jnp.int32).reshape(ku_ref.shape)
        zero_row = jnp.zeros((1, tm), jnp.int32)
        rec = jnp.concatenate([e0, e1, ld0, ld1, pltpu.bitcast(w0, jnp.int32), pltpu.bitcast(w1, jnp.int32),
                               zero_row, zero_row], axis=0)
        route_ref[...] = rec.reshape(route_ref.shape)
    else:
        xloc_ref[...] = jnp.zeros(xloc_ref.shape, jnp.int32)
        route_ref[...] = jnp.zeros(route_ref.shape, jnp.int32)
        ku_ref[...] = jnp.zeros(ku_ref.shape, jnp.int32)

    def stream_states_out():
        for c in range(n_chunks):
            sso_r_ref[c] = sout_r[c].T
            sso_g_ref[c] = sout_g[c].T

    if is_s is None:
        stream_states_out()
    else:
        pl.when(is_s)(stream_states_out)

    if tiles_p:
        @pl.when(jnp.logical_and(jnp.logical_not(is_s), t_idx == tiles_per_row - 1))
        def _():
            spo_r_ref[0] = st_ret[...].T
            spo_g_ref[0] = st_gla[...].T
    else:
        spo_r_ref[0] = st_ret[...].T
        spo_g_ref[0] = st_gla[...].T


def _const_spec(shape):
    nd = len(shape)
    return pl.BlockSpec(shape, lambda *_: (0,) * nd, pipeline_mode=pl.Buffered(1))


def _mixer_call(x_prompt, x_streams, cos, sin, st_prompt, st_streams, weights, consts, *, n_chunks, route):
    tm = n_chunks * CHUNK
    n_streams = x_streams.shape[0]
    tiles_s = n_streams // n_chunks
    if x_prompt is not None:
        n_rows, n_seq, _ = x_prompt.shape
        tpr = n_seq // tm
        tiles_p = n_rows * tpr
    else:
        n_rows, tpr, tiles_p = 1, 1, 0
    n_tiles = tiles_p + tiles_s
    p_idx = lambda i: jnp.minimum(i, tiles_p - 1)
    s_idx = lambda i: jnp.maximum(i - tiles_p, 0)

    st_blk = (1, HK, HEAD_V)
    in_specs, args = [], []
    if tiles_p:
        in_specs.append(pl.BlockSpec((1, tm, D_MODEL), lambda i: (p_idx(i) // tpr, p_idx(i) % tpr, 0)))
        args.append(x_prompt)
        cos_spec = pl.BlockSpec((tm, LANES), lambda i: (jnp.where(i < tiles_p, i % tpr, tpr), 0))
    else:
        cos_spec = pl.BlockSpec((tm, LANES), lambda i: (0, 0))
    once = pl.Buffered(1)
    stream_in = pl.BlockSpec((n_chunks, HK, HEAD_V), lambda i: (s_idx(i), 0, 0), pipeline_mode=once)
    in_specs += [pl.BlockSpec((n_chunks, CHUNK, D_MODEL), lambda i: (s_idx(i), 0, 0), pipeline_mode=once),
                 cos_spec, cos_spec, _const_spec(st_blk), _const_spec(st_blk), stream_in, stream_in]
    args += [x_streams, cos, sin, st_prompt[0], st_prompt[1], st_streams[0], st_streams[1]]
    in_specs += [_const_spec(w.shape) for w in weights] + [_const_spec(c.shape) for c in consts]
    args += list(weights) + list(consts)

    out_shape = (jax.ShapeDtypeStruct((n_tiles * tm, D_MODEL), F32),
                 jax.ShapeDtypeStruct((n_tiles * LOCAL_ROWS, D_MODEL // 2), jnp.int32),
                 jax.ShapeDtypeStruct((n_tiles, ROUTE_ROWS, tm), jnp.int32),
                 jax.ShapeDtypeStruct((n_tiles, SUBLANES, LANES), jnp.int32),
                 jax.ShapeDtypeStruct((n_rows, HK, HEAD_V), F32), jax.ShapeDtypeStruct((n_rows, HK, HEAD_V), F32),
                 jax.ShapeDtypeStruct((n_streams, HK, HEAD_V), F32),
                 jax.ShapeDtypeStruct((n_streams, HK, HEAD_V), F32))
    row_spec = pl.BlockSpec(st_blk, lambda i: (p_idx(i) // tpr if tiles_p else 0, 0, 0))
    stream_spec = pl.BlockSpec((n_chunks, HK, HEAD_V), lambda i: (s_idx(i), 0, 0))
    out_specs = (pl.BlockSpec((tm, D_MODEL), lambda i: (i, 0)),
                 pl.BlockSpec((LOCAL_ROWS, D_MODEL // 2), lambda i: (i, 0)),
                 pl.BlockSpec((1, ROUTE_ROWS, tm), lambda i: (i, 0, 0)),
                 pl.BlockSpec((1, SUBLANES, LANES), lambda i: (i, 0, 0)),
                 row_spec, row_spec, stream_spec, stream_spec)

    scratch = [pltpu.VMEM((tm, HK), BF16), pltpu.VMEM((tm, HK), BF16), pltpu.VMEM((tm, HK), BF16),
               pltpu.VMEM((tm, HK), BF16), pltpu.VMEM((tm, HV), BF16),
               pltpu.VMEM((tm, HK), BF16), pltpu.VMEM((tm, HK), BF16), pltpu.VMEM((tm, HV), BF16),
               pltpu.VMEM((n_chunks, SUBLANES, HK), F32),
               pltpu.VMEM((tm, HV), F32), pltpu.VMEM((tm, HV), F32),
               pltpu.VMEM((HEAD_V, HK), F32), pltpu.VMEM((HEAD_V, HK), F32)]
    scratch += [pltpu.VMEM((n_chunks, HEAD_V, HK), F32)] * 4

    return pl.pallas_call(
        functools.partial(_mixer_kernel, n_chunks=n_chunks, tiles_p=tiles_p, tiles_per_row=tpr, route=route),
        grid=(n_tiles,), in_specs=in_specs, out_specs=out_specs, out_shape=out_shape, scratch_shapes=scratch,
        compiler_params=pltpu.CompilerParams(dimension_semantics=("arbitrary",), vmem_limit_bytes=VMEM_LIMIT),
        name="mixer" if route else "mixer_meta",
    )(*args)


def _plan_kernel(ku_ref, src_ref, blk_ref, run_first, run_step, next_unit, seg_start, seg_units, *, n_tiles, n_blocks):
    shift = int(math.log2(BLOCK_UNITS))
    group_shift = int(math.log2(SUBLANES * LANES))

    def init_tile(t, c):
        next_unit[t] = t * LOCAL_UNITS
        return c

    lax.fori_loop(0, n_tiles, init_tile, 0)

    def init_block(b, c):
        blk_ref[0, b] = N_EXPERTS - 1
        blk_ref[1, b] = 0
        blk_ref[2, b] = 0
        blk_ref[3, b] = -1
        blk_ref[4, b] = 0
        return c

    lax.fori_loop(0, n_blocks, init_block, 0)

    def per_expert(e, carry):
        g0, position = carry

        def per_tile(t, c):
            units, d_prev = c
            base = next_unit[t]
            d = base - units
            run_first[e * n_tiles + t] = units
            run_step[e * n_tiles + t] = d - d_prev
            k = ku_ref[t, e]
            next_unit[t] = base + k
            return units + k, d

        units, _ = lax.fori_loop(0, n_tiles, per_tile, (0, 0), unroll=4)
        g_pad = g0 + (((units + (BLOCK_UNITS - 1)) >> shift) << shift)
        seg_start[e] = g0
        seg_units[e] = units

        def set_block(b, c):
            blk_ref[0, b] = e
            blk_ref[1, b] = jnp.minimum(g0 + units - (b << shift), BLOCK_UNITS)
            blk_ref[4, b] = position & 1
            return c

        lax.fori_loop(g0 >> shift, g_pad >> shift, set_block, 0)
        return g_pad, position + (units > 0).astype(jnp.int32)

    g_total, _ = lax.fori_loop(0, N_EXPERTS, per_expert, (0, 0))
    used = g_total >> shift
    blk_ref[2, 0] = used

    def set_next(i, c):
        later, following = c
        b = used - 1 - i
        e = blk_ref[0, b]
        following = jnp.where(e != later, later, following)
        blk_ref[3, b] = following
        return e, following

    lax.fori_loop(0, used, set_next, (-1, -1))

    src_ref[...] = jnp.zeros(src_ref.shape, jnp.int32)
    in_group = (lax.broadcasted_iota(jnp.int32, (SUBLANES, LANES), 0) * LANES
                + lax.broadcasted_iota(jnp.int32, (SUBLANES, LANES), 1))

    def expert_units(e, c):
        g0 = seg_start[e]
        units = seg_units[e]
        g_pad = g0 + (((units + (BLOCK_UNITS - 1)) >> shift) << shift)

        def per_group(grp, c2):
            g = in_group + (grp << group_shift)
            o = g - g0
            mine = jnp.logical_and(o >= 0, g < g_pad)
            o_eff = jnp.where(o < units, o, (o >> shift) << shift)

            def per_tile(t, acc):
                return acc + jnp.where(o_eff >= run_first[e * n_tiles + t], run_step[e * n_tiles + t], 0)

            offset = lax.fori_loop(0, n_tiles, per_tile, jnp.zeros((SUBLANES, LANES), jnp.int32), unroll=4)
            rows = pl.ds(pl.multiple_of(grp * SUBLANES, SUBLANES), SUBLANES)
            src_ref[rows, :] = jnp.where(mine, o_eff + offset, src_ref[rows, :])
            return c2

        lax.fori_loop(g0 >> group_shift, (g_pad + (SUBLANES * LANES - 1)) >> group_shift, per_group, 0)
        return c

    lax.fori_loop(0, N_EXPERTS, expert_units, 0)


def _plan_call(ku, n_blocks):
    n_tiles = ku.shape[0]
    smem = pl.BlockSpec(memory_space=pltpu.SMEM)
    group = SUBLANES * LANES
    src_rows = -(-n_blocks * BLOCK_UNITS // group) * SUBLANES
    src, blk = pl.pallas_call(
        functools.partial(_plan_kernel, n_tiles=n_tiles, n_blocks=n_blocks),
        in_specs=[smem], out_specs=(pl.BlockSpec(memory_space=pltpu.VMEM), smem),
        out_shape=(jax.ShapeDtypeStruct((src_rows, LANES), jnp.int32),
                   jax.ShapeDtypeStruct((BLK_ROWS, n_blocks), jnp.int32)),
        scratch_shapes=[pltpu.SMEM((N_EXPERTS * n_tiles,), jnp.int32), pltpu.SMEM((N_EXPERTS * n_tiles,), jnp.int32),
                        pltpu.SMEM((n_tiles,), jnp.int32), pltpu.SMEM((N_EXPERTS,), jnp.int32),
                        pltpu.SMEM((N_EXPERTS,), jnp.int32)],
        name="moe_plan",
    )(ku)
    return src.reshape(-1), blk


def _experts_kernel(src_ref, blk_ref, x_hbm, wg_hbm, wu_hbm, wd_hbm, y_hbm,
                    xin, yout, wgf, wuf, wdf, wgub, wdb, sem_in, sem_out, sem_w):
    b = pl.program_id(0)
    used = blk_ref[2, 0]
    slot = b & 1

    def weight_copies(e, s):
        return [pltpu.make_async_copy(hbm.at[e], buf.at[s], sem_w.at[s])
                for hbm, buf in ((wg_hbm, wgf), (wu_hbm, wuf), (wd_hbm, wdf))]

    def unit_rows(blk, u):
        return pl.ds(pl.multiple_of(src_ref[blk * BLOCK_UNITS + u] * UNIT, UNIT), UNIT)

    def in_copy(blk, u, s):
        return pltpu.make_async_copy(x_hbm.at[unit_rows(blk, u), :], xin.at[s, pl.ds(u * UNIT, UNIT), :],
                                     sem_in.at[s])

    def out_copy(blk, u, s):
        return pltpu.make_async_copy(yout.at[s, pl.ds(u * UNIT, UNIT), :], y_hbm.at[unit_rows(blk, u), :],
                                     sem_out.at[s])

    @pl.when(jnp.logical_and(b == 0, used > 0))
    def _():
        for copy in weight_copies(blk_ref[0, 0], blk_ref[4, 0]):
            copy.start()
        for u in range(BLOCK_UNITS):
            in_copy(0, u, 0).start()

    @pl.when(b < used)
    def _():
        expert = blk_ref[0, b]
        changed = jnp.logical_or(b == 0, expert != blk_ref[0, jnp.maximum(b - 1, 0)])

        @pl.when(changed)
        def _():
            wslot = blk_ref[4, b]
            following = blk_ref[3, b]
            for copy in weight_copies(expert, wslot):
                copy.wait()

            @pl.when(following >= 0)
            def _():
                for copy in weight_copies(following, 1 - wslot):
                    copy.start()

            wgub[:, :D_EXPERT] = wgf[wslot].astype(BF16)
            wgub[:, D_EXPERT:] = wuf[wslot].astype(BF16)
            wdb[...] = wdf[wslot].astype(BF16)

        @pl.when(b + 1 < used)
        def _():
            for u in range(BLOCK_UNITS):
                in_copy(b + 1, u, 1 - slot).start()

        for u in range(BLOCK_UNITS):
            in_copy(b, u, slot).wait()

        n_real = blk_ref[1, b]

        def mlp(rows):
            xb = _unpack_halves(xin[slot, :rows, :])
            gate_up = _dot(xb, wgub[...])
            gate = gate_up[:, :D_EXPERT]
            hid = (gate * _sigmoid(gate)) * gate_up[:, D_EXPERT:]
            yout[slot, :rows, :] = _pack_halves(_dot(hid.astype(BF16), wdb[...]).astype(BF16).astype(F32))

        @pl.when(n_real > BLOCK_UNITS // 2)
        def _():
            mlp(MOE_BLOCK)

        @pl.when(n_real <= BLOCK_UNITS // 2)
        def _():
            mlp(MOE_BLOCK // 2)


        @pl.when(n_real == BLOCK_UNITS)
        def _():
            for u in range(BLOCK_UNITS):
                out_copy(b, u, slot).start()

        @pl.when(n_real < BLOCK_UNITS)
        def _():
            lax.fori_loop(0, n_real, lambda u, c: (out_copy(b, u, slot).start(), c)[1], 0)

        @pl.when(b > 0)
        def _():
            lax.fori_loop(0, blk_ref[1, b - 1], lambda u, c: (out_copy(b - 1, u, 1 - slot).wait(), c)[1], 0)

        @pl.when(b == used - 1)
        def _():
            lax.fori_loop(0, n_real, lambda u, c: (out_copy(b, u, slot).wait(), c)[1], 0)


def _experts_call(src, blk, xloc, wg, wu, wd, n_blocks):
    hbm = pl.BlockSpec(memory_space=pl.ANY)
    grid_spec = pltpu.PrefetchScalarGridSpec(
        num_scalar_prefetch=2,
        grid=(n_blocks,),
        in_specs=[hbm, hbm, hbm, hbm],
        out_specs=hbm,
        scratch_shapes=[pltpu.VMEM((2, MOE_BLOCK, D_MODEL // 2), jnp.int32),
                        pltpu.VMEM((2, MOE_BLOCK, D_MODEL // 2), jnp.int32),
                        pltpu.VMEM((2, D_MODEL, D_EXPERT), F32), pltpu.VMEM((2, D_MODEL, D_EXPERT), F32),
                        pltpu.VMEM((2, D_EXPERT, D_MODEL), F32),
                        pltpu.VMEM((D_MODEL, 2 * D_EXPERT), BF16), pltpu.VMEM((D_EXPERT, D_MODEL), BF16),
                        pltpu.SemaphoreType.DMA((2,)), pltpu.SemaphoreType.DMA((2,)),
                        pltpu.SemaphoreType.DMA((2,))],
    )
    return pl.pallas_call(
        _experts_kernel, grid_spec=grid_spec,
        out_shape=jax.ShapeDtypeStruct(xloc.shape, jnp.int32),
        input_output_aliases={2: 0},
        compiler_params=pltpu.CompilerParams(dimension_semantics=("arbitrary",), vmem_limit_bytes=VMEM_LIMIT),
        name="moe_experts",
    )(src, blk, xloc, wg, wu, wd)


def _combine_kernel(route_ref, x_ref, y_ref, gfin_ref, out_a_ref, out_b_ref, *, tm, tiles_a):
    t = pl.program_id(0)
    rec = route_ref[0]
    recf = jnp.concatenate([rec[2:4, :].astype(F32), pltpu.bitcast(rec, F32)[4:6, :],
                            jnp.zeros((LANES - 4, tm), F32)], axis=0)
    cols = recf.T
    lrow = lax.broadcasted_iota(jnp.int32, (tm, LOCAL_ROWS), 1).astype(F32)
    select = jnp.where(lrow == cols[:, 0:1], cols[:, 2:3], jnp.where(lrow == cols[:, 1:2], cols[:, 3:4], 0.0))
    xf = x_ref[...] + _dot(select.astype(BF16), _unpack_halves(y_ref[...]))
    out = xf * lax.rsqrt(jnp.mean(xf * xf, axis=-1, keepdims=True) + NORM_EPS) * gfin_ref[...]

    @pl.when(t < tiles_a)
    def _():
        out_a_ref[...] = out

    @pl.when(t >= tiles_a)
    def _():
        out_b_ref[...] = out


def _combine_call(route, xnew, yloc, gfin, tm, tiles_a):
    n_tiles = route.shape[0]
    tiles_b = n_tiles - tiles_a
    return pl.pallas_call(
        functools.partial(_combine_kernel, tm=tm, tiles_a=tiles_a),
        grid=(n_tiles,),
        in_specs=[pl.BlockSpec((1, ROUTE_ROWS, tm), lambda t: (t, 0, 0)),
                  pl.BlockSpec((tm, D_MODEL), lambda t: (t, 0)),
                  pl.BlockSpec((LOCAL_ROWS, D_MODEL // 2), lambda t: (t, 0)),
                  pl.BlockSpec((1, D_MODEL), lambda t: (0, 0))],
        out_specs=(pl.BlockSpec((tm, D_MODEL), lambda t: (jnp.minimum(t, tiles_a - 1), 0)),
                   pl.BlockSpec((tm, D_MODEL), lambda t: (jnp.maximum(t - tiles_a, 0), 0))),
        out_shape=(jax.ShapeDtypeStruct((tiles_a * tm, D_MODEL), F32),
                   jax.ShapeDtypeStruct((tiles_b * tm, D_MODEL), F32)),
        compiler_params=pltpu.CompilerParams(dimension_semantics=("arbitrary",), vmem_limit_bytes=VMEM_LIMIT),
        name="moe_combine",
    )(route, xnew, yloc, gfin)


def _prep_weights(g_mix, w_in, w_gla_gk2, b_gla_gk, g_gla_norm, w_br_ret, w_br_gla, w_out, g_ffn,
                  w_rg, b_rg, w_re, b_re):
    w_int = jnp.swapaxes(w_in, 0, 1).astype(BF16)
    wgk = jnp.pad(w_gla_gk2, ((0, LANES - GLA_RANK), (0, 0))).astype(BF16)
    def router_rows(groups, experts):
        return jnp.concatenate([groups, jnp.zeros((SUBLANES - N_GROUPS,) + groups.shape[1:], F32), experts,
                                jnp.zeros((ROUTER_LANES - SUBLANES - N_EXPERTS,) + experts.shape[1:], F32)], axis=0)

    wrt = router_rows(w_rg.T, w_re.T)
    brt = router_rows(b_rg, b_re).reshape(1, ROUTER_LANES)
    wrt_hi = wrt.astype(BF16)
    wrt = jnp.concatenate([wrt_hi, (wrt - wrt_hi.astype(F32)).astype(BF16)], axis=0)
    return (g_mix.reshape(1, D_MODEL), w_int, wgk, b_gla_gk.reshape(1, HK), g_gla_norm.reshape(1, HEAD_V),
            w_br_ret.astype(BF16), w_br_gla.astype(BF16), w_out.astype(BF16), g_ffn.reshape(1, D_MODEL), wrt, brt)


def _tri_consts(tm):
    r = np.arange(tm)
    low = (r[None, :] <= r[:, None])
    bd = low & ((r[None, :] // CHUNK) == (r[:, None] // CHUNK))
    return jnp.asarray(bd, BF16), jnp.asarray(low, BF16)


def _state_to_kernel(s):
    return s.reshape(s.shape[0], HK, HEAD_V)


def _state_from_kernel(s):
    return s.reshape(1, s.shape[0], N_HEADS, HEAD_K, HEAD_V)


def kernel(x_prompt, x_sample, state_ret, state_gla, meta_tokens, g_mix, w_in, w_gla_gk2, b_gla_gk, g_gla_norm, w_br_ret, w_br_gla, w_out, g_ffn, w_router_group, b_router_group, w_router_expert, b_router_expert, w_exp_gate, w_exp_up, w_exp_down, g_final):
    n_b, seq, _ = x_prompt.shape
    n_s, dec_seq, _ = x_sample.shape
    depth = state_ret.shape[0]
    assert depth == 1 and dec_seq == CHUNK and seq % TILE_ROWS == 0 and n_s % TILE_CHUNKS == 0
    cos_t, sin_t, cos_m, sin_m = _rotary_tables(seq, PAST_LEN)
    weights = _prep_weights(g_mix[0], w_in[0], w_gla_gk2[0], b_gla_gk[0], g_gla_norm[0], w_br_ret[0], w_br_gla[0],
                            w_out[0], g_ffn[0], w_router_group[0], b_router_group[0], w_router_expert[0],
                            b_router_expert[0])
    zero_state = jnp.zeros((1, HK, HEAD_V), F32)

    x_meta = jnp.concatenate([jnp.zeros((CHUNK - N_META, D_MODEL), F32), meta_tokens.astype(F32)], axis=0)
    meta_out = _mixer_call(None, x_meta.reshape(1, CHUNK, D_MODEL), cos_m, sin_m, (zero_state, zero_state),
                           (zero_state, zero_state), weights, _tri_consts(CHUNK), n_chunks=1, route=False)
    meta_ret, meta_gla = meta_out[6], meta_out[7]

    xnew, xloc, route, ku, ret_p, gla_p, ret_s, gla_s = _mixer_call(
        x_prompt, x_sample, cos_t, sin_t, (meta_ret, meta_gla),
        (_state_to_kernel(state_ret[0]), _state_to_kernel(state_gla[0])), weights, _tri_consts(TILE_ROWS),
        n_chunks=TILE_CHUNKS, route=True)

    n_tiles = route.shape[0]
    n_blocks = n_tiles * LOCAL_UNITS // BLOCK_UNITS + N_EXPERTS
    src, blk = _plan_call(ku[:, 0, :N_EXPERTS], n_blocks)
    yloc = _experts_call(src, blk, xloc, w_exp_gate[0], w_exp_up[0], w_exp_down[0], n_blocks)
    y_p, y_s = _combine_call(route, xnew, yloc, g_final.reshape(1, D_MODEL), TILE_ROWS, n_b * seq // TILE_ROWS)

    return (y_p.reshape(n_b, seq, D_MODEL), y_s.reshape(n_s, dec_seq, D_MODEL),
            _state_from_kernel(ret_p), _state_from_kernel(gla_p), _state_from_kernel(ret_s), _state_from_kernel(gla_s))
```

```python
import functools
import math

import jax
import jax.numpy as jnp
import numpy as np
from jax import lax
from jax.experimental import pallas as pl
from jax.experimental.pallas import tpu as pltpu

D_MODEL = 1024
CHUNK = 64
PAST_LEN = 1024
N_META = 16
N_HEADS = 4
HEAD_K = 64
HEAD_V = 128
HK = N_HEADS * HEAD_K
HV = N_HEADS * HEAD_V
GLA_RANK = 16
GATE_NORM = 16.0
ROPE_BASE = 10000.0
N_GROUPS = 4
EXPERTS_PER_GROUP = 8
N_EXPERTS = N_GROUPS * EXPERTS_PER_GROUP
D_EXPERT = 512
NORM_EPS = 1e-6

LANES = 128
SUBLANES = 8
TILE_CHUNKS = 8
TILE_ROWS = TILE_CHUNKS * CHUNK
ROUTE_ROWS = 8
ROUTER_LANES = 128
UNIT = SUBLANES
MOE_BLOCK = 512
BLOCK_UNITS = MOE_BLOCK // UNIT
BLK_ROWS = 5
LOCAL_ROWS = 2 * TILE_ROWS + N_EXPERTS * UNIT
LOCAL_UNITS = LOCAL_ROWS // UNIT
VMEM_LIMIT = 56 * 1024 * 1024

C_RQ, C_RK, C_RV, C_RG = 0, 256, 512, 1024
C_GQ, C_GK, C_GV, C_GG = 1536, 1792, 2048, 2560
W_MIX = 3072
C_Z = W_MIX + GLA_RANK

F32 = jnp.float32
BF16 = jnp.bfloat16
LOG_G = tuple(math.log1p(-(2.0 ** (-5.0 - h))) for h in range(N_HEADS))


def _dot(a, b):
    return jnp.dot(a, b, preferred_element_type=F32)


def _dot_nt(a, b):
    return lax.dot_general(a, b, (((1,), (1,)), ((), ())), preferred_element_type=F32)


def _dot_tn(a, b):
    return lax.dot_general(a, b, (((0,), (0,)), ((), ())), preferred_element_type=F32)


def _sigmoid(x):
    return 1.0 / (1.0 + jnp.exp(-x))


def _pack_halves(x):
    half = x.shape[1] // 2
    bits = pltpu.bitcast(x, jnp.int32)
    return lax.shift_right_logical(bits[:, :half], 16) | (bits[:, half:] & jnp.int32(-65536))


def _unpack_halves(w):
    lo = pltpu.bitcast(lax.shift_left(w, 16), F32)
    hi = pltpu.bitcast(w & jnp.int32(-65536), F32)
    return jnp.concatenate([lo, hi], axis=1).astype(BF16)


def _head_of_lane(shape, width):
    return lax.broadcasted_iota(jnp.int32, shape, len(shape) - 1) >> int(math.log2(width))


def _per_head_lane_const(vals, shape, width):
    hd = _head_of_lane(shape, width)
    out = jnp.full(shape, vals[N_HEADS - 1], F32)
    for h in range(N_HEADS - 2, -1, -1):
        out = jnp.where(hd == h, vals[h], out)
    return out


def _tables_kernel(inv_ref, cp_ref, sp_ref, cm_ref, sm_ref, *, seq, past_len):
    inv = inv_ref[...].reshape(1, 1, LANES)
    lane = lax.broadcasted_iota(jnp.int32, (1, 1, LANES), 2)
    sign = jnp.where((lane & (HEAD_K - 1)) < (HEAD_K // 2), -1.0, 1.0)
    off = lax.broadcasted_iota(jnp.int32, (1, CHUNK, LANES), 1).astype(F32) * inv
    c_off, s_off = jnp.cos(off), jnp.sin(off)

    def chunks(n, first_pos):
        base = (lax.broadcasted_iota(jnp.int32, (n, 1, LANES), 0) * CHUNK + first_pos).astype(F32) * inv
        c_base, s_base = jnp.cos(base), jnp.sin(base)
        cos = (c_base * c_off - s_base * s_off).reshape(n * CHUNK, LANES)
        sin = ((s_base * c_off + c_base * s_off) * sign).reshape(n * CHUNK, LANES)
        return cos, sin

    cp_ref[0:seq, :], sp_ref[0:seq, :] = chunks(seq // CHUNK, 0)
    cos_s, sin_s = chunks(1, past_len)
    for c in range(TILE_CHUNKS):
        cp_ref[seq + c * CHUNK:seq + (c + 1) * CHUNK, :] = cos_s
        sp_ref[seq + c * CHUNK:seq + (c + 1) * CHUNK, :] = sin_s
    cm_ref[...], sm_ref[...] = chunks(1, -CHUNK)


def _rotary_tables(seq, past_len):
    half = HEAD_K // 2
    inv = ROPE_BASE ** (-2.0 * jnp.arange(half, dtype=F32) / HEAD_K)
    inv = jnp.tile(inv, LANES // half).reshape(1, LANES)
    shp = lambda r: jax.ShapeDtypeStruct((r, LANES), F32)
    return pl.pallas_call(
        functools.partial(_tables_kernel, seq=seq, past_len=past_len),
        out_shape=(shp(seq + TILE_ROWS), shp(seq + TILE_ROWS), shp(CHUNK), shp(CHUNK)),
        compiler_params=pltpu.CompilerParams(vmem_limit_bytes=VMEM_LIMIT),
        name="rotary_tables",
    )(inv)


def _mixer_kernel(*refs, n_chunks, tiles_p, tiles_per_row, route):
    tm = n_chunks * CHUNK
    if tiles_p:
        xp_ref, refs = refs[0], refs[1:]
    (xs_ref, cos_ref, sin_ref, spi_r_ref, spi_g_ref, ssi_r_ref, ssi_g_ref,
     gmix_ref, wint_ref, wgk_ref, bgk_ref, gnorm_ref, wbr_ref, wbg_ref, wout_ref,
     gffn_ref, wrt_ref, brt_ref, bdtri_ref, ltri_ref,
     xnew_ref, xloc_ref, route_ref, ku_ref, spo_r_ref, spo_g_ref, sso_r_ref, sso_g_ref,
     qb, qdb, kb, kkb, vb, gqb, gkkb, gvb, ga, o_ret, o_gla, st_ret, st_gla,
     sin_r, sin_g, sout_r, sout_g) = refs

    i = pl.program_id(0)
    if tiles_p:
        is_s = i >= tiles_p
        t_idx = jnp.minimum(i, tiles_p - 1) % tiles_per_row
        x = jnp.where(is_s, xs_ref[...].reshape(tm, D_MODEL), xp_ref[...].reshape(tm, D_MODEL))

        @pl.when(jnp.logical_and(jnp.logical_not(is_s), t_idx == 0))
        def _():
            st_ret[...] = spi_r_ref[0].T
            st_gla[...] = spi_g_ref[0].T

        @pl.when(i == 0)
        def _():
            sin_r[...] = jnp.zeros(sin_r.shape, F32)
            sin_g[...] = jnp.zeros(sin_g.shape, F32)

        @pl.when(is_s)
        def _():
            for c in range(n_chunks):
                sin_r[c] = ssi_r_ref[c].T
                sin_g[c] = ssi_g_ref[c].T
    else:
        is_s = None
        x = xs_ref[...].reshape(tm, D_MODEL)

    h = x * lax.rsqrt(jnp.mean(x * x, axis=-1, keepdims=True) + NORM_EPS) * gmix_ref[...]
    hb = h.astype(BF16)

    def proj(c0, width):
        return _dot_nt(hb, wint_ref[c0:c0 + width, :])

    cos3 = jnp.concatenate([cos_ref[...]] * 2, axis=1).reshape(n_chunks, CHUNK, HK)
    sin3 = jnp.concatenate([sin_ref[...]] * 2, axis=1).reshape(n_chunks, CHUNK, HK)
    lane_hk = lax.broadcasted_iota(jnp.int32, (tm, HK), 1)
    first_half = (lane_hk & (HEAD_K - 1)) < (HEAD_K // 2)

    def rotary(t):
        swapped = jnp.where(first_half, pltpu.roll(t, HK - HEAD_K // 2, 1), pltpu.roll(t, HEAD_K // 2, 1))
        t3 = t.reshape(n_chunks, CHUNK, HK)
        return (t3 * cos3 + swapped.reshape(n_chunks, CHUNK, HK) * sin3).reshape(tm, HK)

    logg_hk = _per_head_lane_const(LOG_G, (CHUNK, HK), HEAD_K)
    l_idx = lax.broadcasted_iota(jnp.int32, (CHUNK, HK), 0).astype(F32)
    qdec = jnp.exp((l_idx + 1.0) * logg_hk)
    kdec = jnp.exp((CHUNK - 1.0 - l_idx) * logg_hk)
    cdec = jnp.exp(float(CHUNK) * _per_head_lane_const(LOG_G, (1, HK), HEAD_K))
    r_idx = lax.broadcasted_iota(jnp.int32, (N_HEADS * CHUNK, CHUNK), 0)
    m_idx = lax.broadcasted_iota(jnp.int32, (N_HEADS * CHUNK, CHUNK), 1)
    logg_rows = jnp.full((N_HEADS * CHUNK, CHUNK), LOG_G[N_HEADS - 1], F32)
    for hh in range(N_HEADS - 2, -1, -1):
        logg_rows = jnp.where((r_idx >> int(math.log2(CHUNK))) == hh, LOG_G[hh], logg_rows)
    dmat = jnp.exp(jnp.abs((r_idx & (CHUNK - 1)) - m_idx).astype(F32) * logg_rows)

    glr = jnp.where(lax.broadcasted_iota(jnp.int32, (tm, LANES), 1) < GLA_RANK, proj(W_MIX, LANES), 0.0)
    rq = rotary(proj(C_RQ, HK))
    qb[...] = rq.astype(BF16)
    qdb[...] = (rq.reshape(n_chunks, CHUNK, HK) * qdec).reshape(tm, HK).astype(BF16)
    gl = _dot(glr.astype(BF16), wgk_ref[...]) + bgk_ref[...]
    rk = rotary(proj(C_RK, HK)) * (HEAD_K ** -0.5)
    kb[...] = rk.astype(BF16)
    kkb[...] = (rk.reshape(n_chunks, CHUNK, HK) * kdec).reshape(tm, HK).astype(BF16)
    log_a = (jnp.minimum(gl, 0.0) - jnp.log1p(jnp.exp(-jnp.abs(gl)))) / GATE_NORM
    la_hi = log_a.astype(BF16)
    la_lo = (log_a - la_hi.astype(F32)).astype(BF16)
    vb[...] = proj(C_RV, HV).astype(BF16)
    bdtri = bdtri_ref[...]
    bcum = _dot(bdtri, la_hi) + _dot(bdtri, la_lo)
    gqb[...] = (proj(C_GQ, HK) * (HEAD_K ** -0.5)).astype(BF16)
    gvb[...] = proj(C_GV, HV).astype(BF16)
    gk = proj(C_GK, HK)
    b3 = bcum.reshape(n_chunks, CHUNK, HK)
    bl3 = b3[:, CHUNK - 1:CHUNK, :]
    gkkb[...] = (gk.reshape(n_chunks, CHUNK, HK) * jnp.exp(bl3 - b3)).reshape(tm, HK).astype(BF16)
    ga[...] = jnp.broadcast_to(jnp.exp(bl3), (n_chunks, SUBLANES, HK))

    def stack_masked(a, width):
        head = _head_of_lane(a.shape, width)
        zero = jnp.zeros_like(a)
        return jnp.concatenate([jnp.where(head == hh, a, zero) for hh in range(N_HEADS)], axis=0)

    def heads_to_rows(a):
        return jnp.concatenate([a[:, hh * HEAD_V:(hh + 1) * HEAD_V] for hh in range(N_HEADS)], axis=0)

    def rows_to_heads(a):
        return jnp.concatenate([a[hh * CHUNK:(hh + 1) * CHUNK, :] for hh in range(N_HEADS)], axis=1)

    chunk_rows = [slice(c * CHUNK, (c + 1) * CHUNK) for c in range(n_chunks)]
    probs = [(_dot_nt(stack_masked(qb[r, :], HEAD_K), kb[r, :]) * dmat).astype(BF16) for r in chunk_rows]
    inc_ret = [_dot_tn(heads_to_rows(vb[r, :]), stack_masked(kkb[r, :], HEAD_K)) for r in chunk_rows]
    inc_gla = [_dot_tn(heads_to_rows(gvb[r, :]), stack_masked(gkkb[r, :], HEAD_K)) for r in chunk_rows]
    rg = proj(C_RG, HV)
    gg = proj(C_GG, HV)
    s_in, g_out = [], []
    s_cur, g_cur = st_ret[...], st_gla[...]
    for c in range(n_chunks):
        if is_s is None:
            s_cur, g_cur = ssi_r_ref[c].T, ssi_g_ref[c].T
        else:
            s_cur = jnp.where(is_s, sin_r[c], s_cur)
            g_cur = jnp.where(is_s, sin_g[c], g_cur)
        s_in.append(s_cur.astype(BF16))
        s_cur = s_cur * cdec + inc_ret[c]
        g_cur = g_cur * ga[c][0:1, :] + inc_gla[c]
        g_out.append(g_cur.astype(BF16))
        sout_r[c] = s_cur
        sout_g[c] = g_cur
    st_ret[...] = s_cur
    st_gla[...] = g_cur
    for c, r in enumerate(chunk_rows):
        v = vb[r, :]
        intra = jnp.concatenate(
            [_dot(probs[c][hh * CHUNK:(hh + 1) * CHUNK, :], v[:, hh * HEAD_V:(hh + 1) * HEAD_V])
             for hh in range(N_HEADS)], axis=1)
        inter = rows_to_heads(_dot_nt(stack_masked(qdb[r, :], HEAD_K), s_in[c]))
        o_ret[r, :] = intra + inter
        o_gla[r, :] = rows_to_heads(_dot_nt(stack_masked(gqb[r, :], HEAD_K), g_out[c]))

    gnorm = gnorm_ref[...]
    orr = o_ret[...]
    ogg = o_gla[...]
    ret_parts, gla_parts = [], []
    for hh in range(N_HEADS):
        sl = slice(hh * HEAD_V, (hh + 1) * HEAD_V)
        oh = orr[:, sl]
        mu = jnp.mean(oh, axis=-1, keepdims=True)
        dev = oh - mu
        var = jnp.mean(dev * dev, axis=-1, keepdims=True)
        ret_parts.append(dev * lax.rsqrt(var + NORM_EPS))
        og = ogg[:, sl]
        gla_parts.append(og * lax.rsqrt(jnp.mean(og * og, axis=-1, keepdims=True) + NORM_EPS) * gnorm)
    o_r = jnp.concatenate(ret_parts, axis=1) * (rg * _sigmoid(rg))
    o_g = jnp.concatenate(gla_parts, axis=1) * (gg * _sigmoid(gg))
    merged = (_sigmoid(proj(C_Z, D_MODEL)) * _dot(o_r.astype(BF16), wbr_ref[...])
              + _sigmoid(proj(C_Z + D_MODEL, D_MODEL)) * _dot(o_g.astype(BF16), wbg_ref[...]))
    xn = x + _dot(merged.astype(BF16), wout_ref[...])
    xnew_ref[...] = xn

    if route:
        h2 = xn * lax.rsqrt(jnp.mean(xn * xn, axis=-1, keepdims=True) + NORM_EPS) * gffn_ref[...]
        h2_hi = h2.astype(BF16)
        h2_lo = (h2 - h2_hi.astype(F32)).astype(BF16)
        hi_both = _dot_nt(h2_hi, wrt_ref[...])
        logits = (hi_both[:, :ROUTER_LANES] + _dot_nt(h2_lo, wrt_ref[:ROUTER_LANES, :])
                  + hi_both[:, ROUTER_LANES:]) + brt_ref[...]
        lt = logits.T
        row8 = lax.broadcasted_iota(jnp.int32, (SUBLANES, tm), 0)
        neg_inf = jnp.float32(-jnp.inf)
        glog = jnp.where(row8 < N_GROUPS, lt[0:SUBLANES, :], neg_inf)
        gmax = jnp.max(glog, axis=0, keepdims=True)
        grp = jnp.min(jnp.where(glog == gmax, row8, SUBLANES), axis=0, keepdims=True)
        p_grp = 1.0 / jnp.sum(jnp.exp(glog - gmax), axis=0, keepdims=True)
        le = jnp.zeros((SUBLANES, tm), F32)
        for g in range(N_GROUPS):
            le = jnp.where(grp == g, lt[SUBLANES * (g + 1):SUBLANES * (g + 2), :], le)
        m1 = jnp.max(le, axis=0, keepdims=True)
        i1 = jnp.min(jnp.where(le == m1, row8, SUBLANES), axis=0, keepdims=True)
        le2 = jnp.where(row8 == i1, neg_inf, le)
        m2 = jnp.max(le2, axis=0, keepdims=True)
        i2 = jnp.min(jnp.where(le2 == m2, row8, SUBLANES), axis=0, keepdims=True)
        e0 = grp * EXPERTS_PER_GROUP + i1
        e1 = grp * EXPERTS_PER_GROUP + i2
        t21 = jnp.exp(m2 - m1)
        w0 = p_grp / (1.0 + t21)
        w1 = p_grp * t21 / (1.0 + t21)

        erow = lax.broadcasted_iota(jnp.int32, (N_EXPERTS, tm), 0)
        hit0 = erow == e0
        hit1 = erow == e1
        onehot = jnp.where(jnp.logical_or(hit0, hit1), 1.0, 0.0).astype(BF16)
        cum = _dot_nt(onehot, ltri_ref[...])
        n_run = cum[:, tm - 1:tm]
        n_pad = jnp.ceil(n_run / UNIT) * UNIT
        rank0 = jnp.sum(jnp.where(hit0, cum - 1.0, 0.0), axis=0, keepdims=True)
        rank1 = jnp.sum(jnp.where(hit1, cum - 1.0, 0.0), axis=0, keepdims=True)
        start0 = jnp.sum(jnp.where(erow < e0, n_pad, 0.0), axis=0, keepdims=True)
        start1 = jnp.sum(jnp.where(erow < e1, n_pad, 0.0), axis=0, keepdims=True)
        ld0 = (start0 + rank0).astype(jnp.int32)
        ld1 = (start1 + rank1).astype(jnp.int32)
        lrow = lax.broadcasted_iota(jnp.int32, (LOCAL_ROWS, tm), 0)
        perm = jnp.where(jnp.logical_or(lrow == ld0, lrow == ld1), 1.0, 0.0).astype(BF16)
        xloc_ref[...] = _pack_halves(_dot(perm, h2_hi))
        lane_e = lax.broadcasted_iota(jnp.int32, (N_EXPERTS, LANES), 1)
        erow_l = lax.broadcasted_iota(jnp.int32, (N_EXPERTS, LANES), 0)
        units_row = jnp.sum(jnp.where(erow_l == lane_e, n_pad / UNIT, 0.0), axis=0, keepdims=True)
        ku_ref[...] = jnp.broadcast_to(units_row, (SUBLANES, LANES)).astype(jnp.int32).reshape(ku_ref.shape)
        zero_row = jnp.zeros((1, tm), jnp.int32)
        rec = jnp.concatenate([e0, e1, ld0, ld1, pltpu.bitcast(w0, jnp.int32), pltpu.bitcast(w1, jnp.int32),
                               zero_row, zero_row], axis=0)
        route_ref[...] = rec.reshape(route_ref.shape)
    else:
        xloc_ref[...] = jnp.zeros(xloc_ref.shape, jnp.int32)
        route_ref[...] = jnp.zeros(route_ref.shape, jnp.int32)
        ku_ref[...] = jnp.zeros(ku_ref.shape, jnp.int32)

    def stream_states_out():
        for c in range(n_chunks):
            sso_r_ref[c] = sout_r[c].T
            sso_g_ref[c] = sout_g[c].T

    if is_s is None:
        stream_states_out()
    else:
        pl.when(is_s)(stream_states_out)

    if tiles_p:
        @pl.when(jnp.logical_and(jnp.logical_not(is_s), t_idx == tiles_per_row - 1))
        def _():
            spo_r_ref[0] = st_ret[...].T
            spo_g_ref[0] = st_gla[...].T
    else:
        spo_r_ref[0] = st_ret[...].T
        spo_g_ref[0] = st_gla[...].T


def _const_spec(shape):
    nd = len(shape)
    return pl.BlockSpec(shape, lambda *_: (0,) * nd, pipeline_mode=pl.Buffered(1))


def _mixer_call(x_prompt, x_streams, cos, sin, st_prompt, st_streams, weights, consts, *, n_chunks, route):
    tm = n_chunks * CHUNK
    n_streams = x_streams.shape[0]
    tiles_s = n_streams // n_chunks
    if x_prompt is not None:
        n_rows, n_seq, _ = x_prompt.shape
        tpr = n_seq // tm
        tiles_p = n_rows * tpr
    else:
        n_rows, tpr, tiles_p = 1, 1, 0
    n_tiles = tiles_p + tiles_s
    p_idx = lambda i: jnp.minimum(i, tiles_p - 1)
    s_idx = lambda i: jnp.maximum(i - tiles_p, 0)

    st_blk = (1, HK, HEAD_V)
    in_specs, args = [], []
    if tiles_p:
        in_specs.append(pl.BlockSpec((1, tm, D_MODEL), lambda i: (p_idx(i) // tpr, p_idx(i) % tpr, 0)))
        args.append(x_prompt)
        cos_spec = pl.BlockSpec((tm, LANES), lambda i: (jnp.where(i < tiles_p, i % tpr, tpr), 0))
    else:
        cos_spec = pl.BlockSpec((tm, LANES), lambda i: (0, 0))
    once = pl.Buffered(1)
    stream_in = pl.BlockSpec((n_chunks, HK, HEAD_V), lambda i: (s_idx(i), 0, 0), pipeline_mode=once)
    in_specs += [pl.BlockSpec((n_chunks, CHUNK, D_MODEL), lambda i: (s_idx(i), 0, 0), pipeline_mode=once),
                 cos_spec, cos_spec, _const_spec(st_blk), _const_spec(st_blk), stream_in, stream_in]
    args += [x_streams, cos, sin, st_prompt[0], st_prompt[1], st_streams[0], st_streams[1]]
    in_specs += [_const_spec(w.shape) for w in weights] + [_const_spec(c.shape) for c in consts]
    args += list(weights) + list(consts)

    out_shape = (jax.ShapeDtypeStruct((n_tiles * tm, D_MODEL), F32),
                 jax.ShapeDtypeStruct((n_tiles * LOCAL_ROWS, D_MODEL // 2), jnp.int32),
                 jax.ShapeDtypeStruct((n_tiles, ROUTE_ROWS, tm), jnp.int32),
                 jax.ShapeDtypeStruct((n_tiles, SUBLANES, LANES), jnp.int32),
                 jax.ShapeDtypeStruct((n_rows, HK, HEAD_V), F32), jax.ShapeDtypeStruct((n_rows, HK, HEAD_V), F32),
                 jax.ShapeDtypeStruct((n_streams, HK, HEAD_V), F32),
                 jax.ShapeDtypeStruct((n_streams, HK, HEAD_V), F32))
    row_spec = pl.BlockSpec(st_blk, lambda i: (p_idx(i) // tpr if tiles_p else 0, 0, 0))
    stream_spec = pl.BlockSpec((n_chunks, HK, HEAD_V), lambda i: (s_idx(i), 0, 0))
    out_specs = (pl.BlockSpec((tm, D_MODEL), lambda i: (i, 0)),
                 pl.BlockSpec((LOCAL_ROWS, D_MODEL // 2), lambda i: (i, 0)),
                 pl.BlockSpec((1, ROUTE_ROWS, tm), lambda i: (i, 0, 0)),
                 pl.BlockSpec((1, SUBLANES, LANES), lambda i: (i, 0, 0)),
                 row_spec, row_spec, stream_spec, stream_spec)

    scratch = [pltpu.VMEM((tm, HK), BF16), pltpu.VMEM((tm, HK), BF16), pltpu.VMEM((tm, HK), BF16),
               pltpu.VMEM((tm, HK), BF16), pltpu.VMEM((tm, HV), BF16),
               pltpu.VMEM((tm, HK), BF16), pltpu.VMEM((tm, HK), BF16), pltpu.VMEM((tm, HV), BF16),
               pltpu.VMEM((n_chunks, SUBLANES, HK), F32),
               pltpu.VMEM((tm, HV), F32), pltpu.VMEM((tm, HV), F32),
               pltpu.VMEM((HEAD_V, HK), F32), pltpu.VMEM((HEAD_V, HK), F32)]
    scratch += [pltpu.VMEM((n_chunks, HEAD_V, HK), F32)] * 4

    return pl.pallas_call(
        functools.partial(_mixer_kernel, n_chunks=n_chunks, tiles_p=tiles_p, tiles_per_row=tpr, route=route),
        grid=(n_tiles,), in_specs=in_specs, out_specs=out_specs, out_shape=out_shape, scratch_shapes=scratch,
        compiler_params=pltpu.CompilerParams(dimension_semantics=("arbitrary",), vmem_limit_bytes=VMEM_LIMIT),
        name="mixer" if route else "mixer_meta",
    )(*args)


def _plan_kernel(ku_ref, src_ref, blk_ref, run_first, run_step, next_unit, seg_start, seg_units, *, n_tiles, n_blocks):
    shift = int(math.log2(BLOCK_UNITS))
    group_shift = int(math.log2(SUBLANES * LANES))

    def init_tile(t, c):
        next_unit[t] = t * LOCAL_UNITS
        return c

    lax.fori_loop(0, n_tiles, init_tile, 0)

    def init_block(b, c):
        blk_ref[0, b] = N_EXPERTS - 1
        blk_ref[1, b] = 0
        blk_ref[2, b] = 0
        blk_ref[3, b] = -1
        blk_ref[4, b] = 0
        return c

    lax.fori_loop(0, n_blocks, init_block, 0)

    def per_expert(e, carry):
        g0, position = carry

        def per_tile(t, c):
            units, d_prev = c
            base = next_unit[t]
            d = base - units
            run_first[e * n_tiles + t] = units
            run_step[e * n_tiles + t] = d - d_prev
            k = ku_ref[t, e]
            next_unit[t] = base + k
            return units + k, d

        units, _ = lax.fori_loop(0, n_tiles, per_tile, (0, 0), unroll=4)
        g_pad = g0 + (((units + (BLOCK_UNITS - 1)) >> shift) << shift)
        seg_start[e] = g0
        seg_units[e] = units

        def set_block(b, c):
            blk_ref[0, b] = e
            blk_ref[1, b] = jnp.minimum(g0 + units - (b << shift), BLOCK_UNITS)
            blk_ref[4, b] = position & 1
            return c

        lax.fori_loop(g0 >> shift, g_pad >> shift, set_block, 0)
        return g_pad, position + (units > 0).astype(jnp.int32)

    g_total, _ = lax.fori_loop(0, N_EXPERTS, per_expert, (0, 0))
    used = g_total >> shift
    blk_ref[2, 0] = used

    def set_next(i, c):
        later, following = c
        b = used - 1 - i
        e = blk_ref[0, b]
        following = jnp.where(e != later, later, following)
        blk_ref[3, b] = following
        return e, following

    lax.fori_loop(0, used, set_next, (-1, -1))

    src_ref[...] = jnp.zeros(src_ref.shape, jnp.int32)
    in_group = (lax.broadcasted_iota(jnp.int32, (SUBLANES, LANES), 0) * LANES
                + lax.broadcasted_iota(jnp.int32, (SUBLANES, LANES), 1))

    def expert_units(e, c):
        g0 = seg_start[e]
        units = seg_units[e]
        g_pad = g0 + (((units + (BLOCK_UNITS - 1)) >> shift) << shift)

        def per_group(grp, c2):
            g = in_group + (grp << group_shift)
            o = g - g0
            mine = jnp.logical_and(o >= 0, g < g_pad)
            o_eff = jnp.where(o < units, o, (o >> shift) << shift)

            def per_tile(t, acc):
                return acc + jnp.where(o_eff >= run_first[e * n_tiles + t], run_step[e * n_tiles + t], 0)

            offset = lax.fori_loop(0, n_tiles, per_tile, jnp.zeros((SUBLANES, LANES), jnp.int32), unroll=4)
            rows = pl.ds(pl.multiple_of(grp * SUBLANES, SUBLANES), SUBLANES)
            src_ref[rows, :] = jnp.where(mine, o_eff + offset, src_ref[rows, :])
            return c2

        lax.fori_loop(g0 >> group_shift, (g_pad + (SUBLANES * LANES - 1)) >> group_shift, per_group, 0)
        return c

    lax.fori_loop(0, N_EXPERTS, expert_units, 0)


def _plan_call(ku, n_blocks):
    n_tiles = ku.shape[0]
    smem = pl.BlockSpec(memory_space=pltpu.SMEM)
    group = SUBLANES * LANES
    src_rows = -(-n_blocks * BLOCK_UNITS // group) * SUBLANES
    src, blk = pl.pallas_call(
        functools.partial(_plan_kernel, n_tiles=n_tiles, n_blocks=n_blocks),
        in_specs=[smem], out_specs=(pl.BlockSpec(memory_space=pltpu.VMEM), smem),
        out_shape=(jax.ShapeDtypeStruct((src_rows, LANES), jnp.int32),
                   jax.ShapeDtypeStruct((BLK_ROWS, n_blocks), jnp.int32)),
        scratch_shapes=[pltpu.SMEM((N_EXPERTS * n_tiles,), jnp.int32), pltpu.SMEM((N_EXPERTS * n_tiles,), jnp.int32),
                        pltpu.SMEM((n_tiles,), jnp.int32), pltpu.SMEM((N_EXPERTS,), jnp.int32),
                        pltpu.SMEM((N_EXPERTS,), jnp.int32)],
        name="moe_plan",
    )(ku)
    return src.reshape(-1), blk


def _experts_kernel(src_ref, blk_ref, x_hbm, wg_hbm, wu_hbm, wd_hbm, y_hbm,
                    xin, yout, wgf, wuf, wdf, wgub, wdb, sem_in, sem_out, sem_w):
    b = pl.program_id(0)
    used = blk_ref[2, 0]
    slot = b & 1

    def weight_copies(e, s):
        return [pltpu.make_async_copy(hbm.at[e], buf.at[s], sem_w.at[s])
                for hbm, buf in ((wg_hbm, wgf), (wu_hbm, wuf), (wd_hbm, wdf))]

    def unit_rows(blk, u):
        return pl.ds(pl.multiple_of(src_ref[blk * BLOCK_UNITS + u] * UNIT, UNIT), UNIT)

    def in_copy(blk, u, s):
        return pltpu.make_async_copy(x_hbm.at[unit_rows(blk, u), :], xin.at[s, pl.ds(u * UNIT, UNIT), :],
                                     sem_in.at[s])

    def out_copy(blk, u, s):
        return pltpu.make_async_copy(yout.at[s, pl.ds(u * UNIT, UNIT), :], y_hbm.at[unit_rows(blk, u), :],
                                     sem_out.at[s])

    @pl.when(jnp.logical_and(b == 0, used > 0))
    def _():
        for copy in weight_copies(blk_ref[0, 0], blk_ref[4, 0]):
            copy.start()
        for u in range(BLOCK_UNITS):
            in_copy(0, u, 0).start()

    @pl.when(b < used)
    def _():
        expert = blk_ref[0, b]
        changed = jnp.logical_or(b == 0, expert != blk_ref[0, jnp.maximum(b - 1, 0)])

        @pl.when(changed)
        def _():
            wslot = blk_ref[4, b]
            following = blk_ref[3, b]
            for copy in weight_copies(expert, wslot):
                copy.wait()

            @pl.when(following >= 0)
            def _():
                for copy in weight_copies(following, 1 - wslot):
                    copy.start()

            wgub[:, :D_EXPERT] = wgf[wslot].astype(BF16)
            wgub[:, D_EXPERT:] = wuf[wslot].astype(BF16)
            wdb[...] = wdf[wslot].astype(BF16)

        nxt = jnp.minimum(b + 1, pl.num_programs(0) - 1)
        piece_units = BLOCK_UNITS // 8

        for u in range(BLOCK_UNITS):
            in_copy(b, u, slot).wait()

        n_real = blk_ref[1, b]

        def mlp(rows):
            def prefetch(piece):
                for u in range(piece * piece_units, (piece + 1) * piece_units):
                    in_copy(nxt, u, 1 - slot).start()

            half = D_EXPERT // 2
            xb = _unpack_halves(xin[slot, :rows, :])
            parts = []
            for q in range(4):
                parts.append(_dot(xb, wgub[:, q * half:(q + 1) * half]))
                prefetch(q)
            gate = jnp.concatenate(parts[:2], axis=1)
            hid = ((gate * _sigmoid(gate)) * jnp.concatenate(parts[2:], axis=1)).astype(BF16)
            for q in range(2):
                lo = _dot(hid, wdb[:, q * half:(q + 1) * half])
                prefetch(4 + 2 * q)
                hi = _dot(hid, wdb[:, D_EXPERT + q * half:D_EXPERT + (q + 1) * half])
                prefetch(5 + 2 * q)
                both = jnp.concatenate([lo, hi], axis=1).astype(BF16).astype(F32)
                yout[slot, :rows, q * half:(q + 1) * half] = _pack_halves(both)

        @pl.when(n_real > BLOCK_UNITS // 2)
        def _():
            mlp(MOE_BLOCK)

        @pl.when(n_real <= BLOCK_UNITS // 2)
        def _():
            mlp(MOE_BLOCK // 2)


        @pl.when(n_real == BLOCK_UNITS)
        def _():
            for u in range(BLOCK_UNITS):
                out_copy(b, u, slot).start()

        @pl.when(n_real < BLOCK_UNITS)
        def _():
            lax.fori_loop(0, n_real, lambda u, c: (out_copy(b, u, slot).start(), c)[1], 0)

        @pl.when(b > 0)
        def _():
            lax.fori_loop(0, blk_ref[1, b - 1], lambda u, c: (out_copy(b - 1, u, 1 - slot).wait(), c)[1], 0)

        @pl.when(b == used - 1)
        def _():
            lax.fori_loop(0, n_real, lambda u, c: (out_copy(b, u, slot).wait(), c)[1], 0)
            for u in range(BLOCK_UNITS):
                in_copy(nxt, u, 1 - slot).wait()


def _experts_call(src, blk, xloc, wg, wu, wd, n_blocks):
    hbm = pl.BlockSpec(memory_space=pl.ANY)
    grid_spec = pltpu.PrefetchScalarGridSpec(
        num_scalar_prefetch=2,
        grid=(n_blocks,),
        in_specs=[hbm, hbm, hbm, hbm],
        out_specs=hbm,
        scratch_shapes=[pltpu.VMEM((2, MOE_BLOCK, D_MODEL // 2), jnp.int32),
                        pltpu.VMEM((2, MOE_BLOCK, D_MODEL // 2), jnp.int32),
                        pltpu.VMEM((2, D_MODEL, D_EXPERT), F32), pltpu.VMEM((2, D_MODEL, D_EXPERT), F32),
                        pltpu.VMEM((2, D_EXPERT, D_MODEL), F32),
                        pltpu.VMEM((D_MODEL, 2 * D_EXPERT), BF16), pltpu.VMEM((D_EXPERT, D_MODEL), BF16),
                        pltpu.SemaphoreType.DMA((2,)), pltpu.SemaphoreType.DMA((2,)),
                        pltpu.SemaphoreType.DMA((2,))],
    )
    return pl.pallas_call(
        _experts_kernel, grid_spec=grid_spec,
        out_shape=jax.ShapeDtypeStruct(xloc.shape, jnp.int32),
        input_output_aliases={2: 0},
        compiler_params=pltpu.CompilerParams(dimension_semantics=("arbitrary",), vmem_limit_bytes=VMEM_LIMIT),
        name="moe_experts",
    )(src, blk, xloc, wg, wu, wd)


def _combine_kernel(route_ref, x_ref, y_ref, gfin_ref, out_a_ref, out_b_ref, *, tm, tiles_a):
    t = pl.program_id(0)
    rec = route_ref[0]
    recf = jnp.concatenate([rec[2:4, :].astype(F32), pltpu.bitcast(rec, F32)[4:6, :],
                            jnp.zeros((LANES - 4, tm), F32)], axis=0)
    cols = recf.T
    lrow = lax.broadcasted_iota(jnp.int32, (tm, LOCAL_ROWS), 1).astype(F32)
    select = jnp.where(lrow == cols[:, 0:1], cols[:, 2:3], jnp.where(lrow == cols[:, 1:2], cols[:, 3:4], 0.0))
    xf = x_ref[...] + _dot(select.astype(BF16), _unpack_halves(y_ref[...]))
    out = xf * lax.rsqrt(jnp.mean(xf * xf, axis=-1, keepdims=True) + NORM_EPS) * gfin_ref[...]

    @pl.when(t < tiles_a)
    def _():
        out_a_ref[...] = out

    @pl.when(t >= tiles_a)
    def _():
        out_b_ref[...] = out


def _combine_call(route, xnew, yloc, gfin, tm, tiles_a):
    n_tiles = route.shape[0]
    tiles_b = n_tiles - tiles_a
    return pl.pallas_call(
        functools.partial(_combine_kernel, tm=tm, tiles_a=tiles_a),
        grid=(n_tiles,),
        in_specs=[pl.BlockSpec((1, ROUTE_ROWS, tm), lambda t: (t, 0, 0)),
                  pl.BlockSpec((tm, D_MODEL), lambda t: (t, 0)),
                  pl.BlockSpec((LOCAL_ROWS, D_MODEL // 2), lambda t: (t, 0)),
                  pl.BlockSpec((1, D_MODEL), lambda t: (0, 0))],
        out_specs=(pl.BlockSpec((tm, D_MODEL), lambda t: (jnp.minimum(t, tiles_a - 1), 0)),
                   pl.BlockSpec((tm, D_MODEL), lambda t: (jnp.maximum(t - tiles_a, 0), 0))),
        out_shape=(jax.ShapeDtypeStruct((tiles_a * tm, D_MODEL), F32),
                   jax.ShapeDtypeStruct((tiles_b * tm, D_MODEL), F32)),
        compiler_params=pltpu.CompilerParams(dimension_semantics=("arbitrary",), vmem_limit_bytes=VMEM_LIMIT),
        name="moe_combine",
    )(route, xnew, yloc, gfin)


def _prep_weights(g_mix, w_in, w_gla_gk2, b_gla_gk, g_gla_norm, w_br_ret, w_br_gla, w_out, g_ffn,
                  w_rg, b_rg, w_re, b_re):
    w_int = jnp.swapaxes(w_in, 0, 1).astype(BF16)
    wgk = jnp.pad(w_gla_gk2, ((0, LANES - GLA_RANK), (0, 0))).astype(BF16)
    def router_rows(groups, experts):
        return jnp.concatenate([groups, jnp.zeros((SUBLANES - N_GROUPS,) + groups.shape[1:], F32), experts,
                                jnp.zeros((ROUTER_LANES - SUBLANES - N_EXPERTS,) + experts.shape[1:], F32)], axis=0)

    wrt = router_rows(w_rg.T, w_re.T)
    brt = router_rows(b_rg, b_re).reshape(1, ROUTER_LANES)
    wrt_hi = wrt.astype(BF16)
    wrt = jnp.concatenate([wrt_hi, (wrt - wrt_hi.astype(F32)).astype(BF16)], axis=0)
    return (g_mix.reshape(1, D_MODEL), w_int, wgk, b_gla_gk.reshape(1, HK), g_gla_norm.reshape(1, HEAD_V),
            w_br_ret.astype(BF16), w_br_gla.astype(BF16), w_out.astype(BF16), g_ffn.reshape(1, D_MODEL), wrt, brt)


def _tri_consts(tm):
    r = np.arange(tm)
    low = (r[None, :] <= r[:, None])
    bd = low & ((r[None, :] // CHUNK) == (r[:, None] // CHUNK))
    return jnp.asarray(bd, BF16), jnp.asarray(low, BF16)


def _state_to_kernel(s):
    return s.reshape(s.shape[0], HK, HEAD_V)


def _state_from_kernel(s):
    return s.reshape(1, s.shape[0], N_HEADS, HEAD_K, HEAD_V)


def kernel(x_prompt, x_sample, state_ret, state_gla, meta_tokens, g_mix, w_in, w_gla_gk2, b_gla_gk, g_gla_norm, w_br_ret, w_br_gla, w_out, g_ffn, w_router_group, b_router_group, w_router_expert, b_router_expert, w_exp_gate, w_exp_up, w_exp_down, g_final):
    n_b, seq, _ = x_prompt.shape
    n_s, dec_seq, _ = x_sample.shape
    depth = state_ret.shape[0]
    assert depth == 1 and dec_seq == CHUNK and seq % TILE_ROWS == 0 and n_s % TILE_CHUNKS == 0
    cos_t, sin_t, cos_m, sin_m = _rotary_tables(seq, PAST_LEN)
    weights = _prep_weights(g_mix[0], w_in[0], w_gla_gk2[0], b_gla_gk[0], g_gla_norm[0], w_br_ret[0], w_br_gla[0],
                            w_out[0], g_ffn[0], w_router_group[0], b_router_group[0], w_router_expert[0],
                            b_router_expert[0])
    zero_state = jnp.zeros((1, HK, HEAD_V), F32)

    x_meta = jnp.concatenate([jnp.zeros((CHUNK - N_META, D_MODEL), F32), meta_tokens.astype(F32)], axis=0)
    meta_out = _mixer_call(None, x_meta.reshape(1, CHUNK, D_MODEL), cos_m, sin_m, (zero_state, zero_state),
                           (zero_state, zero_state), weights, _tri_consts(CHUNK), n_chunks=1, route=False)
    meta_ret, meta_gla = meta_out[6], meta_out[7]

    xnew, xloc, route, ku, ret_p, gla_p, ret_s, gla_s = _mixer_call(
        x_prompt, x_sample, cos_t, sin_t, (meta_ret, meta_gla),
        (_state_to_kernel(state_ret[0]), _state_to_kernel(state_gla[0])), weights, _tri_consts(TILE_ROWS),
        n_chunks=TILE_CHUNKS, route=True)

    n_tiles = route.shape[0]
    n_blocks = n_tiles * LOCAL_UNITS // BLOCK_UNITS + N_EXPERTS
    src, blk = _plan_call(ku[:, 0, :N_EXPERTS], n_blocks)
    yloc = _experts_call(src, blk, xloc, w_exp_gate[0], w_exp_up[0], w_exp_down[0], n_blocks)
    y_p, y_s = _combine_call(route, xnew, yloc, g_final.reshape(1, D_MODEL), TILE_ROWS, n_b * seq // TILE_ROWS)

    return (y_p.reshape(n_b, seq, D_MODEL), y_s.reshape(n_s, dec_seq, D_MODEL),
            _state_from_kernel(ret_p), _state_from_kernel(gla_p), _state_from_kernel(ret_s), _state_from_kernel(gla_s))
```

```python
import functools
import math

import jax
import jax.numpy as jnp
import numpy as np
from jax import lax
from jax.experimental import pallas as pl
from jax.experimental.pallas import tpu as pltpu

D_MODEL = 1024
CHUNK = 64
PAST_LEN = 1024
N_META = 16
N_HEADS = 4
HEAD_K = 64
HEAD_V = 128
HK = N_HEADS * HEAD_K
HV = N_HEADS * HEAD_V
GLA_RANK = 16
GATE_NORM = 16.0
ROPE_BASE = 10000.0
N_GROUPS = 4
EXPERTS_PER_GROUP = 8
N_EXPERTS = N_GROUPS * EXPERTS_PER_GROUP
D_EXPERT = 512
NORM_EPS = 1e-6

LANES = 128
SUBLANES = 8
TILE_CHUNKS = 8
TILE_ROWS = TILE_CHUNKS * CHUNK
ROUTE_ROWS = 8
ROUTER_LANES = 128
UNIT = SUBLANES
MOE_BLOCK = 512
BLOCK_UNITS = MOE_BLOCK // UNIT
BLK_ROWS = 5
LOCAL_ROWS = 2 * TILE_ROWS + N_EXPERTS * UNIT
LOCAL_UNITS = LOCAL_ROWS // UNIT
VMEM_LIMIT = 56 * 1024 * 1024

C_RQ, C_RK, C_RV, C_RG = 0, 256, 512, 1024
C_GQ, C_GK, C_GV, C_GG = 1536, 1792, 2048, 2560
W_MIX = 3072
C_Z = W_MIX + GLA_RANK

F32 = jnp.float32
BF16 = jnp.bfloat16
LOG_G = tuple(math.log1p(-(2.0 ** (-5.0 - h))) for h in range(N_HEADS))


def _dot(a, b):
    return jnp.dot(a, b, preferred_element_type=F32)


def _dot_nt(a, b):
    return lax.dot_general(a, b, (((1,), (1,)), ((), ())), preferred_element_type=F32)


def _dot_tn(a, b):
    return lax.dot_general(a, b, (((0,), (0,)), ((), ())), preferred_element_type=F32)


def _sigmoid(x):
    return 1.0 / (1.0 + jnp.exp(-x))


def _pack_halves(x):
    half = x.shape[1] // 2
    bits = pltpu.bitcast(x, jnp.int32)
    return lax.shift_right_logical(bits[:, :half], 16) | (bits[:, half:] & jnp.int32(-65536))


def _unpack_halves(w):
    lo = pltpu.bitcast(lax.shift_left(w, 16), F32)
    hi = pltpu.bitcast(w & jnp.int32(-65536), F32)
    return jnp.concatenate([lo, hi], axis=1).astype(BF16)


def _head_of_lane(shape, width):
    return lax.broadcasted_iota(jnp.int32, shape, len(shape) - 1) >> int(math.log2(width))


def _per_head_lane_const(vals, shape, width):
    hd = _head_of_lane(shape, width)
    out = jnp.full(shape, vals[N_HEADS - 1], F32)
    for h in range(N_HEADS - 2, -1, -1):
        out = jnp.where(hd == h, vals[h], out)
    return out


def _tables_kernel(inv_ref, cp_ref, sp_ref, cm_ref, sm_ref, *, seq, past_len):
    inv = inv_ref[...].reshape(1, 1, LANES)
    lane = lax.broadcasted_iota(jnp.int32, (1, 1, LANES), 2)
    sign = jnp.where((lane & (HEAD_K - 1)) < (HEAD_K // 2), -1.0, 1.0)
    off = lax.broadcasted_iota(jnp.int32, (1, CHUNK, LANES), 1).astype(F32) * inv
    c_off, s_off = jnp.cos(off), jnp.sin(off)

    def chunks(n, first_pos):
        base = (lax.broadcasted_iota(jnp.int32, (n, 1, LANES), 0) * CHUNK + first_pos).astype(F32) * inv
        c_base, s_base = jnp.cos(base), jnp.sin(base)
        cos = (c_base * c_off - s_base * s_off).reshape(n * CHUNK, LANES)
        sin = ((s_base * c_off + c_base * s_off) * sign).reshape(n * CHUNK, LANES)
        return cos, sin

    cp_ref[0:seq, :], sp_ref[0:seq, :] = chunks(seq // CHUNK, 0)
    cos_s, sin_s = chunks(1, past_len)
    for c in range(TILE_CHUNKS):
        cp_ref[seq + c * CHUNK:seq + (c + 1) * CHUNK, :] = cos_s
        sp_ref[seq + c * CHUNK:seq + (c + 1) * CHUNK, :] = sin_s
    cm_ref[...], sm_ref[...] = chunks(1, -CHUNK)


def _rotary_tables(seq, past_len):
    half = HEAD_K // 2
    inv = ROPE_BASE ** (-2.0 * jnp.arange(half, dtype=F32) / HEAD_K)
    inv = jnp.tile(inv, LANES // half).reshape(1, LANES)
    shp = lambda r: jax.ShapeDtypeStruct((r, LANES), F32)
    return pl.pallas_call(
        functools.partial(_tables_kernel, seq=seq, past_len=past_len),
        out_shape=(shp(seq + TILE_ROWS), shp(seq + TILE_ROWS), shp(CHUNK), shp(CHUNK)),
        compiler_params=pltpu.CompilerParams(vmem_limit_bytes=VMEM_LIMIT),
        name="rotary_tables",
    )(inv)


def _mixer_kernel(*refs, n_chunks, tiles_p, tiles_per_row, route):
    tm = n_chunks * CHUNK
    if tiles_p:
        xp_ref, refs = refs[0], refs[1:]
    (xs_ref, cos_ref, sin_ref, spi_r_ref, spi_g_ref, ssi_r_ref, ssi_g_ref,
     gmix_ref, wint_ref, wgk_ref, bgk_ref, gnorm_ref, wbr_ref, wbg_ref, wout_ref,
     gffn_ref, wrt_ref, brt_ref, bdtri_ref, ltri_ref,
     xnew_ref, xloc_ref, route_ref, ku_ref, spo_r_ref, spo_g_ref, sso_r_ref, sso_g_ref,
     qb, qdb, kb, kkb, vb, gqb, gkkb, gvb, ga, o_ret, o_gla, st_ret, st_gla,
     sin_r, sin_g, sout_r, sout_g) = refs

    i = pl.program_id(0)
    if tiles_p:
        is_s = i >= tiles_p
        t_idx = jnp.minimum(i, tiles_p - 1) % tiles_per_row
        x = jnp.where(is_s, xs_ref[...].reshape(tm, D_MODEL), xp_ref[...].reshape(tm, D_MODEL))

        @pl.when(jnp.logical_and(jnp.logical_not(is_s), t_idx == 0))
        def _():
            st_ret[...] = spi_r_ref[0].T
            st_gla[...] = spi_g_ref[0].T

        @pl.when(i == 0)
        def _():
            sin_r[...] = jnp.zeros(sin_r.shape, F32)
            sin_g[...] = jnp.zeros(sin_g.shape, F32)

        @pl.when(is_s)
        def _():
            for c in range(n_chunks):
                sin_r[c] = ssi_r_ref[c].T
                sin_g[c] = ssi_g_ref[c].T
    else:
        is_s = None
        x = xs_ref[...].reshape(tm, D_MODEL)

    h = x * lax.rsqrt(jnp.mean(x * x, axis=-1, keepdims=True) + NORM_EPS) * gmix_ref[...]
    hb = h.astype(BF16)

    def proj(c0, width):
        return _dot_nt(hb, wint_ref[c0:c0 + width, :])

    cos3 = jnp.concatenate([cos_ref[...]] * 2, axis=1).reshape(n_chunks, CHUNK, HK)
    sin3 = jnp.concatenate([sin_ref[...]] * 2, axis=1).reshape(n_chunks, CHUNK, HK)
    lane_hk = lax.broadcasted_iota(jnp.int32, (tm, HK), 1)
    first_half = (lane_hk & (HEAD_K - 1)) < (HEAD_K // 2)

    def rotary(t):
        swapped = jnp.where(first_half, pltpu.roll(t, HK - HEAD_K // 2, 1), pltpu.roll(t, HEAD_K // 2, 1))
        t3 = t.reshape(n_chunks, CHUNK, HK)
        return (t3 * cos3 + swapped.reshape(n_chunks, CHUNK, HK) * sin3).reshape(tm, HK)

    logg_hk = _per_head_lane_const(LOG_G, (CHUNK, HK), HEAD_K)
    l_idx = lax.broadcasted_iota(jnp.int32, (CHUNK, HK), 0).astype(F32)
    qdec = jnp.exp((l_idx + 1.0) * logg_hk)
    kdec = jnp.exp((CHUNK - 1.0 - l_idx) * logg_hk)
    cdec = jnp.exp(float(CHUNK) * _per_head_lane_const(LOG_G, (1, HK), HEAD_K))
    r_idx = lax.broadcasted_iota(jnp.int32, (N_HEADS * CHUNK, CHUNK), 0)
    m_idx = lax.broadcasted_iota(jnp.int32, (N_HEADS * CHUNK, CHUNK), 1)
    logg_rows = jnp.full((N_HEADS * CHUNK, CHUNK), LOG_G[N_HEADS - 1], F32)
    for hh in range(N_HEADS - 2, -1, -1):
        logg_rows = jnp.where((r_idx >> int(math.log2(CHUNK))) == hh, LOG_G[hh], logg_rows)
    dmat = jnp.exp(jnp.abs((r_idx & (CHUNK - 1)) - m_idx).astype(F32) * logg_rows)

    glr = jnp.where(lax.broadcasted_iota(jnp.int32, (tm, LANES), 1) < GLA_RANK, proj(W_MIX, LANES), 0.0)
    rq = rotary(proj(C_RQ, HK))
    qb[...] = rq.astype(BF16)
    qdb[...] = (rq.reshape(n_chunks, CHUNK, HK) * qdec).reshape(tm, HK).astype(BF16)
    gl = _dot(glr.astype(BF16), wgk_ref[...]) + bgk_ref[...]
    rk = rotary(proj(C_RK, HK)) * (HEAD_K ** -0.5)
    kb[...] = rk.astype(BF16)
    kkb[...] = (rk.reshape(n_chunks, CHUNK, HK) * kdec).reshape(tm, HK).astype(BF16)
    log_a = (jnp.minimum(gl, 0.0) - jnp.log1p(jnp.exp(-jnp.abs(gl)))) / GATE_NORM
    la_hi = log_a.astype(BF16)
    la_lo = (log_a - la_hi.astype(F32)).astype(BF16)
    vb[...] = proj(C_RV, HV).astype(BF16)
    bdtri = bdtri_ref[...]
    bcum = _dot(bdtri, la_hi) + _dot(bdtri, la_lo)
    gqb[...] = (proj(C_GQ, HK) * (HEAD_K ** -0.5)).astype(BF16)
    gvb[...] = proj(C_GV, HV).astype(BF16)
    gk = proj(C_GK, HK)
    b3 = bcum.reshape(n_chunks, CHUNK, HK)
    bl3 = b3[:, CHUNK - 1:CHUNK, :]
    gkkb[...] = (gk.reshape(n_chunks, CHUNK, HK) * jnp.exp(bl3 - b3)).reshape(tm, HK).astype(BF16)
    ga[...] = jnp.broadcast_to(jnp.exp(bl3), (n_chunks, SUBLANES, HK))

    def stack_masked(a, width):
        head = _head_of_lane(a.shape, width)
        zero = jnp.zeros_like(a)
        return jnp.concatenate([jnp.where(head == hh, a, zero) for hh in range(N_HEADS)], axis=0)

    def heads_to_rows(a):
        return jnp.concatenate([a[:, hh * HEAD_V:(hh + 1) * HEAD_V] for hh in range(N_HEADS)], axis=0)

    def rows_to_heads(a):
        return jnp.concatenate([a[hh * CHUNK:(hh + 1) * CHUNK, :] for hh in range(N_HEADS)], axis=1)

    chunk_rows = [slice(c * CHUNK, (c + 1) * CHUNK) for c in range(n_chunks)]
    probs = [(_dot_nt(stack_masked(qb[r, :], HEAD_K), kb[r, :]) * dmat).astype(BF16) for r in chunk_rows]
    inc_ret = [_dot_tn(heads_to_rows(vb[r, :]), stack_masked(kkb[r, :], HEAD_K)) for r in chunk_rows]
    inc_gla = [_dot_tn(heads_to_rows(gvb[r, :]), stack_masked(gkkb[r, :], HEAD_K)) for r in chunk_rows]
    rg = proj(C_RG, HV)
    gg = proj(C_GG, HV)
    s_in, g_out = [], []
    s_cur, g_cur = st_ret[...], st_gla[...]
    for c in range(n_chunks):
        if is_s is None:
            s_cur, g_cur = ssi_r_ref[c].T, ssi_g_ref[c].T
        else:
            s_cur = jnp.where(is_s, sin_r[c], s_cur)
            g_cur = jnp.where(is_s, sin_g[c], g_cur)
        s_in.append(s_cur.astype(BF16))
        s_cur = s_cur * cdec + inc_ret[c]
        g_cur = g_cur * ga[c][0:1, :] + inc_gla[c]
        g_out.append(g_cur.astype(BF16))
        sout_r[c] = s_cur
        sout_g[c] = g_cur
    st_ret[...] = s_cur
    st_gla[...] = g_cur
    for c, r in enumerate(chunk_rows):
        v = vb[r, :]
        intra = jnp.concatenate(
            [_dot(probs[c][hh * CHUNK:(hh + 1) * CHUNK, :], v[:, hh * HEAD_V:(hh + 1) * HEAD_V])
             for hh in range(N_HEADS)], axis=1)
        inter = rows_to_heads(_dot_nt(stack_masked(qdb[r, :], HEAD_K), s_in[c]))
        o_ret[r, :] = intra + inter
        o_gla[r, :] = rows_to_heads(_dot_nt(stack_masked(gqb[r, :], HEAD_K), g_out[c]))

    gnorm = gnorm_ref[...]
    orr = o_ret[...]
    ogg = o_gla[...]
    ret_parts, gla_parts = [], []
    for hh in range(N_HEADS):
        sl = slice(hh * HEAD_V, (hh + 1) * HEAD_V)
        oh = orr[:, sl]
        mu = jnp.mean(oh, axis=-1, keepdims=True)
        dev = oh - mu
        var = jnp.mean(dev * dev, axis=-1, keepdims=True)
        ret_parts.append(dev * lax.rsqrt(var + NORM_EPS))
        og = ogg[:, sl]
        gla_parts.append(og * lax.rsqrt(jnp.mean(og * og, axis=-1, keepdims=True) + NORM_EPS) * gnorm)
    o_r = jnp.concatenate(ret_parts, axis=1) * (rg * _sigmoid(rg))
    o_g = jnp.concatenate(gla_parts, axis=1) * (gg * _sigmoid(gg))
    merged = (_sigmoid(proj(C_Z, D_MODEL)) * _dot(o_r.astype(BF16), wbr_ref[...])
              + _sigmoid(proj(C_Z + D_MODEL, D_MODEL)) * _dot(o_g.astype(BF16), wbg_ref[...]))
    xn = x + _dot(merged.astype(BF16), wout_ref[...])
    xnew_ref[...] = xn

    if route:
        h2 = xn * lax.rsqrt(jnp.mean(xn * xn, axis=-1, keepdims=True) + NORM_EPS) * gffn_ref[...]
        h2_hi = h2.astype(BF16)
        h2_lo = (h2 - h2_hi.astype(F32)).astype(BF16)
        hi_both = _dot_nt(h2_hi, wrt_ref[...])
        logits = (hi_both[:, :ROUTER_LANES] + _dot_nt(h2_lo, wrt_ref[:ROUTER_LANES, :])
                  + hi_both[:, ROUTER_LANES:]) + brt_ref[...]
        lt = logits.T
        row8 = lax.broadcasted_iota(jnp.int32, (SUBLANES, tm), 0)
        neg_inf = jnp.float32(-jnp.inf)
        glog = jnp.where(row8 < N_GROUPS, lt[0:SUBLANES, :], neg_inf)
        gmax = jnp.max(glog, axis=0, keepdims=True)
        grp = jnp.min(jnp.where(glog == gmax, row8, SUBLANES), axis=0, keepdims=True)
        p_grp = 1.0 / jnp.sum(jnp.exp(glog - gmax), axis=0, keepdims=True)
        le = jnp.zeros((SUBLANES, tm), F32)
        for g in range(N_GROUPS):
            le = jnp.where(grp == g, lt[SUBLANES * (g + 1):SUBLANES * (g + 2), :], le)
        m1 = jnp.max(le, axis=0, keepdims=True)
        i1 = jnp.min(jnp.where(le == m1, row8, SUBLANES), axis=0, keepdims=True)
        le2 = jnp.where(row8 == i1, neg_inf, le)
        m2 = jnp.max(le2, axis=0, keepdims=True)
        i2 = jnp.min(jnp.where(le2 == m2, row8, SUBLANES), axis=0, keepdims=True)
        e0 = grp * EXPERTS_PER_GROUP + i1
        e1 = grp * EXPERTS_PER_GROUP + i2
        t21 = jnp.exp(m2 - m1)
        w0 = p_grp / (1.0 + t21)
        w1 = p_grp * t21 / (1.0 + t21)

        erow = lax.broadcasted_iota(jnp.int32, (N_EXPERTS, tm), 0)
        hit0 = erow == e0
        hit1 = erow == e1
        onehot = jnp.where(jnp.logical_or(hit0, hit1), 1.0, 0.0).astype(BF16)
        cum = _dot_nt(onehot, ltri_ref[...])
        n_run = cum[:, tm - 1:tm]
        n_pad = jnp.ceil(n_run / UNIT) * UNIT
        rank0 = jnp.sum(jnp.where(hit0, cum - 1.0, 0.0), axis=0, keepdims=True)
        rank1 = jnp.sum(jnp.where(hit1, cum - 1.0, 0.0), axis=0, keepdims=True)
        start0 = jnp.sum(jnp.where(erow < e0, n_pad, 0.0), axis=0, keepdims=True)
        start1 = jnp.sum(jnp.where(erow < e1, n_pad, 0.0), axis=0, keepdims=True)
        ld0 = (start0 + rank0).astype(jnp.int32)
        ld1 = (start1 + rank1).astype(jnp.int32)
        lrow = lax.broadcasted_iota(jnp.int32, (LOCAL_ROWS, tm), 0)
        perm = jnp.where(jnp.logical_or(lrow == ld0, lrow == ld1), 1.0, 0.0).astype(BF16)
        xloc_ref[...] = _pack_halves(_dot(perm, h2_hi))
        lane_e = lax.broadcasted_iota(jnp.int32, (N_EXPERTS, LANES), 1)
        erow_l = lax.broadcasted_iota(jnp.int32, (N_EXPERTS, LANES), 0)
        units_row = jnp.sum(jnp.where(erow_l == lane_e, n_pad / UNIT, 0.0), axis=0, keepdims=True)
        ku_ref[...] = jnp.broadcast_to(units_row, (SUBLANES, LANES)).astype(jnp.int32).reshape(ku_ref.shape)
        zero_row = jnp.zeros((1, tm), jnp.int32)
        rec = jnp.concatenate([e0, e1, ld0, ld1, pltpu.bitcast(w0, jnp.int32), pltpu.bitcast(w1, jnp.int32),
                               zero_row, zero_row], axis=0)
        route_ref[...] = rec.reshape(route_ref.shape)
    else:
        xloc_ref[...] = jnp.zeros(xloc_ref.shape, jnp.int32)
        route_ref[...] = jnp.zeros(route_ref.shape, jnp.int32)
        ku_ref[...] = jnp.zeros(ku_ref.shape, jnp.int32)

    def stream_states_out():
        for c in range(n_chunks):
            sso_r_ref[c] = sout_r[c].T
            sso_g_ref[c] = sout_g[c].T

    if is_s is None:
        stream_states_out()
    else:
        pl.when(is_s)(stream_states_out)

    if tiles_p:
        @pl.when(jnp.logical_and(jnp.logical_not(is_s), t_idx == tiles_per_row - 1))
        def _():
            spo_r_ref[0] = st_ret[...].T
            spo_g_ref[0] = st_gla[...].T
    else:
        spo_r_ref[0] = st_ret[...].T
        spo_g_ref[0] = st_gla[...].T


def _const_spec(shape):
    nd = len(shape)
    return pl.BlockSpec(shape, lambda *_: (0,) * nd, pipeline_mode=pl.Buffered(1))


def _mixer_call(x_prompt, x_streams, cos, sin, st_prompt, st_streams, weights, consts, *, n_chunks, route):
    tm = n_chunks * CHUNK
    n_streams = x_streams.shape[0]
    tiles_s = n_streams // n_chunks
    if x_prompt is not None:
        n_rows, n_seq, _ = x_prompt.shape
        tpr = n_seq // tm
        tiles_p = n_rows * tpr
    else:
        n_rows, tpr, tiles_p = 1, 1, 0
    n_tiles = tiles_p + tiles_s
    p_idx = lambda i: jnp.minimum(i, tiles_p - 1)
    s_idx = lambda i: jnp.maximum(i - tiles_p, 0)

    st_blk = (1, HK, HEAD_V)
    in_specs, args = [], []
    if tiles_p:
        in_specs.append(pl.BlockSpec((1, tm, D_MODEL), lambda i: (p_idx(i) // tpr, p_idx(i) % tpr, 0)))
        args.append(x_prompt)
        cos_spec = pl.BlockSpec((tm, LANES), lambda i: (jnp.where(i < tiles_p, i % tpr, tpr), 0))
    else:
        cos_spec = pl.BlockSpec((tm, LANES), lambda i: (0, 0))
    once = pl.Buffered(1)
    stream_in = pl.BlockSpec((n_chunks, HK, HEAD_V), lambda i: (s_idx(i), 0, 0), pipeline_mode=once)
    in_specs += [pl.BlockSpec((n_chunks, CHUNK, D_MODEL), lambda i: (s_idx(i), 0, 0), pipeline_mode=once),
                 cos_spec, cos_spec, _const_spec(st_blk), _const_spec(st_blk), stream_in, stream_in]
    args += [x_streams, cos, sin, st_prompt[0], st_prompt[1], st_streams[0], st_streams[1]]
    in_specs += [_const_spec(w.shape) for w in weights] + [_const_spec(c.shape) for c in consts]
    args += list(weights) + list(consts)

    out_shape = (jax.ShapeDtypeStruct((n_tiles * tm, D_MODEL), F32),
                 jax.ShapeDtypeStruct((n_tiles * LOCAL_ROWS, D_MODEL // 2), jnp.int32),
                 jax.ShapeDtypeStruct((n_tiles, ROUTE_ROWS, tm), jnp.int32),
                 jax.ShapeDtypeStruct((n_tiles, SUBLANES, LANES), jnp.int32),
                 jax.ShapeDtypeStruct((n_rows, HK, HEAD_V), F32), jax.ShapeDtypeStruct((n_rows, HK, HEAD_V), F32),
                 jax.ShapeDtypeStruct((n_streams, HK, HEAD_V), F32),
                 jax.ShapeDtypeStruct((n_streams, HK, HEAD_V), F32))
    row_spec = pl.BlockSpec(st_blk, lambda i: (p_idx(i) // tpr if tiles_p else 0, 0, 0))
    stream_spec = pl.BlockSpec((n_chunks, HK, HEAD_V), lambda i: (s_idx(i), 0, 0))
    out_specs = (pl.BlockSpec((tm, D_MODEL), lambda i: (i, 0)),
                 pl.BlockSpec((LOCAL_ROWS, D_MODEL // 2), lambda i: (i, 0)),
                 pl.BlockSpec((1, ROUTE_ROWS, tm), lambda i: (i, 0, 0)),
                 pl.BlockSpec((1, SUBLANES, LANES), lambda i: (i, 0, 0)),
                 row_spec, row_spec, stream_spec, stream_spec)

    scratch = [pltpu.VMEM((tm, HK), BF16), pltpu.VMEM((tm, HK), BF16), pltpu.VMEM((tm, HK), BF16),
               pltpu.VMEM((tm, HK), BF16), pltpu.VMEM((tm, HV), BF16),
               pltpu.VMEM((tm, HK), BF16), pltpu.VMEM((tm, HK), BF16), pltpu.VMEM((tm, HV), BF16),
               pltpu.VMEM((n_chunks, SUBLANES, HK), F32),
               pltpu.VMEM((tm, HV), F32), pltpu.VMEM((tm, HV), F32),
               pltpu.VMEM((HEAD_V, HK), F32), pltpu.VMEM((HEAD_V, HK), F32)]
    scratch += [pltpu.VMEM((n_chunks, HEAD_V, HK), F32)] * 4

    return pl.pallas_call(
        functools.partial(_mixer_kernel, n_chunks=n_chunks, tiles_p=tiles_p, tiles_per_row=tpr, route=route),
        grid=(n_tiles,), in_specs=in_specs, out_specs=out_specs, out_shape=out_shape, scratch_shapes=scratch,
        compiler_params=pltpu.CompilerParams(dimension_semantics=("arbitrary",), vmem_limit_bytes=VMEM_LIMIT),
        name="mixer" if route else "mixer_meta",
    )(*args)


def _plan_kernel(ku_ref, src_ref, blk_ref, run_first, run_step, next_unit, seg_start, seg_units, *, n_tiles, n_blocks):
    shift = int(math.log2(BLOCK_UNITS))
    group_shift = int(math.log2(SUBLANES * LANES))

    def init_tile(t, c):
        next_unit[t] = t * LOCAL_UNITS
        return c

    lax.fori_loop(0, n_tiles, init_tile, 0)

    def init_block(b, c):
        blk_ref[0, b] = N_EXPERTS - 1
        blk_ref[1, b] = 0
        blk_ref[2, b] = 0
        blk_ref[3, b] = -1
        blk_ref[4, b] = 0
        return c

    lax.fori_loop(0, n_blocks, init_block, 0)

    def per_expert(e, carry):
        g0, position = carry

        def per_tile(t, c):
            units, d_prev = c
            base = next_unit[t]
            d = base - units
            run_first[e * n_tiles + t] = units
            run_step[e * n_tiles + t] = d - d_prev
            k = ku_ref[t, e]
            next_unit[t] = base + k
            return units + k, d

        units, _ = lax.fori_loop(0, n_tiles, per_tile, (0, 0), unroll=4)
        g_pad = g0 + (((units + (BLOCK_UNITS - 1)) >> shift) << shift)
        seg_start[e] = g0
        seg_units[e] = units

        def set_block(b, c):
            blk_ref[0, b] = e
            blk_ref[1, b] = jnp.minimum(g0 + units - (b << shift), BLOCK_UNITS)
            blk_ref[3, b] = g_pad >> shift
            blk_ref[4, b] = position & 1
            return c

        lax.fori_loop(g0 >> shift, g_pad >> shift, set_block, 0)
        return g_pad, position + (units > 0).astype(jnp.int32)

    g_total, _ = lax.fori_loop(0, N_EXPERTS, per_expert, (0, 0))
    blk_ref[2, 0] = g_total >> shift

    src_ref[...] = jnp.zeros(src_ref.shape, jnp.int32)
    in_group = (lax.broadcasted_iota(jnp.int32, (SUBLANES, LANES), 0) * LANES
                + lax.broadcasted_iota(jnp.int32, (SUBLANES, LANES), 1))

    def expert_units(e, c):
        g0 = seg_start[e]
        units = seg_units[e]
        g_pad = g0 + (((units + (BLOCK_UNITS - 1)) >> shift) << shift)

        def per_group(grp, c2):
            g = in_group + (grp << group_shift)
            o = g - g0
            mine = jnp.logical_and(o >= 0, g < g_pad)
            o_eff = jnp.where(o < units, o, (o >> shift) << shift)

            def per_tile(t, acc):
                return acc + jnp.where(o_eff >= run_first[e * n_tiles + t], run_step[e * n_tiles + t], 0)

            offset = lax.fori_loop(0, n_tiles, per_tile, jnp.zeros((SUBLANES, LANES), jnp.int32), unroll=4)
            rows = pl.ds(pl.multiple_of(grp * SUBLANES, SUBLANES), SUBLANES)
            src_ref[rows, :] = jnp.where(mine, o_eff + offset, src_ref[rows, :])
            return c2

        lax.fori_loop(g0 >> group_shift, (g_pad + (SUBLANES * LANES - 1)) >> group_shift, per_group, 0)
        return c

    lax.fori_loop(0, N_EXPERTS, expert_units, 0)


def _plan_call(ku, n_blocks):
    n_tiles = ku.shape[0]
    smem = pl.BlockSpec(memory_space=pltpu.SMEM)
    group = SUBLANES * LANES
    src_rows = -(-n_blocks * BLOCK_UNITS // group) * SUBLANES
    src, blk = pl.pallas_call(
        functools.partial(_plan_kernel, n_tiles=n_tiles, n_blocks=n_blocks),
        in_specs=[smem], out_specs=(pl.BlockSpec(memory_space=pltpu.VMEM), smem),
        out_shape=(jax.ShapeDtypeStruct((src_rows, LANES), jnp.int32),
                   jax.ShapeDtypeStruct((BLK_ROWS, n_blocks), jnp.int32)),
        scratch_shapes=[pltpu.SMEM((N_EXPERTS * n_tiles,), jnp.int32), pltpu.SMEM((N_EXPERTS * n_tiles,), jnp.int32),
                        pltpu.SMEM((n_tiles,), jnp.int32), pltpu.SMEM((N_EXPERTS,), jnp.int32),
                        pltpu.SMEM((N_EXPERTS,), jnp.int32)],
        name="moe_plan",
    )(ku)
    return src.reshape(-1), blk


def _experts_kernel(src_ref, blk_ref, x_hbm, wg_hbm, wu_hbm, wd_hbm, y_hbm,
                    xin, yout, wgf, wuf, wdf, wgub, wdb, sem_in, sem_out, sem_w):
    b = pl.program_id(0)
    used = blk_ref[2, 0]
    slot = b & 1

    def weight_copies(e, s):
        return [pltpu.make_async_copy(hbm.at[e], buf.at[s], sem_w.at[s])
                for hbm, buf in ((wg_hbm, wgf), (wu_hbm, wuf), (wd_hbm, wdf))]

    def unit_rows(blk, u):
        return pl.ds(pl.multiple_of(src_ref[blk * BLOCK_UNITS + u] * UNIT, UNIT), UNIT)

    def in_copy(blk, u, s):
        return pltpu.make_async_copy(x_hbm.at[unit_rows(blk, u), :], xin.at[s, pl.ds(u * UNIT, UNIT), :],
                                     sem_in.at[s])

    def out_copy(blk, u, s):
        return pltpu.make_async_copy(yout.at[s, pl.ds(u * UNIT, UNIT), :], y_hbm.at[unit_rows(blk, u), :],
                                     sem_out.at[s])

    @pl.when(jnp.logical_and(b == 0, used > 0))
    def _():
        for copy in weight_copies(blk_ref[0, 0], blk_ref[4, 0]):
            copy.start()
        for u in range(BLOCK_UNITS):
            in_copy(0, u, 0).start()

    @pl.when(b < used)
    def _():
        expert = blk_ref[0, b]
        changed = jnp.logical_or(b == 0, expert != blk_ref[0, jnp.maximum(b - 1, 0)])

        @pl.when(changed)
        def _():
            wslot = blk_ref[4, b]
            later = blk_ref[3, b]
            following = jnp.where(later < used, blk_ref[0, jnp.minimum(later, pl.num_programs(0) - 1)], -1)
            for copy in weight_copies(expert, wslot):
                copy.wait()

            @pl.when(following >= 0)
            def _():
                for copy in weight_copies(following, 1 - wslot):
                    copy.start()

            wgub[:, :D_EXPERT] = wgf[wslot].astype(BF16)
            wgub[:, D_EXPERT:] = wuf[wslot].astype(BF16)
            wdb[...] = wdf[wslot].astype(BF16)

        @pl.when(b + 1 < used)
        def _():
            for u in range(BLOCK_UNITS):
                in_copy(b + 1, u, 1 - slot).start()

        for u in range(BLOCK_UNITS):
            in_copy(b, u, slot).wait()

        n_real = blk_ref[1, b]

        def mlp(rows):
            xb = _unpack_halves(xin[slot, :rows, :])
            gate_up = _dot(xb, wgub[...])
            gate = gate_up[:, :D_EXPERT]
            hid = (gate * _sigmoid(gate)) * gate_up[:, D_EXPERT:]
            yout[slot, :rows, :] = _pack_halves(_dot(hid.astype(BF16), wdb[...]).astype(BF16).astype(F32))

        @pl.when(n_real > BLOCK_UNITS // 2)
        def _():
            mlp(MOE_BLOCK)

        @pl.when(n_real <= BLOCK_UNITS // 2)
        def _():
            mlp(MOE_BLOCK // 2)


        @pl.when(n_real == BLOCK_UNITS)
        def _():
            for u in range(BLOCK_UNITS):
                out_copy(b, u, slot).start()

        @pl.when(n_real < BLOCK_UNITS)
        def _():
            lax.fori_loop(0, n_real, lambda u, c: (out_copy(b, u, slot).start(), c)[1], 0)

        @pl.when(b > 0)
        def _():
            lax.fori_loop(0, blk_ref[1, b - 1], lambda u, c: (out_copy(b - 1, u, 1 - slot).wait(), c)[1], 0)

        @pl.when(b == used - 1)
        def _():
            lax.fori_loop(0, n_real, lambda u, c: (out_copy(b, u, slot).wait(), c)[1], 0)


def _experts_call(src, blk, xloc, wg, wu, wd, n_blocks):
    hbm = pl.BlockSpec(memory_space=pl.ANY)
    grid_spec = pltpu.PrefetchScalarGridSpec(
        num_scalar_prefetch=2,
        grid=(n_blocks,),
        in_specs=[hbm, hbm, hbm, hbm],
        out_specs=hbm,
        scratch_shapes=[pltpu.VMEM((2, MOE_BLOCK, D_MODEL // 2), jnp.int32),
                        pltpu.VMEM((2, MOE_BLOCK, D_MODEL // 2), jnp.int32),
                        pltpu.VMEM((2, D_MODEL, D_EXPERT), F32), pltpu.VMEM((2, D_MODEL, D_EXPERT), F32),
                        pltpu.VMEM((2, D_EXPERT, D_MODEL), F32),
                        pltpu.VMEM((D_MODEL, 2 * D_EXPERT), BF16), pltpu.VMEM((D_EXPERT, D_MODEL), BF16),
                        pltpu.SemaphoreType.DMA((2,)), pltpu.SemaphoreType.DMA((2,)),
                        pltpu.SemaphoreType.DMA((2,))],
    )
    return pl.pallas_call(
        _experts_kernel, grid_spec=grid_spec,
        out_shape=jax.ShapeDtypeStruct(xloc.shape, jnp.int32),
        input_output_aliases={2: 0},
        compiler_params=pltpu.CompilerParams(dimension_semantics=("arbitrary",), vmem_limit_bytes=VMEM_LIMIT),
        name="moe_experts",
    )(src, blk, xloc, wg, wu, wd)


def _combine_kernel(route_ref, x_ref, y_ref, gfin_ref, out_a_ref, out_b_ref, *, tm, tiles_a):
    t = pl.program_id(0)
    rec = route_ref[0]
    recf = jnp.concatenate([rec[2:4, :].astype(F32), pltpu.bitcast(rec, F32)[4:6, :],
                            jnp.zeros((LANES - 4, tm), F32)], axis=0)
    cols = recf.T
    lrow = lax.broadcasted_iota(jnp.int32, (tm, LOCAL_ROWS), 1).astype(F32)
    select = jnp.where(lrow == cols[:, 0:1], cols[:, 2:3], jnp.where(lrow == cols[:, 1:2], cols[:, 3:4], 0.0))
    xf = x_ref[...] + _dot(select.astype(BF16), _unpack_halves(y_ref[...]))
    out = xf * lax.rsqrt(jnp.mean(xf * xf, axis=-1, keepdims=True) + NORM_EPS) * gfin_ref[...]

    @pl.when(t < tiles_a)
    def _():
        out_a_ref[...] = out

    @pl.when(t >= tiles_a)
    def _():
        out_b_ref[...] = out


def _combine_call(route, xnew, yloc, gfin, tm, tiles_a):
    n_tiles = route.shape[0]
    tiles_b = n_tiles - tiles_a
    return pl.pallas_call(
        functools.partial(_combine_kernel, tm=tm, tiles_a=tiles_a),
        grid=(n_tiles,),
        in_specs=[pl.BlockSpec((1, ROUTE_ROWS, tm), lambda t: (t, 0, 0)),
                  pl.BlockSpec((tm, D_MODEL), lambda t: (t, 0)),
                  pl.BlockSpec((LOCAL_ROWS, D_MODEL // 2), lambda t: (t, 0)),
                  pl.BlockSpec((1, D_MODEL), lambda t: (0, 0))],
        out_specs=(pl.BlockSpec((tm, D_MODEL), lambda t: (jnp.minimum(t, tiles_a - 1), 0)),
                   pl.BlockSpec((tm, D_MODEL), lambda t: (jnp.maximum(t - tiles_a, 0), 0))),
        out_shape=(jax.ShapeDtypeStruct((tiles_a * tm, D_MODEL), F32),
                   jax.ShapeDtypeStruct((tiles_b * tm, D_MODEL), F32)),
        compiler_params=pltpu.CompilerParams(dimension_semantics=("arbitrary",), vmem_limit_bytes=VMEM_LIMIT),
        name="moe_combine",
    )(route, xnew, yloc, gfin)


def _prep_weights(g_mix, w_in, w_gla_gk2, b_gla_gk, g_gla_norm, w_br_ret, w_br_gla, w_out, g_ffn,
                  w_rg, b_rg, w_re, b_re):
    w_int = jnp.swapaxes(w_in, 0, 1).astype(BF16)
    wgk = jnp.pad(w_gla_gk2, ((0, LANES - GLA_RANK), (0, 0))).astype(BF16)
    def router_rows(groups, experts):
        return jnp.concatenate([groups, jnp.zeros((SUBLANES - N_GROUPS,) + groups.shape[1:], F32), experts,
                                jnp.zeros((ROUTER_LANES - SUBLANES - N_EXPERTS,) + experts.shape[1:], F32)], axis=0)

    wrt = router_rows(w_rg.T, w_re.T)
    brt = router_rows(b_rg, b_re).reshape(1, ROUTER_LANES)
    wrt_hi = wrt.astype(BF16)
    wrt = jnp.concatenate([wrt_hi, (wrt - wrt_hi.astype(F32)).astype(BF16)], axis=0)
    return (g_mix.reshape(1, D_MODEL), w_int, wgk, b_gla_gk.reshape(1, HK), g_gla_norm.reshape(1, HEAD_V),
            w_br_ret.astype(BF16), w_br_gla.astype(BF16), w_out.astype(BF16), g_ffn.reshape(1, D_MODEL), wrt, brt)


def _tri_consts(tm):
    r = np.arange(tm)
    low = (r[None, :] <= r[:, None])
    bd = low & ((r[None, :] // CHUNK) == (r[:, None] // CHUNK))
    return jnp.asarray(bd, BF16), jnp.asarray(low, BF16)


def _state_to_kernel(s):
    return s.reshape(s.shape[0], HK, HEAD_V)


def _state_from_kernel(s):
    return s.reshape(1, s.shape[0], N_HEADS, HEAD_K, HEAD_V)


def kernel(x_prompt, x_sample, state_ret, state_gla, meta_tokens, g_mix, w_in, w_gla_gk2, b_gla_gk, g_gla_norm, w_br_ret, w_br_gla, w_out, g_ffn, w_router_group, b_router_group, w_router_expert, b_router_expert, w_exp_gate, w_exp_up, w_exp_down, g_final):
    n_b, seq, _ = x_prompt.shape
    n_s, dec_seq, _ = x_sample.shape
    depth = state_ret.shape[0]
    assert depth == 1 and dec_seq == CHUNK and seq % TILE_ROWS == 0 and n_s % TILE_CHUNKS == 0
    cos_t, sin_t, cos_m, sin_m = _rotary_tables(seq, PAST_LEN)
    weights = _prep_weights(g_mix[0], w_in[0], w_gla_gk2[0], b_gla_gk[0], g_gla_norm[0], w_br_ret[0], w_br_gla[0],
                            w_out[0], g_ffn[0], w_router_group[0], b_router_group[0], w_router_expert[0],
                            b_router_expert[0])
    zero_state = jnp.zeros((1, HK, HEAD_V), F32)

    x_meta = jnp.concatenate([jnp.zeros((CHUNK - N_META, D_MODEL), F32), meta_tokens.astype(F32)], axis=0)
    meta_out = _mixer_call(None, x_meta.reshape(1, CHUNK, D_MODEL), cos_m, sin_m, (zero_state, zero_state),
                           (zero_state, zero_state), weights, _tri_consts(CHUNK), n_chunks=1, route=False)
    meta_ret, meta_gla = meta_out[6], meta_out[7]

    xnew, xloc, route, ku, ret_p, gla_p, ret_s, gla_s = _mixer_call(
        x_prompt, x_sample, cos_t, sin_t, (meta_ret, meta_gla),
        (_state_to_kernel(state_ret[0]), _state_to_kernel(state_gla[0])), weights, _tri_consts(TILE_ROWS),
        n_chunks=TILE_CHUNKS, route=True)

    n_tiles = route.shape[0]
    n_blocks = n_tiles * LOCAL_UNITS // BLOCK_UNITS + N_EXPERTS
    src, blk = _plan_call(ku[:, 0, :N_EXPERTS], n_blocks)
    yloc = _experts_call(src, blk, xloc, w_exp_gate[0], w_exp_up[0], w_exp_down[0], n_blocks)
    y_p, y_s = _combine_call(route, xnew, yloc, g_final.reshape(1, D_MODEL), TILE_ROWS, n_b * seq // TILE_ROWS)

    return (y_p.reshape(n_b, seq, D_MODEL), y_s.reshape(n_s, dec_seq, D_MODEL),
            _state_from_kernel(ret_p), _state_from_kernel(gla_p), _state_from_kernel(ret_s), _state_from_kernel(gla_s))
```

```python
import functools
import math

import jax
import jax.numpy as jnp
import numpy as np
from jax import lax
from jax.experimental import pallas as pl
from jax.experimental.pallas import tpu as pltpu

D_MODEL = 1024
CHUNK = 64
PAST_LEN = 1024
N_META = 16
N_HEADS = 4
HEAD_K = 64
HEAD_V = 128
HK = N_HEADS * HEAD_K
HV = N_HEADS * HEAD_V
GLA_RANK = 16
GATE_NORM = 16.0
ROPE_BASE = 10000.0
N_GROUPS = 4
EXPERTS_PER_GROUP = 8
N_EXPERTS = N_GROUPS * EXPERTS_PER_GROUP
D_EXPERT = 512
NORM_EPS = 1e-6

LANES = 128
SUBLANES = 8
TILE_CHUNKS = 8
TILE_ROWS = TILE_CHUNKS * CHUNK
ROUTE_ROWS = 8
ROUTER_LANES = 128
UNIT = SUBLANES
MOE_BLOCK = 512
BLOCK_UNITS = MOE_BLOCK // UNIT
BLK_ROWS = 5
LOCAL_ROWS = 2 * TILE_ROWS + N_EXPERTS * UNIT
LOCAL_UNITS = LOCAL_ROWS // UNIT
VMEM_LIMIT = 56 * 1024 * 1024

C_RQ, C_RK, C_RV, C_RG = 0, 256, 512, 1024
C_GQ, C_GK, C_GV, C_GG = 1536, 1792, 2048, 2560
W_MIX = 3072
C_Z = W_MIX + GLA_RANK

F32 = jnp.float32
BF16 = jnp.bfloat16
LOG_G = tuple(math.log1p(-(2.0 ** (-5.0 - h))) for h in range(N_HEADS))


def _dot(a, b):
    return jnp.dot(a, b, preferred_element_type=F32)


def _dot_nt(a, b):
    return lax.dot_general(a, b, (((1,), (1,)), ((), ())), preferred_element_type=F32)


def _dot_tn(a, b):
    return lax.dot_general(a, b, (((0,), (0,)), ((), ())), preferred_element_type=F32)


def _sigmoid(x):
    return 1.0 / (1.0 + jnp.exp(-x))


def _pack_halves(x):
    half = x.shape[1] // 2
    bits = pltpu.bitcast(x, jnp.int32)
    return lax.shift_right_logical(bits[:, :half], 16) | (bits[:, half:] & jnp.int32(-65536))


def _unpack_halves(w):
    lo = pltpu.bitcast(lax.shift_left(w, 16), F32)
    hi = pltpu.bitcast(w & jnp.int32(-65536), F32)
    return jnp.concatenate([lo, hi], axis=1).astype(BF16)


def _head_of_lane(shape, width):
    return lax.broadcasted_iota(jnp.int32, shape, len(shape) - 1) >> int(math.log2(width))


def _per_head_lane_const(vals, shape, width):
    hd = _head_of_lane(shape, width)
    out = jnp.full(shape, vals[N_HEADS - 1], F32)
    for h in range(N_HEADS - 2, -1, -1):
        out = jnp.where(hd == h, vals[h], out)
    return out


def _tables_kernel(inv_ref, cp_ref, sp_ref, cm_ref, sm_ref, *, seq, past_len):
    inv = inv_ref[...].reshape(1, 1, LANES)
    lane = lax.broadcasted_iota(jnp.int32, (1, 1, LANES), 2)
    sign = jnp.where((lane & (HEAD_K - 1)) < (HEAD_K // 2), -1.0, 1.0)
    off = lax.broadcasted_iota(jnp.int32, (1, CHUNK, LANES), 1).astype(F32) * inv
    c_off, s_off = jnp.cos(off), jnp.sin(off)

    def chunks(n, first_pos):
        base = (lax.broadcasted_iota(jnp.int32, (n, 1, LANES), 0) * CHUNK + first_pos).astype(F32) * inv
        c_base, s_base = jnp.cos(base), jnp.sin(base)
        cos = (c_base * c_off - s_base * s_off).reshape(n * CHUNK, LANES)
        sin = ((s_base * c_off + c_base * s_off) * sign).reshape(n * CHUNK, LANES)
        return cos, sin

    cp_ref[0:seq, :], sp_ref[0:seq, :] = chunks(seq // CHUNK, 0)
    cos_s, sin_s = chunks(1, past_len)
    for c in range(TILE_CHUNKS):
        cp_ref[seq + c * CHUNK:seq + (c + 1) * CHUNK, :] = cos_s
        sp_ref[seq + c * CHUNK:seq + (c + 1) * CHUNK, :] = sin_s
    cm_ref[...], sm_ref[...] = chunks(1, -CHUNK)


def _rotary_tables(seq, past_len):
    half = HEAD_K // 2
    inv = ROPE_BASE ** (-2.0 * jnp.arange(half, dtype=F32) / HEAD_K)
    inv = jnp.tile(inv, LANES // half).reshape(1, LANES)
    shp = lambda r: jax.ShapeDtypeStruct((r, LANES), F32)
    return pl.pallas_call(
        functools.partial(_tables_kernel, seq=seq, past_len=past_len),
        out_shape=(shp(seq + TILE_ROWS), shp(seq + TILE_ROWS), shp(CHUNK), shp(CHUNK)),
        compiler_params=pltpu.CompilerParams(vmem_limit_bytes=VMEM_LIMIT),
        name="rotary_tables",
    )(inv)


def _mixer_kernel(*refs, n_chunks, tiles_p, tiles_per_row, route):
    tm = n_chunks * CHUNK
    if tiles_p:
        xp_ref, refs = refs[0], refs[1:]
    (xs_ref, cos_ref, sin_ref, spi_r_ref, spi_g_ref, ssi_r_ref, ssi_g_ref,
     gmix_ref, wint_ref, wgk_ref, bgk_ref, gnorm_ref, wbr_ref, wbg_ref, wout_ref,
     gffn_ref, wrt_ref, brt_ref, bdtri_ref, ltri_ref,
     xnew_ref, xloc_ref, route_ref, ku_ref, spo_r_ref, spo_g_ref, sso_r_ref, sso_g_ref,
     qb, qdb, kb, kkb, vb, gqb, gkkb, gvb, ga, o_ret, o_gla, st_ret, st_gla,
     sin_r, sin_g, sout_r, sout_g) = refs

    i = pl.program_id(0)
    if tiles_p:
        is_s = i >= tiles_p
        t_idx = jnp.minimum(i, tiles_p - 1) % tiles_per_row
        x = jnp.where(is_s, xs_ref[...].reshape(tm, D_MODEL), xp_ref[...].reshape(tm, D_MODEL))

        @pl.when(jnp.logical_and(jnp.logical_not(is_s), t_idx == 0))
        def _():
            st_ret[...] = spi_r_ref[0].T
            st_gla[...] = spi_g_ref[0].T

        @pl.when(i == 0)
        def _():
            sin_r[...] = jnp.zeros(sin_r.shape, F32)
            sin_g[...] = jnp.zeros(sin_g.shape, F32)

        @pl.when(is_s)
        def _():
            for c in range(n_chunks):
                sin_r[c] = ssi_r_ref[c].T
                sin_g[c] = ssi_g_ref[c].T
    else:
        is_s = None
        x = xs_ref[...].reshape(tm, D_MODEL)

    h = x * lax.rsqrt(jnp.mean(x * x, axis=-1, keepdims=True) + NORM_EPS) * gmix_ref[...]
    hb = h.astype(BF16)

    def proj(c0, width):
        return _dot_nt(hb, wint_ref[c0:c0 + width, :])

    cos3 = jnp.concatenate([cos_ref[...]] * 2, axis=1).reshape(n_chunks, CHUNK, HK)
    sin3 = jnp.concatenate([sin_ref[...]] * 2, axis=1).reshape(n_chunks, CHUNK, HK)
    lane_hk = lax.broadcasted_iota(jnp.int32, (tm, HK), 1)
    first_half = (lane_hk & (HEAD_K - 1)) < (HEAD_K // 2)

    def rotary(t):
        swapped = jnp.where(first_half, pltpu.roll(t, HK - HEAD_K // 2, 1), pltpu.roll(t, HEAD_K // 2, 1))
        t3 = t.reshape(n_chunks, CHUNK, HK)
        return (t3 * cos3 + swapped.reshape(n_chunks, CHUNK, HK) * sin3).reshape(tm, HK)

    logg_hk = _per_head_lane_const(LOG_G, (CHUNK, HK), HEAD_K)
    l_idx = lax.broadcasted_iota(jnp.int32, (CHUNK, HK), 0).astype(F32)
    qdec = jnp.exp((l_idx + 1.0) * logg_hk)
    kdec = jnp.exp((CHUNK - 1.0 - l_idx) * logg_hk)
    cdec = jnp.exp(float(CHUNK) * _per_head_lane_const(LOG_G, (1, HK), HEAD_K))
    r_idx = lax.broadcasted_iota(jnp.int32, (N_HEADS * CHUNK, CHUNK), 0)
    m_idx = lax.broadcasted_iota(jnp.int32, (N_HEADS * CHUNK, CHUNK), 1)
    logg_rows = jnp.full((N_HEADS * CHUNK, CHUNK), LOG_G[N_HEADS - 1], F32)
    for hh in range(N_HEADS - 2, -1, -1):
        logg_rows = jnp.where((r_idx >> int(math.log2(CHUNK))) == hh, LOG_G[hh], logg_rows)
    dmat = jnp.exp(jnp.abs((r_idx & (CHUNK - 1)) - m_idx).astype(F32) * logg_rows)

    glr = jnp.where(lax.broadcasted_iota(jnp.int32, (tm, LANES), 1) < GLA_RANK, proj(W_MIX, LANES), 0.0)
    rq = rotary(proj(C_RQ, HK))
    qb[...] = rq.astype(BF16)
    qdb[...] = (rq.reshape(n_chunks, CHUNK, HK) * qdec).reshape(tm, HK).astype(BF16)
    gl = _dot(glr.astype(BF16), wgk_ref[...]) + bgk_ref[...]
    rk = rotary(proj(C_RK, HK)) * (HEAD_K ** -0.5)
    kb[...] = rk.astype(BF16)
    kkb[...] = (rk.reshape(n_chunks, CHUNK, HK) * kdec).reshape(tm, HK).astype(BF16)
    log_a = (jnp.minimum(gl, 0.0) - jnp.log1p(jnp.exp(-jnp.abs(gl)))) / GATE_NORM
    la_hi = log_a.astype(BF16)
    la_lo = (log_a - la_hi.astype(F32)).astype(BF16)
    vb[...] = proj(C_RV, HV).astype(BF16)
    bdtri = bdtri_ref[...]
    bcum = _dot(bdtri, la_hi) + _dot(bdtri, la_lo)
    gqb[...] = (proj(C_GQ, HK) * (HEAD_K ** -0.5)).astype(BF16)
    gvb[...] = proj(C_GV, HV).astype(BF16)
    gk = proj(C_GK, HK)
    b3 = bcum.reshape(n_chunks, CHUNK, HK)
    bl3 = b3[:, CHUNK - 1:CHUNK, :]
    gkkb[...] = (gk.reshape(n_chunks, CHUNK, HK) * jnp.exp(bl3 - b3)).reshape(tm, HK).astype(BF16)
    ga[...] = jnp.broadcast_to(jnp.exp(bl3), (n_chunks, SUBLANES, HK))

    def stack_masked(a, width):
        head = _head_of_lane(a.shape, width)
        zero = jnp.zeros_like(a)
        return jnp.concatenate([jnp.where(head == hh, a, zero) for hh in range(N_HEADS)], axis=0)

    def heads_to_rows(a):
        return jnp.concatenate([a[:, hh * HEAD_V:(hh + 1) * HEAD_V] for hh in range(N_HEADS)], axis=0)

    def rows_to_heads(a):
        return jnp.concatenate([a[hh * CHUNK:(hh + 1) * CHUNK, :] for hh in range(N_HEADS)], axis=1)

    chunk_rows = [slice(c * CHUNK, (c + 1) * CHUNK) for c in range(n_chunks)]
    probs = [(_dot_nt(stack_masked(qb[r, :], HEAD_K), kb[r, :]) * dmat).astype(BF16) for r in chunk_rows]
    inc_ret = [_dot_tn(heads_to_rows(vb[r, :]), stack_masked(kkb[r, :], HEAD_K)) for r in chunk_rows]
    inc_gla = [_dot_tn(heads_to_rows(gvb[r, :]), stack_masked(gkkb[r, :], HEAD_K)) for r in chunk_rows]
    rg = proj(C_RG, HV)
    gg = proj(C_GG, HV)
    s_in, g_out = [], []
    s_cur, g_cur = st_ret[...], st_gla[...]
    for c in range(n_chunks):
        if is_s is None:
            s_cur, g_cur = ssi_r_ref[c].T, ssi_g_ref[c].T
        else:
            s_cur = jnp.where(is_s, sin_r[c], s_cur)
            g_cur = jnp.where(is_s, sin_g[c], g_cur)
        s_in.append(s_cur.astype(BF16))
        s_cur = s_cur * cdec + inc_ret[c]
        g_cur = g_cur * ga[c][0:1, :] + inc_gla[c]
        g_out.append(g_cur.astype(BF16))
        sout_r[c] = s_cur
        sout_g[c] = g_cur
    st_ret[...] = s_cur
    st_gla[...] = g_cur
    for c, r in enumerate(chunk_rows):
        v = vb[r, :]
        intra = jnp.concatenate(
            [_dot(probs[c][hh * CHUNK:(hh + 1) * CHUNK, :], v[:, hh * HEAD_V:(hh + 1) * HEAD_V])
             for hh in range(N_HEADS)], axis=1)
        inter = rows_to_heads(_dot_nt(stack_masked(qdb[r, :], HEAD_K), s_in[c]))
        o_ret[r, :] = intra + inter
        o_gla[r, :] = rows_to_heads(_dot_nt(stack_masked(gqb[r, :], HEAD_K), g_out[c]))

    gnorm = gnorm_ref[...]
    orr = o_ret[...]
    ogg = o_gla[...]
    ret_parts, gla_parts = [], []
    for hh in range(N_HEADS):
        sl = slice(hh * HEAD_V, (hh + 1) * HEAD_V)
        oh = orr[:, sl]
        mu = jnp.mean(oh, axis=-1, keepdims=True)
        dev = oh - mu
        var = jnp.mean(dev * dev, axis=-1, keepdims=True)
        ret_parts.append(dev * lax.rsqrt(var + NORM_EPS))
        og = ogg[:, sl]
        gla_parts.append(og * lax.rsqrt(jnp.mean(og * og, axis=-1, keepdims=True) + NORM_EPS) * gnorm)
    o_r = jnp.concatenate(ret_parts, axis=1) * (rg * _sigmoid(rg))
    o_g = jnp.concatenate(gla_parts, axis=1) * (gg * _sigmoid(gg))
    merged = (_sigmoid(proj(C_Z, D_MODEL)) * _dot(o_r.astype(BF16), wbr_ref[...])
              + _sigmoid(proj(C_Z + D_MODEL, D_MODEL)) * _dot(o_g.astype(BF16), wbg_ref[...]))
    xn = x + _dot(merged.astype(BF16), wout_ref[...])
    xnew_ref[...] = xn

    if route:
        h2 = xn * lax.rsqrt(jnp.mean(xn * xn, axis=-1, keepdims=True) + NORM_EPS) * gffn_ref[...]
        h2_hi = h2.astype(BF16)
        h2_lo = (h2 - h2_hi.astype(F32)).astype(BF16)
        hi_both = _dot_nt(h2_hi, wrt_ref[...])
        logits = (hi_both[:, :ROUTER_LANES] + _dot_nt(h2_lo, wrt_ref[:ROUTER_LANES, :])
                  + hi_both[:, ROUTER_LANES:]) + brt_ref[...]
        lt = logits.T
        row8 = lax.broadcasted_iota(jnp.int32, (SUBLANES, tm), 0)
        neg_inf = jnp.float32(-jnp.inf)
        glog = jnp.where(row8 < N_GROUPS, lt[0:SUBLANES, :], neg_inf)
        gmax = jnp.max(glog, axis=0, keepdims=True)
        grp = jnp.min(jnp.where(glog == gmax, row8, SUBLANES), axis=0, keepdims=True)
        p_grp = 1.0 / jnp.sum(jnp.exp(glog - gmax), axis=0, keepdims=True)
        le = jnp.zeros((SUBLANES, tm), F32)
        for g in range(N_GROUPS):
            le = jnp.where(grp == g, lt[SUBLANES * (g + 1):SUBLANES * (g + 2), :], le)
        m1 = jnp.max(le, axis=0, keepdims=True)
        i1 = jnp.min(jnp.where(le == m1, row8, SUBLANES), axis=0, keepdims=True)
        le2 = jnp.where(row8 == i1, neg_inf, le)
        m2 = jnp.max(le2, axis=0, keepdims=True)
        i2 = jnp.min(jnp.where(le2 == m2, row8, SUBLANES), axis=0, keepdims=True)
        e0 = grp * EXPERTS_PER_GROUP + i1
        e1 = grp * EXPERTS_PER_GROUP + i2
        t21 = jnp.exp(m2 - m1)
        w0 = p_grp / (1.0 + t21)
        w1 = p_grp * t21 / (1.0 + t21)

        erow = lax.broadcasted_iota(jnp.int32, (N_EXPERTS, tm), 0)
        hit0 = erow == e0
        hit1 = erow == e1
        onehot = jnp.where(jnp.logical_or(hit0, hit1), 1.0, 0.0).astype(BF16)
        cum = _dot_nt(onehot, ltri_ref[...])
        n_run = cum[:, tm - 1:tm]
        n_pad = jnp.ceil(n_run / UNIT) * UNIT
        rank0 = jnp.sum(jnp.where(hit0, cum - 1.0, 0.0), axis=0, keepdims=True)
        rank1 = jnp.sum(jnp.where(hit1, cum - 1.0, 0.0), axis=0, keepdims=True)
        start0 = jnp.sum(jnp.where(erow < e0, n_pad, 0.0), axis=0, keepdims=True)
        start1 = jnp.sum(jnp.where(erow < e1, n_pad, 0.0), axis=0, keepdims=True)
        ld0 = (start0 + rank0).astype(jnp.int32)
        ld1 = (start1 + rank1).astype(jnp.int32)
        lrow = lax.broadcasted_iota(jnp.int32, (LOCAL_ROWS, tm), 0)
        perm = jnp.where(jnp.logical_or(lrow == ld0, lrow == ld1), 1.0, 0.0).astype(BF16)
        xloc_ref[...] = _pack_halves(_dot(perm, h2_hi))
        lane_e = lax.broadcasted_iota(jnp.int32, (N_EXPERTS, LANES), 1)
        erow_l = lax.broadcasted_iota(jnp.int32, (N_EXPERTS, LANES), 0)
        units_row = jnp.sum(jnp.where(erow_l == lane_e, n_pad / UNIT, 0.0), axis=0, keepdims=True)
        ku_ref[...] = jnp.broadcast_to(units_row, (SUBLANES, LANES)).astype(jnp.int32).reshape(ku_ref.shape)
        zero_row = jnp.zeros((1, tm), jnp.int32)
        rec = jnp.concatenate([e0, e1, ld0, ld1, pltpu.bitcast(w0, jnp.int32), pltpu.bitcast(w1, jnp.int32),
                               zero_row, zero_row], axis=0)
        route_ref[...] = rec.reshape(route_ref.shape)
    else:
        xloc_ref[...] = jnp.zeros(xloc_ref.shape, jnp.int32)
        route_ref[...] = jnp.zeros(route_ref.shape, jnp.int32)
        ku_ref[...] = jnp.zeros(ku_ref.shape, jnp.int32)

    def stream_states_out():
        for c in range(n_chunks):
            sso_r_ref[c] = sout_r[c].T
            sso_g_ref[c] = sout_g[c].T

    if is_s is None:
        stream_states_out()
    else:
        pl.when(is_s)(stream_states_out)

    if tiles_p:
        @pl.when(jnp.logical_and(jnp.logical_not(is_s), t_idx == tiles_per_row - 1))
        def _():
            spo_r_ref[0] = st_ret[...].T
            spo_g_ref[0] = st_gla[...].T
    else:
        spo_r_ref[0] = st_ret[...].T
        spo_g_ref[0] = st_gla[...].T


def _const_spec(shape):
    nd = len(shape)
    return pl.BlockSpec(shape, lambda *_: (0,) * nd, pipeline_mode=pl.Buffered(1))


def _mixer_call(x_prompt, x_streams, cos, sin, st_prompt, st_streams, weights, consts, *, n_chunks, route):
    tm = n_chunks * CHUNK
    n_streams = x_streams.shape[0]
    tiles_s = n_streams // n_chunks
    if x_prompt is not None:
        n_rows, n_seq, _ = x_prompt.shape
        tpr = n_seq // tm
        tiles_p = n_rows * tpr
    else:
        n_rows, tpr, tiles_p = 1, 1, 0
    n_tiles = tiles_p + tiles_s
    p_idx = lambda i: jnp.minimum(i, tiles_p - 1)
    s_idx = lambda i: jnp.maximum(i - tiles_p, 0)

    st_blk = (1, HK, HEAD_V)
    in_specs, args = [], []
    if tiles_p:
        in_specs.append(pl.BlockSpec((1, tm, D_MODEL), lambda i: (p_idx(i) // tpr, p_idx(i) % tpr, 0)))
        args.append(x_prompt)
        cos_spec = pl.BlockSpec((tm, LANES), lambda i: (jnp.where(i < tiles_p, i % tpr, tpr), 0))
    else:
        cos_spec = pl.BlockSpec((tm, LANES), lambda i: (0, 0))
    once = pl.Buffered(1)
    stream_in = pl.BlockSpec((n_chunks, HK, HEAD_V), lambda i: (s_idx(i), 0, 0), pipeline_mode=once)
    in_specs += [pl.BlockSpec((n_chunks, CHUNK, D_MODEL), lambda i: (s_idx(i), 0, 0), pipeline_mode=once),
                 cos_spec, cos_spec, _const_spec(st_blk), _const_spec(st_blk), stream_in, stream_in]
    args += [x_streams, cos, sin, st_prompt[0], st_prompt[1], st_streams[0], st_streams[1]]
    in_specs += [_const_spec(w.shape) for w in weights] + [_const_spec(c.shape) for c in consts]
    args += list(weights) + list(consts)

    out_shape = (jax.ShapeDtypeStruct((n_tiles * tm, D_MODEL), F32),
                 jax.ShapeDtypeStruct((n_tiles * LOCAL_ROWS, D_MODEL // 2), jnp.int32),
                 jax.ShapeDtypeStruct((n_tiles, ROUTE_ROWS, tm), jnp.int32),
                 jax.ShapeDtypeStruct((n_tiles, SUBLANES, LANES), jnp.int32),
                 jax.ShapeDtypeStruct((n_rows, HK, HEAD_V), F32), jax.ShapeDtypeStruct((n_rows, HK, HEAD_V), F32),
                 jax.ShapeDtypeStruct((n_streams, HK, HEAD_V), F32),
                 jax.ShapeDtypeStruct((n_streams, HK, HEAD_V), F32))
    row_spec = pl.BlockSpec(st_blk, lambda i: (p_idx(i) // tpr if tiles_p else 0, 0, 0))
    stream_spec = pl.BlockSpec((n_chunks, HK, HEAD_V), lambda i: (s_idx(i), 0, 0))
    out_specs = (pl.BlockSpec((tm, D_MODEL), lambda i: (i, 0)),
                 pl.BlockSpec((LOCAL_ROWS, D_MODEL // 2), lambda i: (i, 0)),
                 pl.BlockSpec((1, ROUTE_ROWS, tm), lambda i: (i, 0, 0)),
                 pl.BlockSpec((1, SUBLANES, LANES), lambda i: (i, 0, 0)),
                 row_spec, row_spec, stream_spec, stream_spec)

    scratch = [pltpu.VMEM((tm, HK), BF16), pltpu.VMEM((tm, HK), BF16), pltpu.VMEM((tm, HK), BF16),
               pltpu.VMEM((tm, HK), BF16), pltpu.VMEM((tm, HV), BF16),
               pltpu.VMEM((tm, HK), BF16), pltpu.VMEM((tm, HK), BF16), pltpu.VMEM((tm, HV), BF16),
               pltpu.VMEM((n_chunks, SUBLANES, HK), F32),
               pltpu.VMEM((tm, HV), F32), pltpu.VMEM((tm, HV), F32),
               pltpu.VMEM((HEAD_V, HK), F32), pltpu.VMEM((HEAD_V, HK), F32)]
    scratch += [pltpu.VMEM((n_chunks, HEAD_V, HK), F32)] * 4

    return pl.pallas_call(
        functools.partial(_mixer_kernel, n_chunks=n_chunks, tiles_p=tiles_p, tiles_per_row=tpr, route=route),
        grid=(n_tiles,), in_specs=in_specs, out_specs=out_specs, out_shape=out_shape, scratch_shapes=scratch,
        compiler_params=pltpu.CompilerParams(dimension_semantics=("arbitrary",), vmem_limit_bytes=VMEM_LIMIT),
        name="mixer" if route else "mixer_meta",
    )(*args)


def _plan_kernel(ku_ref, src_ref, blk_ref, run_first, run_step, next_unit, seg_start, seg_units, *, n_tiles, n_blocks):
    shift = int(math.log2(BLOCK_UNITS))
    group_shift = int(math.log2(SUBLANES * LANES))

    def init_tile(t, c):
        next_unit[t] = t * LOCAL_UNITS
        return c

    lax.fori_loop(0, n_tiles, init_tile, 0)

    def init_block(b, c):
        blk_ref[0, b] = N_EXPERTS - 1
        blk_ref[1, b] = 0
        blk_ref[2, b] = 0
        blk_ref[3, b] = -1
        blk_ref[4, b] = 0
        return c

    lax.fori_loop(0, n_blocks, init_block, 0)

    def per_expert(e, carry):
        g0, position = carry

        def per_tile(t, c):
            units, d_prev = c
            base = next_unit[t]
            d = base - units
            run_first[e * n_tiles + t] = units
            run_step[e * n_tiles + t] = d - d_prev
            k = ku_ref[t, e]
            next_unit[t] = base + k
            return units + k, d

        units, _ = lax.fori_loop(0, n_tiles, per_tile, (0, 0), unroll=4)
        g_pad = g0 + (((units + (BLOCK_UNITS - 1)) >> shift) << shift)
        seg_start[e] = g0
        seg_units[e] = units

        def set_block(b, c):
            blk_ref[0, b] = e
            blk_ref[1, b] = jnp.minimum(g0 + units - (b << shift), BLOCK_UNITS)
            blk_ref[3, b] = g_pad >> shift
            blk_ref[4, b] = position & 1
            return c

        lax.fori_loop(g0 >> shift, g_pad >> shift, set_block, 0)
        return g_pad, position + (units > 0).astype(jnp.int32)

    g_total, _ = lax.fori_loop(0, N_EXPERTS, per_expert, (0, 0))
    blk_ref[2, 0] = g_total >> shift

    src_ref[...] = jnp.zeros(src_ref.shape, jnp.int32)
    in_group = (lax.broadcasted_iota(jnp.int32, (SUBLANES, LANES), 0) * LANES
                + lax.broadcasted_iota(jnp.int32, (SUBLANES, LANES), 1))

    def expert_units(e, c):
        g0 = seg_start[e]
        units = seg_units[e]
        g_pad = g0 + (((units + (BLOCK_UNITS - 1)) >> shift) << shift)

        def per_group(grp, c2):
            g = in_group + (grp << group_shift)
            o = g - g0
            mine = jnp.logical_and(o >= 0, g < g_pad)
            o_eff = jnp.where(o < units, o, (o >> shift) << shift)

            def per_tile(t, acc):
                return acc + jnp.where(o_eff >= run_first[e * n_tiles + t], run_step[e * n_tiles + t], 0)

            offset = lax.fori_loop(0, n_tiles, per_tile, jnp.zeros((SUBLANES, LANES), jnp.int32), unroll=4)
            rows = pl.ds(pl.multiple_of(grp * SUBLANES, SUBLANES), SUBLANES)
            src_ref[rows, :] = jnp.where(mine, o_eff + offset, src_ref[rows, :])
            return c2

        lax.fori_loop(g0 >> group_shift, (g_pad + (SUBLANES * LANES - 1)) >> group_shift, per_group, 0)
        return c

    lax.fori_loop(0, N_EXPERTS, expert_units, 0)


def _plan_call(ku, n_blocks):
    n_tiles = ku.shape[0]
    smem = pl.BlockSpec(memory_space=pltpu.SMEM)
    group = SUBLANES * LANES
    src_rows = -(-n_blocks * BLOCK_UNITS // group) * SUBLANES
    src, blk = pl.pallas_call(
        functools.partial(_plan_kernel, n_tiles=n_tiles, n_blocks=n_blocks),
        in_specs=[smem], out_specs=(pl.BlockSpec(memory_space=pltpu.VMEM), smem),
        out_shape=(jax.ShapeDtypeStruct((src_rows, LANES), jnp.int32),
                   jax.ShapeDtypeStruct((BLK_ROWS, n_blocks), jnp.int32)),
        scratch_shapes=[pltpu.SMEM((N_EXPERTS * n_tiles,), jnp.int32), pltpu.SMEM((N_EXPERTS * n_tiles,), jnp.int32),
                        pltpu.SMEM((n_tiles,), jnp.int32), pltpu.SMEM((N_EXPERTS,), jnp.int32),
                        pltpu.SMEM((N_EXPERTS,), jnp.int32)],
        name="moe_plan",
    )(ku)
    return src.reshape(-1), blk


def _experts_kernel(src_ref, blk_ref, x_hbm, wg_hbm, wu_hbm, wd_hbm, y_hbm,
                    xin, yout, wgf, wuf, wdf, wgub, wdb, sem_in, sem_out, sem_w):
    b = pl.program_id(0)
    used = blk_ref[2, 0]
    slot = b & 1

    def weight_copies(e, s):
        return [pltpu.make_async_copy(hbm.at[e], buf.at[s], sem_w.at[s])
                for hbm, buf in ((wg_hbm, wgf), (wu_hbm, wuf), (wd_hbm, wdf))]

    def unit_rows(blk, u):
        return pl.ds(pl.multiple_of(src_ref[blk * BLOCK_UNITS + u] * UNIT, UNIT), UNIT)

    def in_copy(blk, u, s):
        return pltpu.make_async_copy(x_hbm.at[unit_rows(blk, u), :], xin.at[s, pl.ds(u * UNIT, UNIT), :],
                                     sem_in.at[s])

    def out_copy(blk, u, s):
        return pltpu.make_async_copy(yout.at[s, pl.ds(u * UNIT, UNIT), :], y_hbm.at[unit_rows(blk, u), :],
                                     sem_out.at[s])

    @pl.when(jnp.logical_and(b == 0, used > 0))
    def _():
        for copy in weight_copies(blk_ref[0, 0], blk_ref[4, 0]):
            copy.start()
        for u in range(BLOCK_UNITS):
            in_copy(0, u, 0).start()

    @pl.when(b < used)
    def _():
        expert = blk_ref[0, b]
        changed = jnp.logical_or(b == 0, expert != blk_ref[0, jnp.maximum(b - 1, 0)])

        @pl.when(changed)
        def _():
            wslot = blk_ref[4, b]
            later = blk_ref[3, b]
            following = jnp.where(later < used, blk_ref[0, jnp.minimum(later, pl.num_programs(0) - 1)], -1)
            for copy in weight_copies(expert, wslot):
                copy.wait()

            @pl.when(following >= 0)
            def _():
                for copy in weight_copies(following, 1 - wslot):
                    copy.start()

            wgub[:, :D_EXPERT] = wgf[wslot].astype(BF16)
            wgub[:, D_EXPERT:] = wuf[wslot].astype(BF16)
            wdb[...] = wdf[wslot].astype(BF16)

        @pl.when(b + 1 < used)
        def _():
            for u in range(BLOCK_UNITS):
                in_copy(b + 1, u, 1 - slot).start()

        pltpu.make_async_copy(x_hbm.at[pl.ds(0, MOE_BLOCK), :], xin.at[slot], sem_in.at[slot]).wait()

        n_real = blk_ref[1, b]

        def mlp(rows):
            xb = _unpack_halves(xin[slot, :rows, :])
            gate_up = _dot(xb, wgub[...])
            gate = gate_up[:, :D_EXPERT]
            hid = (gate * _sigmoid(gate)) * gate_up[:, D_EXPERT:]
            yout[slot, :rows, :] = _pack_halves(_dot(hid.astype(BF16), wdb[...]).astype(BF16).astype(F32))

        @pl.when(n_real > BLOCK_UNITS // 2)
        def _():
            mlp(MOE_BLOCK)

        @pl.when(n_real <= BLOCK_UNITS // 2)
        def _():
            mlp(MOE_BLOCK // 2)


        @pl.when(n_real == BLOCK_UNITS)
        def _():
            for u in range(BLOCK_UNITS):
                out_copy(b, u, slot).start()

        @pl.when(n_real < BLOCK_UNITS)
        def _():
            lax.fori_loop(0, n_real, lambda u, c: (out_copy(b, u, slot).start(), c)[1], 0)

        @pl.when(jnp.logical_and(b > 0, blk_ref[1, jnp.maximum(b - 1, 0)] == BLOCK_UNITS))
        def _():
            pltpu.make_async_copy(yout.at[1 - slot], y_hbm.at[pl.ds(0, MOE_BLOCK), :], sem_out.at[1 - slot]).wait()

        @pl.when(jnp.logical_and(b > 0, blk_ref[1, jnp.maximum(b - 1, 0)] < BLOCK_UNITS))
        def _():
            lax.fori_loop(0, blk_ref[1, b - 1], lambda u, c: (out_copy(b - 1, u, 1 - slot).wait(), c)[1], 0)

        @pl.when(b == used - 1)
        def _():
            lax.fori_loop(0, n_real, lambda u, c: (out_copy(b, u, slot).wait(), c)[1], 0)


def _experts_call(src, blk, xloc, wg, wu, wd, n_blocks):
    hbm = pl.BlockSpec(memory_space=pl.ANY)
    grid_spec = pltpu.PrefetchScalarGridSpec(
        num_scalar_prefetch=2,
        grid=(n_blocks,),
        in_specs=[hbm, hbm, hbm, hbm],
        out_specs=hbm,
        scratch_shapes=[pltpu.VMEM((2, MOE_BLOCK, D_MODEL // 2), jnp.int32),
                        pltpu.VMEM((2, MOE_BLOCK, D_MODEL // 2), jnp.int32),
                        pltpu.VMEM((2, D_MODEL, D_EXPERT), F32), pltpu.VMEM((2, D_MODEL, D_EXPERT), F32),
                        pltpu.VMEM((2, D_EXPERT, D_MODEL), F32),
                        pltpu.VMEM((D_MODEL, 2 * D_EXPERT), BF16), pltpu.VMEM((D_EXPERT, D_MODEL), BF16),
                        pltpu.SemaphoreType.DMA((2,)), pltpu.SemaphoreType.DMA((2,)),
                        pltpu.SemaphoreType.DMA((2,))],
    )
    return pl.pallas_call(
        _experts_kernel, grid_spec=grid_spec,
        out_shape=jax.ShapeDtypeStruct(xloc.shape, jnp.int32),
        input_output_aliases={2: 0},
        compiler_params=pltpu.CompilerParams(dimension_semantics=("arbitrary",), vmem_limit_bytes=VMEM_LIMIT),
        name="moe_experts",
    )(src, blk, xloc, wg, wu, wd)


def _combine_kernel(route_ref, x_ref, y_ref, gfin_ref, out_a_ref, out_b_ref, *, tm, tiles_a):
    t = pl.program_id(0)
    rec = route_ref[0]
    recf = jnp.concatenate([rec[2:4, :].astype(F32), pltpu.bitcast(rec, F32)[4:6, :],
                            jnp.zeros((LANES - 4, tm), F32)], axis=0)
    cols = recf.T
    lrow = lax.broadcasted_iota(jnp.int32, (tm, LOCAL_ROWS), 1).astype(F32)
    select = jnp.where(lrow == cols[:, 0:1], cols[:, 2:3], jnp.where(lrow == cols[:, 1:2], cols[:, 3:4], 0.0))
    xf = x_ref[...] + _dot(select.astype(BF16), _unpack_halves(y_ref[...]))
    out = xf * lax.rsqrt(jnp.mean(xf * xf, axis=-1, keepdims=True) + NORM_EPS) * gfin_ref[...]

    @pl.when(t < tiles_a)
    def _():
        out_a_ref[...] = out

    @pl.when(t >= tiles_a)
    def _():
        out_b_ref[...] = out


def _combine_call(route, xnew, yloc, gfin, tm, tiles_a):
    n_tiles = route.shape[0]
    tiles_b = n_tiles - tiles_a
    return pl.pallas_call(
        functools.partial(_combine_kernel, tm=tm, tiles_a=tiles_a),
        grid=(n_tiles,),
        in_specs=[pl.BlockSpec((1, ROUTE_ROWS, tm), lambda t: (t, 0, 0)),
                  pl.BlockSpec((tm, D_MODEL), lambda t: (t, 0)),
                  pl.BlockSpec((LOCAL_ROWS, D_MODEL // 2), lambda t: (t, 0)),
                  pl.BlockSpec((1, D_MODEL), lambda t: (0, 0))],
        out_specs=(pl.BlockSpec((tm, D_MODEL), lambda t: (jnp.minimum(t, tiles_a - 1), 0)),
                   pl.BlockSpec((tm, D_MODEL), lambda t: (jnp.maximum(t - tiles_a, 0), 0))),
        out_shape=(jax.ShapeDtypeStruct((tiles_a * tm, D_MODEL), F32),
                   jax.ShapeDtypeStruct((tiles_b * tm, D_MODEL), F32)),
        compiler_params=pltpu.CompilerParams(dimension_semantics=("arbitrary",), vmem_limit_bytes=VMEM_LIMIT),
        name="moe_combine",
    )(route, xnew, yloc, gfin)


def _prep_weights(g_mix, w_in, w_gla_gk2, b_gla_gk, g_gla_norm, w_br_ret, w_br_gla, w_out, g_ffn,
                  w_rg, b_rg, w_re, b_re):
    w_int = jnp.swapaxes(w_in, 0, 1).astype(BF16)
    wgk = jnp.pad(w_gla_gk2, ((0, LANES - GLA_RANK), (0, 0))).astype(BF16)
    def router_rows(groups, experts):
        return jnp.concatenate([groups, jnp.zeros((SUBLANES - N_GROUPS,) + groups.shape[1:], F32), experts,
                                jnp.zeros((ROUTER_LANES - SUBLANES - N_EXPERTS,) + experts.shape[1:], F32)], axis=0)

    wrt = router_rows(w_rg.T, w_re.T)
    brt = router_rows(b_rg, b_re).reshape(1, ROUTER_LANES)
    wrt_hi = wrt.astype(BF16)
    wrt = jnp.concatenate([wrt_hi, (wrt - wrt_hi.astype(F32)).astype(BF16)], axis=0)
    return (g_mix.reshape(1, D_MODEL), w_int, wgk, b_gla_gk.reshape(1, HK), g_gla_norm.reshape(1, HEAD_V),
            w_br_ret.astype(BF16), w_br_gla.astype(BF16), w_out.astype(BF16), g_ffn.reshape(1, D_MODEL), wrt, brt)


def _tri_consts(tm):
    r = np.arange(tm)
    low = (r[None, :] <= r[:, None])
    bd = low & ((r[None, :] // CHUNK) == (r[:, None] // CHUNK))
    return jnp.asarray(bd, BF16), jnp.asarray(low, BF16)


def _state_to_kernel(s):
    return s.reshape(s.shape[0], HK, HEAD_V)


def _state_from_kernel(s):
    return s.reshape(1, s.shape[0], N_HEADS, HEAD_K, HEAD_V)


def kernel(x_prompt, x_sample, state_ret, state_gla, meta_tokens, g_mix, w_in, w_gla_gk2, b_gla_gk, g_gla_norm, w_br_ret, w_br_gla, w_out, g_ffn, w_router_group, b_router_group, w_router_expert, b_router_expert, w_exp_gate, w_exp_up, w_exp_down, g_final):
    n_b, seq, _ = x_prompt.shape
    n_s, dec_seq, _ = x_sample.shape
    depth = state_ret.shape[0]
    assert depth == 1 and dec_seq == CHUNK and seq % TILE_ROWS == 0 and n_s % TILE_CHUNKS == 0
    cos_t, sin_t, cos_m, sin_m = _rotary_tables(seq, PAST_LEN)
    weights = _prep_weights(g_mix[0], w_in[0], w_gla_gk2[0], b_gla_gk[0], g_gla_norm[0], w_br_ret[0], w_br_gla[0],
                            w_out[0], g_ffn[0], w_router_group[0], b_router_group[0], w_router_expert[0],
                            b_router_expert[0])
    zero_state = jnp.zeros((1, HK, HEAD_V), F32)

    x_meta = jnp.concatenate([jnp.zeros((CHUNK - N_META, D_MODEL), F32), meta_tokens.astype(F32)], axis=0)
    meta_out = _mixer_call(None, x_meta.reshape(1, CHUNK, D_MODEL), cos_m, sin_m, (zero_state, zero_state),
                           (zero_state, zero_state), weights, _tri_consts(CHUNK), n_chunks=1, route=False)
    meta_ret, meta_gla = meta_out[6], meta_out[7]

    xnew, xloc, route, ku, ret_p, gla_p, ret_s, gla_s = _mixer_call(
        x_prompt, x_sample, cos_t, sin_t, (meta_ret, meta_gla),
        (_state_to_kernel(state_ret[0]), _state_to_kernel(state_gla[0])), weights, _tri_consts(TILE_ROWS),
        n_chunks=TILE_CHUNKS, route=True)

    n_tiles = route.shape[0]
    n_blocks = n_tiles * LOCAL_UNITS // BLOCK_UNITS + N_EXPERTS
    src, blk = _plan_call(ku[:, 0, :N_EXPERTS], n_blocks)
    yloc = _experts_call(src, blk, xloc, w_exp_gate[0], w_exp_up[0], w_exp_down[0], n_blocks)
    y_p, y_s = _combine_call(route, xnew, yloc, g_final.reshape(1, D_MODEL), TILE_ROWS, n_b * seq // TILE_ROWS)

    return (y_p.reshape(n_b, seq, D_MODEL), y_s.reshape(n_s, dec_seq, D_MODEL),
            _state_from_kernel(ret_p), _state_from_kernel(gla_p), _state_from_kernel(ret_s), _state_from_kernel(gla_s))
```

```python
import functools
import math

import jax
import jax.numpy as jnp
import numpy as np
from jax import lax
from jax.experimental import pallas as pl
from jax.experimental.pallas import tpu as pltpu

D_MODEL = 1024
CHUNK = 64
PAST_LEN = 1024
N_META = 16
N_HEADS = 4
HEAD_K = 64
HEAD_V = 128
HK = N_HEADS * HEAD_K
HV = N_HEADS * HEAD_V
GLA_RANK = 16
GATE_NORM = 16.0
ROPE_BASE = 10000.0
N_GROUPS = 4
EXPERTS_PER_GROUP = 8
N_EXPERTS = N_GROUPS * EXPERTS_PER_GROUP
D_EXPERT = 512
NORM_EPS = 1e-6

LANES = 128
SUBLANES = 8
TILE_CHUNKS = 8
TILE_ROWS = TILE_CHUNKS * CHUNK
ROUTE_ROWS = 8
ROUTER_LANES = 128
UNIT = SUBLANES
MOE_BLOCK = 512
BLOCK_UNITS = MOE_BLOCK // UNIT
BLK_ROWS = 5
LOCAL_ROWS = 2 * TILE_ROWS + N_EXPERTS * UNIT
LOCAL_UNITS = LOCAL_ROWS // UNIT
VMEM_LIMIT = 56 * 1024 * 1024

C_RQ, C_RK, C_RV, C_RG = 0, 256, 512, 1024
C_GQ, C_GK, C_GV, C_GG = 1536, 1792, 2048, 2560
W_MIX = 3072
C_Z = W_MIX + GLA_RANK

F32 = jnp.float32
BF16 = jnp.bfloat16
LOG_G = tuple(math.log1p(-(2.0 ** (-5.0 - h))) for h in range(N_HEADS))


def _dot(a, b):
    return jnp.dot(a, b, preferred_element_type=F32)


def _dot_nt(a, b):
    return lax.dot_general(a, b, (((1,), (1,)), ((), ())), preferred_element_type=F32)


def _dot_tn(a, b):
    return lax.dot_general(a, b, (((0,), (0,)), ((), ())), preferred_element_type=F32)


def _sigmoid(x):
    return 1.0 / (1.0 + jnp.exp(-x))


def _pack_halves(x):
    half = x.shape[1] // 2
    bits = pltpu.bitcast(x, jnp.int32)
    return lax.shift_right_logical(bits[:, :half], 16) | (bits[:, half:] & jnp.int32(-65536))


def _unpack_halves(w):
    lo = pltpu.bitcast(lax.shift_left(w, 16), F32)
    hi = pltpu.bitcast(w & jnp.int32(-65536), F32)
    return jnp.concatenate([lo, hi], axis=1).astype(BF16)


def _head_of_lane(shape, width):
    return lax.broadcasted_iota(jnp.int32, shape, len(shape) - 1) >> int(math.log2(width))


def _per_head_lane_const(vals, shape, width):
    hd = _head_of_lane(shape, width)
    out = jnp.full(shape, vals[N_HEADS - 1], F32)
    for h in range(N_HEADS - 2, -1, -1):
        out = jnp.where(hd == h, vals[h], out)
    return out


def _tables_kernel(inv_ref, cp_ref, sp_ref, cm_ref, sm_ref, *, seq, past_len):
    inv = inv_ref[...].reshape(1, 1, LANES)
    lane = lax.broadcasted_iota(jnp.int32, (1, 1, LANES), 2)
    sign = jnp.where((lane & (HEAD_K - 1)) < (HEAD_K // 2), -1.0, 1.0)
    off = lax.broadcasted_iota(jnp.int32, (1, CHUNK, LANES), 1).astype(F32) * inv
    c_off, s_off = jnp.cos(off), jnp.sin(off)

    def chunks(n, first_pos):
        base = (lax.broadcasted_iota(jnp.int32, (n, 1, LANES), 0) * CHUNK + first_pos).astype(F32) * inv
        c_base, s_base = jnp.cos(base), jnp.sin(base)
        cos = (c_base * c_off - s_base * s_off).reshape(n * CHUNK, LANES)
        sin = ((s_base * c_off + c_base * s_off) * sign).reshape(n * CHUNK, LANES)
        return cos, sin

    cp_ref[0:seq, :], sp_ref[0:seq, :] = chunks(seq // CHUNK, 0)
    cos_s, sin_s = chunks(1, past_len)
    for c in range(TILE_CHUNKS):
        cp_ref[seq + c * CHUNK:seq + (c + 1) * CHUNK, :] = cos_s
        sp_ref[seq + c * CHUNK:seq + (c + 1) * CHUNK, :] = sin_s
    cm_ref[...], sm_ref[...] = chunks(1, -CHUNK)


def _rotary_tables(seq, past_len):
    half = HEAD_K // 2
    inv = ROPE_BASE ** (-2.0 * jnp.arange(half, dtype=F32) / HEAD_K)
    inv = jnp.tile(inv, LANES // half).reshape(1, LANES)
    shp = lambda r: jax.ShapeDtypeStruct((r, LANES), F32)
    return pl.pallas_call(
        functools.partial(_tables_kernel, seq=seq, past_len=past_len),
        out_shape=(shp(seq + TILE_ROWS), shp(seq + TILE_ROWS), shp(CHUNK), shp(CHUNK)),
        compiler_params=pltpu.CompilerParams(vmem_limit_bytes=VMEM_LIMIT),
        name="rotary_tables",
    )(inv)


def _mixer_kernel(*refs, n_chunks, tiles_p, tiles_per_row, route):
    tm = n_chunks * CHUNK
    if tiles_p:
        xp_ref, refs = refs[0], refs[1:]
    (xs_ref, cos_ref, sin_ref, spi_r_ref, spi_g_ref, ssi_r_ref, ssi_g_ref,
     gmix_ref, wint_ref, wgk_ref, bgk_ref, gnorm_ref, wbr_ref, wbg_ref, wout_ref,
     gffn_ref, wrt_ref, brt_ref, bdtri_ref, ltri_ref,
     xnew_ref, xloc_ref, route_ref, ku_ref, spo_r_ref, spo_g_ref, sso_r_ref, sso_g_ref,
     qb, qdb, kb, kkb, vb, gqb, gkkb, gvb, ga, o_ret, o_gla, st_ret, st_gla,
     sin_r, sin_g, sout_r, sout_g) = refs

    i = pl.program_id(0)
    if tiles_p:
        is_s = i >= tiles_p
        t_idx = jnp.minimum(i, tiles_p - 1) % tiles_per_row
        x = jnp.where(is_s, xs_ref[...].reshape(tm, D_MODEL), xp_ref[...].reshape(tm, D_MODEL))

        @pl.when(jnp.logical_and(jnp.logical_not(is_s), t_idx == 0))
        def _():
            st_ret[...] = spi_r_ref[0].T
            st_gla[...] = spi_g_ref[0].T

        @pl.when(i == 0)
        def _():
            sin_r[...] = jnp.zeros(sin_r.shape, F32)
            sin_g[...] = jnp.zeros(sin_g.shape, F32)

        @pl.when(is_s)
        def _():
            for c in range(n_chunks):
                sin_r[c] = ssi_r_ref[c].T
                sin_g[c] = ssi_g_ref[c].T
    else:
        is_s = None
        x = xs_ref[...].reshape(tm, D_MODEL)

    h = x * lax.rsqrt(jnp.mean(x * x, axis=-1, keepdims=True) + NORM_EPS) * gmix_ref[...]
    hb = h.astype(BF16)

    def proj(c0, width):
        return _dot_nt(hb, wint_ref[c0:c0 + width, :])

    cos3 = jnp.concatenate([cos_ref[...]] * 2, axis=1).reshape(n_chunks, CHUNK, HK)
    sin3 = jnp.concatenate([sin_ref[...]] * 2, axis=1).reshape(n_chunks, CHUNK, HK)
    lane_hk = lax.broadcasted_iota(jnp.int32, (tm, HK), 1)
    first_half = (lane_hk & (HEAD_K - 1)) < (HEAD_K // 2)

    def rotary(t):
        swapped = jnp.where(first_half, pltpu.roll(t, HK - HEAD_K // 2, 1), pltpu.roll(t, HEAD_K // 2, 1))
        t3 = t.reshape(n_chunks, CHUNK, HK)
        return (t3 * cos3 + swapped.reshape(n_chunks, CHUNK, HK) * sin3).reshape(tm, HK)

    logg_hk = _per_head_lane_const(LOG_G, (CHUNK, HK), HEAD_K)
    l_idx = lax.broadcasted_iota(jnp.int32, (CHUNK, HK), 0).astype(F32)
    qdec = jnp.exp((l_idx + 1.0) * logg_hk)
    kdec = jnp.exp((CHUNK - 1.0 - l_idx) * logg_hk)
    cdec = jnp.exp(float(CHUNK) * _per_head_lane_const(LOG_G, (1, HK), HEAD_K))
    r_idx = lax.broadcasted_iota(jnp.int32, (N_HEADS * CHUNK, CHUNK), 0)
    m_idx = lax.broadcasted_iota(jnp.int32, (N_HEADS * CHUNK, CHUNK), 1)
    logg_rows = jnp.full((N_HEADS * CHUNK, CHUNK), LOG_G[N_HEADS - 1], F32)
    for hh in range(N_HEADS - 2, -1, -1):
        logg_rows = jnp.where((r_idx >> int(math.log2(CHUNK))) == hh, LOG_G[hh], logg_rows)
    dmat = jnp.exp(jnp.abs((r_idx & (CHUNK - 1)) - m_idx).astype(F32) * logg_rows)

    glr = jnp.where(lax.broadcasted_iota(jnp.int32, (tm, LANES), 1) < GLA_RANK, proj(W_MIX, LANES), 0.0)
    rq = rotary(proj(C_RQ, HK))
    qb[...] = rq.astype(BF16)
    qdb[...] = (rq.reshape(n_chunks, CHUNK, HK) * qdec).reshape(tm, HK).astype(BF16)
    gl = _dot(glr.astype(BF16), wgk_ref[...]) + bgk_ref[...]
    rk = rotary(proj(C_RK, HK)) * (HEAD_K ** -0.5)
    kb[...] = rk.astype(BF16)
    kkb[...] = (rk.reshape(n_chunks, CHUNK, HK) * kdec).reshape(tm, HK).astype(BF16)
    log_a = (jnp.minimum(gl, 0.0) - jnp.log1p(jnp.exp(-jnp.abs(gl)))) / GATE_NORM
    la_hi = log_a.astype(BF16)
    la_lo = (log_a - la_hi.astype(F32)).astype(BF16)
    vb[...] = proj(C_RV, HV).astype(BF16)
    bdtri = bdtri_ref[...]
    bcum = _dot(bdtri, la_hi) + _dot(bdtri, la_lo)
    gqb[...] = (proj(C_GQ, HK) * (HEAD_K ** -0.5)).astype(BF16)
    gvb[...] = proj(C_GV, HV).astype(BF16)
    gk = proj(C_GK, HK)
    b3 = bcum.reshape(n_chunks, CHUNK, HK)
    bl3 = b3[:, CHUNK - 1:CHUNK, :]
    gkkb[...] = (gk.reshape(n_chunks, CHUNK, HK) * jnp.exp(bl3 - b3)).reshape(tm, HK).astype(BF16)
    ga[...] = jnp.broadcast_to(jnp.exp(bl3), (n_chunks, SUBLANES, HK))

    def stack_masked(a, width):
        head = _head_of_lane(a.shape, width)
        zero = jnp.zeros_like(a)
        return jnp.concatenate([jnp.where(head == hh, a, zero) for hh in range(N_HEADS)], axis=0)

    def heads_to_rows(a):
        return jnp.concatenate([a[:, hh * HEAD_V:(hh + 1) * HEAD_V] for hh in range(N_HEADS)], axis=0)

    def rows_to_heads(a):
        return jnp.concatenate([a[hh * CHUNK:(hh + 1) * CHUNK, :] for hh in range(N_HEADS)], axis=1)

    chunk_rows = [slice(c * CHUNK, (c + 1) * CHUNK) for c in range(n_chunks)]
    probs = [(_dot_nt(stack_masked(qb[r, :], HEAD_K), kb[r, :]) * dmat).astype(BF16) for r in chunk_rows]
    inc_ret = [_dot_tn(heads_to_rows(vb[r, :]), stack_masked(kkb[r, :], HEAD_K)) for r in chunk_rows]
    inc_gla = [_dot_tn(heads_to_rows(gvb[r, :]), stack_masked(gkkb[r, :], HEAD_K)) for r in chunk_rows]
    rg = proj(C_RG, HV)
    gg = proj(C_GG, HV)
    s_in, g_out = [], []
    s_cur, g_cur = st_ret[...], st_gla[...]
    for c in range(n_chunks):
        if is_s is None:
            s_cur, g_cur = ssi_r_ref[c].T, ssi_g_ref[c].T
        else:
            s_cur = jnp.where(is_s, sin_r[c], s_cur)
            g_cur = jnp.where(is_s, sin_g[c], g_cur)
        s_in.append(s_cur.astype(BF16))
        s_cur = s_cur * cdec + inc_ret[c]
        g_cur = g_cur * ga[c][0:1, :] + inc_gla[c]
        g_out.append(g_cur.astype(BF16))
        sout_r[c] = s_cur
        sout_g[c] = g_cur
    st_ret[...] = s_cur
    st_gla[...] = g_cur
    for c, r in enumerate(chunk_rows):
        v = vb[r, :]
        intra = jnp.concatenate(
            [_dot(probs[c][hh * CHUNK:(hh + 1) * CHUNK, :], v[:, hh * HEAD_V:(hh + 1) * HEAD_V])
             for hh in range(N_HEADS)], axis=1)
        inter = rows_to_heads(_dot_nt(stack_masked(qdb[r, :], HEAD_K), s_in[c]))
        o_ret[r, :] = intra + inter
        o_gla[r, :] = rows_to_heads(_dot_nt(stack_masked(gqb[r, :], HEAD_K), g_out[c]))

    gnorm = gnorm_ref[...]
    orr = o_ret[...]
    ogg = o_gla[...]
    ret_parts, gla_parts = [], []
    for hh in range(N_HEADS):
        sl = slice(hh * HEAD_V, (hh + 1) * HEAD_V)
        oh = orr[:, sl]
        mu = jnp.mean(oh, axis=-1, keepdims=True)
        dev = oh - mu
        var = jnp.mean(dev * dev, axis=-1, keepdims=True)
        ret_parts.append(dev * lax.rsqrt(var + NORM_EPS))
        og = ogg[:, sl]
        gla_parts.append(og * lax.rsqrt(jnp.mean(og * og, axis=-1, keepdims=True) + NORM_EPS) * gnorm)
    o_r = jnp.concatenate(ret_parts, axis=1) * (rg * _sigmoid(rg))
    o_g = jnp.concatenate(gla_parts, axis=1) * (gg * _sigmoid(gg))
    merged = (_sigmoid(proj(C_Z, D_MODEL)) * _dot(o_r.astype(BF16), wbr_ref[...])
              + _sigmoid(proj(C_Z + D_MODEL, D_MODEL)) * _dot(o_g.astype(BF16), wbg_ref[...]))
    xn = x + _dot(merged.astype(BF16), wout_ref[...])
    xnew_ref[...] = xn

    if route:
        h2 = xn * lax.rsqrt(jnp.mean(xn * xn, axis=-1, keepdims=True) + NORM_EPS) * gffn_ref[...]
        h2_hi = h2.astype(BF16)
        h2_lo = (h2 - h2_hi.astype(F32)).astype(BF16)
        hi_both = _dot_nt(h2_hi, wrt_ref[...])
        logits = (hi_both[:, :ROUTER_LANES] + _dot_nt(h2_lo, wrt_ref[:ROUTER_LANES, :])
                  + hi_both[:, ROUTER_LANES:]) + brt_ref[...]
        lt = logits.T
        row8 = lax.broadcasted_iota(jnp.int32, (SUBLANES, tm), 0)
        neg_inf = jnp.float32(-jnp.inf)
        glog = jnp.where(row8 < N_GROUPS, lt[0:SUBLANES, :], neg_inf)
        gmax = jnp.max(glog, axis=0, keepdims=True)
        grp = jnp.min(jnp.where(glog == gmax, row8, SUBLANES), axis=0, keepdims=True)
        p_grp = 1.0 / jnp.sum(jnp.exp(glog - gmax), axis=0, keepdims=True)
        le = jnp.zeros((SUBLANES, tm), F32)
        for g in range(N_GROUPS):
            le = jnp.where(grp == g, lt[SUBLANES * (g + 1):SUBLANES * (g + 2), :], le)
        m1 = jnp.max(le, axis=0, keepdims=True)
        i1 = jnp.min(jnp.where(le == m1, row8, SUBLANES), axis=0, keepdims=True)
        le2 = jnp.where(row8 == i1, neg_inf, le)
        m2 = jnp.max(le2, axis=0, keepdims=True)
        i2 = jnp.min(jnp.where(le2 == m2, row8, SUBLANES), axis=0, keepdims=True)
        e0 = grp * EXPERTS_PER_GROUP + i1
        e1 = grp * EXPERTS_PER_GROUP + i2
        t21 = jnp.exp(m2 - m1)
        w0 = p_grp / (1.0 + t21)
        w1 = p_grp * t21 / (1.0 + t21)

        erow = lax.broadcasted_iota(jnp.int32, (N_EXPERTS, tm), 0)
        hit0 = erow == e0
        hit1 = erow == e1
        onehot = jnp.where(jnp.logical_or(hit0, hit1), 1.0, 0.0).astype(BF16)
        cum = _dot_nt(onehot, ltri_ref[...])
        n_run = cum[:, tm - 1:tm]
        n_pad = jnp.ceil(n_run / UNIT) * UNIT
        rank0 = jnp.sum(jnp.where(hit0, cum - 1.0, 0.0), axis=0, keepdims=True)
        rank1 = jnp.sum(jnp.where(hit1, cum - 1.0, 0.0), axis=0, keepdims=True)
        start0 = jnp.sum(jnp.where(erow < e0, n_pad, 0.0), axis=0, keepdims=True)
        start1 = jnp.sum(jnp.where(erow < e1, n_pad, 0.0), axis=0, keepdims=True)
        ld0 = (start0 + rank0).astype(jnp.int32)
        ld1 = (start1 + rank1).astype(jnp.int32)
        lrow = lax.broadcasted_iota(jnp.int32, (LOCAL_ROWS, tm), 0)
        perm = jnp.where(jnp.logical_or(lrow == ld0, lrow == ld1), 1.0, 0.0).astype(BF16)
        xloc_ref[...] = _pack_halves(_dot(perm, h2_hi))
        lane_e = lax.broadcasted_iota(jnp.int32, (N_EXPERTS, LANES), 1)
        erow_l = lax.broadcasted_iota(jnp.int32, (N_EXPERTS, LANES), 0)
        units_row = jnp.sum(jnp.where(erow_l == lane_e, n_pad / UNIT, 0.0), axis=0, keepdims=True)
        ku_ref[...] = jnp.broadcast_to(units_row, (SUBLANES, LANES)).astype(jnp.int32).reshape(ku_ref.shape)
        zero_row = jnp.zeros((1, tm), jnp.int32)
        rec = jnp.concatenate([e0, e1, ld0, ld1, pltpu.bitcast(w0, jnp.int32), pltpu.bitcast(w1, jnp.int32),
                               zero_row, zero_row], axis=0)
        route_ref[...] = rec.reshape(route_ref.shape)
    else:
        xloc_ref[...] = jnp.zeros(xloc_ref.shape, jnp.int32)
        route_ref[...] = jnp.zeros(route_ref.shape, jnp.int32)
        ku_ref[...] = jnp.zeros(ku_ref.shape, jnp.int32)

    def stream_states_out():
        for c in range(n_chunks):
            sso_r_ref[c] = sout_r[c].T
            sso_g_ref[c] = sout_g[c].T

    if is_s is None:
        stream_states_out()
    else:
        pl.when(is_s)(stream_states_out)

    if tiles_p:
        @pl.when(jnp.logical_and(jnp.logical_not(is_s), t_idx == tiles_per_row - 1))
        def _():
            spo_r_ref[0] = st_ret[...].T
            spo_g_ref[0] = st_gla[...].T
    else:
        spo_r_ref[0] = st_ret[...].T
        spo_g_ref[0] = st_gla[...].T


def _const_spec(shape):
    nd = len(shape)
    return pl.BlockSpec(shape, lambda *_: (0,) * nd, pipeline_mode=pl.Buffered(1))


def _mixer_call(x_prompt, x_streams, cos, sin, st_prompt, st_streams, weights, consts, *, n_chunks, route):
    tm = n_chunks * CHUNK
    n_streams = x_streams.shape[0]
    tiles_s = n_streams // n_chunks
    if x_prompt is not None:
        n_rows, n_seq, _ = x_prompt.shape
        tpr = n_seq // tm
        tiles_p = n_rows * tpr
    else:
        n_rows, tpr, tiles_p = 1, 1, 0
    n_tiles = tiles_p + tiles_s
    p_idx = lambda i: jnp.minimum(i, tiles_p - 1)
    s_idx = lambda i: jnp.maximum(i - tiles_p, 0)

    st_blk = (1, HK, HEAD_V)
    in_specs, args = [], []
    if tiles_p:
        in_specs.append(pl.BlockSpec((1, tm, D_MODEL), lambda i: (p_idx(i) // tpr, p_idx(i) % tpr, 0)))
        args.append(x_prompt)
        cos_spec = pl.BlockSpec((tm, LANES), lambda i: (jnp.where(i < tiles_p, i % tpr, tpr), 0))
    else:
        cos_spec = pl.BlockSpec((tm, LANES), lambda i: (0, 0))
    once = pl.Buffered(1)
    stream_in = pl.BlockSpec((n_chunks, HK, HEAD_V), lambda i: (s_idx(i), 0, 0), pipeline_mode=once)
    in_specs += [pl.BlockSpec((n_chunks, CHUNK, D_MODEL), lambda i: (s_idx(i), 0, 0), pipeline_mode=once),
                 cos_spec, cos_spec, _const_spec(st_blk), _const_spec(st_blk), stream_in, stream_in]
    args += [x_streams, cos, sin, st_prompt[0], st_prompt[1], st_streams[0], st_streams[1]]
    in_specs += [_const_spec(w.shape) for w in weights] + [_const_spec(c.shape) for c in consts]
    args += list(weights) + list(consts)

    out_shape = (jax.ShapeDtypeStruct((n_tiles * tm, D_MODEL), F32),
                 jax.ShapeDtypeStruct((n_tiles * LOCAL_ROWS, D_MODEL // 2), jnp.int32),
                 jax.ShapeDtypeStruct((n_tiles, ROUTE_ROWS, tm), jnp.int32),
                 jax.ShapeDtypeStruct((n_tiles, SUBLANES, LANES), jnp.int32),
                 jax.ShapeDtypeStruct((n_rows, HK, HEAD_V), F32), jax.ShapeDtypeStruct((n_rows, HK, HEAD_V), F32),
                 jax.ShapeDtypeStruct((n_streams, HK, HEAD_V), F32),
                 jax.ShapeDtypeStruct((n_streams, HK, HEAD_V), F32))
    row_spec = pl.BlockSpec(st_blk, lambda i: (p_idx(i) // tpr if tiles_p else 0, 0, 0))
    stream_spec = pl.BlockSpec((n_chunks, HK, HEAD_V), lambda i: (s_idx(i), 0, 0))
    out_specs = (pl.BlockSpec((tm, D_MODEL), lambda i: (i, 0)),
                 pl.BlockSpec((LOCAL_ROWS, D_MODEL // 2), lambda i: (i, 0)),
                 pl.BlockSpec((1, ROUTE_ROWS, tm), lambda i: (i, 0, 0)),
                 pl.BlockSpec((1, SUBLANES, LANES), lambda i: (i, 0, 0)),
                 row_spec, row_spec, stream_spec, stream_spec)

    scratch = [pltpu.VMEM((tm, HK), BF16), pltpu.VMEM((tm, HK), BF16), pltpu.VMEM((tm, HK), BF16),
               pltpu.VMEM((tm, HK), BF16), pltpu.VMEM((tm, HV), BF16),
               pltpu.VMEM((tm, HK), BF16), pltpu.VMEM((tm, HK), BF16), pltpu.VMEM((tm, HV), BF16),
               pltpu.VMEM((n_chunks, SUBLANES, HK), F32),
               pltpu.VMEM((tm, HV), F32), pltpu.VMEM((tm, HV), F32),
               pltpu.VMEM((HEAD_V, HK), F32), pltpu.VMEM((HEAD_V, HK), F32)]
    scratch += [pltpu.VMEM((n_chunks, HEAD_V, HK), F32)] * 4

    return pl.pallas_call(
        functools.partial(_mixer_kernel, n_chunks=n_chunks, tiles_p=tiles_p, tiles_per_row=tpr, route=route),
        grid=(n_tiles,), in_specs=in_specs, out_specs=out_specs, out_shape=out_shape, scratch_shapes=scratch,
        compiler_params=pltpu.CompilerParams(dimension_semantics=("arbitrary",), vmem_limit_bytes=VMEM_LIMIT),
        name="mixer" if route else "mixer_meta",
    )(*args)


def _plan_kernel(ku_ref, src_ref, blk_ref, run_first, run_step, next_unit, seg_start, seg_units, *, n_tiles, n_blocks):
    shift = int(math.log2(BLOCK_UNITS))
    group_shift = int(math.log2(SUBLANES * LANES))

    def init_tile(t, c):
        next_unit[t] = t * LOCAL_UNITS
        return c

    lax.fori_loop(0, n_tiles, init_tile, 0)

    def init_block(b, c):
        blk_ref[0, b] = N_EXPERTS - 1
        blk_ref[1, b] = 0
        blk_ref[2, b] = 0
        blk_ref[3, b] = -1
        blk_ref[4, b] = 0
        return c

    lax.fori_loop(0, n_blocks, init_block, 0)

    def per_expert(e, carry):
        g0, position = carry

        def per_tile(t, c):
            units, d_prev = c
            base = next_unit[t]
            d = base - units
            run_first[e * n_tiles + t] = units
            run_step[e * n_tiles + t] = d - d_prev
            k = ku_ref[t, e]
            next_unit[t] = base + k
            return units + k, d

        units, _ = lax.fori_loop(0, n_tiles, per_tile, (0, 0), unroll=4)
        g_pad = g0 + (((units + (BLOCK_UNITS - 1)) >> shift) << shift)
        seg_start[e] = g0
        seg_units[e] = units

        def set_block(b, c):
            blk_ref[0, b] = e
            blk_ref[1, b] = jnp.minimum(g0 + units - (b << shift), BLOCK_UNITS)
            blk_ref[3, b] = g_pad >> shift
            blk_ref[4, b] = position & 1
            return c

        lax.fori_loop(g0 >> shift, g_pad >> shift, set_block, 0)
        return g_pad, position + (units > 0).astype(jnp.int32)

    g_total, _ = lax.fori_loop(0, N_EXPERTS, per_expert, (0, 0))
    blk_ref[2, 0] = g_total >> shift

    src_ref[...] = jnp.zeros(src_ref.shape, jnp.int32)
    in_group = (lax.broadcasted_iota(jnp.int32, (SUBLANES, LANES), 0) * LANES
                + lax.broadcasted_iota(jnp.int32, (SUBLANES, LANES), 1))

    def expert_units(e, c):
        g0 = seg_start[e]
        units = seg_units[e]
        g_pad = g0 + (((units + (BLOCK_UNITS - 1)) >> shift) << shift)

        def per_group(grp, c2):
            g = in_group + (grp << group_shift)
            o = g - g0
            mine = jnp.logical_and(o >= 0, g < g_pad)
            o_eff = jnp.where(o < units, o, (o >> shift) << shift)

            def per_tile(t, acc):
                return acc + jnp.where(o_eff >= run_first[e * n_tiles + t], run_step[e * n_tiles + t], 0)

            offset = lax.fori_loop(0, n_tiles, per_tile, jnp.zeros((SUBLANES, LANES), jnp.int32), unroll=4)
            rows = pl.ds(pl.multiple_of(grp * SUBLANES, SUBLANES), SUBLANES)
            src_ref[rows, :] = jnp.where(mine, o_eff + offset, src_ref[rows, :])
            return c2

        lax.fori_loop(g0 >> group_shift, (g_pad + (SUBLANES * LANES - 1)) >> group_shift, per_group, 0)
        return c

    lax.fori_loop(0, N_EXPERTS, expert_units, 0)


def _plan_call(ku, n_blocks):
    n_tiles = ku.shape[0]
    smem = pl.BlockSpec(memory_space=pltpu.SMEM)
    group = SUBLANES * LANES
    src_rows = -(-n_blocks * BLOCK_UNITS // group) * SUBLANES
    src, blk = pl.pallas_call(
        functools.partial(_plan_kernel, n_tiles=n_tiles, n_blocks=n_blocks),
        in_specs=[smem], out_specs=(pl.BlockSpec(memory_space=pltpu.VMEM), smem),
        out_shape=(jax.ShapeDtypeStruct((src_rows, LANES), jnp.int32),
                   jax.ShapeDtypeStruct((BLK_ROWS, n_blocks), jnp.int32)),
        scratch_shapes=[pltpu.SMEM((N_EXPERTS * n_tiles,), jnp.int32), pltpu.SMEM((N_EXPERTS * n_tiles,), jnp.int32),
                        pltpu.SMEM((n_tiles,), jnp.int32), pltpu.SMEM((N_EXPERTS,), jnp.int32),
                        pltpu.SMEM((N_EXPERTS,), jnp.int32)],
        name="moe_plan",
    )(ku)
    return src.reshape(-1), blk


def _experts_kernel(src_ref, blk_ref, x_hbm, wg_hbm, wu_hbm, wd_hbm, y_hbm,
                    xin, yout, wgf, wuf, wdf, wgub, wdb, sem_in, sem_out, sem_w):
    b = pl.program_id(0)
    used = blk_ref[2, 0]
    slot = b & 1

    def weight_copies(e, s):
        return [pltpu.make_async_copy(hbm.at[e], buf.at[s], sem_w.at[s])
                for hbm, buf in ((wg_hbm, wgf), (wu_hbm, wuf), (wd_hbm, wdf))]

    def unit_rows(blk, u):
        return pl.ds(pl.multiple_of(src_ref[blk * BLOCK_UNITS + u] * UNIT, UNIT), UNIT)

    def in_copy(blk, u, s):
        return pltpu.make_async_copy(x_hbm.at[unit_rows(blk, u), :], xin.at[s, pl.ds(u * UNIT, UNIT), :],
                                     sem_in.at[s])

    def out_copy(blk, u, s):
        return pltpu.make_async_copy(yout.at[s, pl.ds(u * UNIT, UNIT), :], y_hbm.at[unit_rows(blk, u), :],
                                     sem_out.at[s])

    @pl.when(jnp.logical_and(b == 0, used > 0))
    def _():
        for copy in weight_copies(blk_ref[0, 0], blk_ref[4, 0]):
            copy.start()
        for u in range(BLOCK_UNITS):
            in_copy(0, u, 0).start()

    @pl.when(b < used)
    def _():
        expert = blk_ref[0, b]
        changed = jnp.logical_or(b == 0, expert != blk_ref[0, jnp.maximum(b - 1, 0)])

        @pl.when(changed)
        def _():
            wslot = blk_ref[4, b]
            later = blk_ref[3, b]
            following = jnp.where(later < used, blk_ref[0, jnp.minimum(later, pl.num_programs(0) - 1)], -1)
            for copy in weight_copies(expert, wslot):
                copy.wait()

            @pl.when(following >= 0)
            def _():
                for copy in weight_copies(following, 1 - wslot):
                    copy.start()

            wgub[:, :D_EXPERT] = wgf[wslot].astype(BF16)
            wgub[:, D_EXPERT:] = wuf[wslot].astype(BF16)
            wdb[...] = wdf[wslot].astype(BF16)

        @pl.when(b + 1 < used)
        def _():
            for u in range(BLOCK_UNITS):
                in_copy(b + 1, u, 1 - slot).start()

        pltpu.make_async_copy(x_hbm.at[pl.ds(0, MOE_BLOCK), :], xin.at[slot], sem_in.at[slot]).wait()

        n_real = blk_ref[1, b]

        def mlp(rows):
            xb = _unpack_halves(xin[slot, :rows, :])
            gate_up = _dot(xb, wgub[...])
            gate = gate_up[:, :D_EXPERT]
            hid = (gate * _sigmoid(gate)) * gate_up[:, D_EXPERT:]
            yout[slot, :rows, :] = _pack_halves(_dot(hid.astype(BF16), wdb[...]).astype(BF16).astype(F32))

        @pl.when(n_real > BLOCK_UNITS // 2)
        def _():
            mlp(MOE_BLOCK)

        @pl.when(n_real <= BLOCK_UNITS // 2)
        def _():
            mlp(MOE_BLOCK // 2)


        @pl.when(n_real == BLOCK_UNITS)
        def _():
            for u in range(BLOCK_UNITS):
                out_copy(b, u, slot).start()

        @pl.when(n_real < BLOCK_UNITS)
        def _():
            lax.fori_loop(0, n_real, lambda u, c: (out_copy(b, u, slot).start(), c)[1], 0)

        @pl.when(jnp.logical_and(b > 0, blk_ref[1, jnp.maximum(b - 1, 0)] == BLOCK_UNITS))
        def _():
            pltpu.make_async_copy(yout.at[1 - slot], y_hbm.at[pl.ds(0, MOE_BLOCK), :], sem_out.at[1 - slot]).wait()

        @pl.when(jnp.logical_and(b > 0, blk_ref[1, jnp.maximum(b - 1, 0)] < BLOCK_UNITS))
        def _():
            lax.fori_loop(0, blk_ref[1, b - 1], lambda u, c: (out_copy(b - 1, u, 1 - slot).wait(), c)[1], 0)

        @pl.when(b == used - 1)
        def _():
            lax.fori_loop(0, n_real, lambda u, c: (out_copy(b, u, slot).wait(), c)[1], 0)


def _experts_call(src, blk, xloc, wg, wu, wd, n_blocks):
    hbm = pl.BlockSpec(memory_space=pl.ANY)
    grid_spec = pltpu.PrefetchScalarGridSpec(
        num_scalar_prefetch=2,
        grid=(n_blocks,),
        in_specs=[hbm, hbm, hbm, hbm],
        out_specs=hbm,
        scratch_shapes=[pltpu.VMEM((2, MOE_BLOCK, D_MODEL // 2), jnp.int32),
                        pltpu.VMEM((2, MOE_BLOCK, D_MODEL // 2), jnp.int32),
                        pltpu.VMEM((2, D_MODEL, D_EXPERT), F32), pltpu.VMEM((2, D_MODEL, D_EXPERT), F32),
                        pltpu.VMEM((2, D_EXPERT, D_MODEL), F32),
                        pltpu.VMEM((D_MODEL, 2 * D_EXPERT), BF16), pltpu.VMEM((D_EXPERT, D_MODEL), BF16),
                        pltpu.SemaphoreType.DMA((2,)), pltpu.SemaphoreType.DMA((2,)),
                        pltpu.SemaphoreType.DMA((2,))],
    )
    return pl.pallas_call(
        _experts_kernel, grid_spec=grid_spec,
        out_shape=jax.ShapeDtypeStruct(xloc.shape, jnp.int32),
        input_output_aliases={2: 0},
        compiler_params=pltpu.CompilerParams(dimension_semantics=("arbitrary",), vmem_limit_bytes=VMEM_LIMIT),
        name="moe_experts",
    )(src, blk, xloc, wg, wu, wd)


def _combine_kernel(route_ref, x_ref, y_ref, gfin_ref, out_a_hbm, out_b_hbm, obuf, sem, *, tm, tiles_a):
    t = pl.program_id(0)
    slot = t & 1

    def tile_out(dst_hbm, tile, s):
        return pltpu.make_async_copy(obuf.at[s], dst_hbm.at[pl.ds(pl.multiple_of(tile * tm, tm), tm), :], sem.at[s])

    @pl.when(t >= 2)
    def _():
        tile_out(out_a_hbm, 0, slot).wait()

    rec = route_ref[0]
    recf = jnp.concatenate([rec[2:4, :].astype(F32), pltpu.bitcast(rec, F32)[4:6, :],
                            jnp.zeros((LANES - 4, tm), F32)], axis=0)
    cols = recf.T
    lrow = lax.broadcasted_iota(jnp.int32, (tm, LOCAL_ROWS), 1).astype(F32)
    select = jnp.where(lrow == cols[:, 0:1], cols[:, 2:3], jnp.where(lrow == cols[:, 1:2], cols[:, 3:4], 0.0))
    xf = x_ref[...] + _dot(select.astype(BF16), _unpack_halves(y_ref[...]))
    obuf[slot] = xf * lax.rsqrt(jnp.mean(xf * xf, axis=-1, keepdims=True) + NORM_EPS) * gfin_ref[...]

    @pl.when(t < tiles_a)
    def _():
        tile_out(out_a_hbm, t, slot).start()

    @pl.when(t >= tiles_a)
    def _():
        tile_out(out_b_hbm, t - tiles_a, slot).start()

    @pl.when(t == pl.num_programs(0) - 1)
    def _():
        tile_out(out_a_hbm, 0, slot).wait()

        @pl.when(t >= 1)
        def _():
            tile_out(out_a_hbm, 0, 1 - slot).wait()


def _combine_call(route, xnew, yloc, gfin, tm, tiles_a):
    n_tiles = route.shape[0]
    tiles_b = n_tiles - tiles_a
    return pl.pallas_call(
        functools.partial(_combine_kernel, tm=tm, tiles_a=tiles_a),
        grid=(n_tiles,),
        in_specs=[pl.BlockSpec((1, ROUTE_ROWS, tm), lambda t: (t, 0, 0)),
                  pl.BlockSpec((tm, D_MODEL), lambda t: (t, 0)),
                  pl.BlockSpec((LOCAL_ROWS, D_MODEL // 2), lambda t: (t, 0)),
                  pl.BlockSpec((1, D_MODEL), lambda t: (0, 0))],
        out_specs=(pl.BlockSpec(memory_space=pl.ANY), pl.BlockSpec(memory_space=pl.ANY)),
        out_shape=(jax.ShapeDtypeStruct((tiles_a * tm, D_MODEL), F32),
                   jax.ShapeDtypeStruct((tiles_b * tm, D_MODEL), F32)),
        scratch_shapes=[pltpu.VMEM((2, tm, D_MODEL), F32), pltpu.SemaphoreType.DMA((2,))],
        compiler_params=pltpu.CompilerParams(dimension_semantics=("arbitrary",), vmem_limit_bytes=VMEM_LIMIT),
        name="moe_combine",
    )(route, xnew, yloc, gfin)


def _prep_weights(g_mix, w_in, w_gla_gk2, b_gla_gk, g_gla_norm, w_br_ret, w_br_gla, w_out, g_ffn,
                  w_rg, b_rg, w_re, b_re):
    w_int = jnp.swapaxes(w_in, 0, 1).astype(BF16)
    wgk = jnp.pad(w_gla_gk2, ((0, LANES - GLA_RANK), (0, 0))).astype(BF16)
    def router_rows(groups, experts):
        return jnp.concatenate([groups, jnp.zeros((SUBLANES - N_GROUPS,) + groups.shape[1:], F32), experts,
                                jnp.zeros((ROUTER_LANES - SUBLANES - N_EXPERTS,) + experts.shape[1:], F32)], axis=0)

    wrt = router_rows(w_rg.T, w_re.T)
    brt = router_rows(b_rg, b_re).reshape(1, ROUTER_LANES)
    wrt_hi = wrt.astype(BF16)
    wrt = jnp.concatenate([wrt_hi, (wrt - wrt_hi.astype(F32)).astype(BF16)], axis=0)
    return (g_mix.reshape(1, D_MODEL), w_int, wgk, b_gla_gk.reshape(1, HK), g_gla_norm.reshape(1, HEAD_V),
            w_br_ret.astype(BF16), w_br_gla.astype(BF16), w_out.astype(BF16), g_ffn.reshape(1, D_MODEL), wrt, brt)


def _tri_consts(tm):
    r = np.arange(tm)
    low = (r[None, :] <= r[:, None])
    bd = low & ((r[None, :] // CHUNK) == (r[:, None] // CHUNK))
    return jnp.asarray(bd, BF16), jnp.asarray(low, BF16)


def _state_to_kernel(s):
    return s.reshape(s.shape[0], HK, HEAD_V)


def _state_from_kernel(s):
    return s.reshape(1, s.shape[0], N_HEADS, HEAD_K, HEAD_V)


def kernel(x_prompt, x_sample, state_ret, state_gla, meta_tokens, g_mix, w_in, w_gla_gk2, b_gla_gk, g_gla_norm, w_br_ret, w_br_gla, w_out, g_ffn, w_router_group, b_router_group, w_router_expert, b_router_expert, w_exp_gate, w_exp_up, w_exp_down, g_final):
    n_b, seq, _ = x_prompt.shape
    n_s, dec_seq, _ = x_sample.shape
    depth = state_ret.shape[0]
    assert depth == 1 and dec_seq == CHUNK and seq % TILE_ROWS == 0 and n_s % TILE_CHUNKS == 0
    cos_t, sin_t, cos_m, sin_m = _rotary_tables(seq, PAST_LEN)
    weights = _prep_weights(g_mix[0], w_in[0], w_gla_gk2[0], b_gla_gk[0], g_gla_norm[0], w_br_ret[0], w_br_gla[0],
                            w_out[0], g_ffn[0], w_router_group[0], b_router_group[0], w_router_expert[0],
                            b_router_expert[0])
    zero_state = jnp.zeros((1, HK, HEAD_V), F32)

    x_meta = jnp.concatenate([jnp.zeros((CHUNK - N_META, D_MODEL), F32), meta_tokens.astype(F32)], axis=0)
    meta_out = _mixer_call(None, x_meta.reshape(1, CHUNK, D_MODEL), cos_m, sin_m, (zero_state, zero_state),
                           (zero_state, zero_state), weights, _tri_consts(CHUNK), n_chunks=1, route=False)
    meta_ret, meta_gla = meta_out[6], meta_out[7]

    xnew, xloc, route, ku, ret_p, gla_p, ret_s, gla_s = _mixer_call(
        x_prompt, x_sample, cos_t, sin_t, (meta_ret, meta_gla),
        (_state_to_kernel(state_ret[0]), _state_to_kernel(state_gla[0])), weights, _tri_consts(TILE_ROWS),
        n_chunks=TILE_CHUNKS, route=True)

    n_tiles = route.shape[0]
    n_blocks = n_tiles * LOCAL_UNITS // BLOCK_UNITS + N_EXPERTS
    src, blk = _plan_call(ku[:, 0, :N_EXPERTS], n_blocks)
    yloc = _experts_call(src, blk, xloc, w_exp_gate[0], w_exp_up[0], w_exp_down[0], n_blocks)
    y_p, y_s = _combine_call(route, xnew, yloc, g_final.reshape(1, D_MODEL), TILE_ROWS, n_b * seq // TILE_ROWS)

    return (y_p.reshape(n_b, seq, D_MODEL), y_s.reshape(n_s, dec_seq, D_MODEL),
            _state_from_kernel(ret_p), _state_from_kernel(gla_p), _state_from_kernel(ret_s), _state_from_kernel(gla_s))
```

```python
import functools
import math

import jax
import jax.numpy as jnp
import numpy as np
from jax import lax
from jax.experimental import pallas as pl
from jax.experimental.pallas import tpu as pltpu

D_MODEL = 1024
CHUNK = 64
PAST_LEN = 1024
N_META = 16
N_HEADS = 4
HEAD_K = 64
HEAD_V = 128
HK = N_HEADS * HEAD_K
HV = N_HEADS * HEAD_V
GLA_RANK = 16
GATE_NORM = 16.0
ROPE_BASE = 10000.0
N_GROUPS = 4
EXPERTS_PER_GROUP = 8
N_EXPERTS = N_GROUPS * EXPERTS_PER_GROUP
D_EXPERT = 512
NORM_EPS = 1e-6

LANES = 128
SUBLANES = 8
TILE_CHUNKS = 8
TILE_ROWS = TILE_CHUNKS * CHUNK
ROUTE_ROWS = 8
ROUTER_LANES = 128
UNIT = SUBLANES
MOE_BLOCK = 512
BLOCK_UNITS = MOE_BLOCK // UNIT
BLK_ROWS = 5
LOCAL_ROWS = 2 * TILE_ROWS + N_EXPERTS * UNIT
LOCAL_UNITS = LOCAL_ROWS // UNIT
VMEM_LIMIT = 56 * 1024 * 1024

C_RQ, C_RK, C_RV, C_RG = 0, 256, 512, 1024
C_GQ, C_GK, C_GV, C_GG = 1536, 1792, 2048, 2560
W_MIX = 3072
C_Z = W_MIX + GLA_RANK

F32 = jnp.float32
BF16 = jnp.bfloat16
LOG_G = tuple(math.log1p(-(2.0 ** (-5.0 - h))) for h in range(N_HEADS))


def _dot(a, b):
    return jnp.dot(a, b, preferred_element_type=F32)


def _dot_nt(a, b):
    return lax.dot_general(a, b, (((1,), (1,)), ((), ())), preferred_element_type=F32)


def _dot_tn(a, b):
    return lax.dot_general(a, b, (((0,), (0,)), ((), ())), preferred_element_type=F32)


def _sigmoid(x):
    return 1.0 / (1.0 + jnp.exp(-x))


def _pack_halves(x):
    half = x.shape[1] // 2
    bits = pltpu.bitcast(x, jnp.int32)
    return lax.shift_right_logical(bits[:, :half], 16) | (bits[:, half:] & jnp.int32(-65536))


def _unpack_halves(w):
    lo = pltpu.bitcast(lax.shift_left(w, 16), F32)
    hi = pltpu.bitcast(w & jnp.int32(-65536), F32)
    return jnp.concatenate([lo, hi], axis=1).astype(BF16)


def _head_of_lane(shape, width):
    return lax.broadcasted_iota(jnp.int32, shape, len(shape) - 1) >> int(math.log2(width))


def _per_head_lane_const(vals, shape, width):
    hd = _head_of_lane(shape, width)
    out = jnp.full(shape, vals[N_HEADS - 1], F32)
    for h in range(N_HEADS - 2, -1, -1):
        out = jnp.where(hd == h, vals[h], out)
    return out


def _tables_kernel(inv_ref, cp_ref, sp_ref, cm_ref, sm_ref, *, seq, past_len):
    inv = inv_ref[...].reshape(1, 1, LANES)
    lane = lax.broadcasted_iota(jnp.int32, (1, 1, LANES), 2)
    sign = jnp.where((lane & (HEAD_K - 1)) < (HEAD_K // 2), -1.0, 1.0)
    off = lax.broadcasted_iota(jnp.int32, (1, CHUNK, LANES), 1).astype(F32) * inv
    c_off, s_off = jnp.cos(off), jnp.sin(off)

    def chunks(n, first_pos):
        base = (lax.broadcasted_iota(jnp.int32, (n, 1, LANES), 0) * CHUNK + first_pos).astype(F32) * inv
        c_base, s_base = jnp.cos(base), jnp.sin(base)
        cos = (c_base * c_off - s_base * s_off).reshape(n * CHUNK, LANES)
        sin = ((s_base * c_off + c_base * s_off) * sign).reshape(n * CHUNK, LANES)
        return cos, sin

    cp_ref[0:seq, :], sp_ref[0:seq, :] = chunks(seq // CHUNK, 0)
    cos_s, sin_s = chunks(1, past_len)
    for c in range(TILE_CHUNKS):
        cp_ref[seq + c * CHUNK:seq + (c + 1) * CHUNK, :] = cos_s
        sp_ref[seq + c * CHUNK:seq + (c + 1) * CHUNK, :] = sin_s
    cm_ref[...], sm_ref[...] = chunks(1, -CHUNK)


def _rotary_tables(seq, past_len):
    half = HEAD_K // 2
    inv = ROPE_BASE ** (-2.0 * jnp.arange(half, dtype=F32) / HEAD_K)
    inv = jnp.tile(inv, LANES // half).reshape(1, LANES)
    shp = lambda r: jax.ShapeDtypeStruct((r, LANES), F32)
    return pl.pallas_call(
        functools.partial(_tables_kernel, seq=seq, past_len=past_len),
        out_shape=(shp(seq + TILE_ROWS), shp(seq + TILE_ROWS), shp(CHUNK), shp(CHUNK)),
        compiler_params=pltpu.CompilerParams(vmem_limit_bytes=VMEM_LIMIT),
        name="rotary_tables",
    )(inv)


def _mixer_kernel(*refs, n_chunks, tiles_p, tiles_per_row, route):
    tm = n_chunks * CHUNK
    if tiles_p:
        xp_ref, refs = refs[0], refs[1:]
    (xs_ref, cos_ref, sin_ref, spi_r_ref, spi_g_ref, ssi_r_ref, ssi_g_ref,
     gmix_ref, wint_ref, wgk_ref, bgk_ref, gnorm_ref, wbr_ref, wbg_ref, wout_ref,
     gffn_ref, wrt_ref, brt_ref, bdtri_ref, ltri_ref,
     xnew_ref, xloc_ref, route_ref, ku_ref, spo_r_ref, spo_g_ref, sso_r_ref, sso_g_ref,
     qb, qdb, kb, kkb, vb, gqb, gkkb, gvb, ga, o_ret, o_gla, st_ret, st_gla,
     sin_r, sin_g, sout_r, sout_g) = refs

    i = pl.program_id(0)
    if tiles_p:
        is_s = i >= tiles_p
        t_idx = jnp.minimum(i, tiles_p - 1) % tiles_per_row
        x = jnp.where(is_s, xs_ref[...].reshape(tm, D_MODEL), xp_ref[...].reshape(tm, D_MODEL))

        @pl.when(jnp.logical_and(jnp.logical_not(is_s), t_idx == 0))
        def _():
            st_ret[...] = spi_r_ref[0].T
            st_gla[...] = spi_g_ref[0].T

        @pl.when(i == 0)
        def _():
            sin_r[...] = jnp.zeros(sin_r.shape, F32)
            sin_g[...] = jnp.zeros(sin_g.shape, F32)

        @pl.when(is_s)
        def _():
            for c in range(n_chunks):
                sin_r[c] = ssi_r_ref[c].T
                sin_g[c] = ssi_g_ref[c].T
    else:
        is_s = None
        x = xs_ref[...].reshape(tm, D_MODEL)

    h = x * lax.rsqrt(jnp.mean(x * x, axis=-1, keepdims=True) + NORM_EPS) * gmix_ref[...]
    hb = h.astype(BF16)

    def proj(c0, width):
        return _dot_nt(hb, wint_ref[c0:c0 + width, :])

    cos3 = jnp.concatenate([cos_ref[...]] * 2, axis=1).reshape(n_chunks, CHUNK, HK)
    sin3 = jnp.concatenate([sin_ref[...]] * 2, axis=1).reshape(n_chunks, CHUNK, HK)
    lane_hk = lax.broadcasted_iota(jnp.int32, (tm, HK), 1)
    first_half = (lane_hk & (HEAD_K - 1)) < (HEAD_K // 2)

    def rotary(t):
        swapped = jnp.where(first_half, pltpu.roll(t, HK - HEAD_K // 2, 1), pltpu.roll(t, HEAD_K // 2, 1))
        t3 = t.reshape(n_chunks, CHUNK, HK)
        return (t3 * cos3 + swapped.reshape(n_chunks, CHUNK, HK) * sin3).reshape(tm, HK)

    logg_hk = _per_head_lane_const(LOG_G, (CHUNK, HK), HEAD_K)
    l_idx = lax.broadcasted_iota(jnp.int32, (CHUNK, HK), 0).astype(F32)
    qdec = jnp.exp((l_idx + 1.0) * logg_hk)
    kdec = jnp.exp((CHUNK - 1.0 - l_idx) * logg_hk)
    cdec = jnp.exp(float(CHUNK) * _per_head_lane_const(LOG_G, (1, HK), HEAD_K))
    r_idx = lax.broadcasted_iota(jnp.int32, (N_HEADS * CHUNK, CHUNK), 0)
    m_idx = lax.broadcasted_iota(jnp.int32, (N_HEADS * CHUNK, CHUNK), 1)
    logg_rows = jnp.full((N_HEADS * CHUNK, CHUNK), LOG_G[N_HEADS - 1], F32)
    for hh in range(N_HEADS - 2, -1, -1):
        logg_rows = jnp.where((r_idx >> int(math.log2(CHUNK))) == hh, LOG_G[hh], logg_rows)
    dmat = jnp.exp(jnp.abs((r_idx & (CHUNK - 1)) - m_idx).astype(F32) * logg_rows)

    glr = jnp.where(lax.broadcasted_iota(jnp.int32, (tm, LANES), 1) < GLA_RANK, proj(W_MIX, LANES), 0.0)
    rq = rotary(proj(C_RQ, HK))
    qb[...] = rq.astype(BF16)
    qdb[...] = (rq.reshape(n_chunks, CHUNK, HK) * qdec).reshape(tm, HK).astype(BF16)
    gl = _dot(glr.astype(BF16), wgk_ref[...]) + bgk_ref[...]
    rk = rotary(proj(C_RK, HK)) * (HEAD_K ** -0.5)
    kb[...] = rk.astype(BF16)
    kkb[...] = (rk.reshape(n_chunks, CHUNK, HK) * kdec).reshape(tm, HK).astype(BF16)
    log_a = (jnp.minimum(gl, 0.0) - jnp.log1p(jnp.exp(-jnp.abs(gl)))) / GATE_NORM
    la_hi = log_a.astype(BF16)
    la_lo = (log_a - la_hi.astype(F32)).astype(BF16)
    vb[...] = proj(C_RV, HV).astype(BF16)
    bdtri = bdtri_ref[...]
    bcum = _dot(bdtri, la_hi) + _dot(bdtri, la_lo)
    gqb[...] = (proj(C_GQ, HK) * (HEAD_K ** -0.5)).astype(BF16)
    gvb[...] = proj(C_GV, HV).astype(BF16)
    gk = proj(C_GK, HK)
    b3 = bcum.reshape(n_chunks, CHUNK, HK)
    bl3 = b3[:, CHUNK - 1:CHUNK, :]
    gkkb[...] = (gk.reshape(n_chunks, CHUNK, HK) * jnp.exp(bl3 - b3)).reshape(tm, HK).astype(BF16)
    ga[...] = jnp.broadcast_to(jnp.exp(bl3), (n_chunks, SUBLANES, HK))

    def stack_masked(a, width):
        head = _head_of_lane(a.shape, width)
        zero = jnp.zeros_like(a)
        return jnp.concatenate([jnp.where(head == hh, a, zero) for hh in range(N_HEADS)], axis=0)

    def heads_to_rows(a):
        return jnp.concatenate([a[:, hh * HEAD_V:(hh + 1) * HEAD_V] for hh in range(N_HEADS)], axis=0)

    def rows_to_heads(a):
        return jnp.concatenate([a[hh * CHUNK:(hh + 1) * CHUNK, :] for hh in range(N_HEADS)], axis=1)

    chunk_rows = [slice(c * CHUNK, (c + 1) * CHUNK) for c in range(n_chunks)]
    probs = [(_dot_nt(stack_masked(qb[r, :], HEAD_K), kb[r, :]) * dmat).astype(BF16) for r in chunk_rows]
    inc_ret = [_dot_tn(heads_to_rows(vb[r, :]), stack_masked(kkb[r, :], HEAD_K)) for r in chunk_rows]
    inc_gla = [_dot_tn(heads_to_rows(gvb[r, :]), stack_masked(gkkb[r, :], HEAD_K)) for r in chunk_rows]
    rg = proj(C_RG, HV)
    gg = proj(C_GG, HV)
    s_in, g_out = [], []
    s_cur, g_cur = st_ret[...], st_gla[...]
    for c in range(n_chunks):
        if is_s is None:
            s_cur, g_cur = ssi_r_ref[c].T, ssi_g_ref[c].T
        else:
            s_cur = jnp.where(is_s, sin_r[c], s_cur)
            g_cur = jnp.where(is_s, sin_g[c], g_cur)
        s_in.append(s_cur.astype(BF16))
        s_cur = s_cur * cdec + inc_ret[c]
        g_cur = g_cur * ga[c][0:1, :] + inc_gla[c]
        g_out.append(g_cur.astype(BF16))
        sout_r[c] = s_cur
        sout_g[c] = g_cur
    st_ret[...] = s_cur
    st_gla[...] = g_cur
    for c, r in enumerate(chunk_rows):
        v = vb[r, :]
        intra = jnp.concatenate(
            [_dot(probs[c][hh * CHUNK:(hh + 1) * CHUNK, :], v[:, hh * HEAD_V:(hh + 1) * HEAD_V])
             for hh in range(N_HEADS)], axis=1)
        inter = rows_to_heads(_dot_nt(stack_masked(qdb[r, :], HEAD_K), s_in[c]))
        o_ret[r, :] = intra + inter
        o_gla[r, :] = rows_to_heads(_dot_nt(stack_masked(gqb[r, :], HEAD_K), g_out[c]))

    gnorm = gnorm_ref[...]
    orr = o_ret[...]
    ogg = o_gla[...]
    ret_parts, gla_parts = [], []
    for hh in range(N_HEADS):
        sl = slice(hh * HEAD_V, (hh + 1) * HEAD_V)
        oh = orr[:, sl]
        mu = jnp.mean(oh, axis=-1, keepdims=True)
        dev = oh - mu
        var = jnp.mean(dev * dev, axis=-1, keepdims=True)
        ret_parts.append(dev * lax.rsqrt(var + NORM_EPS))
        og = ogg[:, sl]
        gla_parts.append(og * lax.rsqrt(jnp.mean(og * og, axis=-1, keepdims=True) + NORM_EPS) * gnorm)
    o_r = jnp.concatenate(ret_parts, axis=1) * (rg * _sigmoid(rg))
    o_g = jnp.concatenate(gla_parts, axis=1) * (gg * _sigmoid(gg))
    merged = (_sigmoid(proj(C_Z, D_MODEL)) * _dot(o_r.astype(BF16), wbr_ref[...])
              + _sigmoid(proj(C_Z + D_MODEL, D_MODEL)) * _dot(o_g.astype(BF16), wbg_ref[...]))
    xn = x + _dot(merged.astype(BF16), wout_ref[...])
    xnew_ref[...] = xn

    if route:
        h2 = xn * lax.rsqrt(jnp.mean(xn * xn, axis=-1, keepdims=True) + NORM_EPS) * gffn_ref[...]
        h2_hi = h2.astype(BF16)
        h2_lo = (h2 - h2_hi.astype(F32)).astype(BF16)
        hi_both = _dot_nt(h2_hi, wrt_ref[...])
        logits = (hi_both[:, :ROUTER_LANES] + _dot_nt(h2_lo, wrt_ref[:ROUTER_LANES, :])
                  + hi_both[:, ROUTER_LANES:]) + brt_ref[...]
        lt = logits.T
        row8 = lax.broadcasted_iota(jnp.int32, (SUBLANES, tm), 0)
        neg_inf = jnp.float32(-jnp.inf)
        glog = jnp.where(row8 < N_GROUPS, lt[0:SUBLANES, :], neg_inf)
        gmax = jnp.max(glog, axis=0, keepdims=True)
        grp = jnp.min(jnp.where(glog == gmax, row8, SUBLANES), axis=0, keepdims=True)
        p_grp = 1.0 / jnp.sum(jnp.exp(glog - gmax), axis=0, keepdims=True)
        le = jnp.zeros((SUBLANES, tm), F32)
        for g in range(N_GROUPS):
            le = jnp.where(grp == g, lt[SUBLANES * (g + 1):SUBLANES * (g + 2), :], le)
        m1 = jnp.max(le, axis=0, keepdims=True)
        i1 = jnp.min(jnp.where(le == m1, row8, SUBLANES), axis=0, keepdims=True)
        le2 = jnp.where(row8 == i1, neg_inf, le)
        m2 = jnp.max(le2, axis=0, keepdims=True)
        i2 = jnp.min(jnp.where(le2 == m2, row8, SUBLANES), axis=0, keepdims=True)
        e0 = grp * EXPERTS_PER_GROUP + i1
        e1 = grp * EXPERTS_PER_GROUP + i2
        t21 = jnp.exp(m2 - m1)
        w0 = p_grp / (1.0 + t21)
        w1 = p_grp * t21 / (1.0 + t21)

        erow = lax.broadcasted_iota(jnp.int32, (N_EXPERTS, tm), 0)
        hit0 = erow == e0
        hit1 = erow == e1
        onehot = jnp.where(jnp.logical_or(hit0, hit1), 1.0, 0.0).astype(BF16)
        cum = _dot_nt(onehot, ltri_ref[...])
        n_run = cum[:, tm - 1:tm]
        n_pad = jnp.ceil(n_run / UNIT) * UNIT
        rank0 = jnp.sum(jnp.where(hit0, cum - 1.0, 0.0), axis=0, keepdims=True)
        rank1 = jnp.sum(jnp.where(hit1, cum - 1.0, 0.0), axis=0, keepdims=True)
        start0 = jnp.sum(jnp.where(erow < e0, n_pad, 0.0), axis=0, keepdims=True)
        start1 = jnp.sum(jnp.where(erow < e1, n_pad, 0.0), axis=0, keepdims=True)
        ld0 = (start0 + rank0).astype(jnp.int32)
        ld1 = (start1 + rank1).astype(jnp.int32)
        lrow = lax.broadcasted_iota(jnp.int32, (LOCAL_ROWS, tm), 0)
        perm = jnp.where(jnp.logical_or(lrow == ld0, lrow == ld1), 1.0, 0.0).astype(BF16)
        xloc_ref[...] = _pack_halves(_dot(perm, h2_hi))
        lane_e = lax.broadcasted_iota(jnp.int32, (N_EXPERTS, LANES), 1)
        erow_l = lax.broadcasted_iota(jnp.int32, (N_EXPERTS, LANES), 0)
        units_row = jnp.sum(jnp.where(erow_l == lane_e, n_pad / UNIT, 0.0), axis=0, keepdims=True)
        ku_ref[...] = jnp.broadcast_to(units_row, (SUBLANES, LANES)).astype(jnp.int32).reshape(ku_ref.shape)
        zero_row = jnp.zeros((1, tm), jnp.int32)
        rec = jnp.concatenate([e0, e1, ld0, ld1, pltpu.bitcast(w0, jnp.int32), pltpu.bitcast(w1, jnp.int32),
                               zero_row, zero_row], axis=0)
        route_ref[...] = rec.reshape(route_ref.shape)
    else:
        xloc_ref[...] = jnp.zeros(xloc_ref.shape, jnp.int32)
        route_ref[...] = jnp.zeros(route_ref.shape, jnp.int32)
        ku_ref[...] = jnp.zeros(ku_ref.shape, jnp.int32)

    def stream_states_out():
        for c in range(n_chunks):
            sso_r_ref[c] = sout_r[c].T
            sso_g_ref[c] = sout_g[c].T

    if is_s is None:
        stream_states_out()
    else:
        pl.when(is_s)(stream_states_out)

    if tiles_p:
        @pl.when(jnp.logical_and(jnp.logical_not(is_s), t_idx == tiles_per_row - 1))
        def _():
            spo_r_ref[0] = st_ret[...].T
            spo_g_ref[0] = st_gla[...].T
    else:
        spo_r_ref[0] = st_ret[...].T
        spo_g_ref[0] = st_gla[...].T


def _const_spec(shape):
    nd = len(shape)
    return pl.BlockSpec(shape, lambda *_: (0,) * nd, pipeline_mode=pl.Buffered(1))


def _mixer_call(x_prompt, x_streams, cos, sin, st_prompt, st_streams, weights, consts, *, n_chunks, route):
    tm = n_chunks * CHUNK
    n_streams = x_streams.shape[0]
    tiles_s = n_streams // n_chunks
    if x_prompt is not None:
        n_rows, n_seq, _ = x_prompt.shape
        tpr = n_seq // tm
        tiles_p = n_rows * tpr
    else:
        n_rows, tpr, tiles_p = 1, 1, 0
    n_tiles = tiles_p + tiles_s
    p_idx = lambda i: jnp.minimum(i, tiles_p - 1)
    s_idx = lambda i: jnp.maximum(i - tiles_p, 0)

    st_blk = (1, HK, HEAD_V)
    in_specs, args = [], []
    if tiles_p:
        in_specs.append(pl.BlockSpec((1, tm, D_MODEL), lambda i: (p_idx(i) // tpr, p_idx(i) % tpr, 0)))
        args.append(x_prompt)
        cos_spec = pl.BlockSpec((tm, LANES), lambda i: (jnp.where(i < tiles_p, i % tpr, tpr), 0))
    else:
        cos_spec = pl.BlockSpec((tm, LANES), lambda i: (0, 0))
    once = pl.Buffered(1)
    stream_in = pl.BlockSpec((n_chunks, HK, HEAD_V), lambda i: (s_idx(i), 0, 0), pipeline_mode=once)
    in_specs += [pl.BlockSpec((n_chunks, CHUNK, D_MODEL), lambda i: (s_idx(i), 0, 0), pipeline_mode=once),
                 cos_spec, cos_spec, _const_spec(st_blk), _const_spec(st_blk), stream_in, stream_in]
    args += [x_streams, cos, sin, st_prompt[0], st_prompt[1], st_streams[0], st_streams[1]]
    in_specs += [_const_spec(w.shape) for w in weights] + [_const_spec(c.shape) for c in consts]
    args += list(weights) + list(consts)

    out_shape = (jax.ShapeDtypeStruct((n_tiles * tm, D_MODEL), F32),
                 jax.ShapeDtypeStruct((n_tiles * LOCAL_ROWS, D_MODEL // 2), jnp.int32),
                 jax.ShapeDtypeStruct((n_tiles, ROUTE_ROWS, tm), jnp.int32),
                 jax.ShapeDtypeStruct((n_tiles, SUBLANES, LANES), jnp.int32),
                 jax.ShapeDtypeStruct((n_rows, HK, HEAD_V), F32), jax.ShapeDtypeStruct((n_rows, HK, HEAD_V), F32),
                 jax.ShapeDtypeStruct((n_streams, HK, HEAD_V), F32),
                 jax.ShapeDtypeStruct((n_streams, HK, HEAD_V), F32))
    row_spec = pl.BlockSpec(st_blk, lambda i: (p_idx(i) // tpr if tiles_p else 0, 0, 0))
    stream_spec = pl.BlockSpec((n_chunks, HK, HEAD_V), lambda i: (s_idx(i), 0, 0))
    out_specs = (pl.BlockSpec((tm, D_MODEL), lambda i: (i, 0)),
                 pl.BlockSpec((LOCAL_ROWS, D_MODEL // 2), lambda i: (i, 0)),
                 pl.BlockSpec((1, ROUTE_ROWS, tm), lambda i: (i, 0, 0)),
                 pl.BlockSpec((1, SUBLANES, LANES), lambda i: (i, 0, 0)),
                 row_spec, row_spec, stream_spec, stream_spec)

    scratch = [pltpu.VMEM((tm, HK), BF16), pltpu.VMEM((tm, HK), BF16), pltpu.VMEM((tm, HK), BF16),
               pltpu.VMEM((tm, HK), BF16), pltpu.VMEM((tm, HV), BF16),
               pltpu.VMEM((tm, HK), BF16), pltpu.VMEM((tm, HK), BF16), pltpu.VMEM((tm, HV), BF16),
               pltpu.VMEM((n_chunks, SUBLANES, HK), F32),
               pltpu.VMEM((tm, HV), F32), pltpu.VMEM((tm, HV), F32),
               pltpu.VMEM((HEAD_V, HK), F32), pltpu.VMEM((HEAD_V, HK), F32)]
    scratch += [pltpu.VMEM((n_chunks, HEAD_V, HK), F32)] * 4

    return pl.pallas_call(
        functools.partial(_mixer_kernel, n_chunks=n_chunks, tiles_p=tiles_p, tiles_per_row=tpr, route=route),
        grid=(n_tiles,), in_specs=in_specs, out_specs=out_specs, out_shape=out_shape, scratch_shapes=scratch,
        compiler_params=pltpu.CompilerParams(dimension_semantics=("arbitrary",), vmem_limit_bytes=VMEM_LIMIT),
        name="mixer" if route else "mixer_meta",
    )(*args)


def _plan_kernel(ku_ref, src_ref, blk_ref, run_first, run_step, next_unit, seg_start, seg_units, *, n_tiles, n_blocks):
    shift = int(math.log2(BLOCK_UNITS))
    group_shift = int(math.log2(SUBLANES * LANES))

    def init_tile(t, c):
        next_unit[t] = t * LOCAL_UNITS
        return c

    lax.fori_loop(0, n_tiles, init_tile, 0)

    def init_block(b, c):
        blk_ref[0, b] = N_EXPERTS - 1
        blk_ref[1, b] = 0
        blk_ref[2, b] = 0
        blk_ref[3, b] = -1
        blk_ref[4, b] = 0
        return c

    lax.fori_loop(0, n_blocks, init_block, 0)

    def per_expert(e, carry):
        g0, position = carry

        def per_tile(t, c):
            units, d_prev = c
            base = next_unit[t]
            d = base - units
            run_first[e * n_tiles + t] = units
            run_step[e * n_tiles + t] = d - d_prev
            k = ku_ref[t, e]
            next_unit[t] = base + k
            return units + k, d

        units, _ = lax.fori_loop(0, n_tiles, per_tile, (0, 0), unroll=4)
        g_pad = g0 + (((units + (BLOCK_UNITS - 1)) >> shift) << shift)
        seg_start[e] = g0
        seg_units[e] = units

        def set_block(b, c):
            blk_ref[0, b] = e
            blk_ref[1, b] = jnp.minimum(g0 + units - (b << shift), BLOCK_UNITS)
            blk_ref[3, b] = g_pad >> shift
            blk_ref[4, b] = position & 1
            return c

        lax.fori_loop(g0 >> shift, g_pad >> shift, set_block, 0)
        return g_pad, position + (units > 0).astype(jnp.int32)

    g_total, _ = lax.fori_loop(0, N_EXPERTS, per_expert, (0, 0))
    blk_ref[2, 0] = g_total >> shift

    src_ref[...] = jnp.zeros(src_ref.shape, jnp.int32)
    in_group = (lax.broadcasted_iota(jnp.int32, (SUBLANES, LANES), 0) * LANES
                + lax.broadcasted_iota(jnp.int32, (SUBLANES, LANES), 1))

    def expert_units(e, c):
        g0 = seg_start[e]
        units = seg_units[e]
        g_pad = g0 + (((units + (BLOCK_UNITS - 1)) >> shift) << shift)

        def per_group(grp, c2):
            g = in_group + (grp << group_shift)
            o = g - g0
            mine = jnp.logical_and(o >= 0, g < g_pad)
            o_eff = jnp.where(o < units, o, (o >> shift) << shift)

            def per_tile(t, acc):
                return acc + jnp.where(o_eff >= run_first[e * n_tiles + t], run_step[e * n_tiles + t], 0)

            offset = lax.fori_loop(0, n_tiles, per_tile, jnp.zeros((SUBLANES, LANES), jnp.int32), unroll=4)
            rows = pl.ds(pl.multiple_of(grp * SUBLANES, SUBLANES), SUBLANES)
            src_ref[rows, :] = jnp.where(mine, o_eff + offset, src_ref[rows, :])
            return c2

        lax.fori_loop(g0 >> group_shift, (g_pad + (SUBLANES * LANES - 1)) >> group_shift, per_group, 0)
        return c

    lax.fori_loop(0, N_EXPERTS, expert_units, 0)


def _plan_call(ku, n_blocks):
    n_tiles = ku.shape[0]
    smem = pl.BlockSpec(memory_space=pltpu.SMEM)
    group = SUBLANES * LANES
    src_rows = -(-n_blocks * BLOCK_UNITS // group) * SUBLANES
    src, blk = pl.pallas_call(
        functools.partial(_plan_kernel, n_tiles=n_tiles, n_blocks=n_blocks),
        in_specs=[smem], out_specs=(pl.BlockSpec(memory_space=pltpu.VMEM), smem),
        out_shape=(jax.ShapeDtypeStruct((src_rows, LANES), jnp.int32),
                   jax.ShapeDtypeStruct((BLK_ROWS, n_blocks), jnp.int32)),
        scratch_shapes=[pltpu.SMEM((N_EXPERTS * n_tiles,), jnp.int32), pltpu.SMEM((N_EXPERTS * n_tiles,), jnp.int32),
                        pltpu.SMEM((n_tiles,), jnp.int32), pltpu.SMEM((N_EXPERTS,), jnp.int32),
                        pltpu.SMEM((N_EXPERTS,), jnp.int32)],
        name="moe_plan",
    )(ku)
    return src.reshape(-1), blk


def _experts_kernel(src_ref, blk_ref, x_hbm, wg_hbm, wu_hbm, wd_hbm, y_hbm,
                    xin, yout, wgf, wuf, wdf, wgub, wdb, sem_in, sem_out, sem_w):
    b = pl.program_id(0)
    used = blk_ref[2, 0]
    slot = b & 1

    def weight_copies(e, s):
        return [pltpu.make_async_copy(hbm.at[e], buf.at[s], sem_w.at[s])
                for hbm, buf in ((wg_hbm, wgf), (wu_hbm, wuf), (wd_hbm, wdf))]

    def unit_rows(blk, u):
        return pl.ds(pl.multiple_of(src_ref[blk * BLOCK_UNITS + u] * UNIT, UNIT), UNIT)

    def in_copy(blk, u, s):
        return pltpu.make_async_copy(x_hbm.at[unit_rows(blk, u), :], xin.at[s, pl.ds(u * UNIT, UNIT), :],
                                     sem_in.at[s])

    def out_copy(blk, u, s):
        return pltpu.make_async_copy(yout.at[s, pl.ds(u * UNIT, UNIT), :], y_hbm.at[unit_rows(blk, u), :],
                                     sem_out.at[s])

    @pl.when(jnp.logical_and(b == 0, used > 0))
    def _():
        for copy in weight_copies(blk_ref[0, 0], blk_ref[4, 0]):
            copy.start(priority=1)
        for u in range(BLOCK_UNITS):
            in_copy(0, u, 0).start()

    @pl.when(b < used)
    def _():
        expert = blk_ref[0, b]
        changed = jnp.logical_or(b == 0, expert != blk_ref[0, jnp.maximum(b - 1, 0)])

        @pl.when(changed)
        def _():
            wslot = blk_ref[4, b]
            later = blk_ref[3, b]
            following = jnp.where(later < used, blk_ref[0, jnp.minimum(later, pl.num_programs(0) - 1)], -1)
            for copy in weight_copies(expert, wslot):
                copy.wait()

            @pl.when(following >= 0)
            def _():
                for copy in weight_copies(following, 1 - wslot):
                    copy.start(priority=1)

            wgub[:, :D_EXPERT] = wgf[wslot].astype(BF16)
            wgub[:, D_EXPERT:] = wuf[wslot].astype(BF16)
            wdb[...] = wdf[wslot].astype(BF16)

        @pl.when(b + 1 < used)
        def _():
            for u in range(BLOCK_UNITS):
                in_copy(b + 1, u, 1 - slot).start()

        pltpu.make_async_copy(x_hbm.at[pl.ds(0, MOE_BLOCK), :], xin.at[slot], sem_in.at[slot]).wait()

        n_real = blk_ref[1, b]

        def mlp(rows):
            xb = _unpack_halves(xin[slot, :rows, :])
            gate_up = _dot(xb, wgub[...])
            gate = gate_up[:, :D_EXPERT]
            hid = (gate * _sigmoid(gate)) * gate_up[:, D_EXPERT:]
            yout[slot, :rows, :] = _pack_halves(_dot(hid.astype(BF16), wdb[...]).astype(BF16).astype(F32))

        @pl.when(n_real > BLOCK_UNITS // 2)
        def _():
            mlp(MOE_BLOCK)

        @pl.when(n_real <= BLOCK_UNITS // 2)
        def _():
            mlp(MOE_BLOCK // 2)


        @pl.when(n_real == BLOCK_UNITS)
        def _():
            for u in range(BLOCK_UNITS):
                out_copy(b, u, slot).start(priority=u % 2)

        @pl.when(n_real < BLOCK_UNITS)
        def _():
            lax.fori_loop(0, n_real, lambda u, c: (out_copy(b, u, slot).start(), c)[1], 0)

        @pl.when(jnp.logical_and(b > 0, blk_ref[1, jnp.maximum(b - 1, 0)] == BLOCK_UNITS))
        def _():
            pltpu.make_async_copy(yout.at[1 - slot], y_hbm.at[pl.ds(0, MOE_BLOCK), :], sem_out.at[1 - slot]).wait()

        @pl.when(jnp.logical_and(b > 0, blk_ref[1, jnp.maximum(b - 1, 0)] < BLOCK_UNITS))
        def _():
            lax.fori_loop(0, blk_ref[1, b - 1], lambda u, c: (out_copy(b - 1, u, 1 - slot).wait(), c)[1], 0)

        @pl.when(b == used - 1)
        def _():
            lax.fori_loop(0, n_real, lambda u, c: (out_copy(b, u, slot).wait(), c)[1], 0)


def _experts_call(src, blk, xloc, wg, wu, wd, n_blocks):
    hbm = pl.BlockSpec(memory_space=pl.ANY)
    grid_spec = pltpu.PrefetchScalarGridSpec(
        num_scalar_prefetch=2,
        grid=(n_blocks,),
        in_specs=[hbm, hbm, hbm, hbm],
        out_specs=hbm,
        scratch_shapes=[pltpu.VMEM((2, MOE_BLOCK, D_MODEL // 2), jnp.int32),
                        pltpu.VMEM((2, MOE_BLOCK, D_MODEL // 2), jnp.int32),
                        pltpu.VMEM((2, D_MODEL, D_EXPERT), F32), pltpu.VMEM((2, D_MODEL, D_EXPERT), F32),
                        pltpu.VMEM((2, D_EXPERT, D_MODEL), F32),
                        pltpu.VMEM((D_MODEL, 2 * D_EXPERT), BF16), pltpu.VMEM((D_EXPERT, D_MODEL), BF16),
                        pltpu.SemaphoreType.DMA((2,)), pltpu.SemaphoreType.DMA((2,)),
                        pltpu.SemaphoreType.DMA((2,))],
    )
    return pl.pallas_call(
        _experts_kernel, grid_spec=grid_spec,
        out_shape=jax.ShapeDtypeStruct(xloc.shape, jnp.int32),
        input_output_aliases={2: 0},
        compiler_params=pltpu.CompilerParams(dimension_semantics=("arbitrary",), vmem_limit_bytes=VMEM_LIMIT),
        name="moe_experts",
    )(src, blk, xloc, wg, wu, wd)


def _combine_kernel(route_ref, x_ref, y_ref, gfin_ref, out_a_hbm, out_b_hbm, obuf, sem, *, tm, tiles_a):
    t = pl.program_id(0)
    slot = t & 1

    def tile_out(dst_hbm, tile, s):
        return pltpu.make_async_copy(obuf.at[s], dst_hbm.at[pl.ds(pl.multiple_of(tile * tm, tm), tm), :], sem.at[s])

    @pl.when(t >= 2)
    def _():
        tile_out(out_a_hbm, 0, slot).wait()

    rec = route_ref[0]
    recf = jnp.concatenate([rec[2:4, :].astype(F32), pltpu.bitcast(rec, F32)[4:6, :],
                            jnp.zeros((LANES - 4, tm), F32)], axis=0)
    cols = recf.T
    lrow = lax.broadcasted_iota(jnp.int32, (tm, LOCAL_ROWS), 1).astype(F32)
    select = jnp.where(lrow == cols[:, 0:1], cols[:, 2:3], jnp.where(lrow == cols[:, 1:2], cols[:, 3:4], 0.0))
    xf = x_ref[...] + _dot(select.astype(BF16), _unpack_halves(y_ref[...]))
    obuf[slot] = xf * lax.rsqrt(jnp.mean(xf * xf, axis=-1, keepdims=True) + NORM_EPS) * gfin_ref[...]

    @pl.when(t < tiles_a)
    def _():
        tile_out(out_a_hbm, t, slot).start()

    @pl.when(t >= tiles_a)
    def _():
        tile_out(out_b_hbm, t - tiles_a, slot).start()

    @pl.when(t == pl.num_programs(0) - 1)
    def _():
        tile_out(out_a_hbm, 0, slot).wait()

        @pl.when(t >= 1)
        def _():
            tile_out(out_a_hbm, 0, 1 - slot).wait()


def _combine_call(route, xnew, yloc, gfin, tm, tiles_a):
    n_tiles = route.shape[0]
    tiles_b = n_tiles - tiles_a
    return pl.pallas_call(
        functools.partial(_combine_kernel, tm=tm, tiles_a=tiles_a),
        grid=(n_tiles,),
        in_specs=[pl.BlockSpec((1, ROUTE_ROWS, tm), lambda t: (t, 0, 0)),
                  pl.BlockSpec((tm, D_MODEL), lambda t: (t, 0)),
                  pl.BlockSpec((LOCAL_ROWS, D_MODEL // 2), lambda t: (t, 0)),
                  pl.BlockSpec((1, D_MODEL), lambda t: (0, 0))],
        out_specs=(pl.BlockSpec(memory_space=pl.ANY), pl.BlockSpec(memory_space=pl.ANY)),
        out_shape=(jax.ShapeDtypeStruct((tiles_a * tm, D_MODEL), F32),
                   jax.ShapeDtypeStruct((tiles_b * tm, D_MODEL), F32)),
        scratch_shapes=[pltpu.VMEM((2, tm, D_MODEL), F32), pltpu.SemaphoreType.DMA((2,))],
        compiler_params=pltpu.CompilerParams(dimension_semantics=("arbitrary",), vmem_limit_bytes=VMEM_LIMIT),
        name="moe_combine",
    )(route, xnew, yloc, gfin)


def _prep_weights(g_mix, w_in, w_gla_gk2, b_gla_gk, g_gla_norm, w_br_ret, w_br_gla, w_out, g_ffn,
                  w_rg, b_rg, w_re, b_re):
    w_int = jnp.swapaxes(w_in, 0, 1).astype(BF16)
    wgk = jnp.pad(w_gla_gk2, ((0, LANES - GLA_RANK), (0, 0))).astype(BF16)
    def router_rows(groups, experts):
        return jnp.concatenate([groups, jnp.zeros((SUBLANES - N_GROUPS,) + groups.shape[1:], F32), experts,
                                jnp.zeros((ROUTER_LANES - SUBLANES - N_EXPERTS,) + experts.shape[1:], F32)], axis=0)

    wrt = router_rows(w_rg.T, w_re.T)
    brt = router_rows(b_rg, b_re).reshape(1, ROUTER_LANES)
    wrt_hi = wrt.astype(BF16)
    wrt = jnp.concatenate([wrt_hi, (wrt - wrt_hi.astype(F32)).astype(BF16)], axis=0)
    return (g_mix.reshape(1, D_MODEL), w_int, wgk, b_gla_gk.reshape(1, HK), g_gla_norm.reshape(1, HEAD_V),
            w_br_ret.astype(BF16), w_br_gla.astype(BF16), w_out.astype(BF16), g_ffn.reshape(1, D_MODEL), wrt, brt)


def _tri_consts(tm):
    r = np.arange(tm)
    low = (r[None, :] <= r[:, None])
    bd = low & ((r[None, :] // CHUNK) == (r[:, None] // CHUNK))
    return jnp.asarray(bd, BF16), jnp.asarray(low, BF16)


def _state_to_kernel(s):
    return s.reshape(s.shape[0], HK, HEAD_V)


def _state_from_kernel(s):
    return s.reshape(1, s.shape[0], N_HEADS, HEAD_K, HEAD_V)


def kernel(x_prompt, x_sample, state_ret, state_gla, meta_tokens, g_mix, w_in, w_gla_gk2, b_gla_gk, g_gla_norm, w_br_ret, w_br_gla, w_out, g_ffn, w_router_group, b_router_group, w_router_expert, b_router_expert, w_exp_gate, w_exp_up, w_exp_down, g_final):
    n_b, seq, _ = x_prompt.shape
    n_s, dec_seq, _ = x_sample.shape
    depth = state_ret.shape[0]
    assert depth == 1 and dec_seq == CHUNK and seq % TILE_ROWS == 0 and n_s % TILE_CHUNKS == 0
    cos_t, sin_t, cos_m, sin_m = _rotary_tables(seq, PAST_LEN)
    weights = _prep_weights(g_mix[0], w_in[0], w_gla_gk2[0], b_gla_gk[0], g_gla_norm[0], w_br_ret[0], w_br_gla[0],
                            w_out[0], g_ffn[0], w_router_group[0], b_router_group[0], w_router_expert[0],
                            b_router_expert[0])
    zero_state = jnp.zeros((1, HK, HEAD_V), F32)

    x_meta = jnp.concatenate([jnp.zeros((CHUNK - N_META, D_MODEL), F32), meta_tokens.astype(F32)], axis=0)
    meta_out = _mixer_call(None, x_meta.reshape(1, CHUNK, D_MODEL), cos_m, sin_m, (zero_state, zero_state),
                           (zero_state, zero_state), weights, _tri_consts(CHUNK), n_chunks=1, route=False)
    meta_ret, meta_gla = meta_out[6], meta_out[7]

    xnew, xloc, route, ku, ret_p, gla_p, ret_s, gla_s = _mixer_call(
        x_prompt, x_sample, cos_t, sin_t, (meta_ret, meta_gla),
        (_state_to_kernel(state_ret[0]), _state_to_kernel(state_gla[0])), weights, _tri_consts(TILE_ROWS),
        n_chunks=TILE_CHUNKS, route=True)

    n_tiles = route.shape[0]
    n_blocks = n_tiles * LOCAL_UNITS // BLOCK_UNITS + N_EXPERTS
    src, blk = _plan_call(ku[:, 0, :N_EXPERTS], n_blocks)
    yloc = _experts_call(src, blk, xloc, w_exp_gate[0], w_exp_up[0], w_exp_down[0], n_blocks)
    y_p, y_s = _combine_call(route, xnew, yloc, g_final.reshape(1, D_MODEL), TILE_ROWS, n_b * seq // TILE_ROWS)

    return (y_p.reshape(n_b, seq, D_MODEL), y_s.reshape(n_s, dec_seq, D_MODEL),
            _state_from_kernel(ret_p), _state_from_kernel(gla_p), _state_from_kernel(ret_s), _state_from_kernel(gla_s))
```

```python
import functools
import math

import jax
import jax.numpy as jnp
import numpy as np
from jax import lax
from jax.experimental import pallas as pl
from jax.experimental.pallas import tpu as pltpu

D_MODEL = 1024
CHUNK = 64
PAST_LEN = 1024
N_META = 16
N_HEADS = 4
HEAD_K = 64
HEAD_V = 128
HK = N_HEADS * HEAD_K
HV = N_HEADS * HEAD_V
GLA_RANK = 16
GATE_NORM = 16.0
ROPE_BASE = 10000.0
N_GROUPS = 4
EXPERTS_PER_GROUP = 8
N_EXPERTS = N_GROUPS * EXPERTS_PER_GROUP
D_EXPERT = 512
NORM_EPS = 1e-6

LANES = 128
SUBLANES = 8
TILE_CHUNKS = 8
TILE_ROWS = TILE_CHUNKS * CHUNK
ROUTE_ROWS = 8
ROUTER_LANES = 128
UNIT = SUBLANES
MOE_BLOCK = 512
BLOCK_UNITS = MOE_BLOCK // UNIT
BLK_ROWS = 5
LOCAL_ROWS = 2 * TILE_ROWS + N_EXPERTS * UNIT
LOCAL_UNITS = LOCAL_ROWS // UNIT
VMEM_LIMIT = 56 * 1024 * 1024

C_RQ, C_RK, C_RV, C_RG = 0, 256, 512, 1024
C_GQ, C_GK, C_GV, C_GG = 1536, 1792, 2048, 2560
W_MIX = 3072
C_Z = W_MIX + GLA_RANK

F32 = jnp.float32
BF16 = jnp.bfloat16
LOG_G = tuple(math.log1p(-(2.0 ** (-5.0 - h))) for h in range(N_HEADS))


def _dot(a, b):
    return jnp.dot(a, b, preferred_element_type=F32)


def _dot_nt(a, b):
    return lax.dot_general(a, b, (((1,), (1,)), ((), ())), preferred_element_type=F32)


def _dot_tn(a, b):
    return lax.dot_general(a, b, (((0,), (0,)), ((), ())), preferred_element_type=F32)


def _sigmoid(x):
    return 1.0 / (1.0 + jnp.exp(-x))


def _pack_halves(x):
    half = x.shape[1] // 2
    bits = pltpu.bitcast(x, jnp.int32)
    return lax.shift_right_logical(bits[:, :half], 16) | (bits[:, half:] & jnp.int32(-65536))


def _unpack_halves(w):
    lo = pltpu.bitcast(lax.shift_left(w, 16), F32)
    hi = pltpu.bitcast(w & jnp.int32(-65536), F32)
    return jnp.concatenate([lo, hi], axis=1).astype(BF16)


def _head_of_lane(shape, width):
    return lax.broadcasted_iota(jnp.int32, shape, len(shape) - 1) >> int(math.log2(width))


def _per_head_lane_const(vals, shape, width):
    hd = _head_of_lane(shape, width)
    out = jnp.full(shape, vals[N_HEADS - 1], F32)
    for h in range(N_HEADS - 2, -1, -1):
        out = jnp.where(hd == h, vals[h], out)
    return out


def _tables_kernel(inv_ref, cp_ref, sp_ref, cm_ref, sm_ref, *, seq, past_len):
    inv = inv_ref[...].reshape(1, 1, LANES)
    lane = lax.broadcasted_iota(jnp.int32, (1, 1, LANES), 2)
    sign = jnp.where((lane & (HEAD_K - 1)) < (HEAD_K // 2), -1.0, 1.0)
    off = lax.broadcasted_iota(jnp.int32, (1, CHUNK, LANES), 1).astype(F32) * inv
    c_off, s_off = jnp.cos(off), jnp.sin(off)

    def chunks(n, first_pos):
        base = (lax.broadcasted_iota(jnp.int32, (n, 1, LANES), 0) * CHUNK + first_pos).astype(F32) * inv
        c_base, s_base = jnp.cos(base), jnp.sin(base)
        cos = (c_base * c_off - s_base * s_off).reshape(n * CHUNK, LANES)
        sin = ((s_base * c_off + c_base * s_off) * sign).reshape(n * CHUNK, LANES)
        return cos, sin

    cp_ref[0:seq, :], sp_ref[0:seq, :] = chunks(seq // CHUNK, 0)
    cos_s, sin_s = chunks(1, past_len)
    for c in range(TILE_CHUNKS):
        cp_ref[seq + c * CHUNK:seq + (c + 1) * CHUNK, :] = cos_s
        sp_ref[seq + c * CHUNK:seq + (c + 1) * CHUNK, :] = sin_s
    cm_ref[...], sm_ref[...] = chunks(1, -CHUNK)


def _rotary_tables(seq, past_len):
    half = HEAD_K // 2
    inv = ROPE_BASE ** (-2.0 * jnp.arange(half, dtype=F32) / HEAD_K)
    inv = jnp.tile(inv, LANES // half).reshape(1, LANES)
    shp = lambda r: jax.ShapeDtypeStruct((r, LANES), F32)
    return pl.pallas_call(
        functools.partial(_tables_kernel, seq=seq, past_len=past_len),
        out_shape=(shp(seq + TILE_ROWS), shp(seq + TILE_ROWS), shp(CHUNK), shp(CHUNK)),
        compiler_params=pltpu.CompilerParams(vmem_limit_bytes=VMEM_LIMIT),
        name="rotary_tables",
    )(inv)


def _mixer_kernel(*refs, n_chunks, tiles_p, tiles_per_row, route):
    tm = n_chunks * CHUNK
    if tiles_p:
        xp_ref, refs = refs[0], refs[1:]
    (xs_ref, cos_ref, sin_ref, spi_r_ref, spi_g_ref, ssi_r_ref, ssi_g_ref,
     gmix_ref, wint_ref, wgk_ref, bgk_ref, gnorm_ref, wbr_ref, wbg_ref, wout_ref,
     gffn_ref, wrt_ref, brt_ref, bdtri_ref, ltri_ref,
     xnew_ref, xloc_ref, route_ref, ku_ref, spo_r_ref, spo_g_ref, sso_r_ref, sso_g_ref,
     qb, qdb, kb, kkb, vb, gqb, gkkb, gvb, ga, o_ret, o_gla, st_ret, st_gla,
     sin_r, sin_g, sout_r, sout_g) = refs

    i = pl.program_id(0)
    if tiles_p:
        is_s = i >= tiles_p
        t_idx = jnp.minimum(i, tiles_p - 1) % tiles_per_row
        x = jnp.where(is_s, xs_ref[...].reshape(tm, D_MODEL), xp_ref[...].reshape(tm, D_MODEL))

        @pl.when(jnp.logical_and(jnp.logical_not(is_s), t_idx == 0))
        def _():
            st_ret[...] = spi_r_ref[0].T
            st_gla[...] = spi_g_ref[0].T

        @pl.when(i == 0)
        def _():
            sin_r[...] = jnp.zeros(sin_r.shape, F32)
            sin_g[...] = jnp.zeros(sin_g.shape, F32)

        @pl.when(is_s)
        def _():
            for c in range(n_chunks):
                sin_r[c] = ssi_r_ref[c].T
                sin_g[c] = ssi_g_ref[c].T
    else:
        is_s = None
        x = xs_ref[...].reshape(tm, D_MODEL)

    h = x * lax.rsqrt(jnp.mean(x * x, axis=-1, keepdims=True) + NORM_EPS) * gmix_ref[...]
    hb = h.astype(BF16)

    def proj(c0, width):
        return _dot_nt(hb, wint_ref[c0:c0 + width, :])

    cos3 = jnp.concatenate([cos_ref[...]] * 2, axis=1).reshape(n_chunks, CHUNK, HK)
    sin3 = jnp.concatenate([sin_ref[...]] * 2, axis=1).reshape(n_chunks, CHUNK, HK)
    lane_hk = lax.broadcasted_iota(jnp.int32, (tm, HK), 1)
    first_half = (lane_hk & (HEAD_K - 1)) < (HEAD_K // 2)

    def rotary(t):
        swapped = jnp.where(first_half, pltpu.roll(t, HK - HEAD_K // 2, 1), pltpu.roll(t, HEAD_K // 2, 1))
        t3 = t.reshape(n_chunks, CHUNK, HK)
        return (t3 * cos3 + swapped.reshape(n_chunks, CHUNK, HK) * sin3).reshape(tm, HK)

    logg_hk = _per_head_lane_const(LOG_G, (CHUNK, HK), HEAD_K)
    l_idx = lax.broadcasted_iota(jnp.int32, (CHUNK, HK), 0).astype(F32)
    qdec = jnp.exp((l_idx + 1.0) * logg_hk)
    kdec = jnp.exp((CHUNK - 1.0 - l_idx) * logg_hk)
    cdec = jnp.exp(float(CHUNK) * _per_head_lane_const(LOG_G, (1, HK), HEAD_K))
    r_idx = lax.broadcasted_iota(jnp.int32, (N_HEADS * CHUNK, CHUNK), 0)
    m_idx = lax.broadcasted_iota(jnp.int32, (N_HEADS * CHUNK, CHUNK), 1)
    logg_rows = jnp.full((N_HEADS * CHUNK, CHUNK), LOG_G[N_HEADS - 1], F32)
    for hh in range(N_HEADS - 2, -1, -1):
        logg_rows = jnp.where((r_idx >> int(math.log2(CHUNK))) == hh, LOG_G[hh], logg_rows)
    dmat = jnp.exp(jnp.abs((r_idx & (CHUNK - 1)) - m_idx).astype(F32) * logg_rows)

    glr = jnp.where(lax.broadcasted_iota(jnp.int32, (tm, LANES), 1) < GLA_RANK, proj(W_MIX, LANES), 0.0)
    rq = rotary(proj(C_RQ, HK))
    qb[...] = rq.astype(BF16)
    qdb[...] = (rq.reshape(n_chunks, CHUNK, HK) * qdec).reshape(tm, HK).astype(BF16)
    gl = _dot(glr.astype(BF16), wgk_ref[...]) + bgk_ref[...]
    rk = rotary(proj(C_RK, HK)) * (HEAD_K ** -0.5)
    kb[...] = rk.astype(BF16)
    kkb[...] = (rk.reshape(n_chunks, CHUNK, HK) * kdec).reshape(tm, HK).astype(BF16)
    log_a = (jnp.minimum(gl, 0.0) - jnp.log1p(jnp.exp(-jnp.abs(gl)))) / GATE_NORM
    la_hi = log_a.astype(BF16)
    la_lo = (log_a - la_hi.astype(F32)).astype(BF16)
    vb[...] = proj(C_RV, HV).astype(BF16)
    bdtri = bdtri_ref[...]
    bcum = _dot(bdtri, la_hi) + _dot(bdtri, la_lo)
    gqb[...] = (proj(C_GQ, HK) * (HEAD_K ** -0.5)).astype(BF16)
    gvb[...] = proj(C_GV, HV).astype(BF16)
    gk = proj(C_GK, HK)
    b3 = bcum.reshape(n_chunks, CHUNK, HK)
    bl3 = b3[:, CHUNK - 1:CHUNK, :]
    gkkb[...] = (gk.reshape(n_chunks, CHUNK, HK) * jnp.exp(bl3 - b3)).reshape(tm, HK).astype(BF16)
    ga[...] = jnp.broadcast_to(jnp.exp(bl3), (n_chunks, SUBLANES, HK))

    def stack_masked(a, width):
        head = _head_of_lane(a.shape, width)
        zero = jnp.zeros_like(a)
        return jnp.concatenate([jnp.where(head == hh, a, zero) for hh in range(N_HEADS)], axis=0)

    def heads_to_rows(a):
        return jnp.concatenate([a[:, hh * HEAD_V:(hh + 1) * HEAD_V] for hh in range(N_HEADS)], axis=0)

    def rows_to_heads(a):
        return jnp.concatenate([a[hh * CHUNK:(hh + 1) * CHUNK, :] for hh in range(N_HEADS)], axis=1)

    chunk_rows = [slice(c * CHUNK, (c + 1) * CHUNK) for c in range(n_chunks)]
    probs = [(_dot_nt(stack_masked(qb[r, :], HEAD_K), kb[r, :]) * dmat).astype(BF16) for r in chunk_rows]
    inc_ret = [_dot_tn(heads_to_rows(vb[r, :]), stack_masked(kkb[r, :], HEAD_K)) for r in chunk_rows]
    inc_gla = [_dot_tn(heads_to_rows(gvb[r, :]), stack_masked(gkkb[r, :], HEAD_K)) for r in chunk_rows]
    rg = proj(C_RG, HV)
    gg = proj(C_GG, HV)
    s_in, g_out = [], []
    s_cur, g_cur = st_ret[...], st_gla[...]
    for c in range(n_chunks):
        if is_s is None:
            s_cur, g_cur = ssi_r_ref[c].T, ssi_g_ref[c].T
        else:
            s_cur = jnp.where(is_s, sin_r[c], s_cur)
            g_cur = jnp.where(is_s, sin_g[c], g_cur)
        s_in.append(s_cur.astype(BF16))
        s_cur = s_cur * cdec + inc_ret[c]
        g_cur = g_cur * ga[c][0:1, :] + inc_gla[c]
        g_out.append(g_cur.astype(BF16))
        sout_r[c] = s_cur
        sout_g[c] = g_cur
    st_ret[...] = s_cur
    st_gla[...] = g_cur
    for c, r in enumerate(chunk_rows):
        v = vb[r, :]
        intra = jnp.concatenate(
            [_dot(probs[c][hh * CHUNK:(hh + 1) * CHUNK, :], v[:, hh * HEAD_V:(hh + 1) * HEAD_V])
             for hh in range(N_HEADS)], axis=1)
        inter = rows_to_heads(_dot_nt(stack_masked(qdb[r, :], HEAD_K), s_in[c]))
        o_ret[r, :] = intra + inter
        o_gla[r, :] = rows_to_heads(_dot_nt(stack_masked(gqb[r, :], HEAD_K), g_out[c]))

    gnorm = gnorm_ref[...]
    orr = o_ret[...]
    ogg = o_gla[...]
    ret_parts, gla_parts = [], []
    for hh in range(N_HEADS):
        sl = slice(hh * HEAD_V, (hh + 1) * HEAD_V)
        oh = orr[:, sl]
        mu = jnp.mean(oh, axis=-1, keepdims=True)
        dev = oh - mu
        var = jnp.mean(dev * dev, axis=-1, keepdims=True)
        ret_parts.append(dev * lax.rsqrt(var + NORM_EPS))
        og = ogg[:, sl]
        gla_parts.append(og * lax.rsqrt(jnp.mean(og * og, axis=-1, keepdims=True) + NORM_EPS) * gnorm)
    o_r = jnp.concatenate(ret_parts, axis=1) * (rg * _sigmoid(rg))
    o_g = jnp.concatenate(gla_parts, axis=1) * (gg * _sigmoid(gg))
    merged = (_sigmoid(proj(C_Z, D_MODEL)) * _dot(o_r.astype(BF16), wbr_ref[...])
              + _sigmoid(proj(C_Z + D_MODEL, D_MODEL)) * _dot(o_g.astype(BF16), wbg_ref[...]))
    xn = x + _dot(merged.astype(BF16), wout_ref[...])
    xnew_ref[...] = xn

    if route:
        h2 = xn * lax.rsqrt(jnp.mean(xn * xn, axis=-1, keepdims=True) + NORM_EPS) * gffn_ref[...]
        h2_hi = h2.astype(BF16)
        h2_lo = (h2 - h2_hi.astype(F32)).astype(BF16)
        hi_both = _dot_nt(h2_hi, wrt_ref[...])
        logits = (hi_both[:, :ROUTER_LANES] + _dot_nt(h2_lo, wrt_ref[:ROUTER_LANES, :])
                  + hi_both[:, ROUTER_LANES:]) + brt_ref[...]
        lt = logits.T
        row8 = lax.broadcasted_iota(jnp.int32, (SUBLANES, tm), 0)
        neg_inf = jnp.float32(-jnp.inf)
        glog = jnp.where(row8 < N_GROUPS, lt[0:SUBLANES, :], neg_inf)
        gmax = jnp.max(glog, axis=0, keepdims=True)
        grp = jnp.min(jnp.where(glog == gmax, row8, SUBLANES), axis=0, keepdims=True)
        p_grp = 1.0 / jnp.sum(jnp.exp(glog - gmax), axis=0, keepdims=True)
        le = jnp.zeros((SUBLANES, tm), F32)
        for g in range(N_GROUPS):
            le = jnp.where(grp == g, lt[SUBLANES * (g + 1):SUBLANES * (g + 2), :], le)
        m1 = jnp.max(le, axis=0, keepdims=True)
        i1 = jnp.min(jnp.where(le == m1, row8, SUBLANES), axis=0, keepdims=True)
        le2 = jnp.where(row8 == i1, neg_inf, le)
        m2 = jnp.max(le2, axis=0, keepdims=True)
        i2 = jnp.min(jnp.where(le2 == m2, row8, SUBLANES), axis=0, keepdims=True)
        e0 = grp * EXPERTS_PER_GROUP + i1
        e1 = grp * EXPERTS_PER_GROUP + i2
        t21 = jnp.exp(m2 - m1)
        w0 = p_grp / (1.0 + t21)
        w1 = p_grp * t21 / (1.0 + t21)

        erow = lax.broadcasted_iota(jnp.int32, (N_EXPERTS, tm), 0)
        hit0 = erow == e0
        hit1 = erow == e1
        onehot = jnp.where(jnp.logical_or(hit0, hit1), 1.0, 0.0).astype(BF16)
        cum = _dot_nt(onehot, ltri_ref[...])
        n_run = cum[:, tm - 1:tm]
        n_pad = jnp.ceil(n_run / UNIT) * UNIT
        rank0 = jnp.sum(jnp.where(hit0, cum - 1.0, 0.0), axis=0, keepdims=True)
        rank1 = jnp.sum(jnp.where(hit1, cum - 1.0, 0.0), axis=0, keepdims=True)
        start0 = jnp.sum(jnp.where(erow < e0, n_pad, 0.0), axis=0, keepdims=True)
        start1 = jnp.sum(jnp.where(erow < e1, n_pad, 0.0), axis=0, keepdims=True)
        ld0 = (start0 + rank0).astype(jnp.int32)
        ld1 = (start1 + rank1).astype(jnp.int32)
        lrow = lax.broadcasted_iota(jnp.int32, (LOCAL_ROWS, tm), 0)
        perm = jnp.where(jnp.logical_or(lrow == ld0, lrow == ld1), 1.0, 0.0).astype(BF16)
        xloc_ref[...] = _pack_halves(_dot(perm, h2_hi))
        lane_e = lax.broadcasted_iota(jnp.int32, (N_EXPERTS, LANES), 1)
        erow_l = lax.broadcasted_iota(jnp.int32, (N_EXPERTS, LANES), 0)
        units_row = jnp.sum(jnp.where(erow_l == lane_e, n_pad / UNIT, 0.0), axis=0, keepdims=True)
        ku_ref[...] = jnp.broadcast_to(units_row, (SUBLANES, LANES)).astype(jnp.int32).reshape(ku_ref.shape)
        zero_row = jnp.zeros((1, tm), jnp.int32)
        rec = jnp.concatenate([e0, e1, ld0, ld1, pltpu.bitcast(w0, jnp.int32), pltpu.bitcast(w1, jnp.int32),
                               zero_row, zero_row], axis=0)
        route_ref[...] = rec.reshape(route_ref.shape)
    else:
        xloc_ref[...] = jnp.zeros(xloc_ref.shape, jnp.int32)
        route_ref[...] = jnp.zeros(route_ref.shape, jnp.int32)
        ku_ref[...] = jnp.zeros(ku_ref.shape, jnp.int32)

    def stream_states_out():
        for c in range(n_chunks):
            sso_r_ref[c] = sout_r[c].T
            sso_g_ref[c] = sout_g[c].T

    if is_s is None:
        stream_states_out()
    else:
        pl.when(is_s)(stream_states_out)

    if tiles_p:
        @pl.when(jnp.logical_and(jnp.logical_not(is_s), t_idx == tiles_per_row - 1))
        def _():
            spo_r_ref[0] = st_ret[...].T
            spo_g_ref[0] = st_gla[...].T
    else:
        spo_r_ref[0] = st_ret[...].T
        spo_g_ref[0] = st_gla[...].T


def _const_spec(shape):
    nd = len(shape)
    return pl.BlockSpec(shape, lambda *_: (0,) * nd, pipeline_mode=pl.Buffered(1))


def _mixer_call(x_prompt, x_streams, cos, sin, st_prompt, st_streams, weights, consts, *, n_chunks, route):
    tm = n_chunks * CHUNK
    n_streams = x_streams.shape[0]
    tiles_s = n_streams // n_chunks
    if x_prompt is not None:
        n_rows, n_seq, _ = x_prompt.shape
        tpr = n_seq // tm
        tiles_p = n_rows * tpr
    else:
        n_rows, tpr, tiles_p = 1, 1, 0
    n_tiles = tiles_p + tiles_s
    p_idx = lambda i: jnp.minimum(i, tiles_p - 1)
    s_idx = lambda i: jnp.maximum(i - tiles_p, 0)

    st_blk = (1, HK, HEAD_V)
    in_specs, args = [], []
    if tiles_p:
        in_specs.append(pl.BlockSpec((1, tm, D_MODEL), lambda i: (p_idx(i) // tpr, p_idx(i) % tpr, 0)))
        args.append(x_prompt)
        cos_spec = pl.BlockSpec((tm, LANES), lambda i: (jnp.where(i < tiles_p, i % tpr, tpr), 0))
    else:
        cos_spec = pl.BlockSpec((tm, LANES), lambda i: (0, 0))
    once = pl.Buffered(1)
    stream_in = pl.BlockSpec((n_chunks, HK, HEAD_V), lambda i: (s_idx(i), 0, 0), pipeline_mode=once)
    in_specs += [pl.BlockSpec((n_chunks, CHUNK, D_MODEL), lambda i: (s_idx(i), 0, 0), pipeline_mode=once),
                 cos_spec, cos_spec, _const_spec(st_blk), _const_spec(st_blk), stream_in, stream_in]
    args += [x_streams, cos, sin, st_prompt[0], st_prompt[1], st_streams[0], st_streams[1]]
    in_specs += [_const_spec(w.shape) for w in weights] + [_const_spec(c.shape) for c in consts]
    args += list(weights) + list(consts)

    out_shape = (jax.ShapeDtypeStruct((n_tiles * tm, D_MODEL), F32),
                 jax.ShapeDtypeStruct((n_tiles * LOCAL_ROWS, D_MODEL // 2), jnp.int32),
                 jax.ShapeDtypeStruct((n_tiles, ROUTE_ROWS, tm), jnp.int32),
                 jax.ShapeDtypeStruct((n_tiles, SUBLANES, LANES), jnp.int32),
                 jax.ShapeDtypeStruct((n_rows, HK, HEAD_V), F32), jax.ShapeDtypeStruct((n_rows, HK, HEAD_V), F32),
                 jax.ShapeDtypeStruct((n_streams, HK, HEAD_V), F32),
                 jax.ShapeDtypeStruct((n_streams, HK, HEAD_V), F32))
    row_spec = pl.BlockSpec(st_blk, lambda i: (p_idx(i) // tpr if tiles_p else 0, 0, 0))
    stream_spec = pl.BlockSpec((n_chunks, HK, HEAD_V), lambda i: (s_idx(i), 0, 0))
    out_specs = (pl.BlockSpec((tm, D_MODEL), lambda i: (i, 0)),
                 pl.BlockSpec((LOCAL_ROWS, D_MODEL // 2), lambda i: (i, 0)),
                 pl.BlockSpec((1, ROUTE_ROWS, tm), lambda i: (i, 0, 0)),
                 pl.BlockSpec((1, SUBLANES, LANES), lambda i: (i, 0, 0)),
                 row_spec, row_spec, stream_spec, stream_spec)

    scratch = [pltpu.VMEM((tm, HK), BF16), pltpu.VMEM((tm, HK), BF16), pltpu.VMEM((tm, HK), BF16),
               pltpu.VMEM((tm, HK), BF16), pltpu.VMEM((tm, HV), BF16),
               pltpu.VMEM((tm, HK), BF16), pltpu.VMEM((tm, HK), BF16), pltpu.VMEM((tm, HV), BF16),
               pltpu.VMEM((n_chunks, SUBLANES, HK), F32),
               pltpu.VMEM((tm, HV), F32), pltpu.VMEM((tm, HV), F32),
               pltpu.VMEM((HEAD_V, HK), F32), pltpu.VMEM((HEAD_V, HK), F32)]
    scratch += [pltpu.VMEM((n_chunks, HEAD_V, HK), F32)] * 4

    return pl.pallas_call(
        functools.partial(_mixer_kernel, n_chunks=n_chunks, tiles_p=tiles_p, tiles_per_row=tpr, route=route),
        grid=(n_tiles,), in_specs=in_specs, out_specs=out_specs, out_shape=out_shape, scratch_shapes=scratch,
        compiler_params=pltpu.CompilerParams(dimension_semantics=("arbitrary",), vmem_limit_bytes=VMEM_LIMIT),
        name="mixer" if route else "mixer_meta",
    )(*args)


def _plan_kernel(ku_ref, src_ref, blk_ref, run_first, run_step, next_unit, seg_start, seg_units, *, n_tiles, n_blocks):
    shift = int(math.log2(BLOCK_UNITS))
    group_shift = int(math.log2(SUBLANES * LANES))

    def init_tile(t, c):
        next_unit[t] = t * LOCAL_UNITS
        return c

    lax.fori_loop(0, n_tiles, init_tile, 0)

    def init_block(b, c):
        blk_ref[0, b] = N_EXPERTS - 1
        blk_ref[1, b] = 0
        blk_ref[2, b] = 0
        blk_ref[3, b] = -1
        blk_ref[4, b] = 0
        return c

    lax.fori_loop(0, n_blocks, init_block, 0)

    def per_expert(e, carry):
        g0, position = carry

        def per_tile(t, c):
            units, d_prev = c
            base = next_unit[t]
            d = base - units
            run_first[e * n_tiles + t] = units
            run_step[e * n_tiles + t] = d - d_prev
            k = ku_ref[t, e]
            next_unit[t] = base + k
            return units + k, d

        units, _ = lax.fori_loop(0, n_tiles, per_tile, (0, 0), unroll=4)
        g_pad = g0 + (((units + (BLOCK_UNITS - 1)) >> shift) << shift)
        seg_start[e] = g0
        seg_units[e] = units

        def set_block(b, c):
            blk_ref[0, b] = e
            blk_ref[1, b] = jnp.minimum(g0 + units - (b << shift), BLOCK_UNITS)
            blk_ref[3, b] = g_pad >> shift
            blk_ref[4, b] = position & 1
            return c

        lax.fori_loop(g0 >> shift, g_pad >> shift, set_block, 0)
        return g_pad, position + (units > 0).astype(jnp.int32)

    g_total, _ = lax.fori_loop(0, N_EXPERTS, per_expert, (0, 0))
    blk_ref[2, 0] = g_total >> shift

    src_ref[...] = jnp.zeros(src_ref.shape, jnp.int32)
    in_group = (lax.broadcasted_iota(jnp.int32, (SUBLANES, LANES), 0) * LANES
                + lax.broadcasted_iota(jnp.int32, (SUBLANES, LANES), 1))

    def expert_units(e, c):
        g0 = seg_start[e]
        units = seg_units[e]
        g_pad = g0 + (((units + (BLOCK_UNITS - 1)) >> shift) << shift)

        def per_group(grp, c2):
            g = in_group + (grp << group_shift)
            o = g - g0
            mine = jnp.logical_and(o >= 0, g < g_pad)
            o_eff = jnp.where(o < units, o, (o >> shift) << shift)

            def per_tile(t, acc):
                return acc + jnp.where(o_eff >= run_first[e * n_tiles + t], run_step[e * n_tiles + t], 0)

            offset = lax.fori_loop(0, n_tiles, per_tile, jnp.zeros((SUBLANES, LANES), jnp.int32), unroll=4)
            rows = pl.ds(pl.multiple_of(grp * SUBLANES, SUBLANES), SUBLANES)
            src_ref[rows, :] = jnp.where(mine, o_eff + offset, src_ref[rows, :])
            return c2

        lax.fori_loop(g0 >> group_shift, (g_pad + (SUBLANES * LANES - 1)) >> group_shift, per_group, 0)
        return c

    lax.fori_loop(0, N_EXPERTS, expert_units, 0)


def _plan_call(ku, n_blocks):
    n_tiles = ku.shape[0]
    smem = pl.BlockSpec(memory_space=pltpu.SMEM)
    group = SUBLANES * LANES
    src_rows = -(-n_blocks * BLOCK_UNITS // group) * SUBLANES
    src, blk = pl.pallas_call(
        functools.partial(_plan_kernel, n_tiles=n_tiles, n_blocks=n_blocks),
        in_specs=[smem], out_specs=(pl.BlockSpec(memory_space=pltpu.VMEM), smem),
        out_shape=(jax.ShapeDtypeStruct((src_rows, LANES), jnp.int32),
                   jax.ShapeDtypeStruct((BLK_ROWS, n_blocks), jnp.int32)),
        scratch_shapes=[pltpu.SMEM((N_EXPERTS * n_tiles,), jnp.int32), pltpu.SMEM((N_EXPERTS * n_tiles,), jnp.int32),
                        pltpu.SMEM((n_tiles,), jnp.int32), pltpu.SMEM((N_EXPERTS,), jnp.int32),
                        pltpu.SMEM((N_EXPERTS,), jnp.int32)],
        name="moe_plan",
    )(ku)
    return src.reshape(-1), blk


def _experts_kernel(src_ref, blk_ref, x_hbm, wg_hbm, wu_hbm, wd_hbm, y_hbm,
                    xin, yout, wgf, wuf, wdf, wgub, wdb, sem_in, sem_out, sem_w):
    b = pl.program_id(0)
    used = blk_ref[2, 0]
    slot = b & 1

    def weight_copies(e, s):
        return [pltpu.make_async_copy(hbm.at[e], buf.at[s], sem_w.at[s])
                for hbm, buf in ((wg_hbm, wgf), (wu_hbm, wuf), (wd_hbm, wdf))]

    def unit_rows(blk, u):
        return pl.ds(pl.multiple_of(src_ref[blk * BLOCK_UNITS + u] * UNIT, UNIT), UNIT)

    def in_copy(blk, u, s):
        return pltpu.make_async_copy(x_hbm.at[unit_rows(blk, u), :], xin.at[s, pl.ds(u * UNIT, UNIT), :],
                                     sem_in.at[s])

    def out_copy(blk, u, s):
        return pltpu.make_async_copy(yout.at[s, pl.ds(u * UNIT, UNIT), :], y_hbm.at[unit_rows(blk, u), :],
                                     sem_out.at[s])

    @pl.when(jnp.logical_and(b == 0, used > 0))
    def _():
        for copy in weight_copies(blk_ref[0, 0], blk_ref[4, 0]):
            copy.start(priority=1)
        for u in range(BLOCK_UNITS):
            in_copy(0, u, 0).start()

    @pl.when(b < used)
    def _():
        expert = blk_ref[0, b]
        changed = jnp.logical_or(b == 0, expert != blk_ref[0, jnp.maximum(b - 1, 0)])

        @pl.when(changed)
        def _():
            wslot = blk_ref[4, b]
            later = blk_ref[3, b]
            following = jnp.where(later < used, blk_ref[0, jnp.minimum(later, pl.num_programs(0) - 1)], -1)
            for copy in weight_copies(expert, wslot):
                copy.wait()

            @pl.when(following >= 0)
            def _():
                for copy in weight_copies(following, 1 - wslot):
                    copy.start(priority=1)

            wgub[:, :D_EXPERT] = wgf[wslot].astype(BF16)
            wgub[:, D_EXPERT:] = wuf[wslot].astype(BF16)
            wdb[...] = wdf[wslot].astype(BF16)

        @pl.when(b + 1 < used)
        def _():
            for u in range(BLOCK_UNITS):
                in_copy(b + 1, u, 1 - slot).start(priority=u % 2)

        pltpu.make_async_copy(x_hbm.at[pl.ds(0, MOE_BLOCK), :], xin.at[slot], sem_in.at[slot]).wait()

        n_real = blk_ref[1, b]

        def mlp(rows):
            xb = _unpack_halves(xin[slot, :rows, :])
            gate_up = _dot(xb, wgub[...])
            gate = gate_up[:, :D_EXPERT]
            hid = (gate * _sigmoid(gate)) * gate_up[:, D_EXPERT:]
            yout[slot, :rows, :] = _pack_halves(_dot(hid.astype(BF16), wdb[...]).astype(BF16).astype(F32))

        @pl.when(n_real > BLOCK_UNITS // 2)
        def _():
            mlp(MOE_BLOCK)

        @pl.when(n_real <= BLOCK_UNITS // 2)
        def _():
            mlp(MOE_BLOCK // 2)


        @pl.when(n_real == BLOCK_UNITS)
        def _():
            for u in range(BLOCK_UNITS):
                out_copy(b, u, slot).start(priority=u % 2)

        @pl.when(n_real < BLOCK_UNITS)
        def _():
            lax.fori_loop(0, n_real, lambda u, c: (out_copy(b, u, slot).start(), c)[1], 0)

        @pl.when(jnp.logical_and(b > 0, blk_ref[1, jnp.maximum(b - 1, 0)] == BLOCK_UNITS))
        def _():
            pltpu.make_async_copy(yout.at[1 - slot], y_hbm.at[pl.ds(0, MOE_BLOCK), :], sem_out.at[1 - slot]).wait()

        @pl.when(jnp.logical_and(b > 0, blk_ref[1, jnp.maximum(b - 1, 0)] < BLOCK_UNITS))
        def _():
            lax.fori_loop(0, blk_ref[1, b - 1], lambda u, c: (out_copy(b - 1, u, 1 - slot).wait(), c)[1], 0)

        @pl.when(b == used - 1)
        def _():
            lax.fori_loop(0, n_real, lambda u, c: (out_copy(b, u, slot).wait(), c)[1], 0)


def _experts_call(src, blk, xloc, wg, wu, wd, n_blocks):
    hbm = pl.BlockSpec(memory_space=pl.ANY)
    grid_spec = pltpu.PrefetchScalarGridSpec(
        num_scalar_prefetch=2,
        grid=(n_blocks,),
        in_specs=[hbm, hbm, hbm, hbm],
        out_specs=hbm,
        scratch_shapes=[pltpu.VMEM((2, MOE_BLOCK, D_MODEL // 2), jnp.int32),
                        pltpu.VMEM((2, MOE_BLOCK, D_MODEL // 2), jnp.int32),
                        pltpu.VMEM((2, D_MODEL, D_EXPERT), F32), pltpu.VMEM((2, D_MODEL, D_EXPERT), F32),
                        pltpu.VMEM((2, D_EXPERT, D_MODEL), F32),
                        pltpu.VMEM((D_MODEL, 2 * D_EXPERT), BF16), pltpu.VMEM((D_EXPERT, D_MODEL), BF16),
                        pltpu.SemaphoreType.DMA((2,)), pltpu.SemaphoreType.DMA((2,)),
                        pltpu.SemaphoreType.DMA((2,))],
    )
    return pl.pallas_call(
        _experts_kernel, grid_spec=grid_spec,
        out_shape=jax.ShapeDtypeStruct(xloc.shape, jnp.int32),
        input_output_aliases={2: 0},
        compiler_params=pltpu.CompilerParams(dimension_semantics=("arbitrary",), vmem_limit_bytes=VMEM_LIMIT),
        name="moe_experts",
    )(src, blk, xloc, wg, wu, wd)


def _combine_kernel(route_ref, x_ref, y_ref, gfin_ref, out_a_hbm, out_b_hbm, obuf, sem, *, tm, tiles_a):
    t = pl.program_id(0)
    slot = t & 1

    def tile_out(dst_hbm, tile, s):
        return pltpu.make_async_copy(obuf.at[s], dst_hbm.at[pl.ds(pl.multiple_of(tile * tm, tm), tm), :], sem.at[s])

    @pl.when(t >= 2)
    def _():
        tile_out(out_a_hbm, 0, slot).wait()

    rec = route_ref[0]
    recf = jnp.concatenate([rec[2:4, :].astype(F32), pltpu.bitcast(rec, F32)[4:6, :],
                            jnp.zeros((LANES - 4, tm), F32)], axis=0)
    cols = recf.T
    lrow = lax.broadcasted_iota(jnp.int32, (tm, LOCAL_ROWS), 1).astype(F32)
    select = jnp.where(lrow == cols[:, 0:1], cols[:, 2:3], jnp.where(lrow == cols[:, 1:2], cols[:, 3:4], 0.0))
    xf = x_ref[...] + _dot(select.astype(BF16), _unpack_halves(y_ref[...]))
    obuf[slot] = xf * lax.rsqrt(jnp.mean(xf * xf, axis=-1, keepdims=True) + NORM_EPS) * gfin_ref[...]

    @pl.when(t < tiles_a)
    def _():
        tile_out(out_a_hbm, t, slot).start()

    @pl.when(t >= tiles_a)
    def _():
        tile_out(out_b_hbm, t - tiles_a, slot).start()

    @pl.when(t == pl.num_programs(0) - 1)
    def _():
        tile_out(out_a_hbm, 0, slot).wait()

        @pl.when(t >= 1)
        def _():
            tile_out(out_a_hbm, 0, 1 - slot).wait()


def _combine_call(route, xnew, yloc, gfin, tm, tiles_a):
    n_tiles = route.shape[0]
    tiles_b = n_tiles - tiles_a
    return pl.pallas_call(
        functools.partial(_combine_kernel, tm=tm, tiles_a=tiles_a),
        grid=(n_tiles,),
        in_specs=[pl.BlockSpec((1, ROUTE_ROWS, tm), lambda t: (t, 0, 0)),
                  pl.BlockSpec((tm, D_MODEL), lambda t: (t, 0)),
                  pl.BlockSpec((LOCAL_ROWS, D_MODEL // 2), lambda t: (t, 0)),
                  pl.BlockSpec((1, D_MODEL), lambda t: (0, 0))],
        out_specs=(pl.BlockSpec(memory_space=pl.ANY), pl.BlockSpec(memory_space=pl.ANY)),
        out_shape=(jax.ShapeDtypeStruct((tiles_a * tm, D_MODEL), F32),
                   jax.ShapeDtypeStruct((tiles_b * tm, D_MODEL), F32)),
        scratch_shapes=[pltpu.VMEM((2, tm, D_MODEL), F32), pltpu.SemaphoreType.DMA((2,))],
        compiler_params=pltpu.CompilerParams(dimension_semantics=("arbitrary",), vmem_limit_bytes=VMEM_LIMIT),
        name="moe_combine",
    )(route, xnew, yloc, gfin)


def _prep_weights(g_mix, w_in, w_gla_gk2, b_gla_gk, g_gla_norm, w_br_ret, w_br_gla, w_out, g_ffn,
                  w_rg, b_rg, w_re, b_re):
    w_int = jnp.swapaxes(w_in, 0, 1).astype(BF16)
    wgk = jnp.pad(w_gla_gk2, ((0, LANES - GLA_RANK), (0, 0))).astype(BF16)
    def router_rows(groups, experts):
        return jnp.concatenate([groups, jnp.zeros((SUBLANES - N_GROUPS,) + groups.shape[1:], F32), experts,
                                jnp.zeros((ROUTER_LANES - SUBLANES - N_EXPERTS,) + experts.shape[1:], F32)], axis=0)

    wrt = router_rows(w_rg.T, w_re.T)
    brt = router_rows(b_rg, b_re).reshape(1, ROUTER_LANES)
    wrt_hi = wrt.astype(BF16)
    wrt = jnp.concatenate([wrt_hi, (wrt - wrt_hi.astype(F32)).astype(BF16)], axis=0)
    return (g_mix.reshape(1, D_MODEL), w_int, wgk, b_gla_gk.reshape(1, HK), g_gla_norm.reshape(1, HEAD_V),
            w_br_ret.astype(BF16), w_br_gla.astype(BF16), w_out.astype(BF16), g_ffn.reshape(1, D_MODEL), wrt, brt)


def _tri_consts(tm):
    r = np.arange(tm)
    low = (r[None, :] <= r[:, None])
    bd = low & ((r[None, :] // CHUNK) == (r[:, None] // CHUNK))
    return jnp.asarray(bd, BF16), jnp.asarray(low, BF16)


def _state_to_kernel(s):
    return s.reshape(s.shape[0], HK, HEAD_V)


def _state_from_kernel(s):
    return s.reshape(1, s.shape[0], N_HEADS, HEAD_K, HEAD_V)


def kernel(x_prompt, x_sample, state_ret, state_gla, meta_tokens, g_mix, w_in, w_gla_gk2, b_gla_gk, g_gla_norm, w_br_ret, w_br_gla, w_out, g_ffn, w_router_group, b_router_group, w_router_expert, b_router_expert, w_exp_gate, w_exp_up, w_exp_down, g_final):
    n_b, seq, _ = x_prompt.shape
    n_s, dec_seq, _ = x_sample.shape
    depth = state_ret.shape[0]
    assert depth == 1 and dec_seq == CHUNK and seq % TILE_ROWS == 0 and n_s % TILE_CHUNKS == 0
    cos_t, sin_t, cos_m, sin_m = _rotary_tables(seq, PAST_LEN)
    weights = _prep_weights(g_mix[0], w_in[0], w_gla_gk2[0], b_gla_gk[0], g_gla_norm[0], w_br_ret[0], w_br_gla[0],
                            w_out[0], g_ffn[0], w_router_group[0], b_router_group[0], w_router_expert[0],
                            b_router_expert[0])
    zero_state = jnp.zeros((1, HK, HEAD_V), F32)

    x_meta = jnp.concatenate([jnp.zeros((CHUNK - N_META, D_MODEL), F32), meta_tokens.astype(F32)], axis=0)
    meta_out = _mixer_call(None, x_meta.reshape(1, CHUNK, D_MODEL), cos_m, sin_m, (zero_state, zero_state),
                           (zero_state, zero_state), weights, _tri_consts(CHUNK), n_chunks=1, route=False)
    meta_ret, meta_gla = meta_out[6], meta_out[7]

    xnew, xloc, route, ku, ret_p, gla_p, ret_s, gla_s = _mixer_call(
        x_prompt, x_sample, cos_t, sin_t, (meta_ret, meta_gla),
        (_state_to_kernel(state_ret[0]), _state_to_kernel(state_gla[0])), weights, _tri_consts(TILE_ROWS),
        n_chunks=TILE_CHUNKS, route=True)

    n_tiles = route.shape[0]
    n_blocks = n_tiles * LOCAL_UNITS // BLOCK_UNITS + N_EXPERTS
    src, blk = _plan_call(ku[:, 0, :N_EXPERTS], n_blocks)
    yloc = _experts_call(src, blk, xloc, w_exp_gate[0], w_exp_up[0], w_exp_down[0], n_blocks)
    y_p, y_s = _combine_call(route, xnew, yloc, g_final.reshape(1, D_MODEL), TILE_ROWS, n_b * seq // TILE_ROWS)

    return (y_p.reshape(n_b, seq, D_MODEL), y_s.reshape(n_s, dec_seq, D_MODEL),
            _state_from_kernel(ret_p), _state_from_kernel(gla_p), _state_from_kernel(ret_s), _state_from_kernel(gla_s))
```
